```python
import math
import jax, jax.numpy as jnp
from jax import lax
import numpy as np

D_MODEL = 1024
BATCH = 4
SEQ = 4096
DEPTH = 1

CTX_LEN = 256
GRID_W = 64
CHUNK = 64
EPS = 1e-6

GLA_HEADS = 4
GLA_DK = 64
GLA_DV = 128
GLA_INNER = GLA_HEADS * GLA_DV
GLA_RANK = 16
GLA_GATE_NORM = 16.0

SSD_HEADS = 8
SSD_P = 64
SSD_N = 64
SSD_GROUPS = 2
SSD_INNER = SSD_HEADS * SSD_P
SSD_CONV_DIM = SSD_INNER + 2 * SSD_GROUPS * SSD_N
CONV_K = 3

D_MIX = GLA_INNER + SSD_INNER
D_FF = -(-8 * D_MODEL // (3 * 256)) * 256

PROJ_SIZES = (GLA_HEADS * GLA_DK, GLA_HEADS * GLA_DK, GLA_INNER, GLA_INNER, 2 * GLA_RANK,
              SSD_INNER, SSD_CONV_DIM, 2 * SSD_HEADS)
D_IN_PROJ = sum(PROJ_SIZES)
PROJ_SPLITS = tuple(int(s) for s in np.cumsum(PROJ_SIZES)[:-1])

kernel_name = "hybrid_gla_ssd_prefix_dit_layer"


def rms_norm(x, g):
    xf = x.astype(jnp.float32)
    y = xf * lax.rsqrt(jnp.mean(xf * xf, axis=-1, keepdims=True) + EPS)
    return (y * g).astype(x.dtype)


def modulate(h, shift, scale):
    return h * (1 + scale[:, None]) + shift[:, None]


def _flip(*arrs):
    return tuple(a[:, ::-1] for a in arrs)


def _chunks(a):
    b, t = a.shape[:2]
    return jnp.moveaxis(a.reshape(b, t // CHUNK, CHUNK, *a.shape[2:]), 1, 0)


def _unchunk(a):
    a = jnp.moveaxis(a, 0, 1)
    return a.reshape(a.shape[0], a.shape[1] * a.shape[2], *a.shape[3:])


def _grid_dwconv(u, w, grid_hw):
    b, t, ch = u.shape
    img = u.reshape(b, grid_hw[0], grid_hw[1], ch)
    out = lax.conv_general_dilated(img, w[:, :, None, :].astype(u.dtype), (1, 1), "SAME",
                                   dimension_numbers=("NHWC", "HWIO", "NHWC"),
                                   feature_group_count=ch)
    return out.reshape(b, t, ch)


def _gla_scan(q, k, v, g, s0):
    mask = jnp.tril(jnp.ones((CHUNK, CHUNK), bool))

    def step(s, inp):
        qc, kc, vc, gc = inp
        cum = jnp.cumsum(gc.astype(jnp.float32), axis=1)
        inter = jnp.einsum("blhk,bhkv->blhv", qc * jnp.exp(cum), s)
        diff = cum[:, :, None] - cum[:, None, :]
        decay = jnp.exp(jnp.where(mask[None, :, :, None, None], diff, -jnp.inf))
        att = jnp.einsum("bihk,bjhk,bijhk->bhij", qc, kc, decay)
        intra = jnp.einsum("bhij,bjhv->bihv", att, vc)
        tot = cum[:, -1]
        s_new = s * jnp.exp(tot)[..., None] + jnp.einsum(
            "blhk,blhv->bhkv", kc * jnp.exp(tot[:, None] - cum), vc)
        return s_new, inter + intra

    s_fin, out = lax.scan(step, s0, (_chunks(q), _chunks(k), _chunks(v), _chunks(g)))
    return _unchunk(out), s_fin


def _ssd_scan(xs, dt, a, bm, cm, s0):
    mask = jnp.tril(jnp.ones((CHUNK, CHUNK), bool))

    def step(s, inp):
        xc, dtc, ac, bc, cc = inp
        cum = jnp.cumsum(ac.astype(jnp.float32), axis=1)
        seg = cum[:, :, None] - cum[:, None, :]
        lmat = jnp.exp(jnp.where(mask[None, :, :, None], seg, -jnp.inf))
        scores = jnp.einsum("bihn,bjhn,bijh->bhij", cc, bc, lmat)
        intra = jnp.einsum("bhij,bjhp->bihp", scores, xc * dtc[..., None])
        inter = jnp.einsum("bihn,bhpn->bihp", cc, s) * jnp.exp(cum)[..., None]
        tot = cum[:, -1]
        w = (jnp.exp(tot[:, None] - cum) * dtc)[..., None]
        s_new = s * jnp.exp(tot)[:, :, None, None] + jnp.einsum("blhn,blhp->bhpn", bc * w, xc)
        return s_new, inter + intra

    s_fin, out = lax.scan(step, s0, (_chunks(xs), _chunks(dt), _chunks(a), _chunks(bm), _chunks(cm)))
    return _unchunk(out), s_fin


def _bidir(scan_fn, ctx_f, ctx_b, lat_f, lat_b, s0):
    yc_f, sc_f = scan_fn(*ctx_f, s0)
    yl_f, _ = scan_fn(*lat_f, sc_f)
    yc_b, sc_b = scan_fn(*_flip(*ctx_b), s0)
    yl_b, _ = scan_fn(*_flip(*lat_b), sc_b)
    return yc_f + yc_b[:, ::-1], yl_f + yl_b[:, ::-1]


def _stream_features(h, grid_hw, w_in, conv_w, conv_b, gla_wg_f, gla_bg_f, gla_wg_b, gla_bg_b,
                     a_log_f, a_log_b, dt_bias_f, dt_bias_b):
    b, t, _ = h.shape
    q, k, v, r, lr, z, xbc, dt_raw = jnp.split(h @ w_in, PROJ_SPLITS, axis=-1)
    heads = lambda arr, n: arr.reshape(b, t, n, -1)
    lr_f, lr_b = jnp.split(lr, 2, axis=-1)
    g_f = heads(jax.nn.log_sigmoid(lr_f @ gla_wg_f + gla_bg_f) / GLA_GATE_NORM, GLA_HEADS)
    g_b = heads(jax.nn.log_sigmoid(lr_b @ gla_wg_b + gla_bg_b) / GLA_GATE_NORM, GLA_HEADS)
    xbc = jax.nn.silu(_grid_dwconv(xbc, conv_w, grid_hw) + conv_b)
    xs, bm, cm = jnp.split(xbc, [SSD_INNER, SSD_INNER + SSD_GROUPS * SSD_N], axis=-1)
    rep = SSD_HEADS // SSD_GROUPS
    dt_f_raw, dt_b_raw = jnp.split(dt_raw, 2, axis=-1)
    dt_f = jax.nn.softplus(dt_f_raw + dt_bias_f)
    dt_b = jax.nn.softplus(dt_b_raw + dt_bias_b)
    return {
        "q": heads(q, GLA_HEADS) * GLA_DK ** -0.5,
        "k": heads(k, GLA_HEADS),
        "v": heads(v, GLA_HEADS),
        "r": r, "g_f": g_f, "g_b": g_b,
        "xs": heads(xs, SSD_HEADS),
        "bm": jnp.repeat(heads(bm, SSD_GROUPS), rep, axis=2),
        "cm": jnp.repeat(heads(cm, SSD_GROUPS), rep, axis=2),
        "dt_f": dt_f, "a_f": dt_f * -jnp.exp(a_log_f),
        "dt_b": dt_b, "a_b": dt_b * -jnp.exp(a_log_b),
        "z": z,
    }


def _merge_heads(f, y_gla, y_ssd, gla_norm, d_skip, ssd_norm, w_out):
    b, t = y_gla.shape[:2]
    o_gla = rms_norm(y_gla, gla_norm) * jax.nn.silu(f["r"]).reshape(b, t, GLA_HEADS, GLA_DV)
    y = (y_ssd + d_skip[:, None] * f["xs"]).reshape(b, t, SSD_INNER) * jax.nn.silu(f["z"])
    o_ssd = rms_norm(y.reshape(b, t, SSD_GROUPS, -1), ssd_norm.reshape(SSD_GROUPS, -1))
    o = jnp.concatenate([o_gla.reshape(b, t, GLA_INNER), o_ssd.reshape(b, t, SSD_INNER)], axis=-1)
    return o @ w_out


def hybrid_mixer(h_ctx, h_lat, rows, w_in, conv_w, conv_b, gla_wg_f, gla_bg_f, gla_wg_b, gla_bg_b,
                 gla_norm, a_log_f, a_log_b, dt_bias_f, dt_bias_b, d_skip, ssd_norm, w_out):
    per_stream = (w_in, conv_w, conv_b, gla_wg_f, gla_bg_f, gla_wg_b, gla_bg_b,
                  a_log_f, a_log_b, dt_bias_f, dt_bias_b)
    fc = _stream_features(h_ctx, (1, h_ctx.shape[1]), *per_stream)
    fl = _stream_features(h_lat, (rows, GRID_W), *per_stream)
    b = h_lat.shape[0]
    gla0 = jnp.zeros((b, GLA_HEADS, GLA_DK, GLA_DV), jnp.float32)
    ssd0 = jnp.zeros((b, SSD_HEADS, SSD_P, SSD_N), jnp.float32)
    gla_in = lambda f, d: (f["q"], f["k"], f["v"], f["g_" + d])
    ssd_in = lambda f, d: (f["xs"], f["dt_" + d], f["a_" + d], f["bm"], f["cm"])
    gla_c, gla_l = _bidir(_gla_scan, gla_in(fc, "f"), gla_in(fc, "b"),
                          gla_in(fl, "f"), gla_in(fl, "b"), gla0)
    ssd_c, ssd_l = _bidir(_ssd_scan, ssd_in(fc, "f"), ssd_in(fc, "b"),
                          ssd_in(fl, "f"), ssd_in(fl, "b"), ssd0)
    y_ctx = _merge_heads(fc, gla_c, ssd_c, gla_norm, d_skip, ssd_norm, w_out)
    y_lat = _merge_heads(fl, gla_l, ssd_l, gla_norm, d_skip, ssd_norm, w_out)
    return y_ctx, y_lat


def swiglu(h, w_gate, w_up, w_down):
    return (jax.nn.silu(h @ w_gate) * (h @ w_up)) @ w_down


def setup_inputs(seed: int = 0) -> dict:
    key = jax.random.key(seed)
    ks = jax.random.split(key, 28)
    f32 = jnp.float32
    nrm = lambda k, shape, scale: jax.random.normal(k, shape, f32) * scale
    gain = lambda k, shape: 1.0 + 0.1 * jax.random.normal(k, shape, f32)

    def dt_bias(k):
        dt = jnp.exp(jax.random.uniform(k, (DEPTH, SSD_HEADS), f32, math.log(1e-3), math.log(1e-1)))
        return dt + jnp.log(-jnp.expm1(-dt))

    def a_log(k):
        return jnp.log(jax.random.uniform(k, (DEPTH, SSD_HEADS), f32, 1.0, 16.0))

    return {
        "x": nrm(ks[0], (BATCH, SEQ, D_MODEL), 1.0),
        "c": nrm(ks[1], (BATCH, D_MODEL), 1.0),
        "ctx": nrm(ks[2], (BATCH, CTX_LEN, D_MODEL), 1.0),
        "c_ctx": nrm(ks[3], (D_MODEL,), 0.5),
        "w_mod": nrm(ks[4], (DEPTH, D_MODEL, 6 * D_MODEL), D_MODEL ** -0.5),
        "b_mod": nrm(ks[5], (DEPTH, 6 * D_MODEL), 0.02),
        "norm_mix_pre": gain(ks[6], (DEPTH, D_MODEL)),
        "norm_mix_post": gain(ks[7], (DEPTH, D_MODEL)),
        "norm_ffn_pre": gain(ks[8], (DEPTH, D_MODEL)),
        "norm_ffn_post": gain(ks[9], (DEPTH, D_MODEL)),
        "w_in": nrm(ks[10], (DEPTH, D_MODEL, D_IN_PROJ), D_MODEL ** -0.5),
        "conv_w": nrm(ks[11], (DEPTH, CONV_K, CONV_K, SSD_CONV_DIM), (CONV_K * CONV_K) ** -0.5),
        "conv_b": nrm(ks[12], (DEPTH, SSD_CONV_DIM), 0.02),
        "gla_wg_f": nrm(ks[13], (DEPTH, GLA_RANK, GLA_HEADS * GLA_DK), GLA_RANK ** -0.5),
        "gla_bg_f": nrm(ks[14], (DEPTH, GLA_HEADS * GLA_DK), 0.1),
        "gla_wg_b": nrm(ks[15], (DEPTH, GLA_RANK, GLA_HEADS * GLA_DK), GLA_RANK ** -0.5),
        "gla_bg_b": nrm(ks[16], (DEPTH, GLA_HEADS * GLA_DK), 0.1),
        "gla_norm": gain(ks[17], (DEPTH, GLA_DV)),
        "a_log_f": a_log(ks[18]),
        "a_log_b": a_log(ks[19]),
        "dt_bias_f": dt_bias(ks[20]),
        "dt_bias_b": dt_bias(ks[21]),
        "d_skip": gain(ks[22], (DEPTH, SSD_HEADS)),
        "ssd_norm": gain(ks[23], (DEPTH, SSD_INNER)),
        "w_out": nrm(ks[24], (DEPTH, D_MIX, D_MODEL), D_MIX ** -0.5),
        "w_gate": nrm(ks[25], (DEPTH, D_MODEL, D_FF), D_MODEL ** -0.5),
        "w_up": nrm(ks[26], (DEPTH, D_MODEL, D_FF), D_MODEL ** -0.5),
        "w_down": nrm(ks[27], (DEPTH, D_FF, D_MODEL), D_FF ** -0.5),
    }


def reference(x, c, ctx, c_ctx, w_mod, b_mod, norm_mix_pre, norm_mix_post, norm_ffn_pre, norm_ffn_post,
              w_in, conv_w, conv_b, gla_wg_f, gla_bg_f, gla_wg_b, gla_bg_b, gla_norm,
              a_log_f, a_log_b, dt_bias_f, dt_bias_b, d_skip, ssd_norm, w_out, w_gate, w_up, w_down):
    rows = x.shape[1] // GRID_W
    x_lat, x_ctx = x, ctx
    for layer in range(DEPTH):
        m_lat = jnp.split(jax.nn.silu(c) @ w_mod[layer] + b_mod[layer], 6, axis=-1)
        m_ctx = jnp.split(jax.nn.silu(c_ctx)[None] @ w_mod[layer] + b_mod[layer], 6, axis=-1)

        h_lat = modulate(rms_norm(x_lat, norm_mix_pre[layer]), m_lat[0], m_lat[1])
        h_ctx = modulate(rms_norm(x_ctx, norm_mix_pre[layer]), m_ctx[0], m_ctx[1])
        y_ctx, y_lat = hybrid_mixer(
            h_ctx, h_lat, rows, w_in[layer], conv_w[layer], conv_b[layer],
            gla_wg_f[layer], gla_bg_f[layer], gla_wg_b[layer], gla_bg_b[layer], gla_norm[layer],
            a_log_f[layer], a_log_b[layer], dt_bias_f[layer], dt_bias_b[layer],
            d_skip[layer], ssd_norm[layer], w_out[layer])

        x_lat = x_lat + m_lat[2][:, None] * rms_norm(y_lat, norm_mix_post[layer])
        f_lat = swiglu(modulate(rms_norm(x_lat, norm_ffn_pre[layer]), m_lat[3], m_lat[4]),
                       w_gate[layer], w_up[layer], w_down[layer])
        x_lat = x_lat + m_lat[5][:, None] * rms_norm(f_lat, norm_ffn_post[layer])

        if layer + 1 < DEPTH:
            x_ctx = x_ctx + m_ctx[2][:, None] * rms_norm(y_ctx, norm_mix_post[layer])
            f_ctx = swiglu(modulate(rms_norm(x_ctx, norm_ffn_pre[layer]), m_ctx[3], m_ctx[4]),
                           w_gate[layer], w_up[layer], w_down[layer])
            x_ctx = x_ctx + m_ctx[5][:, None] * rms_norm(f_ctx, norm_ffn_post[layer])
    return x_lat
```

```python
import functools

import jax
import jax.numpy as jnp
from jax import lax
from jax.experimental import pallas as pl
from jax.experimental.pallas import tpu as pltpu

F32 = jnp.float32
BF16 = jnp.bfloat16

D_MODEL = 1024
CTX_LEN = 256
GRID_W = 64
EPS = 1e-6

GLA_HEADS = 4
GLA_DK = 64
GLA_DV = 128
GLA_QK = GLA_HEADS * GLA_DK
GLA_INNER = GLA_HEADS * GLA_DV
GLA_RANK = 16
GLA_GATE_NORM = 16.0

SSD_HEADS = 8
SSD_P = 64
SSD_N = 64
SSD_GROUPS = 2
SSD_HPG = SSD_HEADS // SSD_GROUPS
SSD_INNER = SSD_HEADS * SSD_P
SSD_BC = SSD_GROUPS * SSD_N
SSD_CONV_DIM = SSD_INNER + 2 * SSD_BC
CONV_K = 3
D_FF = 2816

QKV_W = 2 * GLA_QK + GLA_INNER
COL_R = QKV_W
COL_Z = COL_R + GLA_INNER
COL_XBC = COL_Z + SSD_INNER
COL_TAIL = COL_XBC + SSD_CONV_DIM
TAIL_W = 128
PROJ_W = COL_TAIL + TAIL_W
DT_LANE = 2 * GLA_RANK

TM_PROJ = 256
TM_CONV = 256
TM_FFN = 512
CHUNK = 128
SUB = 32
NEG_BIG = -1e30
VMEM_LIMIT = 56 * 1024 * 1024


def _silu(x):
    return x / (1.0 + jnp.exp(-x))


def _softplus(x):
    return jnp.maximum(x, 0.0) + jnp.log(1.0 + jnp.exp(-jnp.abs(x)))


def _rms(x):
    return x * lax.rsqrt(jnp.mean(x * x, axis=-1, keepdims=True) + EPS)


def _dot(a, b):
    return jnp.dot(a, b, preferred_element_type=F32)


def _dot_nt(a, b):
    return lax.dot_general(a, b, (((1,), (1,)), ((), ())), preferred_element_type=F32)


def _dot_tn(a, b):
    return lax.dot_general(a, b, (((0,), (0,)), ((), ())), preferred_element_type=F32)


def _mod_kernel(c_ref, w_ref, b_ref, o_ref):
    s = _silu(c_ref[...])
    o_ref[...] = _dot(s.astype(BF16), w_ref[...].astype(BF16)) + b_ref[...]


def _modulation(c_all, w_mod, b_mod):
    n = w_mod.shape[1]
    tn = 1024
    return pl.pallas_call(
        _mod_kernel,
        grid=(n // tn,),
        in_specs=[pl.BlockSpec((8, D_MODEL), lambda j: (0, 0)),
                  pl.BlockSpec((D_MODEL, tn), lambda j: (0, j)),
                  pl.BlockSpec((1, tn), lambda j: (0, j))],
        out_specs=pl.BlockSpec((8, tn), lambda j: (0, j)),
        out_shape=jax.ShapeDtypeStruct((8, n), F32),
        name="adaln_mod",
    )(c_all, w_mod, b_mod)


def _inproj_kernel(x_ref, ctx_ref, m_ref, gpre_ref, w_ref, wg_ref, bg_ref, dtb_ref,
                   qkv_ref, r_ref, z_ref, xbc_ref, g_ref, dt_ref):
    b = pl.program_id(0)
    i = pl.program_id(1)
    is_ctx = i == pl.num_programs(1) - 1
    x = jnp.where(is_ctx, ctx_ref[0], x_ref[0])
    mrow = m_ref[pl.ds(jnp.where(is_ctx, 4, b), 1), :]
    shift = mrow[:, 0:D_MODEL]
    scale = mrow[:, D_MODEL:2 * D_MODEL]
    h = (_rms(x) * gpre_ref[...]) * (1.0 + scale) + shift
    acc = _dot(h.astype(BF16), w_ref[...])
    qkv_ref[0, :, 0:GLA_QK] = (acc[:, 0:GLA_QK] * (GLA_DK ** -0.5)).astype(BF16)
    qkv_ref[0, :, GLA_QK:QKV_W] = acc[:, GLA_QK:QKV_W].astype(BF16)
    r_ref[0] = _silu(acc[:, COL_R:COL_Z]).astype(BF16)
    z_ref[0] = _silu(acc[:, COL_Z:COL_XBC]).astype(BF16)
    xbc_ref[0] = acc[:, COL_XBC:COL_TAIL].astype(BF16)
    tail = acc[:, COL_TAIL:PROJ_W]
    logits = _dot(tail.astype(BF16), wg_ref[...]) + bg_ref[...]
    g_ref[0] = -_softplus(-logits) * (1.0 / GLA_GATE_NORM)
    dt_ref[0] = _softplus(tail + dtb_ref[...])


def _input_projection(x, ctx, m_all, gpre, w_perm, wg_pad, bg_cat, dtb_pad):
    bsz, seq, _ = x.shape
    n_lat = seq // TM_PROJ
    tt = CTX_LEN + seq
    const = lambda b, i: (0, 0)
    tok = lambda b, i: (b, i, 0)
    return pl.pallas_call(
        _inproj_kernel,
        grid=(bsz, 1 + n_lat),
        in_specs=[pl.BlockSpec((1, TM_PROJ, D_MODEL), lambda b, i: (b, jnp.minimum(i, n_lat - 1), 0)),
                  pl.BlockSpec((1, TM_PROJ, D_MODEL), lambda b, i: (b, 0, 0)),
                  pl.BlockSpec(m_all.shape, const),
                  pl.BlockSpec((1, D_MODEL), const),
                  pl.BlockSpec((D_MODEL, PROJ_W), const),
                  pl.BlockSpec((TAIL_W, 2 * GLA_QK), const),
                  pl.BlockSpec((1, 2 * GLA_QK), const),
                  pl.BlockSpec((1, TAIL_W), const)],
        out_specs=[pl.BlockSpec((1, TM_PROJ, QKV_W), tok),
                   pl.BlockSpec((1, TM_PROJ, GLA_INNER), tok),
                   pl.BlockSpec((1, TM_PROJ, SSD_INNER), tok),
                   pl.BlockSpec((1, TM_PROJ, SSD_CONV_DIM), tok),
                   pl.BlockSpec((1, TM_PROJ, 2 * GLA_QK), tok),
                   pl.BlockSpec((1, TM_PROJ, TAIL_W), tok)],
        out_shape=[jax.ShapeDtypeStruct((bsz, tt, QKV_W), BF16),
                   jax.ShapeDtypeStruct((bsz, tt, GLA_INNER), BF16),
                   jax.ShapeDtypeStruct((bsz, tt, SSD_INNER), BF16),
                   jax.ShapeDtypeStruct((bsz, tt, SSD_CONV_DIM), BF16),
                   jax.ShapeDtypeStruct((bsz, tt, 2 * GLA_QK), F32),
                   jax.ShapeDtypeStruct((bsz, tt, TAIL_W), F32)],
        compiler_params=pltpu.CompilerParams(
            dimension_semantics=("parallel", "arbitrary"), vmem_limit_bytes=VMEM_LIMIT),
        name="in_proj",
    )(x, ctx, m_all, gpre, w_perm, wg_pad, bg_cat, dtb_pad)


def _conv_kernel(prev_ref, main_ref, next_ref, w_ref, b_ref, o_ref):
    i = pl.program_id(1)
    last = pl.num_programs(1) - 1
    is_ctx = i == last
    main = main_ref[0].astype(F32)
    prev = jnp.where(jnp.logical_and(i >= 1, i < last), prev_ref[0].astype(F32), 0.0)
    nxt = jnp.where(i < last - 1, next_ref[0].astype(F32), 0.0)
    ext = jnp.concatenate([prev, main, nxt], axis=0)
    n_ext = ext.shape[0]
    e = lax.broadcasted_iota(jnp.int32, (n_ext, 1), 0)
    col = jnp.bitwise_and(e, GRID_W - 1)
    has_left = jnp.where(is_ctx, e - GRID_W, col) != 0
    has_right = jnp.where(is_ctx, e - (GRID_W + TM_CONV - 1), col - (GRID_W - 1)) != 0
    src = (jnp.where(has_left, pltpu.roll(ext, 1, axis=0), 0.0),
           ext,
           jnp.where(has_right, pltpu.roll(ext, n_ext - 1, axis=0), 0.0))
    row_on = jnp.where(is_ctx, 0.0, 1.0)
    acc = jnp.zeros_like(main) + b_ref[...]
    for dy in range(CONV_K):
        for dx in range(CONV_K):
            wk = w_ref[CONV_K * dy + dx:CONV_K * dy + dx + 1, :]
            if dy != 1:
                wk = wk * row_on
            acc = acc + wk * src[dx][GRID_W * dy:GRID_W * dy + TM_CONV, :]
    o_ref[0] = _silu(acc).astype(o_ref.dtype)


def _grid_conv(xbc, conv_w9, conv_b):
    bsz, tt, ch = xbc.shape
    n_tiles = tt // TM_CONV
    rows_per_tile = TM_CONV // GRID_W
    last_row = tt // GRID_W - 1
    const = lambda b, i: (0, 0)
    return pl.pallas_call(
        _conv_kernel,
        grid=(bsz, n_tiles),
        in_specs=[pl.BlockSpec((1, GRID_W, ch),
                               lambda b, i: (b, jnp.clip(rows_per_tile * i - 1, 0, last_row), 0)),
                  pl.BlockSpec((1, TM_CONV, ch), lambda b, i: (b, i, 0)),
                  pl.BlockSpec((1, GRID_W, ch),
                               lambda b, i: (b, jnp.clip(rows_per_tile * (i + 1), 0, last_row), 0)),
                  pl.BlockSpec((CONV_K * CONV_K, ch), const),
                  pl.BlockSpec((1, ch), const)],
        out_specs=pl.BlockSpec((1, TM_CONV, ch), lambda b, i: (b, i, 0)),
        out_shape=jax.ShapeDtypeStruct((bsz, tt, ch), BF16),
        compiler_params=pltpu.CompilerParams(
            dimension_semantics=("parallel", "arbitrary"), vmem_limit_bytes=VMEM_LIMIT),
        name="grid_conv",
    )(xbc, xbc, xbc, conv_w9, conv_b)


def _cumsum_rows(x, reverse):
    n = x.shape[0]
    row = lax.broadcasted_iota(jnp.int32, x.shape, 0)
    s = 1
    while s < n:
        if reverse:
            x = x + jnp.where(row < n - s, pltpu.roll(x, n - s, axis=0), 0.0)
        else:
            x = x + jnp.where(row >= s, pltpu.roll(x, s, axis=0), 0.0)
        s *= 2
    return x


def _chunk_maps(n_ctx, n_all):
    n_lat = n_all - n_ctx
    fwd = lambda s: jnp.where(s < n_ctx, n_lat + s, s - n_ctx)
    bwd = lambda s: n_all - 1 - s
    return fwd, bwd


def _gla_direction(qkv, g, s_ref, reverse):
    n = qkv.shape[0]
    nb = n // SUB
    q = qkv[:, 0:GLA_QK].astype(F32)
    k = qkv[:, GLA_QK:2 * GLA_QK].astype(F32)
    cum = _cumsum_rows(g, reverse)
    zero = jnp.zeros((1, GLA_QK), F32)
    if reverse:
        tot = cum[0:1, :]
        refs = [cum[SUB * (blk + 1):SUB * (blk + 1) + 1, :] for blk in range(nb - 1)] + [zero]
    else:
        tot = cum[n - 1:n, :]
        refs = [zero] + [cum[SUB * blk - 1:SUB * blk, :] for blk in range(1, nb)]
    ref_rows = jnp.concatenate([jnp.broadcast_to(r, (SUB, GLA_QK)) for r in refs], axis=0)
    q_blk = q * jnp.exp(cum - ref_rows)
    q_in = (q * jnp.exp(cum)).astype(BF16)
    k_out = (k * jnp.exp(tot - cum)).astype(BF16)
    row = lax.broadcasted_iota(jnp.int32, (n, GLA_QK), 0)
    q_segs, k_segs = [], []
    for blk in range(nb):
        in_blk = jnp.logical_and(row >= SUB * blk, row < SUB * (blk + 1))
        seen = (row >= SUB * blk) if reverse else (row < SUB * (blk + 1))
        q_segs.append(jnp.where(in_blk, q_blk, 0.0).astype(BF16))
        k_segs.append((k * jnp.exp(jnp.where(seen, refs[blk] - cum, NEG_BIG))).astype(BF16))
    ii = lax.broadcasted_iota(jnp.int32, (n, n), 0)
    jj = lax.broadcasted_iota(jnp.int32, (n, n), 1)
    causal = (ii <= jj) if reverse else (ii >= jj)
    decay_col = jnp.transpose(jnp.broadcast_to(tot, (GLA_DV, GLA_QK)))
    outs = []
    for h in range(GLA_HEADS):
        ks = slice(GLA_DK * h, GLA_DK * (h + 1))
        v_h = qkv[:, 2 * GLA_QK + GLA_DV * h:2 * GLA_QK + GLA_DV * (h + 1)]
        q_cat = jnp.concatenate([s[:, ks] for s in q_segs], axis=1)
        k_cat = jnp.concatenate([s[:, ks] for s in k_segs], axis=1)
        att = jnp.where(causal, _dot_nt(q_cat, k_cat), 0.0).astype(BF16)
        state = s_ref[h]
        outs.append(_dot(att, v_h) + _dot(q_in[:, ks], state.astype(BF16)))
        s_ref[h] = state * jnp.exp(decay_col[ks, :]) + _dot_tn(k_out[:, ks], v_h)
    return jnp.concatenate(outs, axis=1)


def _gla_kernel(qkv_f_ref, g_f_ref, qkv_b_ref, g_b_ref, yf_ref, yb_ref, sf_ref, sb_ref):
    @pl.when(pl.program_id(1) == 0)
    def _():
        sf_ref[...] = jnp.zeros_like(sf_ref)
        sb_ref[...] = jnp.zeros_like(sb_ref)

    yf_ref[0] = _gla_direction(qkv_f_ref[0], g_f_ref[0], sf_ref, False).astype(yf_ref.dtype)
    yb_ref[0] = _gla_direction(qkv_b_ref[0], g_b_ref[0], sb_ref, True).astype(yb_ref.dtype)


def _gla_scan(qkv, g):
    bsz, tt, _ = qkv.shape
    n_all = tt // CHUNK
    fwd, bwd = _chunk_maps(CTX_LEN // CHUNK, n_all)
    y_spec = lambda m: pl.BlockSpec((1, CHUNK, GLA_INNER), lambda b, s: (b, m(s), 0))
    return pl.pallas_call(
        _gla_kernel,
        grid=(bsz, n_all),
        in_specs=[pl.BlockSpec((1, CHUNK, QKV_W), lambda b, s: (b, fwd(s), 0)),
                  pl.BlockSpec((1, CHUNK, GLA_QK), lambda b, s: (b, fwd(s), 0)),
                  pl.BlockSpec((1, CHUNK, QKV_W), lambda b, s: (b, bwd(s), 0)),
                  pl.BlockSpec((1, CHUNK, GLA_QK), lambda b, s: (b, bwd(s), 1))],
        out_specs=[y_spec(fwd), y_spec(bwd)],
        out_shape=[jax.ShapeDtypeStruct((bsz, tt, GLA_INNER), BF16)] * 2,
        scratch_shapes=[pltpu.VMEM((GLA_HEADS, GLA_DK, GLA_DV), F32)] * 2,
        compiler_params=pltpu.CompilerParams(
            dimension_semantics=("parallel", "arbitrary"), vmem_limit_bytes=VMEM_LIMIT),
        name="gla_scan",
    )(qkv, g, qkv, g)


def _ssd_direction(xbc, dt, nega, s_ref, lane0, reverse):
    n = xbc.shape[0]
    cum = _cumsum_rows(dt * nega, reverse)
    cum_t = jnp.transpose(cum)
    dt_t = jnp.transpose(dt)
    ii = lax.broadcasted_iota(jnp.int32, (n, n), 0)
    jj = lax.broadcasted_iota(jnp.int32, (n, n), 1)
    causal = (ii <= jj) if reverse else (ii >= jj)
    end = 0 if reverse else n - 1
    outs = []
    for grp in range(SSD_GROUPS):
        b_g = xbc[:, SSD_INNER + SSD_N * grp:SSD_INNER + SSD_N * (grp + 1)]
        c_g = xbc[:, SSD_INNER + SSD_BC + SSD_N * grp:SSD_INNER + SSD_BC + SSD_N * (grp + 1)]
        scores = _dot_nt(c_g, b_g)
        intra, x_w, e_cum, e_tot = [], [], [], []
        for hh in range(SSD_HPG):
            h = SSD_HPG * grp + hh
            lane = lane0 + h
            x_h = xbc[:, SSD_P * h:SSD_P * (h + 1)]
            cum_b = jnp.broadcast_to(cum[:, lane:lane + 1], (n, n))
            seg = jnp.where(causal, cum_b - cum_t[lane:lane + 1, :], NEG_BIG)
            m_h = scores * jnp.exp(seg) * dt_t[lane:lane + 1, :]
            intra.append(_dot(m_h.astype(BF16), x_h))
            cum_p = cum_b[:, 0:SSD_P]
            tot_p = cum_p[end:end + 1, :]
            dt_p = jnp.broadcast_to(dt[:, lane:lane + 1], (n, SSD_P))
            x_w.append((x_h.astype(F32) * (jnp.exp(tot_p - cum_p) * dt_p)).astype(BF16))
            e_cum.append(jnp.exp(cum_p))
            e_tot.append(jnp.exp(tot_p))
        state = s_ref[grp]
        inter = _dot(c_g, state.astype(BF16)) * jnp.concatenate(e_cum, axis=1)
        outs.append(inter + jnp.concatenate(intra, axis=1))
        s_ref[grp] = (state * jnp.concatenate(e_tot, axis=1)
                      + _dot_tn(b_g, jnp.concatenate(x_w, axis=1)))
    return jnp.concatenate(outs, axis=1)


def _ssd_kernel(xbc_f_ref, dt_f_ref, xbc_b_ref, dt_b_ref, nega_ref, yf_ref, yb_ref, sf_ref, sb_ref):
    @pl.when(pl.program_id(1) == 0)
    def _():
        sf_ref[...] = jnp.zeros_like(sf_ref)
        sb_ref[...] = jnp.zeros_like(sb_ref)

    nega = nega_ref[...]
    yf_ref[0] = _ssd_direction(xbc_f_ref[0], dt_f_ref[0], nega, sf_ref, DT_LANE, False
                               ).astype(yf_ref.dtype)
    yb_ref[0] = _ssd_direction(xbc_b_ref[0], dt_b_ref[0], nega, sb_ref, DT_LANE + SSD_HEADS, True
                               ).astype(yb_ref.dtype)


def _ssd_scan(xbc, dt, nega):
    bsz, tt, _ = xbc.shape
    n_all = tt // CHUNK
    fwd, bwd = _chunk_maps(CTX_LEN // CHUNK, n_all)
    spec = lambda w, m: pl.BlockSpec((1, CHUNK, w), lambda b, s: (b, m(s), 0))
    return pl.pallas_call(
        _ssd_kernel,
        grid=(bsz, n_all),
        in_specs=[spec(SSD_CONV_DIM, fwd), spec(TAIL_W, fwd),
                  spec(SSD_CONV_DIM, bwd), spec(TAIL_W, bwd),
                  pl.BlockSpec((1, TAIL_W), lambda b, s: (0, 0))],
        out_specs=[spec(SSD_INNER, fwd), spec(SSD_INNER, bwd)],
        out_shape=[jax.ShapeDtypeStruct((bsz, tt, SSD_INNER), BF16)] * 2,
        scratch_shapes=[pltpu.VMEM((SSD_GROUPS, SSD_N, SSD_HPG * SSD_P), F32)] * 2,
        compiler_params=pltpu.CompilerParams(
            dimension_semantics=("parallel", "arbitrary"), vmem_limit_bytes=VMEM_LIMIT),
        name="ssd_scan",
    )(xbc, dt, xbc, dt, nega)


def _merge_ffn_kernel(x_ref, ygf_ref, ygb_ref, ysf_ref, ysb_ref, xs_ref, r_ref, z_ref, m_ref,
                      gnorm_ref, dskip_ref, snorm_ref, npost_ref, nfpre_ref, nfpost_ref,
                      wout_ref, wgate_ref, wup_ref, wdown_ref, o_ref):
    b = pl.program_id(0)
    mrow = m_ref[pl.ds(b, 1), :]
    mod = [mrow[:, D_MODEL * j:D_MODEL * (j + 1)] for j in range(6)]
    y_gla = ygf_ref[0].astype(F32) + ygb_ref[0].astype(F32)
    o_gla = jnp.concatenate(
        [_rms(y_gla[:, GLA_DV * h:GLA_DV * (h + 1)]) for h in range(GLA_HEADS)], axis=1)
    o_gla = o_gla * gnorm_ref[...] * r_ref[0].astype(F32)
    y_ssd = ysf_ref[0].astype(F32) + ysb_ref[0].astype(F32)
    y_ssd = (y_ssd + dskip_ref[...] * xs_ref[0].astype(F32)) * z_ref[0].astype(F32)
    gw = SSD_INNER // SSD_GROUPS
    o_ssd = jnp.concatenate(
        [_rms(y_ssd[:, gw * g:gw * (g + 1)]) for g in range(SSD_GROUPS)], axis=1)
    o_ssd = o_ssd * snorm_ref[...]
    o = jnp.concatenate([o_gla, o_ssd], axis=1).astype(BF16)
    y = _dot(o, wout_ref[...])
    x1 = x_ref[0] + mod[2] * (_rms(y) * npost_ref[...])
    h = ((_rms(x1) * nfpre_ref[...]) * (1.0 + mod[4]) + mod[3]).astype(BF16)
    act = (_silu(_dot(h, wgate_ref[...])) * _dot(h, wup_ref[...])).astype(BF16)
    f = _dot(act, wdown_ref[...])
    o_ref[0] = x1 + mod[5] * (_rms(f) * nfpost_ref[...])


def _merge_ffn(x, ygf, ygb, ysf, ysb, xbc, r, z, m_all, gnorm, dskip, snorm, npost, nfpre,
               nfpost, wout, wgate, wup, wdown):
    bsz, seq, _ = x.shape
    n_tiles = seq // TM_FFN
    const = lambda b, i: (0, 0)

    def lat(width):
        return pl.BlockSpec((1, TM_FFN, width), lambda b, i: (b, i, 0))

    single = lambda shape: pl.BlockSpec(shape, const, pipeline_mode=pl.Buffered(1))
    vec = lambda w: pl.BlockSpec((1, w), const)
    return pl.pallas_call(
        _merge_ffn_kernel,
        grid=(bsz, n_tiles),
        in_specs=[pl.BlockSpec((1, TM_FFN, D_MODEL), lambda b, i: (b, i, 0)),
                  lat(GLA_INNER), lat(GLA_INNER), lat(SSD_INNER), lat(SSD_INNER),
                  lat(SSD_INNER), lat(GLA_INNER), lat(SSD_INNER),
                  pl.BlockSpec(m_all.shape, const),
                  vec(GLA_INNER), vec(SSD_INNER), vec(SSD_INNER),
                  vec(D_MODEL), vec(D_MODEL), vec(D_MODEL),
                  single(wout.shape), single(wgate.shape), single(wup.shape), single(wdown.shape)],
        out_specs=pl.BlockSpec((1, TM_FFN, D_MODEL), lambda b, i: (b, i, 0)),
        out_shape=jax.ShapeDtypeStruct((bsz, seq, D_MODEL), F32),
        compiler_params=pltpu.CompilerParams(
            dimension_semantics=("parallel", "arbitrary"), vmem_limit_bytes=VMEM_LIMIT),
        name="merge_ffn",
    )(x, ygf, ygb, ysf, ysb, xbc, r, z, m_all, gnorm, dskip, snorm, npost, nfpre, nfpost,
      wout, wgate, wup, wdown)


def kernel(x, c, ctx, c_ctx, w_mod, b_mod, norm_mix_pre, norm_mix_post, norm_ffn_pre, norm_ffn_post,
           w_in, conv_w, conv_b, gla_wg_f, gla_bg_f, gla_wg_b, gla_bg_b, gla_norm,
           a_log_f, a_log_b, dt_bias_f, dt_bias_b, d_skip, ssd_norm, w_out, w_gate, w_up, w_down):
    assert w_mod.shape[0] == 1, "single-layer kernel"
    bsz = x.shape[0]
    row = lambda v: v.reshape(1, -1)

    c_all = jnp.concatenate([c, c_ctx[None], jnp.zeros((8 - bsz - 1, D_MODEL), F32)], axis=0)
    wi = w_in[0]
    lr0 = COL_Z
    z0 = lr0 + 2 * GLA_RANK
    dt0 = z0 + SSD_INNER + SSD_CONV_DIM
    w_perm = jnp.concatenate(
        [wi[:, :lr0], wi[:, z0:dt0], wi[:, lr0:z0], wi[:, dt0:],
         jnp.zeros((D_MODEL, TAIL_W - 2 * GLA_RANK - 2 * SSD_HEADS), F32)], axis=1).astype(BF16)
    wg_pad = jnp.zeros((TAIL_W, 2 * GLA_QK), F32)
    wg_pad = wg_pad.at[0:GLA_RANK, 0:GLA_QK].set(gla_wg_f[0])
    wg_pad = wg_pad.at[GLA_RANK:2 * GLA_RANK, GLA_QK:].set(gla_wg_b[0]).astype(BF16)
    bg_cat = jnp.concatenate([gla_bg_f[0], gla_bg_b[0]]).reshape(1, -1)
    lane_pad = lambda f, bk: jnp.zeros((1, TAIL_W), F32).at[0, DT_LANE:DT_LANE + 2 * SSD_HEADS].set(
        jnp.concatenate([f, bk]))
    dtb_pad = lane_pad(dt_bias_f[0], dt_bias_b[0])
    nega = lane_pad(-jnp.exp(a_log_f[0]), -jnp.exp(a_log_b[0]))
    conv_w9 = conv_w[0].reshape(CONV_K * CONV_K, SSD_CONV_DIM)

    m_all = _modulation(c_all, w_mod[0], row(b_mod[0]))
    qkv, r_act, z_act, xbc_pre, g, dt = _input_projection(
        x, ctx, m_all, row(norm_mix_pre[0]), w_perm, wg_pad, bg_cat, dtb_pad)
    xbc = _grid_conv(xbc_pre, conv_w9, row(conv_b[0]))
    ygf, ygb = _gla_scan(qkv, g)
    ysf, ysb = _ssd_scan(xbc, dt, nega)
    return _merge_ffn(
        x, ygf, ygb, ysf, ysb, xbc, r_act, z_act, m_all,
        row(jnp.tile(gla_norm[0], GLA_HEADS)), row(jnp.repeat(d_skip[0], SSD_P)), row(ssd_norm[0]),
        row(norm_mix_post[0]), row(norm_ffn_pre[0]), row(norm_ffn_post[0]),
        w_out[0].astype(BF16), w_gate[0].astype(BF16), w_up[0].astype(BF16), w_down[0].astype(BF16))
```

```python
import functools

import jax
import jax.numpy as jnp
from jax import lax
from jax.experimental import pallas as pl
from jax.experimental.pallas import tpu as pltpu

F32 = jnp.float32
BF16 = jnp.bfloat16

D_MODEL = 1024
CTX_LEN = 256
GRID_W = 64
EPS = 1e-6

GLA_HEADS = 4
GLA_DK = 64
GLA_DV = 128
GLA_QK = GLA_HEADS * GLA_DK
GLA_INNER = GLA_HEADS * GLA_DV
GLA_RANK = 16
GLA_GATE_NORM = 16.0

SSD_HEADS = 8
SSD_P = 64
SSD_N = 64
SSD_GROUPS = 2
SSD_HPG = SSD_HEADS // SSD_GROUPS
SSD_INNER = SSD_HEADS * SSD_P
SSD_BC = SSD_GROUPS * SSD_N
SSD_CONV_DIM = SSD_INNER + 2 * SSD_BC
CONV_K = 3
D_FF = 2816

QKV_W = 2 * GLA_QK + GLA_INNER
COL_R = QKV_W
COL_Z = COL_R + GLA_INNER
COL_XBC = COL_Z + SSD_INNER
COL_TAIL = COL_XBC + SSD_CONV_DIM
TAIL_W = 128
PROJ_W = COL_TAIL + TAIL_W
DT_LANE = 2 * GLA_RANK

TM_PROJ = 256
TM_CONV = 256
TM_FFN = 512
CHUNK = 128
SCAN_BATCH = 2
SUB = 32
NEG_BIG = -1e30
VMEM_LIMIT = 56 * 1024 * 1024


def _silu(x):
    return x / (1.0 + jnp.exp(-x))


def _softplus(x):
    return jnp.maximum(x, 0.0) + jnp.log(1.0 + jnp.exp(-jnp.abs(x)))


def _rms(x):
    return x * lax.rsqrt(jnp.mean(x * x, axis=-1, keepdims=True) + EPS)


def _dot(a, b):
    return jnp.dot(a, b, preferred_element_type=F32)


def _dot_nt(a, b):
    return lax.dot_general(a, b, (((1,), (1,)), ((), ())), preferred_element_type=F32)


def _dot_tn(a, b):
    return lax.dot_general(a, b, (((0,), (0,)), ((), ())), preferred_element_type=F32)


def _mod_kernel(c_ref, w_ref, b_ref, o_ref):
    s = _silu(c_ref[...])
    o_ref[...] = _dot(s.astype(BF16), w_ref[...].astype(BF16)) + b_ref[...]


def _modulation(c_all, w_mod, b_mod):
    n = w_mod.shape[1]
    tn = 1024
    return pl.pallas_call(
        _mod_kernel,
        grid=(n // tn,),
        in_specs=[pl.BlockSpec((8, D_MODEL), lambda j: (0, 0)),
                  pl.BlockSpec((D_MODEL, tn), lambda j: (0, j)),
                  pl.BlockSpec((1, tn), lambda j: (0, j))],
        out_specs=pl.BlockSpec((8, tn), lambda j: (0, j)),
        out_shape=jax.ShapeDtypeStruct((8, n), F32),
        name="adaln_mod",
    )(c_all, w_mod, b_mod)


def _inproj_kernel(x_ref, ctx_ref, m_ref, gpre_ref, w_ref, wg_ref, bg_ref, dtb_ref,
                   qkv_ref, r_ref, z_ref, xbc_ref, g_ref, dt_ref):
    b = pl.program_id(0)
    i = pl.program_id(1)
    is_ctx = i == pl.num_programs(1) - 1
    x = jnp.where(is_ctx, ctx_ref[0], x_ref[0])
    mrow = m_ref[pl.ds(jnp.where(is_ctx, 4, b), 1), :]
    shift = mrow[:, 0:D_MODEL]
    scale = mrow[:, D_MODEL:2 * D_MODEL]
    h = (_rms(x) * gpre_ref[...]) * (1.0 + scale) + shift
    acc = _dot(h.astype(BF16), w_ref[...])
    qkv_ref[0, :, 0:GLA_QK] = (acc[:, 0:GLA_QK] * (GLA_DK ** -0.5)).astype(BF16)
    qkv_ref[0, :, GLA_QK:QKV_W] = acc[:, GLA_QK:QKV_W].astype(BF16)
    r_ref[0] = _silu(acc[:, COL_R:COL_Z]).astype(BF16)
    z_ref[0] = _silu(acc[:, COL_Z:COL_XBC]).astype(BF16)
    xbc_ref[0] = acc[:, COL_XBC:COL_TAIL].astype(BF16)
    tail = acc[:, COL_TAIL:PROJ_W]
    logits = _dot(tail.astype(BF16), wg_ref[...]) + bg_ref[...]
    g_ref[0] = -_softplus(-logits) * (1.0 / GLA_GATE_NORM)
    dt_ref[0] = _softplus(tail + dtb_ref[...])


def _input_projection(x, ctx, m_all, gpre, w_perm, wg_pad, bg_cat, dtb_pad):
    bsz, seq, _ = x.shape
    n_lat = seq // TM_PROJ
    tt = CTX_LEN + seq
    const = lambda b, i: (0, 0)
    tok = lambda b, i: (b, i, 0)
    return pl.pallas_call(
        _inproj_kernel,
        grid=(bsz, 1 + n_lat),
        in_specs=[pl.BlockSpec((1, TM_PROJ, D_MODEL), lambda b, i: (b, jnp.minimum(i, n_lat - 1), 0)),
                  pl.BlockSpec((1, TM_PROJ, D_MODEL), lambda b, i: (b, 0, 0)),
                  pl.BlockSpec(m_all.shape, const),
                  pl.BlockSpec((1, D_MODEL), const),
                  pl.BlockSpec((D_MODEL, PROJ_W), const),
                  pl.BlockSpec((TAIL_W, 2 * GLA_QK), const),
                  pl.BlockSpec((1, 2 * GLA_QK), const),
                  pl.BlockSpec((1, TAIL_W), const)],
        out_specs=[pl.BlockSpec((1, TM_PROJ, QKV_W), tok),
                   pl.BlockSpec((1, TM_PROJ, GLA_INNER), tok),
                   pl.BlockSpec((1, TM_PROJ, SSD_INNER), tok),
                   pl.BlockSpec((1, TM_PROJ, SSD_CONV_DIM), tok),
                   pl.BlockSpec((1, TM_PROJ, 2 * GLA_QK), tok),
                   pl.BlockSpec((1, TM_PROJ, TAIL_W), tok)],
        out_shape=[jax.ShapeDtypeStruct((bsz, tt, QKV_W), BF16),
                   jax.ShapeDtypeStruct((bsz, tt, GLA_INNER), BF16),
                   jax.ShapeDtypeStruct((bsz, tt, SSD_INNER), BF16),
                   jax.ShapeDtypeStruct((bsz, tt, SSD_CONV_DIM), BF16),
                   jax.ShapeDtypeStruct((bsz, tt, 2 * GLA_QK), F32),
                   jax.ShapeDtypeStruct((bsz, tt, TAIL_W), F32)],
        compiler_params=pltpu.CompilerParams(
            dimension_semantics=("parallel", "arbitrary"), vmem_limit_bytes=VMEM_LIMIT),
        name="in_proj",
    )(x, ctx, m_all, gpre, w_perm, wg_pad, bg_cat, dtb_pad)


def _conv_kernel(prev_ref, main_ref, next_ref, w_ref, b_ref, o_ref):
    i = pl.program_id(1)
    last = pl.num_programs(1) - 1
    is_ctx = i == last
    main = main_ref[0].astype(F32)
    prev = jnp.where(jnp.logical_and(i >= 1, i < last), prev_ref[0].astype(F32), 0.0)
    nxt = jnp.where(i < last - 1, next_ref[0].astype(F32), 0.0)
    ext = jnp.concatenate([prev, main, nxt], axis=0)
    n_ext = ext.shape[0]
    e = lax.broadcasted_iota(jnp.int32, (n_ext, 1), 0)
    col = jnp.bitwise_and(e, GRID_W - 1)
    has_left = jnp.where(is_ctx, e - GRID_W, col) != 0
    has_right = jnp.where(is_ctx, e - (GRID_W + TM_CONV - 1), col - (GRID_W - 1)) != 0
    src = (jnp.where(has_left, pltpu.roll(ext, 1, axis=0), 0.0),
           ext,
           jnp.where(has_right, pltpu.roll(ext, n_ext - 1, axis=0), 0.0))
    row_on = jnp.where(is_ctx, 0.0, 1.0)
    acc = jnp.zeros_like(main) + b_ref[...]
    for dy in range(CONV_K):
        for dx in range(CONV_K):
            wk = w_ref[CONV_K * dy + dx:CONV_K * dy + dx + 1, :]
            if dy != 1:
                wk = wk * row_on
            acc = acc + wk * src[dx][GRID_W * dy:GRID_W * dy + TM_CONV, :]
    o_ref[0] = _silu(acc).astype(o_ref.dtype)


def _grid_conv(xbc, conv_w9, conv_b):
    bsz, tt, ch = xbc.shape
    n_tiles = tt // TM_CONV
    rows_per_tile = TM_CONV // GRID_W
    last_row = tt // GRID_W - 1
    const = lambda b, i: (0, 0)
    return pl.pallas_call(
        _conv_kernel,
        grid=(bsz, n_tiles),
        in_specs=[pl.BlockSpec((1, GRID_W, ch),
                               lambda b, i: (b, jnp.clip(rows_per_tile * i - 1, 0, last_row), 0)),
                  pl.BlockSpec((1, TM_CONV, ch), lambda b, i: (b, i, 0)),
                  pl.BlockSpec((1, GRID_W, ch),
                               lambda b, i: (b, jnp.clip(rows_per_tile * (i + 1), 0, last_row), 0)),
                  pl.BlockSpec((CONV_K * CONV_K, ch), const),
                  pl.BlockSpec((1, ch), const)],
        out_specs=pl.BlockSpec((1, TM_CONV, ch), lambda b, i: (b, i, 0)),
        out_shape=jax.ShapeDtypeStruct((bsz, tt, ch), BF16),
        compiler_params=pltpu.CompilerParams(
            dimension_semantics=("parallel", "arbitrary"), vmem_limit_bytes=VMEM_LIMIT),
        name="grid_conv",
    )(xbc, xbc, xbc, conv_w9, conv_b)


def _cumsum_rows(x, reverse):
    n = x.shape[0]
    row = lax.broadcasted_iota(jnp.int32, x.shape, 0)
    s = 1
    while s < n:
        if reverse:
            x = x + jnp.where(row < n - s, pltpu.roll(x, n - s, axis=0), 0.0)
        else:
            x = x + jnp.where(row >= s, pltpu.roll(x, s, axis=0), 0.0)
        s *= 2
    return x


def _chunk_maps(n_ctx, n_all):
    n_lat = n_all - n_ctx
    fwd = lambda s: jnp.where(s < n_ctx, n_lat + s, s - n_ctx)
    bwd = lambda s: n_all - 1 - s
    return fwd, bwd


def _gla_prepare(qkv, g, reverse):
    n = qkv.shape[0]
    nb = n // SUB
    blocks = range(nb)
    rows = lambda a, blk: a[SUB * blk:SUB * (blk + 1), :]
    q = qkv[:, 0:GLA_QK].astype(F32)
    k = qkv[:, GLA_QK:2 * GLA_QK].astype(F32)
    cum = _cumsum_rows(g, reverse)
    zero = jnp.zeros((1, GLA_QK), F32)
    if reverse:
        ends = [cum[SUB * blk:SUB * blk + 1, :] for blk in blocks]
        refs = ends[1:] + [zero]
        earlier = lambda j, blk: j > blk
        tot = ends[0]
    else:
        ends = [cum[SUB * (blk + 1) - 1:SUB * (blk + 1), :] for blk in blocks]
        refs = [zero] + ends[:-1]
        earlier = lambda j, blk: j < blk
        tot = ends[nb - 1]
    stack = lambda vs: jnp.concatenate([jnp.broadcast_to(v, (SUB, GLA_QK)) for v in vs], axis=0)
    e_in = cum - stack(refs)
    q_blk = q * jnp.exp(e_in)
    k_diag = k * jnp.exp(-e_in)
    k_end = k * jnp.exp(stack(ends) - cum)
    q_in = jnp.concatenate([rows(q_blk, blk) * jnp.exp(refs[blk]) for blk in blocks], axis=0)
    k_out = jnp.concatenate([rows(k_end, blk) * jnp.exp(tot - ends[blk]) for blk in blocks], axis=0)
    q_in = q_in.astype(BF16)
    k_out = k_out.astype(BF16)
    q_bf = q_blk.astype(BF16)
    zeros_piece = jnp.zeros((SUB, GLA_QK), BF16)
    k_piece = [[rows(k_diag, j).astype(BF16) if seg == j
                else (rows(k_end, j) * jnp.exp(refs[seg] - ends[j])).astype(BF16) if earlier(j, seg)
                else zeros_piece
                for seg in blocks] for j in blocks]
    ii = lax.broadcasted_iota(jnp.int32, (n, n), 0)
    jj = lax.broadcasted_iota(jnp.int32, (n, n), 1)
    heads = []
    for h in range(GLA_HEADS):
        ks = slice(GLA_DK * h, GLA_DK * (h + 1))
        q_cat = jnp.concatenate(
            [jnp.concatenate([rows(q_bf, blk)[:, ks] if seg == blk else zeros_piece[:, ks]
                              for seg in blocks], axis=1) for blk in blocks], axis=0)
        k_cat = jnp.concatenate(
            [jnp.concatenate([k_piece[j][seg][:, ks] for seg in blocks], axis=1)
             for j in blocks], axis=0)
        heads.append(dict(q_cat=q_cat, k_cat=k_cat, q_in=q_in[:, ks], k_out=k_out[:, ks],
                          v=qkv[:, 2 * GLA_QK + GLA_DV * h:2 * GLA_QK + GLA_DV * (h + 1)]))
    return dict(heads=heads, causal=(ii <= jj) if reverse else (ii >= jj),
                decay_col=jnp.transpose(jnp.broadcast_to(tot, (GLA_DV, GLA_QK))))


def _gla_kernel(qkv_f_ref, g_f_ref, qkv_b_ref, g_b_ref, yf_ref, yb_ref, sf_ref, sb_ref):
    @pl.when(pl.program_id(1) == 0)
    def _():
        sf_ref[...] = jnp.zeros_like(sf_ref)
        sb_ref[...] = jnp.zeros_like(sb_ref)

    chains = []
    for e in range(qkv_f_ref.shape[0]):
        chains.append((qkv_f_ref.at[e], g_f_ref.at[e], False, sf_ref.at[e], yf_ref.at[e]))
        chains.append((qkv_b_ref.at[e], g_b_ref.at[e], True, sb_ref.at[e], yb_ref.at[e]))

    def start(chain):
        qkv_ref, g_ref, reverse, _, _ = chain
        p = _gla_prepare(qkv_ref[...], g_ref[...], reverse)
        return p, [_dot_nt(hd["q_cat"], hd["k_cat"]) for hd in p["heads"]]

    def finish(chain, p, att):
        _, _, _, s_ref, y_ref = chain
        outs = []
        for h, (hd, a) in enumerate(zip(p["heads"], att)):
            state = s_ref[h]
            lhs = jnp.concatenate([jnp.where(p["causal"], a, 0.0).astype(BF16), hd["q_in"]], axis=1)
            rhs = jnp.concatenate([hd["v"], state.astype(BF16)], axis=0)
            outs.append(_dot(lhs, rhs))
            ks = slice(GLA_DK * h, GLA_DK * (h + 1))
            s_ref[h] = state * jnp.exp(p["decay_col"][ks, :]) + _dot_tn(hd["k_out"], hd["v"])
        y_ref[...] = jnp.concatenate(outs, axis=1).astype(y_ref.dtype)

    started = start(chains[0])
    for i, chain in enumerate(chains):
        following = start(chains[i + 1]) if i + 1 < len(chains) else None
        finish(chain, *started)
        started = following


def _gla_scan(qkv, g):
    bsz, tt, _ = qkv.shape
    n_all = tt // CHUNK
    fwd, bwd = _chunk_maps(CTX_LEN // CHUNK, n_all)
    spec = lambda w, m, c: pl.BlockSpec((SCAN_BATCH, CHUNK, w), lambda b, s: (b, m(s), c))
    return pl.pallas_call(
        _gla_kernel,
        grid=(bsz // SCAN_BATCH, n_all),
        in_specs=[spec(QKV_W, fwd, 0), spec(GLA_QK, fwd, 0), spec(QKV_W, bwd, 0), spec(GLA_QK, bwd, 1)],
        out_specs=[spec(GLA_INNER, fwd, 0), spec(GLA_INNER, bwd, 0)],
        out_shape=[jax.ShapeDtypeStruct((bsz, tt, GLA_INNER), BF16)] * 2,
        scratch_shapes=[pltpu.VMEM((SCAN_BATCH, GLA_HEADS, GLA_DK, GLA_DV), F32)] * 2,
        compiler_params=pltpu.CompilerParams(
            dimension_semantics=("parallel", "arbitrary"), vmem_limit_bytes=VMEM_LIMIT),
        name="gla_scan",
    )(qkv, g, qkv, g)


def _ssd_prepare(xbc, dt, nega, expand, lane0, reverse):
    n = xbc.shape[0]
    end = 0 if reverse else n - 1
    cum = _cumsum_rows(dt * nega, reverse)
    e_cum = jnp.exp(cum)
    w = jnp.exp(cum[end:end + 1, :] - cum) * dt
    e_cum_x = _dot(_split_hi_lo(e_cum), expand)
    x_w = (xbc[:, 0:SSD_INNER].astype(F32) * _dot(_split_hi_lo(w), expand)).astype(BF16)
    cum_t = jnp.transpose(cum)
    dt_t = jnp.transpose(dt)
    ii = lax.broadcasted_iota(jnp.int32, (n, n), 0)
    jj = lax.broadcasted_iota(jnp.int32, (n, n), 1)
    causal = (ii <= jj) if reverse else (ii >= jj)
    first_head = lax.broadcasted_iota(jnp.int32, (n, 2 * SSD_P), 1) < SSD_P
    groups = []
    for grp in range(SSD_GROUPS):
        b_g = xbc[:, SSD_INNER + SSD_N * grp:SSD_INNER + SSD_N * (grp + 1)]
        c_g = xbc[:, SSD_INNER + SSD_BC + SSD_N * grp:SSD_INNER + SSD_BC + SSD_N * (grp + 1)]
        scores = _dot_nt(c_g, b_g)
        pairs = []
        for pair in range(SSD_HPG // 2):
            m_pair = []
            for hh in range(2):
                lane = lane0 + SSD_HPG * grp + 2 * pair + hh
                cum_b = jnp.broadcast_to(cum[:, lane:lane + 1], (n, n))
                seg = jnp.where(causal, cum_b - cum_t[lane:lane + 1, :], NEG_BIG)
                m_pair.append((scores * jnp.exp(seg) * dt_t[lane:lane + 1, :]).astype(BF16))
            col = 2 * SSD_P * (SSD_HPG // 2 * grp + pair)
            x_pair = xbc[:, col:col + 2 * SSD_P]
            none = jnp.zeros_like(x_pair)
            rhs = jnp.concatenate([jnp.where(first_head, x_pair, none),
                                   jnp.where(first_head, none, x_pair)], axis=0)
            pairs.append((jnp.concatenate(m_pair, axis=1), rhs))
        cols = slice(SSD_HPG * SSD_P * grp, SSD_HPG * SSD_P * (grp + 1))
        groups.append(dict(b=b_g, c=c_g, pairs=pairs, e_cum=e_cum_x[:, cols],
                           e_tot=e_cum_x[end:end + 1, cols], x_w=x_w[:, cols]))
    return groups


def _split_hi_lo(a):
    hi = a.astype(BF16)
    lo = (a - hi.astype(F32)).astype(BF16)
    return jnp.concatenate([hi, lo], axis=1)


def _ssd_kernel(xbc_f_ref, dt_f_ref, xbc_b_ref, dt_b_ref, nega_ref, expand_ref,
                yf_ref, yb_ref, sf_ref, sb_ref):
    @pl.when(pl.program_id(1) == 0)
    def _():
        sf_ref[...] = jnp.zeros_like(sf_ref)
        sb_ref[...] = jnp.zeros_like(sb_ref)

    nega = nega_ref[...]
    chains = []
    for e in range(xbc_f_ref.shape[0]):
        chains.append((xbc_f_ref.at[e], dt_f_ref.at[e], 0, False, sf_ref.at[e], yf_ref.at[e]))
        chains.append((xbc_b_ref.at[e], dt_b_ref.at[e], 1, True, sb_ref.at[e], yb_ref.at[e]))

    def start(chain):
        xbc_ref, dt_ref, d, reverse, _, _ = chain
        return _ssd_prepare(xbc_ref[...], dt_ref[...], nega, expand_ref[d],
                            DT_LANE + SSD_HEADS * d, reverse)

    def finish(chain, groups):
        s_ref, y_ref = chain[4], chain[5]
        outs = []
        for grp, gd in enumerate(groups):
            state = s_ref[grp]
            intra = jnp.concatenate([_dot(m, rhs) for m, rhs in gd["pairs"]], axis=1)
            outs.append(_dot(gd["c"], state.astype(BF16)) * gd["e_cum"] + intra)
            s_ref[grp] = state * gd["e_tot"] + _dot_tn(gd["b"], gd["x_w"])
        y_ref[...] = jnp.concatenate(outs, axis=1).astype(y_ref.dtype)

    started = start(chains[0])
    for i, chain in enumerate(chains):
        following = start(chains[i + 1]) if i + 1 < len(chains) else None
        finish(chain, started)
        started = following


def _head_expand_matrix():
    r = jnp.arange(2 * TAIL_W)[:, None] % TAIL_W
    c = jnp.arange(SSD_INNER)[None, :] // SSD_P
    return jnp.stack([(r == DT_LANE + SSD_HEADS * d + c) for d in range(2)]).astype(BF16)


def _ssd_scan(xbc, dt, nega):
    bsz, tt, _ = xbc.shape
    n_all = tt // CHUNK
    fwd, bwd = _chunk_maps(CTX_LEN // CHUNK, n_all)
    spec = lambda w, m: pl.BlockSpec((SCAN_BATCH, CHUNK, w), lambda b, s: (b, m(s), 0))
    expand = _head_expand_matrix()
    return pl.pallas_call(
        _ssd_kernel,
        grid=(bsz // SCAN_BATCH, n_all),
        in_specs=[spec(SSD_CONV_DIM, fwd), spec(TAIL_W, fwd),
                  spec(SSD_CONV_DIM, bwd), spec(TAIL_W, bwd),
                  pl.BlockSpec((1, TAIL_W), lambda b, s: (0, 0)),
                  pl.BlockSpec(expand.shape, lambda b, s: (0, 0, 0))],
        out_specs=[spec(SSD_INNER, fwd), spec(SSD_INNER, bwd)],
        out_shape=[jax.ShapeDtypeStruct((bsz, tt, SSD_INNER), BF16)] * 2,
        scratch_shapes=[pltpu.VMEM((SCAN_BATCH, SSD_GROUPS, SSD_N, SSD_HPG * SSD_P), F32)] * 2,
        compiler_params=pltpu.CompilerParams(
            dimension_semantics=("parallel", "arbitrary"), vmem_limit_bytes=VMEM_LIMIT),
        name="ssd_scan",
    )(xbc, dt, xbc, dt, nega, expand)


def _merge_ffn_kernel(x_ref, ygf_ref, ygb_ref, ysf_ref, ysb_ref, xs_ref, r_ref, z_ref, m_ref,
                      gnorm_ref, dskip_ref, snorm_ref, npost_ref, nfpre_ref, nfpost_ref,
                      wout_ref, wgate_ref, wup_ref, wdown_ref, o_ref):
    b = pl.program_id(0)
    mrow = m_ref[pl.ds(b, 1), :]
    mod = [mrow[:, D_MODEL * j:D_MODEL * (j + 1)] for j in range(6)]
    y_gla = ygf_ref[0].astype(F32) + ygb_ref[0].astype(F32)
    o_gla = jnp.concatenate(
        [_rms(y_gla[:, GLA_DV * h:GLA_DV * (h + 1)]) for h in range(GLA_HEADS)], axis=1)
    o_gla = o_gla * gnorm_ref[...] * r_ref[0].astype(F32)
    y_ssd = ysf_ref[0].astype(F32) + ysb_ref[0].astype(F32)
    y_ssd = (y_ssd + dskip_ref[...] * xs_ref[0].astype(F32)) * z_ref[0].astype(F32)
    gw = SSD_INNER // SSD_GROUPS
    o_ssd = jnp.concatenate(
        [_rms(y_ssd[:, gw * g:gw * (g + 1)]) for g in range(SSD_GROUPS)], axis=1)
    o_ssd = o_ssd * snorm_ref[...]
    o = jnp.concatenate([o_gla, o_ssd], axis=1).astype(BF16)
    y = _dot(o, wout_ref[...])
    x1 = x_ref[0] + mod[2] * (_rms(y) * npost_ref[...])
    h = ((_rms(x1) * nfpre_ref[...]) * (1.0 + mod[4]) + mod[3]).astype(BF16)
    act = (_silu(_dot(h, wgate_ref[...])) * _dot(h, wup_ref[...])).astype(BF16)
    f = _dot(act, wdown_ref[...])
    o_ref[0] = x1 + mod[5] * (_rms(f) * nfpost_ref[...])


def _merge_ffn(x, ygf, ygb, ysf, ysb, xbc, r, z, m_all, gnorm, dskip, snorm, npost, nfpre,
               nfpost, wout, wgate, wup, wdown):
    bsz, seq, _ = x.shape
    n_tiles = seq // TM_FFN
    const = lambda b, i: (0, 0)

    def lat(width):
        return pl.BlockSpec((1, TM_FFN, width), lambda b, i: (b, i, 0))

    single = lambda shape: pl.BlockSpec(shape, const, pipeline_mode=pl.Buffered(1))
    vec = lambda w: pl.BlockSpec((1, w), const)
    return pl.pallas_call(
        _merge_ffn_kernel,
        grid=(bsz, n_tiles),
        in_specs=[pl.BlockSpec((1, TM_FFN, D_MODEL), lambda b, i: (b, i, 0)),
                  lat(GLA_INNER), lat(GLA_INNER), lat(SSD_INNER), lat(SSD_INNER),
                  lat(SSD_INNER), lat(GLA_INNER), lat(SSD_INNER),
                  pl.BlockSpec(m_all.shape, const),
                  vec(GLA_INNER), vec(SSD_INNER), vec(SSD_INNER),
                  vec(D_MODEL), vec(D_MODEL), vec(D_MODEL),
                  single(wout.shape), single(wgate.shape), single(wup.shape), single(wdown.shape)],
        out_specs=pl.BlockSpec((1, TM_FFN, D_MODEL), lambda b, i: (b, i, 0)),
        out_shape=jax.ShapeDtypeStruct((bsz, seq, D_MODEL), F32),
        compiler_params=pltpu.CompilerParams(
            dimension_semantics=("parallel", "arbitrary"), vmem_limit_bytes=VMEM_LIMIT),
        name="merge_ffn",
    )(x, ygf, ygb, ysf, ysb, xbc, r, z, m_all, gnorm, dskip, snorm, npost, nfpre, nfpost,
      wout, wgate, wup, wdown)


def kernel(x, c, ctx, c_ctx, w_mod, b_mod, norm_mix_pre, norm_mix_post, norm_ffn_pre, norm_ffn_post,
           w_in, conv_w, conv_b, gla_wg_f, gla_bg_f, gla_wg_b, gla_bg_b, gla_norm,
           a_log_f, a_log_b, dt_bias_f, dt_bias_b, d_skip, ssd_norm, w_out, w_gate, w_up, w_down):
    assert w_mod.shape[0] == 1, "single-layer kernel"
    bsz = x.shape[0]
    row = lambda v: v.reshape(1, -1)

    c_all = jnp.concatenate([c, c_ctx[None], jnp.zeros((8 - bsz - 1, D_MODEL), F32)], axis=0)
    wi = w_in[0]
    lr0 = COL_Z
    z0 = lr0 + 2 * GLA_RANK
    dt0 = z0 + SSD_INNER + SSD_CONV_DIM
    w_perm = jnp.concatenate(
        [wi[:, :lr0], wi[:, z0:dt0], wi[:, lr0:z0], wi[:, dt0:],
         jnp.zeros((D_MODEL, TAIL_W - 2 * GLA_RANK - 2 * SSD_HEADS), F32)], axis=1).astype(BF16)
    wg_pad = jnp.zeros((TAIL_W, 2 * GLA_QK), F32)
    wg_pad = wg_pad.at[0:GLA_RANK, 0:GLA_QK].set(gla_wg_f[0])
    wg_pad = wg_pad.at[GLA_RANK:2 * GLA_RANK, GLA_QK:].set(gla_wg_b[0]).astype(BF16)
    bg_cat = jnp.concatenate([gla_bg_f[0], gla_bg_b[0]]).reshape(1, -1)
    lane_pad = lambda f, bk: jnp.zeros((1, TAIL_W), F32).at[0, DT_LANE:DT_LANE + 2 * SSD_HEADS].set(
        jnp.concatenate([f, bk]))
    dtb_pad = lane_pad(dt_bias_f[0], dt_bias_b[0])
    nega = lane_pad(-jnp.exp(a_log_f[0]), -jnp.exp(a_log_b[0]))
    conv_w9 = conv_w[0].reshape(CONV_K * CONV_K, SSD_CONV_DIM)

    m_all = _modulation(c_all, w_mod[0], row(b_mod[0]))
    qkv, r_act, z_act, xbc_pre, g, dt = _input_projection(
        x, ctx, m_all, row(norm_mix_pre[0]), w_perm, wg_pad, bg_cat, dtb_pad)
    xbc = _grid_conv(xbc_pre, conv_w9, row(conv_b[0]))
    ygf, ygb = _gla_scan(qkv, g)
    ysf, ysb = _ssd_scan(xbc, dt, nega)
    return _merge_ffn(
        x, ygf, ygb, ysf, ysb, xbc, r_act, z_act, m_all,
        row(jnp.tile(gla_norm[0], GLA_HEADS)), row(jnp.repeat(d_skip[0], SSD_P)), row(ssd_norm[0]),
        row(norm_mix_post[0]), row(norm_ffn_pre[0]), row(norm_ffn_post[0]),
        w_out[0].astype(BF16), w_gate[0].astype(BF16), w_up[0].astype(BF16), w_down[0].astype(BF16))
```

```python
import functools

import jax
import jax.numpy as jnp
from jax import lax
from jax.experimental import pallas as pl
from jax.experimental.pallas import tpu as pltpu

F32 = jnp.float32
BF16 = jnp.bfloat16

D_MODEL = 1024
CTX_LEN = 256
GRID_W = 64
EPS = 1e-6

GLA_HEADS = 4
GLA_DK = 64
GLA_DV = 128
GLA_QK = GLA_HEADS * GLA_DK
GLA_INNER = GLA_HEADS * GLA_DV
GLA_RANK = 16
GLA_GATE_NORM = 16.0

SSD_HEADS = 8
SSD_P = 64
SSD_N = 64
SSD_GROUPS = 2
SSD_HPG = SSD_HEADS // SSD_GROUPS
SSD_INNER = SSD_HEADS * SSD_P
SSD_BC = SSD_GROUPS * SSD_N
SSD_CONV_DIM = SSD_INNER + 2 * SSD_BC
CONV_K = 3
D_FF = 2816

QKV_W = 2 * GLA_QK + GLA_INNER
COL_R = QKV_W
COL_Z = COL_R + GLA_INNER
COL_XBC = COL_Z + SSD_INNER
COL_TAIL = COL_XBC + SSD_CONV_DIM
TAIL_W = 128
PROJ_W = COL_TAIL + TAIL_W
DT_LANE = 2 * GLA_RANK

TM_PROJ = 512
PROJ_SUB = CTX_LEN
TM_CONV = 256
TM_FFN = 512
FFN_SUB = 128
CHUNK = 128
SCAN_BATCH = 2
SUB = 32
NEG_BIG = -1e30
VMEM_LIMIT = 56 * 1024 * 1024


def _silu(x):
    return x / (1.0 + jnp.exp(-x))


def _softplus(x):
    return jnp.maximum(x, 0.0) + jnp.log(1.0 + jnp.exp(-jnp.abs(x)))


def _rms(x):
    return x * lax.rsqrt(jnp.mean(x * x, axis=-1, keepdims=True) + EPS)


def _dot(a, b):
    return jnp.dot(a, b, preferred_element_type=F32)


def _dot_nt(a, b):
    return lax.dot_general(a, b, (((1,), (1,)), ((), ())), preferred_element_type=F32)


def _dot_tn(a, b):
    return lax.dot_general(a, b, (((0,), (0,)), ((), ())), preferred_element_type=F32)


def _mod_kernel(c_ref, w_ref, b_ref, o_ref):
    s = _silu(c_ref[...])
    o_ref[...] = _dot(s.astype(BF16), w_ref[...].astype(BF16)) + b_ref[...]


def _modulation(c_all, w_mod, b_mod):
    n = w_mod.shape[1]
    tn = 1024
    return pl.pallas_call(
        _mod_kernel,
        grid=(n // tn,),
        in_specs=[pl.BlockSpec((8, D_MODEL), lambda j: (0, 0)),
                  pl.BlockSpec((D_MODEL, tn), lambda j: (0, j)),
                  pl.BlockSpec((1, tn), lambda j: (0, j))],
        out_specs=pl.BlockSpec((8, tn), lambda j: (0, j)),
        out_shape=jax.ShapeDtypeStruct((8, n), F32),
        name="adaln_mod",
    )(c_all, w_mod, b_mod)


def _inproj_kernel(x_ref, ctx_ref, m_ref, gpre_ref, w_ref, wg_ref, bg_ref, dtb_ref,
                   qkv_ref, r_ref, z_ref, xbc_ref, g_ref, dt_ref):
    b = pl.program_id(0)
    i = pl.program_id(1)
    is_ctx = i == pl.num_programs(1) - 1
    mrow = m_ref[pl.ds(jnp.where(is_ctx, 4, b), 1), :]
    shift = mrow[:, 0:D_MODEL]
    scale = mrow[:, D_MODEL:2 * D_MODEL]

    def normed(rs):
        x = jnp.where(is_ctx, ctx_ref[0], x_ref[0, rs, :])
        return ((_rms(x) * gpre_ref[...]) * (1.0 + scale) + shift).astype(BF16)

    def project(rs, h):
        acc = _dot(h, w_ref[...])
        qkv_ref[0, rs, 0:GLA_QK] = (acc[:, 0:GLA_QK] * (GLA_DK ** -0.5)).astype(BF16)
        qkv_ref[0, rs, GLA_QK:QKV_W] = acc[:, GLA_QK:QKV_W].astype(BF16)
        r_ref[0, rs, :] = _silu(acc[:, COL_R:COL_Z]).astype(BF16)
        z_ref[0, rs, :] = _silu(acc[:, COL_Z:COL_XBC]).astype(BF16)
        xbc_ref[0, rs, :] = acc[:, COL_XBC:COL_TAIL].astype(BF16)
        tail = acc[:, COL_TAIL:PROJ_W]
        logits = _dot(tail.astype(BF16), wg_ref[...]) + bg_ref[...]
        g_ref[0, rs, :] = -_softplus(-logits) * (1.0 / GLA_GATE_NORM)
        dt_ref[0, rs, :] = _softplus(tail + dtb_ref[...])

    subs = [slice(PROJ_SUB * j, PROJ_SUB * (j + 1)) for j in range(TM_PROJ // PROJ_SUB)]
    ready = normed(subs[0])
    for j, rs in enumerate(subs):
        following = normed(subs[j + 1]) if j + 1 < len(subs) else None
        project(rs, ready)
        ready = following


def _input_projection(x, ctx, m_all, gpre, w_perm, wg_pad, bg_cat, dtb_pad):
    bsz, seq, _ = x.shape
    n_lat = seq // TM_PROJ
    tt = CTX_LEN + seq
    const = lambda b, i: (0, 0)
    tok = lambda b, i: (b, i, 0)
    return pl.pallas_call(
        _inproj_kernel,
        grid=(bsz, 1 + n_lat),
        in_specs=[pl.BlockSpec((1, TM_PROJ, D_MODEL), lambda b, i: (b, jnp.minimum(i, n_lat - 1), 0)),
                  pl.BlockSpec((1, CTX_LEN, D_MODEL), lambda b, i: (b, 0, 0)),
                  pl.BlockSpec(m_all.shape, const),
                  pl.BlockSpec((1, D_MODEL), const),
                  pl.BlockSpec((D_MODEL, PROJ_W), const),
                  pl.BlockSpec((TAIL_W, 2 * GLA_QK), const),
                  pl.BlockSpec((1, 2 * GLA_QK), const),
                  pl.BlockSpec((1, TAIL_W), const)],
        out_specs=[pl.BlockSpec((1, TM_PROJ, QKV_W), tok),
                   pl.BlockSpec((1, TM_PROJ, GLA_INNER), tok),
                   pl.BlockSpec((1, TM_PROJ, SSD_INNER), tok),
                   pl.BlockSpec((1, TM_PROJ, SSD_CONV_DIM), tok),
                   pl.BlockSpec((1, TM_PROJ, 2 * GLA_QK), tok),
                   pl.BlockSpec((1, TM_PROJ, TAIL_W), tok)],
        out_shape=[jax.ShapeDtypeStruct((bsz, tt, QKV_W), BF16),
                   jax.ShapeDtypeStruct((bsz, tt, GLA_INNER), BF16),
                   jax.ShapeDtypeStruct((bsz, tt, SSD_INNER), BF16),
                   jax.ShapeDtypeStruct((bsz, tt, SSD_CONV_DIM), BF16),
                   jax.ShapeDtypeStruct((bsz, tt, 2 * GLA_QK), F32),
                   jax.ShapeDtypeStruct((bsz, tt, TAIL_W), F32)],
        compiler_params=pltpu.CompilerParams(
            dimension_semantics=("parallel", "arbitrary"), vmem_limit_bytes=VMEM_LIMIT),
        name="in_proj",
    )(x, ctx, m_all, gpre, w_perm, wg_pad, bg_cat, dtb_pad)


def _conv_kernel(prev_ref, main_ref, next_ref, w_ref, b_ref, o_ref):
    i = pl.program_id(1)
    last = pl.num_programs(1) - 1
    is_ctx = i == last
    main = main_ref[0].astype(F32)
    prev = jnp.where(jnp.logical_and(i >= 1, i < last), prev_ref[0].astype(F32), 0.0)
    nxt = jnp.where(i < last - 1, next_ref[0].astype(F32), 0.0)
    ext = jnp.concatenate([prev, main, nxt], axis=0)
    n_ext = ext.shape[0]
    e = lax.broadcasted_iota(jnp.int32, (n_ext, 1), 0)
    col = jnp.bitwise_and(e, GRID_W - 1)
    has_left = jnp.where(is_ctx, e - GRID_W, col) != 0
    has_right = jnp.where(is_ctx, e - (GRID_W + TM_CONV - 1), col - (GRID_W - 1)) != 0
    src = (jnp.where(has_left, pltpu.roll(ext, 1, axis=0), 0.0),
           ext,
           jnp.where(has_right, pltpu.roll(ext, n_ext - 1, axis=0), 0.0))
    row_on = jnp.where(is_ctx, 0.0, 1.0)
    acc = jnp.zeros_like(main) + b_ref[...]
    for dy in range(CONV_K):
        for dx in range(CONV_K):
            wk = w_ref[CONV_K * dy + dx:CONV_K * dy + dx + 1, :]
            if dy != 1:
                wk = wk * row_on
            acc = acc + wk * src[dx][GRID_W * dy:GRID_W * dy + TM_CONV, :]
    o_ref[0] = _silu(acc).astype(o_ref.dtype)


def _grid_conv(xbc, conv_w9, conv_b):
    bsz, tt, ch = xbc.shape
    n_tiles = tt // TM_CONV
    rows_per_tile = TM_CONV // GRID_W
    last_row = tt // GRID_W - 1
    const = lambda b, i: (0, 0)
    return pl.pallas_call(
        _conv_kernel,
        grid=(bsz, n_tiles),
        in_specs=[pl.BlockSpec((1, GRID_W, ch),
                               lambda b, i: (b, jnp.clip(rows_per_tile * i - 1, 0, last_row), 0)),
                  pl.BlockSpec((1, TM_CONV, ch), lambda b, i: (b, i, 0)),
                  pl.BlockSpec((1, GRID_W, ch),
                               lambda b, i: (b, jnp.clip(rows_per_tile * (i + 1), 0, last_row), 0)),
                  pl.BlockSpec((CONV_K * CONV_K, ch), const),
                  pl.BlockSpec((1, ch), const)],
        out_specs=pl.BlockSpec((1, TM_CONV, ch), lambda b, i: (b, i, 0)),
        out_shape=jax.ShapeDtypeStruct((bsz, tt, ch), BF16),
        compiler_params=pltpu.CompilerParams(
            dimension_semantics=("parallel", "arbitrary"), vmem_limit_bytes=VMEM_LIMIT),
        name="grid_conv",
    )(xbc, xbc, xbc, conv_w9, conv_b)


def _cumsum_rows(x, reverse):
    n = x.shape[0]
    row = lax.broadcasted_iota(jnp.int32, x.shape, 0)
    s = 1
    while s < n:
        if reverse:
            x = x + jnp.where(row < n - s, pltpu.roll(x, n - s, axis=0), 0.0)
        else:
            x = x + jnp.where(row >= s, pltpu.roll(x, s, axis=0), 0.0)
        s *= 2
    return x


def _chunk_maps(n_ctx, n_all):
    n_lat = n_all - n_ctx
    fwd = lambda s: jnp.where(s < n_ctx, n_lat + s, s - n_ctx)
    bwd = lambda s: n_all - 1 - s
    return fwd, bwd


def _gla_prepare(qkv, g, reverse):
    n = qkv.shape[0]
    nb = n // SUB
    blocks = range(nb)
    rows = lambda a, blk: a[SUB * blk:SUB * (blk + 1), :]
    q = qkv[:, 0:GLA_QK].astype(F32)
    k = qkv[:, GLA_QK:2 * GLA_QK].astype(F32)
    cum = _cumsum_rows(g, reverse)
    zero = jnp.zeros((1, GLA_QK), F32)
    if reverse:
        ends = [cum[SUB * blk:SUB * blk + 1, :] for blk in blocks]
        refs = ends[1:] + [zero]
        earlier = lambda j, blk: j > blk
        tot = ends[0]
    else:
        ends = [cum[SUB * (blk + 1) - 1:SUB * (blk + 1), :] for blk in blocks]
        refs = [zero] + ends[:-1]
        earlier = lambda j, blk: j < blk
        tot = ends[nb - 1]
    stack = lambda vs: jnp.concatenate([jnp.broadcast_to(v, (SUB, GLA_QK)) for v in vs], axis=0)
    e_in = cum - stack(refs)
    q_blk = q * jnp.exp(e_in)
    k_diag = k * jnp.exp(-e_in)
    k_end = k * jnp.exp(stack(ends) - cum)
    q_in = jnp.concatenate([rows(q_blk, blk) * jnp.exp(refs[blk]) for blk in blocks], axis=0)
    k_out = jnp.concatenate([rows(k_end, blk) * jnp.exp(tot - ends[blk]) for blk in blocks], axis=0)
    q_in = q_in.astype(BF16)
    k_out = k_out.astype(BF16)
    q_bf = q_blk.astype(BF16)
    zeros_piece = jnp.zeros((SUB, GLA_QK), BF16)
    k_piece = [[rows(k_diag, j).astype(BF16) if seg == j
                else (rows(k_end, j) * jnp.exp(refs[seg] - ends[j])).astype(BF16) if earlier(j, seg)
                else zeros_piece
                for seg in blocks] for j in blocks]
    ii = lax.broadcasted_iota(jnp.int32, (n, n), 0)
    jj = lax.broadcasted_iota(jnp.int32, (n, n), 1)
    heads = []
    for h in range(GLA_HEADS):
        ks = slice(GLA_DK * h, GLA_DK * (h + 1))
        q_cat = jnp.concatenate(
            [jnp.concatenate([rows(q_bf, blk)[:, ks] if seg == blk else zeros_piece[:, ks]
                              for seg in blocks], axis=1) for blk in blocks], axis=0)
        k_cat = jnp.concatenate(
            [jnp.concatenate([k_piece[j][seg][:, ks] for seg in blocks], axis=1)
             for j in blocks], axis=0)
        heads.append(dict(q_cat=q_cat, k_cat=k_cat, q_in=q_in[:, ks], k_out=k_out[:, ks],
                          v=qkv[:, 2 * GLA_QK + GLA_DV * h:2 * GLA_QK + GLA_DV * (h + 1)]))
    return dict(heads=heads, causal=(ii <= jj) if reverse else (ii >= jj),
                decay_col=jnp.transpose(jnp.broadcast_to(tot, (GLA_DV, GLA_QK))))


def _gla_kernel(qkv_f_ref, g_f_ref, qkv_b_ref, g_b_ref, yf_ref, yb_ref, sf_ref, sb_ref):
    @pl.when(pl.program_id(1) == 0)
    def _():
        sf_ref[...] = jnp.zeros_like(sf_ref)
        sb_ref[...] = jnp.zeros_like(sb_ref)

    chains = []
    for e in range(qkv_f_ref.shape[0]):
        chains.append((qkv_f_ref.at[e], g_f_ref.at[e], False, sf_ref.at[e], yf_ref.at[e]))
        chains.append((qkv_b_ref.at[e], g_b_ref.at[e], True, sb_ref.at[e], yb_ref.at[e]))

    def start(chain):
        qkv_ref, g_ref, reverse, _, _ = chain
        p = _gla_prepare(qkv_ref[...], g_ref[...], reverse)
        return p, [_dot_nt(hd["q_cat"], hd["k_cat"]) for hd in p["heads"]]

    def finish(chain, p, att):
        _, _, _, s_ref, y_ref = chain
        outs = []
        for h, (hd, a) in enumerate(zip(p["heads"], att)):
            state = s_ref[h]
            lhs = jnp.concatenate([jnp.where(p["causal"], a, 0.0).astype(BF16), hd["q_in"]], axis=1)
            rhs = jnp.concatenate([hd["v"], state.astype(BF16)], axis=0)
            outs.append(_dot(lhs, rhs))
            ks = slice(GLA_DK * h, GLA_DK * (h + 1))
            s_ref[h] = state * jnp.exp(p["decay_col"][ks, :]) + _dot_tn(hd["k_out"], hd["v"])
        y_ref[...] = jnp.concatenate(outs, axis=1).astype(y_ref.dtype)

    started = start(chains[0])
    for i, chain in enumerate(chains):
        following = start(chains[i + 1]) if i + 1 < len(chains) else None
        finish(chain, *started)
        started = following


def _gla_scan(qkv, g):
    bsz, tt, _ = qkv.shape
    n_all = tt // CHUNK
    fwd, bwd = _chunk_maps(CTX_LEN // CHUNK, n_all)
    spec = lambda w, m, c: pl.BlockSpec((SCAN_BATCH, CHUNK, w), lambda b, s: (b, m(s), c))
    return pl.pallas_call(
        _gla_kernel,
        grid=(bsz // SCAN_BATCH, n_all),
        in_specs=[spec(QKV_W, fwd, 0), spec(GLA_QK, fwd, 0), spec(QKV_W, bwd, 0), spec(GLA_QK, bwd, 1)],
        out_specs=[spec(GLA_INNER, fwd, 0), spec(GLA_INNER, bwd, 0)],
        out_shape=[jax.ShapeDtypeStruct((bsz, tt, GLA_INNER), BF16)] * 2,
        scratch_shapes=[pltpu.VMEM((SCAN_BATCH, GLA_HEADS, GLA_DK, GLA_DV), F32)] * 2,
        compiler_params=pltpu.CompilerParams(
            dimension_semantics=("parallel", "arbitrary"), vmem_limit_bytes=VMEM_LIMIT),
        name="gla_scan",
    )(qkv, g, qkv, g)


def _ssd_prepare(xbc, dt, nega, expand, lane0, reverse):
    n = xbc.shape[0]
    end = 0 if reverse else n - 1
    cum = _cumsum_rows(dt * nega, reverse)
    e_cum = jnp.exp(cum)
    w = jnp.exp(cum[end:end + 1, :] - cum) * dt
    e_cum_x = _dot(_split_hi_lo(e_cum), expand)
    x_w = (xbc[:, 0:SSD_INNER].astype(F32) * _dot(_split_hi_lo(w), expand)).astype(BF16)
    cum_t = jnp.transpose(cum)
    dt_t = jnp.transpose(dt)
    ii = lax.broadcasted_iota(jnp.int32, (n, n), 0)
    jj = lax.broadcasted_iota(jnp.int32, (n, n), 1)
    causal = (ii <= jj) if reverse else (ii >= jj)
    first_head = lax.broadcasted_iota(jnp.int32, (n, 2 * SSD_P), 1) < SSD_P
    groups = []
    for grp in range(SSD_GROUPS):
        b_g = xbc[:, SSD_INNER + SSD_N * grp:SSD_INNER + SSD_N * (grp + 1)]
        c_g = xbc[:, SSD_INNER + SSD_BC + SSD_N * grp:SSD_INNER + SSD_BC + SSD_N * (grp + 1)]
        scores = _dot_nt(c_g, b_g)
        pairs = []
        for pair in range(SSD_HPG // 2):
            m_pair = []
            for hh in range(2):
                lane = lane0 + SSD_HPG * grp + 2 * pair + hh
                cum_b = jnp.broadcast_to(cum[:, lane:lane + 1], (n, n))
                seg = jnp.where(causal, cum_b - cum_t[lane:lane + 1, :], NEG_BIG)
                m_pair.append((scores * jnp.exp(seg) * dt_t[lane:lane + 1, :]).astype(BF16))
            col = 2 * SSD_P * (SSD_HPG // 2 * grp + pair)
            x_pair = xbc[:, col:col + 2 * SSD_P]
            none = jnp.zeros_like(x_pair)
            rhs = jnp.concatenate([jnp.where(first_head, x_pair, none),
                                   jnp.where(first_head, none, x_pair)], axis=0)
            pairs.append((jnp.concatenate(m_pair, axis=1), rhs))
        cols = slice(SSD_HPG * SSD_P * grp, SSD_HPG * SSD_P * (grp + 1))
        groups.append(dict(b=b_g, c=c_g, pairs=pairs, e_cum=e_cum_x[:, cols],
                           e_tot=e_cum_x[end:end + 1, cols], x_w=x_w[:, cols]))
    return groups


def _split_hi_lo(a):
    hi = a.astype(BF16)
    lo = (a - hi.astype(F32)).astype(BF16)
    return jnp.concatenate([hi, lo], axis=1)


def _ssd_kernel(xbc_f_ref, dt_f_ref, xbc_b_ref, dt_b_ref, nega_ref, expand_ref,
                yf_ref, yb_ref, sf_ref, sb_ref):
    @pl.when(pl.program_id(1) == 0)
    def _():
        sf_ref[...] = jnp.zeros_like(sf_ref)
        sb_ref[...] = jnp.zeros_like(sb_ref)

    nega = nega_ref[...]
    chains = []
    for e in range(xbc_f_ref.shape[0]):
        chains.append((xbc_f_ref.at[e], dt_f_ref.at[e], 0, False, sf_ref.at[e], yf_ref.at[e]))
        chains.append((xbc_b_ref.at[e], dt_b_ref.at[e], 1, True, sb_ref.at[e], yb_ref.at[e]))

    def start(chain):
        xbc_ref, dt_ref, d, reverse, _, _ = chain
        return _ssd_prepare(xbc_ref[...], dt_ref[...], nega, expand_ref[d],
                            DT_LANE + SSD_HEADS * d, reverse)

    def finish(chain, groups):
        s_ref, y_ref = chain[4], chain[5]
        outs = []
        for grp, gd in enumerate(groups):
            state = s_ref[grp]
            intra = jnp.concatenate([_dot(m, rhs) for m, rhs in gd["pairs"]], axis=1)
            outs.append(_dot(gd["c"], state.astype(BF16)) * gd["e_cum"] + intra)
            s_ref[grp] = state * gd["e_tot"] + _dot_tn(gd["b"], gd["x_w"])
        y_ref[...] = jnp.concatenate(outs, axis=1).astype(y_ref.dtype)

    started = start(chains[0])
    for i, chain in enumerate(chains):
        following = start(chains[i + 1]) if i + 1 < len(chains) else None
        finish(chain, started)
        started = following


def _head_expand_matrix():
    r = jnp.arange(2 * TAIL_W)[:, None] % TAIL_W
    c = jnp.arange(SSD_INNER)[None, :] // SSD_P
    return jnp.stack([(r == DT_LANE + SSD_HEADS * d + c) for d in range(2)]).astype(BF16)


def _ssd_scan(xbc, dt, nega):
    bsz, tt, _ = xbc.shape
    n_all = tt // CHUNK
    fwd, bwd = _chunk_maps(CTX_LEN // CHUNK, n_all)
    spec = lambda w, m: pl.BlockSpec((SCAN_BATCH, CHUNK, w), lambda b, s: (b, m(s), 0))
    expand = _head_expand_matrix()
    return pl.pallas_call(
        _ssd_kernel,
        grid=(bsz // SCAN_BATCH, n_all),
        in_specs=[spec(SSD_CONV_DIM, fwd), spec(TAIL_W, fwd),
                  spec(SSD_CONV_DIM, bwd), spec(TAIL_W, bwd),
                  pl.BlockSpec((1, TAIL_W), lambda b, s: (0, 0)),
                  pl.BlockSpec(expand.shape, lambda b, s: (0, 0, 0))],
        out_specs=[spec(SSD_INNER, fwd), spec(SSD_INNER, bwd)],
        out_shape=[jax.ShapeDtypeStruct((bsz, tt, SSD_INNER), BF16)] * 2,
        scratch_shapes=[pltpu.VMEM((SCAN_BATCH, SSD_GROUPS, SSD_N, SSD_HPG * SSD_P), F32)] * 2,
        compiler_params=pltpu.CompilerParams(
            dimension_semantics=("parallel", "arbitrary"), vmem_limit_bytes=VMEM_LIMIT),
        name="ssd_scan",
    )(xbc, dt, xbc, dt, nega, expand)


def _merge_ffn_kernel(x_ref, ygf_ref, ygb_ref, ysf_ref, ysb_ref, xs_ref, r_ref, z_ref, m_ref,
                      gnorm_ref, dskip_ref, snorm_ref, npost_ref, nfpre_ref, nfpost_ref,
                      wout_ref, wgate_ref, wup_ref, wdown_ref, o_ref):
    b = pl.program_id(0)
    mrow = m_ref[pl.ds(b, 1), :]
    mod = [mrow[:, D_MODEL * j:D_MODEL * (j + 1)] for j in range(6)]
    gw = SSD_INNER // SSD_GROUPS

    def mixer_out(rs):
        y_gla = ygf_ref[0, rs, :].astype(F32) + ygb_ref[0, rs, :].astype(F32)
        o_gla = jnp.concatenate(
            [_rms(y_gla[:, GLA_DV * h:GLA_DV * (h + 1)]) for h in range(GLA_HEADS)], axis=1)
        o_gla = o_gla * gnorm_ref[...] * r_ref[0, rs, :].astype(F32)
        y_ssd = ysf_ref[0, rs, :].astype(F32) + ysb_ref[0, rs, :].astype(F32)
        y_ssd = (y_ssd + dskip_ref[...] * xs_ref[0, rs, :].astype(F32)) * z_ref[0, rs, :].astype(F32)
        o_ssd = jnp.concatenate(
            [_rms(y_ssd[:, gw * g:gw * (g + 1)]) for g in range(SSD_GROUPS)], axis=1)
        o_ssd = o_ssd * snorm_ref[...]
        o = jnp.concatenate([o_gla, o_ssd], axis=1).astype(BF16)
        y = _dot(o, wout_ref[...])
        x1 = x_ref[0, rs, :] + mod[2] * (_rms(y) * npost_ref[...])
        h = ((_rms(x1) * nfpre_ref[...]) * (1.0 + mod[4]) + mod[3]).astype(BF16)
        return x1, h

    def ffn(rs, x1, h):
        act = (_silu(_dot(h, wgate_ref[...])) * _dot(h, wup_ref[...])).astype(BF16)
        f = _dot(act, wdown_ref[...])
        o_ref[0, rs, :] = x1 + mod[5] * (_rms(f) * nfpost_ref[...])

    subs = [slice(FFN_SUB * j, FFN_SUB * (j + 1)) for j in range(TM_FFN // FFN_SUB)]
    ready = mixer_out(subs[0])
    for j, rs in enumerate(subs):
        following = mixer_out(subs[j + 1]) if j + 1 < len(subs) else None
        ffn(rs, *ready)
        ready = following


def _merge_ffn(x, ygf, ygb, ysf, ysb, xbc, r, z, m_all, gnorm, dskip, snorm, npost, nfpre,
               nfpost, wout, wgate, wup, wdown):
    bsz, seq, _ = x.shape
    n_tiles = seq // TM_FFN
    const = lambda b, i: (0, 0)

    def lat(width):
        return pl.BlockSpec((1, TM_FFN, width), lambda b, i: (b, i, 0))

    single = lambda shape: pl.BlockSpec(shape, const, pipeline_mode=pl.Buffered(1))
    vec = lambda w: pl.BlockSpec((1, w), const)
    return pl.pallas_call(
        _merge_ffn_kernel,
        grid=(bsz, n_tiles),
        in_specs=[pl.BlockSpec((1, TM_FFN, D_MODEL), lambda b, i: (b, i, 0)),
                  lat(GLA_INNER), lat(GLA_INNER), lat(SSD_INNER), lat(SSD_INNER),
                  lat(SSD_INNER), lat(GLA_INNER), lat(SSD_INNER),
                  pl.BlockSpec(m_all.shape, const),
                  vec(GLA_INNER), vec(SSD_INNER), vec(SSD_INNER),
                  vec(D_MODEL), vec(D_MODEL), vec(D_MODEL),
                  single(wout.shape), single(wgate.shape), single(wup.shape), single(wdown.shape)],
        out_specs=pl.BlockSpec((1, TM_FFN, D_MODEL), lambda b, i: (b, i, 0)),
        out_shape=jax.ShapeDtypeStruct((bsz, seq, D_MODEL), F32),
        compiler_params=pltpu.CompilerParams(
            dimension_semantics=("parallel", "arbitrary"), vmem_limit_bytes=VMEM_LIMIT),
        name="merge_ffn",
    )(x, ygf, ygb, ysf, ysb, xbc, r, z, m_all, gnorm, dskip, snorm, npost, nfpre, nfpost,
      wout, wgate, wup, wdown)


def kernel(x, c, ctx, c_ctx, w_mod, b_mod, norm_mix_pre, norm_mix_post, norm_ffn_pre, norm_ffn_post,
           w_in, conv_w, conv_b, gla_wg_f, gla_bg_f, gla_wg_b, gla_bg_b, gla_norm,
           a_log_f, a_log_b, dt_bias_f, dt_bias_b, d_skip, ssd_norm, w_out, w_gate, w_up, w_down):
    assert w_mod.shape[0] == 1, "single-layer kernel"
    bsz = x.shape[0]
    row = lambda v: v.reshape(1, -1)

    c_all = jnp.concatenate([c, c_ctx[None], jnp.zeros((8 - bsz - 1, D_MODEL), F32)], axis=0)
    wi = w_in[0]
    lr0 = COL_Z
    z0 = lr0 + 2 * GLA_RANK
    dt0 = z0 + SSD_INNER + SSD_CONV_DIM
    w_perm = jnp.concatenate(
        [wi[:, :lr0], wi[:, z0:dt0], wi[:, lr0:z0], wi[:, dt0:],
         jnp.zeros((D_MODEL, TAIL_W - 2 * GLA_RANK - 2 * SSD_HEADS), F32)], axis=1).astype(BF16)
    wg_pad = jnp.zeros((TAIL_W, 2 * GLA_QK), F32)
    wg_pad = wg_pad.at[0:GLA_RANK, 0:GLA_QK].set(gla_wg_f[0])
    wg_pad = wg_pad.at[GLA_RANK:2 * GLA_RANK, GLA_QK:].set(gla_wg_b[0]).astype(BF16)
    bg_cat = jnp.concatenate([gla_bg_f[0], gla_bg_b[0]]).reshape(1, -1)
    lane_pad = lambda f, bk: jnp.zeros((1, TAIL_W), F32).at[0, DT_LANE:DT_LANE + 2 * SSD_HEADS].set(
        jnp.concatenate([f, bk]))
    dtb_pad = lane_pad(dt_bias_f[0], dt_bias_b[0])
    nega = lane_pad(-jnp.exp(a_log_f[0]), -jnp.exp(a_log_b[0]))
    conv_w9 = conv_w[0].reshape(CONV_K * CONV_K, SSD_CONV_DIM)

    m_all = _modulation(c_all, w_mod[0], row(b_mod[0]))
    qkv, r_act, z_act, xbc_pre, g, dt = _input_projection(
        x, ctx, m_all, row(norm_mix_pre[0]), w_perm, wg_pad, bg_cat, dtb_pad)
    xbc = _grid_conv(xbc_pre, conv_w9, row(conv_b[0]))
    ygf, ygb = _gla_scan(qkv, g)
    ysf, ysb = _ssd_scan(xbc, dt, nega)
    return _merge_ffn(
        x, ygf, ygb, ysf, ysb, xbc, r_act, z_act, m_all,
        row(jnp.tile(gla_norm[0], GLA_HEADS)), row(jnp.repeat(d_skip[0], SSD_P)), row(ssd_norm[0]),
        row(norm_mix_post[0]), row(norm_ffn_pre[0]), row(norm_ffn_post[0]),
        w_out[0].astype(BF16), w_gate[0].astype(BF16), w_up[0].astype(BF16), w_down[0].astype(BF16))
```

```python
import functools

import jax
import jax.numpy as jnp
from jax import lax
from jax.experimental import pallas as pl
from jax.experimental.pallas import tpu as pltpu

F32 = jnp.float32
BF16 = jnp.bfloat16

D_MODEL = 1024
CTX_LEN = 256
GRID_W = 64
EPS = 1e-6

GLA_HEADS = 4
GLA_DK = 64
GLA_DV = 128
GLA_QK = GLA_HEADS * GLA_DK
GLA_INNER = GLA_HEADS * GLA_DV
GLA_RANK = 16
GLA_GATE_NORM = 16.0

SSD_HEADS = 8
SSD_P = 64
SSD_N = 64
SSD_GROUPS = 2
SSD_HPG = SSD_HEADS // SSD_GROUPS
SSD_INNER = SSD_HEADS * SSD_P
SSD_BC = SSD_GROUPS * SSD_N
SSD_CONV_DIM = SSD_INNER + 2 * SSD_BC
CONV_K = 3
D_FF = 2816

QKV_W = 2 * GLA_QK + GLA_INNER
COL_R = QKV_W
COL_Z = COL_R + GLA_INNER
COL_XBC = COL_Z + SSD_INNER
COL_TAIL = COL_XBC + SSD_CONV_DIM
TAIL_W = 128
PROJ_W = COL_TAIL + TAIL_W
DT_LANE = 2 * GLA_RANK
W_IN_LR = COL_Z
W_IN_Z = W_IN_LR + 2 * GLA_RANK
W_IN_DT = W_IN_Z + SSD_INNER + SSD_CONV_DIM
W_IN_COLS = W_IN_DT + 2 * SSD_HEADS

TM_PROJ = 512
PROJ_SUB = CTX_LEN
TM_CONV = 256
TM_FFN = 512
FFN_SUB = 128
CHUNK = 128
SCAN_BATCH = 2
SUB = 32
NEG_BIG = -1e30
VMEM_LIMIT = 56 * 1024 * 1024


def _silu(x):
    return x / (1.0 + jnp.exp(-x))


def _softplus(x):
    return jnp.maximum(x, 0.0) + jnp.log(1.0 + jnp.exp(-jnp.abs(x)))


def _rms(x):
    return x * lax.rsqrt(jnp.mean(x * x, axis=-1, keepdims=True) + EPS)


def _dot(a, b):
    return jnp.dot(a, b, preferred_element_type=F32)


def _dot_nt(a, b):
    return lax.dot_general(a, b, (((1,), (1,)), ((), ())), preferred_element_type=F32)


def _dot_tn(a, b):
    return lax.dot_general(a, b, (((0,), (0,)), ((), ())), preferred_element_type=F32)


def _mod_kernel(c_ref, w_ref, b_ref, o_ref):
    s = _silu(c_ref[...])
    o_ref[...] = _dot(s.astype(BF16), w_ref[...].astype(BF16)) + b_ref[...]


def _modulation(c_all, w_mod, b_mod):
    n = w_mod.shape[1]
    tn = 1024
    return pl.pallas_call(
        _mod_kernel,
        grid=(n // tn,),
        in_specs=[pl.BlockSpec((8, D_MODEL), lambda j: (0, 0)),
                  pl.BlockSpec((D_MODEL, tn), lambda j: (0, j)),
                  pl.BlockSpec((1, tn), lambda j: (0, j))],
        out_specs=pl.BlockSpec((8, tn), lambda j: (0, j)),
        out_shape=jax.ShapeDtypeStruct((8, n), F32),
        name="adaln_mod",
    )(c_all, w_mod, b_mod)


def _permute_cast_w_in(w_ref, w_s):
    blk = 128
    pad = jnp.zeros((blk, PROJ_W - W_IN_COLS), F32)

    def body(rb, carry):
        rows = pl.ds(pl.multiple_of(rb * blk, blk), blk)
        w_s[rows, 0:W_IN_LR] = w_ref[rows, 0:W_IN_LR].astype(BF16)
        rest = w_ref[rows, W_IN_LR:W_IN_COLS]
        w_s[rows, COL_Z:COL_TAIL] = rest[:, W_IN_Z - W_IN_LR:W_IN_DT - W_IN_LR].astype(BF16)
        tail = jnp.concatenate([rest[:, 0:W_IN_Z - W_IN_LR], rest[:, W_IN_DT - W_IN_LR:], pad], axis=1)
        w_s[rows, COL_TAIL:PROJ_W] = tail.astype(BF16)
        return carry

    lax.fori_loop(0, D_MODEL // blk, body, 0)


def _grid_conv_tile(above, main, below, is_ctx, w_ref, b_ref):
    n = main.shape[0]
    ext = jnp.concatenate([above, main, below], axis=0).astype(F32)
    row_on = jnp.where(is_ctx, 0.0, 1.0)
    taps = []
    for dx in range(CONV_K):
        tap = None
        for dy in range(CONV_K):
            wk = w_ref[CONV_K * dy + dx:CONV_K * dy + dx + 1, :]
            if dy != 1:
                wk = wk * row_on
            term = wk * ext[GRID_W * dy:GRID_W * dy + n, :]
            tap = term if tap is None else tap + term
        taps.append(tap)
    t = lax.broadcasted_iota(jnp.int32, (n, 1), 0)
    col = jnp.bitwise_and(t, GRID_W - 1)
    has_left = jnp.where(is_ctx, t, col) != 0
    has_right = jnp.where(is_ctx, t - (CTX_LEN - 1), col - (GRID_W - 1)) != 0
    acc = (taps[1] + b_ref[...]
           + jnp.where(has_left, pltpu.roll(taps[0], 1, axis=0), 0.0)
           + jnp.where(has_right, pltpu.roll(taps[2], n - 1, axis=0), 0.0))
    return _silu(acc).astype(BF16)


def _inproj_conv_kernel(x_ref, ctx_ref, m_ref, gpre_ref, w_ref, wg_ref, bg_ref, dtb_ref,
                        cw_ref, cb_ref, qkv_ref, r_ref, z_ref, xbc_ref, g_ref, dt_ref,
                        w_s, ring, above_s):
    b = pl.program_id(0)
    i = pl.program_id(1)
    n_tiles = pl.num_programs(1) - 1
    ctx_tile = n_tiles - 1

    @pl.when(jnp.logical_and(b == 0, i == 0))
    def _():
        _permute_cast_w_in(w_ref, w_s)

    @pl.when(i == 0)
    def _():
        ring[...] = jnp.zeros_like(ring)
        above_s[...] = jnp.zeros_like(above_s)

    def conv_previous(below):
        j = i - 1
        main = ring[jnp.bitwise_and(j, 1)]
        none = jnp.zeros_like(below)
        above = jnp.where(jnp.logical_and(j >= 1, j < ctx_tile), above_s[...], none)
        below = jnp.where(j < ctx_tile - 1, below, none)
        xbc_ref[0] = _grid_conv_tile(above, main, below, j == ctx_tile, cw_ref, cb_ref)
        above_s[...] = main[TM_PROJ - GRID_W:, :]

    @pl.when(i < n_tiles)
    def _():
        is_ctx = i == ctx_tile
        mrow = m_ref[pl.ds(jnp.where(is_ctx, 4, b), 1), :]
        shift = mrow[:, 0:D_MODEL]
        scale = mrow[:, D_MODEL:2 * D_MODEL]
        slot = jnp.bitwise_and(i, 1)

        def normed(rs):
            x = jnp.where(is_ctx, ctx_ref[0], x_ref[0, rs, :])
            return ((_rms(x) * gpre_ref[...]) * (1.0 + scale) + shift).astype(BF16)

        def project(rs, h):
            acc = _dot(h, w_s[...])
            qkv_ref[0, rs, 0:GLA_QK] = (acc[:, 0:GLA_QK] * (GLA_DK ** -0.5)).astype(BF16)
            qkv_ref[0, rs, GLA_QK:QKV_W] = acc[:, GLA_QK:QKV_W].astype(BF16)
            r_ref[0, rs, :] = _silu(acc[:, COL_R:COL_Z]).astype(BF16)
            z_ref[0, rs, :] = _silu(acc[:, COL_Z:COL_XBC]).astype(BF16)
            tail = acc[:, COL_TAIL:PROJ_W]
            logits = _dot(tail.astype(BF16), wg_ref[...]) + bg_ref[...]
            g_ref[0, rs, :] = -_softplus(-logits) * (1.0 / GLA_GATE_NORM)
            dt_ref[0, rs, :] = _softplus(tail + dtb_ref[...])
            return acc[:, COL_XBC:COL_TAIL].astype(BF16)

        subs = [slice(PROJ_SUB * j, PROJ_SUB * (j + 1)) for j in range(TM_PROJ // PROJ_SUB)]
        ready = normed(subs[0])
        fresh = []
        for j, rs in enumerate(subs):
            following = normed(subs[j + 1]) if j + 1 < len(subs) else None
            fresh.append(project(rs, ready))
            if j == 0:
                conv_previous(fresh[0][0:GRID_W, :])
            ready = following
        for rs, xbc_new in zip(subs, fresh):
            ring[slot, rs, :] = xbc_new

    @pl.when(i == n_tiles)
    def _():
        conv_previous(jnp.zeros((GRID_W, SSD_CONV_DIM), BF16))


def _input_projection_conv(x, ctx, m_all, gpre, w_in, wg_pad, bg_cat, dtb_pad, conv_w9, conv_b):
    bsz, seq, _ = x.shape
    n_lat = seq // TM_PROJ
    n_tiles = n_lat + 1
    tt = CTX_LEN + seq
    const = lambda b, i: (0, 0)
    tok = lambda b, i: (b, jnp.minimum(i, n_tiles - 1), 0)
    conv_tok = lambda b, i: (b, jnp.maximum(i - 1, 0), 0)
    return pl.pallas_call(
        _inproj_conv_kernel,
        grid=(bsz, n_tiles + 1),
        in_specs=[pl.BlockSpec((1, TM_PROJ, D_MODEL), lambda b, i: (b, jnp.minimum(i, n_lat - 1), 0)),
                  pl.BlockSpec((1, CTX_LEN, D_MODEL), lambda b, i: (b, 0, 0)),
                  pl.BlockSpec(m_all.shape, const),
                  pl.BlockSpec((1, D_MODEL), const),
                  pl.BlockSpec((D_MODEL, W_IN_COLS), const, pipeline_mode=pl.Buffered(1)),
                  pl.BlockSpec((TAIL_W, 2 * GLA_QK), const),
                  pl.BlockSpec((1, 2 * GLA_QK), const),
                  pl.BlockSpec((1, TAIL_W), const),
                  pl.BlockSpec((CONV_K * CONV_K, SSD_CONV_DIM), const),
                  pl.BlockSpec((1, SSD_CONV_DIM), const)],
        out_specs=[pl.BlockSpec((1, TM_PROJ, QKV_W), tok),
                   pl.BlockSpec((1, TM_PROJ, GLA_INNER), tok),
                   pl.BlockSpec((1, TM_PROJ, SSD_INNER), tok),
                   pl.BlockSpec((1, TM_PROJ, SSD_CONV_DIM), conv_tok),
                   pl.BlockSpec((1, TM_PROJ, 2 * GLA_QK), tok),
                   pl.BlockSpec((1, TM_PROJ, TAIL_W), tok)],
        out_shape=[jax.ShapeDtypeStruct((bsz, tt, QKV_W), BF16),
                   jax.ShapeDtypeStruct((bsz, tt, GLA_INNER), BF16),
                   jax.ShapeDtypeStruct((bsz, tt, SSD_INNER), BF16),
                   jax.ShapeDtypeStruct((bsz, tt, SSD_CONV_DIM), BF16),
                   jax.ShapeDtypeStruct((bsz, tt, 2 * GLA_QK), F32),
                   jax.ShapeDtypeStruct((bsz, tt, TAIL_W), F32)],
        scratch_shapes=[pltpu.VMEM((D_MODEL, PROJ_W), BF16),
                        pltpu.VMEM((2, TM_PROJ, SSD_CONV_DIM), BF16),
                        pltpu.VMEM((GRID_W, SSD_CONV_DIM), BF16)],
        compiler_params=pltpu.CompilerParams(
            dimension_semantics=("arbitrary", "arbitrary"), vmem_limit_bytes=VMEM_LIMIT),
        name="in_proj_conv",
    )(x, ctx, m_all, gpre, w_in, wg_pad, bg_cat, dtb_pad, conv_w9, conv_b)


def _cumsum_rows(x, reverse):
    n = x.shape[0]
    row = lax.broadcasted_iota(jnp.int32, x.shape, 0)
    s = 1
    while s < n:
        if reverse:
            x = x + jnp.where(row < n - s, pltpu.roll(x, n - s, axis=0), 0.0)
        else:
            x = x + jnp.where(row >= s, pltpu.roll(x, s, axis=0), 0.0)
        s *= 2
    return x


def _chunk_maps(n_ctx, n_all):
    n_lat = n_all - n_ctx
    fwd = lambda s: jnp.where(s < n_ctx, n_lat + s, s - n_ctx)
    bwd = lambda s: n_all - 1 - s
    return fwd, bwd


def _gla_prepare(qkv, g, reverse):
    n = qkv.shape[0]
    nb = n // SUB
    blocks = range(nb)
    rows = lambda a, blk: a[SUB * blk:SUB * (blk + 1), :]
    q = qkv[:, 0:GLA_QK].astype(F32)
    k = qkv[:, GLA_QK:2 * GLA_QK].astype(F32)
    cum = _cumsum_rows(g, reverse)
    zero = jnp.zeros((1, GLA_QK), F32)
    if reverse:
        ends = [cum[SUB * blk:SUB * blk + 1, :] for blk in blocks]
        refs = ends[1:] + [zero]
        earlier = lambda j, blk: j > blk
        tot = ends[0]
    else:
        ends = [cum[SUB * (blk + 1) - 1:SUB * (blk + 1), :] for blk in blocks]
        refs = [zero] + ends[:-1]
        earlier = lambda j, blk: j < blk
        tot = ends[nb - 1]
    stack = lambda vs: jnp.concatenate([jnp.broadcast_to(v, (SUB, GLA_QK)) for v in vs], axis=0)
    e_in = cum - stack(refs)
    q_blk = q * jnp.exp(e_in)
    k_diag = k * jnp.exp(-e_in)
    k_end = k * jnp.exp(stack(ends) - cum)
    q_in = jnp.concatenate([rows(q_blk, blk) * jnp.exp(refs[blk]) for blk in blocks], axis=0)
    k_out = jnp.concatenate([rows(k_end, blk) * jnp.exp(tot - ends[blk]) for blk in blocks], axis=0)
    q_in = q_in.astype(BF16)
    k_out = k_out.astype(BF16)
    q_bf = q_blk.astype(BF16)
    zeros_piece = jnp.zeros((SUB, GLA_QK), BF16)
    k_piece = [[rows(k_diag, j).astype(BF16) if seg == j
                else (rows(k_end, j) * jnp.exp(refs[seg] - ends[j])).astype(BF16) if earlier(j, seg)
                else zeros_piece
                for seg in blocks] for j in blocks]
    ii = lax.broadcasted_iota(jnp.int32, (n, n), 0)
    jj = lax.broadcasted_iota(jnp.int32, (n, n), 1)
    heads = []
    for h in range(GLA_HEADS):
        ks = slice(GLA_DK * h, GLA_DK * (h + 1))
        q_cat = jnp.concatenate(
            [jnp.concatenate([rows(q_bf, blk)[:, ks] if seg == blk else zeros_piece[:, ks]
                              for seg in blocks], axis=1) for blk in blocks], axis=0)
        k_cat = jnp.concatenate(
            [jnp.concatenate([k_piece[j][seg][:, ks] for seg in blocks], axis=1)
             for j in blocks], axis=0)
        heads.append(dict(q_cat=q_cat, k_cat=k_cat, q_in=q_in[:, ks], k_out=k_out[:, ks],
                          v=qkv[:, 2 * GLA_QK + GLA_DV * h:2 * GLA_QK + GLA_DV * (h + 1)]))
    return dict(heads=heads, causal=(ii <= jj) if reverse else (ii >= jj),
                decay_col=jnp.transpose(jnp.broadcast_to(tot, (GLA_DV, GLA_QK))))


def _gla_kernel(qkv_f_ref, g_f_ref, qkv_b_ref, g_b_ref, yf_ref, yb_ref, sf_ref, sb_ref):
    @pl.when(pl.program_id(1) == 0)
    def _():
        sf_ref[...] = jnp.zeros_like(sf_ref)
        sb_ref[...] = jnp.zeros_like(sb_ref)

    chains = []
    for e in range(qkv_f_ref.shape[0]):
        chains.append((qkv_f_ref.at[e], g_f_ref.at[e], False, sf_ref.at[e], yf_ref.at[e]))
        chains.append((qkv_b_ref.at[e], g_b_ref.at[e], True, sb_ref.at[e], yb_ref.at[e]))

    def start(chain):
        qkv_ref, g_ref, reverse, _, _ = chain
        p = _gla_prepare(qkv_ref[...], g_ref[...], reverse)
        return p, [_dot_nt(hd["q_cat"], hd["k_cat"]) for hd in p["heads"]]

    def finish(chain, p, att):
        _, _, _, s_ref, y_ref = chain
        outs = []
        for h, (hd, a) in enumerate(zip(p["heads"], att)):
            state = s_ref[h]
            lhs = jnp.concatenate([jnp.where(p["causal"], a, 0.0).astype(BF16), hd["q_in"]], axis=1)
            rhs = jnp.concatenate([hd["v"], state.astype(BF16)], axis=0)
            outs.append(_dot(lhs, rhs))
            ks = slice(GLA_DK * h, GLA_DK * (h + 1))
            s_ref[h] = state * jnp.exp(p["decay_col"][ks, :]) + _dot_tn(hd["k_out"], hd["v"])
        y_ref[...] = jnp.concatenate(outs, axis=1).astype(y_ref.dtype)

    started = start(chains[0])
    for i, chain in enumerate(chains):
        following = start(chains[i + 1]) if i + 1 < len(chains) else None
        finish(chain, *started)
        started = following


def _gla_scan(qkv, g):
    bsz, tt, _ = qkv.shape
    n_all = tt // CHUNK
    fwd, bwd = _chunk_maps(CTX_LEN // CHUNK, n_all)
    spec = lambda w, m, c: pl.BlockSpec((SCAN_BATCH, CHUNK, w), lambda b, s: (b, m(s), c))
    return pl.pallas_call(
        _gla_kernel,
        grid=(bsz // SCAN_BATCH, n_all),
        in_specs=[spec(QKV_W, fwd, 0), spec(GLA_QK, fwd, 0), spec(QKV_W, bwd, 0), spec(GLA_QK, bwd, 1)],
        out_specs=[spec(GLA_INNER, fwd, 0), spec(GLA_INNER, bwd, 0)],
        out_shape=[jax.ShapeDtypeStruct((bsz, tt, GLA_INNER), BF16)] * 2,
        scratch_shapes=[pltpu.VMEM((SCAN_BATCH, GLA_HEADS, GLA_DK, GLA_DV), F32)] * 2,
        compiler_params=pltpu.CompilerParams(
            dimension_semantics=("parallel", "arbitrary"), vmem_limit_bytes=VMEM_LIMIT),
        name="gla_scan",
    )(qkv, g, qkv, g)


def _ssd_prepare(xbc, dt, nega, expand, lane0, reverse):
    n = xbc.shape[0]
    end = 0 if reverse else n - 1
    cum = _cumsum_rows(dt * nega, reverse)
    e_cum = jnp.exp(cum)
    w = jnp.exp(cum[end:end + 1, :] - cum) * dt
    e_cum_x = _dot(_split_hi_lo(e_cum), expand)
    x_w = (xbc[:, 0:SSD_INNER].astype(F32) * _dot(_split_hi_lo(w), expand)).astype(BF16)
    cum_t = jnp.transpose(cum)
    dt_t = jnp.transpose(dt)
    ii = lax.broadcasted_iota(jnp.int32, (n, n), 0)
    jj = lax.broadcasted_iota(jnp.int32, (n, n), 1)
    causal = (ii <= jj) if reverse else (ii >= jj)
    first_head = lax.broadcasted_iota(jnp.int32, (n, 2 * SSD_P), 1) < SSD_P
    groups = []
    for grp in range(SSD_GROUPS):
        b_g = xbc[:, SSD_INNER + SSD_N * grp:SSD_INNER + SSD_N * (grp + 1)]
        c_g = xbc[:, SSD_INNER + SSD_BC + SSD_N * grp:SSD_INNER + SSD_BC + SSD_N * (grp + 1)]
        scores = _dot_nt(c_g, b_g)
        pairs = []
        for pair in range(SSD_HPG // 2):
            m_pair = []
            for hh in range(2):
                lane = lane0 + SSD_HPG * grp + 2 * pair + hh
                cum_b = jnp.broadcast_to(cum[:, lane:lane + 1], (n, n))
                seg = jnp.where(causal, cum_b - cum_t[lane:lane + 1, :], NEG_BIG)
                m_pair.append((scores * jnp.exp(seg) * dt_t[lane:lane + 1, :]).astype(BF16))
            col = 2 * SSD_P * (SSD_HPG // 2 * grp + pair)
            x_pair = xbc[:, col:col + 2 * SSD_P]
            none = jnp.zeros_like(x_pair)
            rhs = jnp.concatenate([jnp.where(first_head, x_pair, none),
                                   jnp.where(first_head, none, x_pair)], axis=0)
            pairs.append((jnp.concatenate(m_pair, axis=1), rhs))
        cols = slice(SSD_HPG * SSD_P * grp, SSD_HPG * SSD_P * (grp + 1))
        groups.append(dict(b=b_g, c=c_g, pairs=pairs, e_cum=e_cum_x[:, cols],
                           e_tot=e_cum_x[end:end + 1, cols], x_w=x_w[:, cols]))
    return groups


def _split_hi_lo(a):
    hi = a.astype(BF16)
    lo = (a - hi.astype(F32)).astype(BF16)
    return jnp.concatenate([hi, lo], axis=1)


def _ssd_kernel(xbc_f_ref, dt_f_ref, xbc_b_ref, dt_b_ref, nega_ref, expand_ref,
                yf_ref, yb_ref, sf_ref, sb_ref):
    @pl.when(pl.program_id(1) == 0)
    def _():
        sf_ref[...] = jnp.zeros_like(sf_ref)
        sb_ref[...] = jnp.zeros_like(sb_ref)

    nega = nega_ref[...]
    chains = []
    for e in range(xbc_f_ref.shape[0]):
        chains.append((xbc_f_ref.at[e], dt_f_ref.at[e], 0, False, sf_ref.at[e], yf_ref.at[e]))
        chains.append((xbc_b_ref.at[e], dt_b_ref.at[e], 1, True, sb_ref.at[e], yb_ref.at[e]))

    def start(chain):
        xbc_ref, dt_ref, d, reverse, _, _ = chain
        return _ssd_prepare(xbc_ref[...], dt_ref[...], nega, expand_ref[d],
                            DT_LANE + SSD_HEADS * d, reverse)

    def finish(chain, groups):
        s_ref, y_ref = chain[4], chain[5]
        outs = []
        for grp, gd in enumerate(groups):
            state = s_ref[grp]
            intra = jnp.concatenate([_dot(m, rhs) for m, rhs in gd["pairs"]], axis=1)
            outs.append(_dot(gd["c"], state.astype(BF16)) * gd["e_cum"] + intra)
            s_ref[grp] = state * gd["e_tot"] + _dot_tn(gd["b"], gd["x_w"])
        y_ref[...] = jnp.concatenate(outs, axis=1).astype(y_ref.dtype)

    started = start(chains[0])
    for i, chain in enumerate(chains):
        following = start(chains[i + 1]) if i + 1 < len(chains) else None
        finish(chain, started)
        started = following


def _head_expand_matrix():
    r = jnp.arange(2 * TAIL_W)[:, None] % TAIL_W
    c = jnp.arange(SSD_INNER)[None, :] // SSD_P
    return jnp.stack([(r == DT_LANE + SSD_HEADS * d + c) for d in range(2)]).astype(BF16)


def _ssd_scan(xbc, dt, nega):
    bsz, tt, _ = xbc.shape
    n_all = tt // CHUNK
    fwd, bwd = _chunk_maps(CTX_LEN // CHUNK, n_all)
    spec = lambda w, m: pl.BlockSpec((SCAN_BATCH, CHUNK, w), lambda b, s: (b, m(s), 0))
    expand = _head_expand_matrix()
    return pl.pallas_call(
        _ssd_kernel,
        grid=(bsz // SCAN_BATCH, n_all),
        in_specs=[spec(SSD_CONV_DIM, fwd), spec(TAIL_W, fwd),
                  spec(SSD_CONV_DIM, bwd), spec(TAIL_W, bwd),
                  pl.BlockSpec((1, TAIL_W), lambda b, s: (0, 0)),
                  pl.BlockSpec(expand.shape, lambda b, s: (0, 0, 0))],
        out_specs=[spec(SSD_INNER, fwd), spec(SSD_INNER, bwd)],
        out_shape=[jax.ShapeDtypeStruct((bsz, tt, SSD_INNER), BF16)] * 2,
        scratch_shapes=[pltpu.VMEM((SCAN_BATCH, SSD_GROUPS, SSD_N, SSD_HPG * SSD_P), F32)] * 2,
        compiler_params=pltpu.CompilerParams(
            dimension_semantics=("parallel", "arbitrary"), vmem_limit_bytes=VMEM_LIMIT),
        name="ssd_scan",
    )(xbc, dt, xbc, dt, nega, expand)


def _merge_ffn_kernel(x_ref, ygf_ref, ygb_ref, ysf_ref, ysb_ref, xs_ref, r_ref, z_ref, m_ref,
                      gnorm_ref, dskip_ref, snorm_ref, npost_ref, nfpre_ref, nfpost_ref,
                      wout_ref, wgate_ref, wup_ref, wdown_ref, o_ref):
    b = pl.program_id(0)
    mrow = m_ref[pl.ds(b, 1), :]
    mod = [mrow[:, D_MODEL * j:D_MODEL * (j + 1)] for j in range(6)]
    gw = SSD_INNER // SSD_GROUPS

    def mixer_out(rs):
        y_gla = ygf_ref[0, rs, :].astype(F32) + ygb_ref[0, rs, :].astype(F32)
        o_gla = jnp.concatenate(
            [_rms(y_gla[:, GLA_DV * h:GLA_DV * (h + 1)]) for h in range(GLA_HEADS)], axis=1)
        o_gla = o_gla * gnorm_ref[...] * r_ref[0, rs, :].astype(F32)
        y_ssd = ysf_ref[0, rs, :].astype(F32) + ysb_ref[0, rs, :].astype(F32)
        y_ssd = (y_ssd + dskip_ref[...] * xs_ref[0, rs, :].astype(F32)) * z_ref[0, rs, :].astype(F32)
        o_ssd = jnp.concatenate(
            [_rms(y_ssd[:, gw * g:gw * (g + 1)]) for g in range(SSD_GROUPS)], axis=1)
        o_ssd = o_ssd * snorm_ref[...]
        o = jnp.concatenate([o_gla, o_ssd], axis=1).astype(BF16)
        y = _dot(o, wout_ref[...])
        x1 = x_ref[0, rs, :] + mod[2] * (_rms(y) * npost_ref[...])
        h = ((_rms(x1) * nfpre_ref[...]) * (1.0 + mod[4]) + mod[3]).astype(BF16)
        return x1, h

    def ffn(rs, x1, h):
        act = (_silu(_dot(h, wgate_ref[...])) * _dot(h, wup_ref[...])).astype(BF16)
        f = _dot(act, wdown_ref[...])
        o_ref[0, rs, :] = x1 + mod[5] * (_rms(f) * nfpost_ref[...])

    subs = [slice(FFN_SUB * j, FFN_SUB * (j + 1)) for j in range(TM_FFN // FFN_SUB)]
    ready = mixer_out(subs[0])
    for j, rs in enumerate(subs):
        following = mixer_out(subs[j + 1]) if j + 1 < len(subs) else None
        ffn(rs, *ready)
        ready = following


def _merge_ffn(x, ygf, ygb, ysf, ysb, xbc, r, z, m_all, gnorm, dskip, snorm, npost, nfpre,
               nfpost, wout, wgate, wup, wdown):
    bsz, seq, _ = x.shape
    n_tiles = seq // TM_FFN
    const = lambda b, i: (0, 0)

    def lat(width):
        return pl.BlockSpec((1, TM_FFN, width), lambda b, i: (b, i, 0))

    single = lambda shape: pl.BlockSpec(shape, const, pipeline_mode=pl.Buffered(1))
    vec = lambda w: pl.BlockSpec((1, w), const)
    return pl.pallas_call(
        _merge_ffn_kernel,
        grid=(bsz, n_tiles),
        in_specs=[pl.BlockSpec((1, TM_FFN, D_MODEL), lambda b, i: (b, i, 0)),
                  lat(GLA_INNER), lat(GLA_INNER), lat(SSD_INNER), lat(SSD_INNER),
                  lat(SSD_INNER), lat(GLA_INNER), lat(SSD_INNER),
                  pl.BlockSpec(m_all.shape, const),
                  vec(GLA_INNER), vec(SSD_INNER), vec(SSD_INNER),
                  vec(D_MODEL), vec(D_MODEL), vec(D_MODEL),
                  single(wout.shape), single(wgate.shape), single(wup.shape), single(wdown.shape)],
        out_specs=pl.BlockSpec((1, TM_FFN, D_MODEL), lambda b, i: (b, i, 0)),
        out_shape=jax.ShapeDtypeStruct((bsz, seq, D_MODEL), F32),
        compiler_params=pltpu.CompilerParams(
            dimension_semantics=("parallel", "arbitrary"), vmem_limit_bytes=VMEM_LIMIT),
        name="merge_ffn",
    )(x, ygf, ygb, ysf, ysb, xbc, r, z, m_all, gnorm, dskip, snorm, npost, nfpre, nfpost,
      wout, wgate, wup, wdown)


def kernel(x, c, ctx, c_ctx, w_mod, b_mod, norm_mix_pre, norm_mix_post, norm_ffn_pre, norm_ffn_post,
           w_in, conv_w, conv_b, gla_wg_f, gla_bg_f, gla_wg_b, gla_bg_b, gla_norm,
           a_log_f, a_log_b, dt_bias_f, dt_bias_b, d_skip, ssd_norm, w_out, w_gate, w_up, w_down):
    assert w_mod.shape[0] == 1, "single-layer kernel"
    bsz = x.shape[0]
    row = lambda v: v.reshape(1, -1)

    c_all = jnp.concatenate([c, c_ctx[None], jnp.zeros((8 - bsz - 1, D_MODEL), F32)], axis=0)
    wg_pad = jnp.zeros((TAIL_W, 2 * GLA_QK), F32)
    wg_pad = wg_pad.at[0:GLA_RANK, 0:GLA_QK].set(gla_wg_f[0])
    wg_pad = wg_pad.at[GLA_RANK:2 * GLA_RANK, GLA_QK:].set(gla_wg_b[0]).astype(BF16)
    bg_cat = jnp.concatenate([gla_bg_f[0], gla_bg_b[0]]).reshape(1, -1)
    lane_pad = lambda f, bk: jnp.zeros((1, TAIL_W), F32).at[0, DT_LANE:DT_LANE + 2 * SSD_HEADS].set(
        jnp.concatenate([f, bk]))
    dtb_pad = lane_pad(dt_bias_f[0], dt_bias_b[0])
    nega = lane_pad(-jnp.exp(a_log_f[0]), -jnp.exp(a_log_b[0]))
    conv_w9 = conv_w[0].reshape(CONV_K * CONV_K, SSD_CONV_DIM)

    m_all = _modulation(c_all, w_mod[0], row(b_mod[0]))
    qkv, r_act, z_act, xbc, g, dt = _input_projection_conv(
        x, ctx, m_all, row(norm_mix_pre[0]), w_in[0], wg_pad, bg_cat, dtb_pad, conv_w9, row(conv_b[0]))
    ygf, ygb = _gla_scan(qkv, g)
    ysf, ysb = _ssd_scan(xbc, dt, nega)
    return _merge_ffn(
        x, ygf, ygb, ysf, ysb, xbc, r_act, z_act, m_all,
        row(jnp.tile(gla_norm[0], GLA_HEADS)), row(jnp.repeat(d_skip[0], SSD_P)), row(ssd_norm[0]),
        row(norm_mix_post[0]), row(norm_ffn_pre[0]), row(norm_ffn_post[0]),
        w_out[0].astype(BF16), w_gate[0].astype(BF16), w_up[0].astype(BF16), w_down[0].astype(BF16))
```

```python
import functools

import jax
import jax.numpy as jnp
from jax import lax
from jax.experimental import pallas as pl
from jax.experimental.pallas import tpu as pltpu

F32 = jnp.float32
BF16 = jnp.bfloat16

D_MODEL = 1024
CTX_LEN = 256
GRID_W = 64
EPS = 1e-6

GLA_HEADS = 4
GLA_DK = 64
GLA_DV = 128
GLA_QK = GLA_HEADS * GLA_DK
GLA_INNER = GLA_HEADS * GLA_DV
GLA_RANK = 16
GLA_GATE_NORM = 16.0

SSD_HEADS = 8
SSD_P = 64
SSD_N = 64
SSD_GROUPS = 2
SSD_HPG = SSD_HEADS // SSD_GROUPS
SSD_INNER = SSD_HEADS * SSD_P
SSD_BC = SSD_GROUPS * SSD_N
SSD_CONV_DIM = SSD_INNER + 2 * SSD_BC
CONV_K = 3
D_FF = 2816

QKV_W = 2 * GLA_QK + GLA_INNER
COL_R = QKV_W
COL_Z = COL_R + GLA_INNER
COL_XBC = COL_Z + SSD_INNER
COL_TAIL = COL_XBC + SSD_CONV_DIM
TAIL_W = 128
PROJ_W = COL_TAIL + TAIL_W
DT_LANE = 2 * GLA_RANK
W_IN_LR = COL_Z
W_IN_Z = W_IN_LR + 2 * GLA_RANK
W_IN_DT = W_IN_Z + SSD_INNER + SSD_CONV_DIM
W_IN_COLS = W_IN_DT + 2 * SSD_HEADS

TM_PROJ = 512
PROJ_SUB = CTX_LEN
TM_CONV = 256
TM_FFN = 512
FFN_SUB = 128
CHUNK = 128
SCAN_BATCH = 2
SUB = 32
NEG_BIG = -1e30
VMEM_LIMIT = 56 * 1024 * 1024


def _silu(x):
    return x / (1.0 + jnp.exp(-x))


def _softplus(x):
    return jnp.maximum(x, 0.0) + jnp.log(1.0 + jnp.exp(-jnp.abs(x)))


def _rms(x):
    return x * lax.rsqrt(jnp.mean(x * x, axis=-1, keepdims=True) + EPS)


def _dot(a, b):
    return jnp.dot(a, b, preferred_element_type=F32)


def _dot_nt(a, b):
    return lax.dot_general(a, b, (((1,), (1,)), ((), ())), preferred_element_type=F32)


def _dot_tn(a, b):
    return lax.dot_general(a, b, (((0,), (0,)), ((), ())), preferred_element_type=F32)


def _mod_kernel(c_ref, w_ref, b_ref, o_ref):
    s = _silu(c_ref[...])
    o_ref[...] = _dot(s.astype(BF16), w_ref[...].astype(BF16)) + b_ref[...]


def _modulation(c_all, w_mod, b_mod):
    n = w_mod.shape[1]
    tn = 1024
    return pl.pallas_call(
        _mod_kernel,
        grid=(n // tn,),
        in_specs=[pl.BlockSpec((8, D_MODEL), lambda j: (0, 0)),
                  pl.BlockSpec((D_MODEL, tn), lambda j: (0, j)),
                  pl.BlockSpec((1, tn), lambda j: (0, j))],
        out_specs=pl.BlockSpec((8, tn), lambda j: (0, j)),
        out_shape=jax.ShapeDtypeStruct((8, n), F32),
        name="adaln_mod",
    )(c_all, w_mod, b_mod)


def _permute_cast_w_in(wt_ref, w_s):
    blk = 128
    pad = jnp.zeros((TAIL_W - (W_IN_Z - W_IN_LR) - (W_IN_COLS - W_IN_DT), blk), F32)

    def body(kb, carry):
        k0 = pl.multiple_of(kb * blk, blk)
        cols = pl.ds(k0, blk)
        rows = pl.ds(k0, blk)
        w_s[rows, 0:W_IN_LR] = jnp.transpose(wt_ref[0:W_IN_LR, cols]).astype(BF16)
        w_s[rows, COL_Z:COL_TAIL] = jnp.transpose(wt_ref[W_IN_Z:W_IN_DT, cols]).astype(BF16)
        tail = jnp.concatenate([wt_ref[W_IN_LR:W_IN_Z, cols], wt_ref[W_IN_DT:W_IN_COLS, cols], pad],
                               axis=0)
        w_s[rows, COL_TAIL:PROJ_W] = jnp.transpose(tail).astype(BF16)
        return carry

    lax.fori_loop(0, D_MODEL // blk, body, 0)


def _grid_conv_rows(window, t0, is_ctx, w_ref, b_ref):
    n = window.shape[0] - 2 * GRID_W
    ext = window.astype(F32)
    row_on = jnp.where(is_ctx, 0.0, 1.0)
    taps = []
    for dx in range(CONV_K):
        tap = None
        for dy in range(CONV_K):
            wk = w_ref[CONV_K * dy + dx:CONV_K * dy + dx + 1, :]
            if dy != 1:
                wk = wk * row_on
            term = wk * ext[GRID_W * dy:GRID_W * dy + n, :]
            tap = term if tap is None else tap + term
        taps.append(tap)
    t = t0 + lax.broadcasted_iota(jnp.int32, (n, 1), 0)
    col = jnp.bitwise_and(t, GRID_W - 1)
    has_left = jnp.where(is_ctx, t, col) != 0
    has_right = jnp.where(is_ctx, t - (CTX_LEN - 1), col - (GRID_W - 1)) != 0
    acc = (taps[1] + b_ref[...]
           + jnp.where(has_left, pltpu.roll(taps[0], 1, axis=0), 0.0)
           + jnp.where(has_right, pltpu.roll(taps[2], n - 1, axis=0), 0.0))
    return _silu(acc)


def _inproj_conv_kernel(x_ref, ctx_ref, m_ref, gpre_ref, w_ref, wg_ref, bg_ref, dtb_ref,
                        cw_ref, cb_ref, qkv_ref, rz_ref, xbc_ref, g_ref, dt_ref,
                        w_s, ring, above_s):
    b = pl.program_id(0)
    i = pl.program_id(1)
    n_tiles = pl.num_programs(1) - 1
    ctx_tile = n_tiles - 1

    @pl.when(jnp.logical_and(b == 0, i == 0))
    def _():
        _permute_cast_w_in(w_ref, w_s)

    @pl.when(i == 0)
    def _():
        ring[...] = jnp.zeros_like(ring)
        above_s[...] = jnp.zeros_like(above_s)

    j = i - 1
    conv_ctx = j == ctx_tile
    split = TM_PROJ - GRID_W
    none = jnp.zeros((GRID_W, SSD_CONV_DIM), BF16)

    def conv_previous_head():
        main = ring[jnp.bitwise_and(j, 1)]
        above = jnp.where(jnp.logical_and(j >= 1, j < ctx_tile), above_s[...], none)
        window = jnp.concatenate([above, main], axis=0)
        xbc_ref[0, 0:split, :] = _grid_conv_rows(window, 0, conv_ctx, cw_ref, cb_ref).astype(BF16)
        above_s[...] = main[split:, :]

    def conv_previous_last_row(below):
        main = ring[jnp.bitwise_and(j, 1)]
        below = jnp.where(j < ctx_tile - 1, below, none)
        window = jnp.concatenate([main[split - GRID_W:, :], below], axis=0)
        xbc_ref[0, split:, :] = _grid_conv_rows(window, split, conv_ctx, cw_ref, cb_ref).astype(BF16)

    @pl.when(i < n_tiles)
    def _():
        is_ctx = i == ctx_tile
        mrow = m_ref[pl.ds(jnp.where(is_ctx, 4, b), 1), :]
        shift = mrow[:, 0:D_MODEL]
        scale = mrow[:, D_MODEL:2 * D_MODEL]
        slot = jnp.bitwise_and(i, 1)

        def normed(rs):
            x = jnp.where(is_ctx, ctx_ref[0], x_ref[0, rs, :])
            return ((_rms(x) * gpre_ref[...]) * (1.0 + scale) + shift).astype(BF16)

        def project(rs, h):
            acc = _dot(h, w_s[...])
            qkv_ref[0, rs, 0:GLA_QK] = (acc[:, 0:GLA_QK] * (GLA_DK ** -0.5)).astype(BF16)
            qkv_ref[0, rs, GLA_QK:QKV_W] = acc[:, GLA_QK:QKV_W].astype(BF16)
            rz_ref[0, rs, :] = _silu(acc[:, COL_R:COL_XBC]).astype(BF16)
            tail = acc[:, COL_TAIL:PROJ_W]
            logits = _dot(tail.astype(BF16), wg_ref[...]) + bg_ref[...]
            g_ref[0, rs, :] = -_softplus(-logits) * (1.0 / GLA_GATE_NORM)
            dt_ref[0, rs, :] = _softplus(tail + dtb_ref[...])
            return acc[:, COL_XBC:COL_TAIL].astype(BF16)

        subs = [slice(PROJ_SUB * s, PROJ_SUB * (s + 1)) for s in range(TM_PROJ // PROJ_SUB)]
        ready = normed(subs[0])
        fresh = []
        for s, rs in enumerate(subs):
            following = normed(subs[s + 1]) if s + 1 < len(subs) else None
            fresh.append(project(rs, ready))
            if s == 0:
                conv_previous_head()
                conv_previous_last_row(fresh[0][0:GRID_W, :])
            ready = following
        for rs, xbc_new in zip(subs, fresh):
            ring[slot, rs, :] = xbc_new

    @pl.when(i == n_tiles)
    def _():
        conv_previous_head()
        conv_previous_last_row(none)


def _input_projection_conv(x, ctx, m_all, gpre, w_in, wg_pad, bg_cat, dtb_pad, conv_w9, conv_b):
    bsz, seq, _ = x.shape
    n_lat = seq // TM_PROJ
    n_tiles = n_lat + 1
    tt = CTX_LEN + seq
    const = lambda b, i: (0, 0)
    tok = lambda b, i: (b, jnp.minimum(i, n_tiles - 1), 0)
    conv_tok = lambda b, i: (b, jnp.maximum(i - 1, 0), 0)
    return pl.pallas_call(
        _inproj_conv_kernel,
        grid=(bsz, n_tiles + 1),
        in_specs=[pl.BlockSpec((1, TM_PROJ, D_MODEL), lambda b, i: (b, jnp.minimum(i, n_lat - 1), 0)),
                  pl.BlockSpec((1, CTX_LEN, D_MODEL), lambda b, i: (b, 0, 0)),
                  pl.BlockSpec(m_all.shape, const),
                  pl.BlockSpec((1, D_MODEL), const),
                  pl.BlockSpec((W_IN_COLS, D_MODEL), const, pipeline_mode=pl.Buffered(1)),
                  pl.BlockSpec((TAIL_W, 2 * GLA_QK), const),
                  pl.BlockSpec((1, 2 * GLA_QK), const),
                  pl.BlockSpec((1, TAIL_W), const),
                  pl.BlockSpec((CONV_K * CONV_K, SSD_CONV_DIM), const),
                  pl.BlockSpec((1, SSD_CONV_DIM), const)],
        out_specs=[pl.BlockSpec((1, TM_PROJ, QKV_W), tok),
                   pl.BlockSpec((1, TM_PROJ, GLA_INNER + SSD_INNER), tok),
                   pl.BlockSpec((1, TM_PROJ, SSD_CONV_DIM), conv_tok),
                   pl.BlockSpec((1, TM_PROJ, 2 * GLA_QK), tok),
                   pl.BlockSpec((1, TM_PROJ, TAIL_W), tok)],
        out_shape=[jax.ShapeDtypeStruct((bsz, tt, QKV_W), BF16),
                   jax.ShapeDtypeStruct((bsz, tt, GLA_INNER + SSD_INNER), BF16),
                   jax.ShapeDtypeStruct((bsz, tt, SSD_CONV_DIM), BF16),
                   jax.ShapeDtypeStruct((bsz, tt, 2 * GLA_QK), F32),
                   jax.ShapeDtypeStruct((bsz, tt, TAIL_W), F32)],
        scratch_shapes=[pltpu.VMEM((D_MODEL, PROJ_W), BF16),
                        pltpu.VMEM((2, TM_PROJ, SSD_CONV_DIM), BF16),
                        pltpu.VMEM((GRID_W, SSD_CONV_DIM), BF16)],
        compiler_params=pltpu.CompilerParams(
            dimension_semantics=("arbitrary", "arbitrary"), vmem_limit_bytes=VMEM_LIMIT),
        name="in_proj_conv",
    )(x, ctx, m_all, gpre, w_in, wg_pad, bg_cat, dtb_pad, conv_w9, conv_b)


def _cumsum_rows(x, reverse):
    n = x.shape[0]
    row = lax.broadcasted_iota(jnp.int32, x.shape, 0)
    s = 1
    while s < n:
        if reverse:
            x = x + jnp.where(row < n - s, pltpu.roll(x, n - s, axis=0), 0.0)
        else:
            x = x + jnp.where(row >= s, pltpu.roll(x, s, axis=0), 0.0)
        s *= 2
    return x


def _chunk_maps(n_ctx, n_all):
    n_lat = n_all - n_ctx
    fwd = lambda s: jnp.where(s < n_ctx, n_lat + s, s - n_ctx)
    bwd = lambda s: n_all - 1 - s
    return fwd, bwd


def _gla_prepare(qkv, g, reverse):
    n = qkv.shape[0]
    nb = n // SUB
    blocks = range(nb)
    rows = lambda a, blk: a[SUB * blk:SUB * (blk + 1), :]
    q = qkv[:, 0:GLA_QK].astype(F32)
    k = qkv[:, GLA_QK:2 * GLA_QK].astype(F32)
    cum = _cumsum_rows(g, reverse)
    zero = jnp.zeros((1, GLA_QK), F32)
    if reverse:
        ends = [cum[SUB * blk:SUB * blk + 1, :] for blk in blocks]
        refs = ends[1:] + [zero]
        earlier = lambda j, blk: j > blk
        tot = ends[0]
    else:
        ends = [cum[SUB * (blk + 1) - 1:SUB * (blk + 1), :] for blk in blocks]
        refs = [zero] + ends[:-1]
        earlier = lambda j, blk: j < blk
        tot = ends[nb - 1]
    stack = lambda vs: jnp.concatenate([jnp.broadcast_to(v, (SUB, GLA_QK)) for v in vs], axis=0)
    e_in = cum - stack(refs)
    q_blk = q * jnp.exp(e_in)
    k_diag = k * jnp.exp(-e_in)
    k_end = k * jnp.exp(stack(ends) - cum)
    q_in = jnp.concatenate([rows(q_blk, blk) * jnp.exp(refs[blk]) for blk in blocks], axis=0)
    k_out = jnp.concatenate([rows(k_end, blk) * jnp.exp(tot - ends[blk]) for blk in blocks], axis=0)
    q_in = q_in.astype(BF16)
    k_out = k_out.astype(BF16)
    q_bf = q_blk.astype(BF16)
    zeros_piece = jnp.zeros((SUB, GLA_QK), BF16)
    k_piece = [[rows(k_diag, j).astype(BF16) if seg == j
                else (rows(k_end, j) * jnp.exp(refs[seg] - ends[j])).astype(BF16) if earlier(j, seg)
                else zeros_piece
                for seg in blocks] for j in blocks]
    ii = lax.broadcasted_iota(jnp.int32, (n, n), 0)
    jj = lax.broadcasted_iota(jnp.int32, (n, n), 1)
    heads = []
    for h in range(GLA_HEADS):
        ks = slice(GLA_DK * h, GLA_DK * (h + 1))
        q_cat = jnp.concatenate(
            [jnp.concatenate([rows(q_bf, blk)[:, ks] if seg == blk else zeros_piece[:, ks]
                              for seg in blocks], axis=1) for blk in blocks], axis=0)
        k_cat = jnp.concatenate(
            [jnp.concatenate([k_piece[j][seg][:, ks] for seg in blocks], axis=1)
             for j in blocks], axis=0)
        heads.append(dict(q_cat=q_cat, k_cat=k_cat, q_in=q_in[:, ks], k_out=k_out[:, ks],
                          v=qkv[:, 2 * GLA_QK + GLA_DV * h:2 * GLA_QK + GLA_DV * (h + 1)]))
    return dict(heads=heads, causal=(ii <= jj) if reverse else (ii >= jj),
                decay_col=jnp.transpose(jnp.broadcast_to(tot, (GLA_DV, GLA_QK))))


def _gla_kernel(qkv_f_ref, g_f_ref, qkv_b_ref, g_b_ref, yf_ref, yb_ref, sf_ref, sb_ref):
    @pl.when(pl.program_id(1) == 0)
    def _():
        sf_ref[...] = jnp.zeros_like(sf_ref)
        sb_ref[...] = jnp.zeros_like(sb_ref)

    chains = []
    for e in range(qkv_f_ref.shape[0]):
        chains.append((qkv_f_ref.at[e], g_f_ref.at[e], False, sf_ref.at[e], yf_ref.at[e]))
        chains.append((qkv_b_ref.at[e], g_b_ref.at[e], True, sb_ref.at[e], yb_ref.at[e]))

    def start(chain):
        qkv_ref, g_ref, reverse, _, _ = chain
        p = _gla_prepare(qkv_ref[...], g_ref[...], reverse)
        return p, [_dot_nt(hd["q_cat"], hd["k_cat"]) for hd in p["heads"]]

    def finish(chain, p, att):
        _, _, _, s_ref, y_ref = chain
        outs = []
        for h, (hd, a) in enumerate(zip(p["heads"], att)):
            state = s_ref[h]
            lhs = jnp.concatenate([jnp.where(p["causal"], a, 0.0).astype(BF16), hd["q_in"]], axis=1)
            rhs = jnp.concatenate([hd["v"], state.astype(BF16)], axis=0)
            outs.append(_dot(lhs, rhs))
            ks = slice(GLA_DK * h, GLA_DK * (h + 1))
            s_ref[h] = state * jnp.exp(p["decay_col"][ks, :]) + _dot_tn(hd["k_out"], hd["v"])
        y_ref[...] = jnp.concatenate(outs, axis=1).astype(y_ref.dtype)

    started = start(chains[0])
    for i, chain in enumerate(chains):
        following = start(chains[i + 1]) if i + 1 < len(chains) else None
        finish(chain, *started)
        started = following


def _gla_scan(qkv, g):
    bsz, tt, _ = qkv.shape
    n_all = tt // CHUNK
    fwd, bwd = _chunk_maps(CTX_LEN // CHUNK, n_all)
    spec = lambda w, m, c: pl.BlockSpec((SCAN_BATCH, CHUNK, w), lambda b, s: (b, m(s), c))
    return pl.pallas_call(
        _gla_kernel,
        grid=(bsz // SCAN_BATCH, n_all),
        in_specs=[spec(QKV_W, fwd, 0), spec(GLA_QK, fwd, 0), spec(QKV_W, bwd, 0), spec(GLA_QK, bwd, 1)],
        out_specs=[spec(GLA_INNER, fwd, 0), spec(GLA_INNER, bwd, 0)],
        out_shape=[jax.ShapeDtypeStruct((bsz, tt, GLA_INNER), BF16)] * 2,
        scratch_shapes=[pltpu.VMEM((SCAN_BATCH, GLA_HEADS, GLA_DK, GLA_DV), F32)] * 2,
        compiler_params=pltpu.CompilerParams(
            dimension_semantics=("parallel", "arbitrary"), vmem_limit_bytes=VMEM_LIMIT),
        name="gla_scan",
    )(qkv, g, qkv, g)


def _ssd_prepare(xbc, dt, nega, expand, lane0, reverse):
    n = xbc.shape[0]
    end = 0 if reverse else n - 1
    cum = _cumsum_rows(dt * nega, reverse)
    e_cum = jnp.exp(cum)
    w = jnp.exp(cum[end:end + 1, :] - cum) * dt
    e_cum_x = _dot(_split_hi_lo(e_cum), expand)
    x_w = (xbc[:, 0:SSD_INNER].astype(F32) * _dot(_split_hi_lo(w), expand)).astype(BF16)
    cum_t = jnp.transpose(cum)
    dt_t = jnp.transpose(dt)
    ii = lax.broadcasted_iota(jnp.int32, (n, n), 0)
    jj = lax.broadcasted_iota(jnp.int32, (n, n), 1)
    causal = (ii <= jj) if reverse else (ii >= jj)
    first_head = lax.broadcasted_iota(jnp.int32, (n, 2 * SSD_P), 1) < SSD_P
    groups = []
    for grp in range(SSD_GROUPS):
        b_g = xbc[:, SSD_INNER + SSD_N * grp:SSD_INNER + SSD_N * (grp + 1)]
        c_g = xbc[:, SSD_INNER + SSD_BC + SSD_N * grp:SSD_INNER + SSD_BC + SSD_N * (grp + 1)]
        scores = _dot_nt(c_g, b_g)
        pairs = []
        for pair in range(SSD_HPG // 2):
            m_pair = []
            for hh in range(2):
                lane = lane0 + SSD_HPG * grp + 2 * pair + hh
                cum_b = jnp.broadcast_to(cum[:, lane:lane + 1], (n, n))
                seg = jnp.where(causal, cum_b - cum_t[lane:lane + 1, :], NEG_BIG)
                m_pair.append((scores * jnp.exp(seg) * dt_t[lane:lane + 1, :]).astype(BF16))
            col = 2 * SSD_P * (SSD_HPG // 2 * grp + pair)
            x_pair = xbc[:, col:col + 2 * SSD_P]
            none = jnp.zeros_like(x_pair)
            rhs = jnp.concatenate([jnp.where(first_head, x_pair, none),
                                   jnp.where(first_head, none, x_pair)], axis=0)
            pairs.append((jnp.concatenate(m_pair, axis=1), rhs))
        cols = slice(SSD_HPG * SSD_P * grp, SSD_HPG * SSD_P * (grp + 1))
        groups.append(dict(b=b_g, c=c_g, pairs=pairs, e_cum=e_cum_x[:, cols],
                           e_tot=e_cum_x[end:end + 1, cols], x_w=x_w[:, cols]))
    return groups


def _split_hi_lo(a):
    hi = a.astype(BF16)
    lo = (a - hi.astype(F32)).astype(BF16)
    return jnp.concatenate([hi, lo], axis=1)


def _ssd_kernel(xbc_f_ref, dt_f_ref, xbc_b_ref, dt_b_ref, nega_ref, expand_ref,
                yf_ref, yb_ref, sf_ref, sb_ref):
    @pl.when(pl.program_id(1) == 0)
    def _():
        sf_ref[...] = jnp.zeros_like(sf_ref)
        sb_ref[...] = jnp.zeros_like(sb_ref)

    nega = nega_ref[...]
    chains = []
    for e in range(xbc_f_ref.shape[0]):
        chains.append((xbc_f_ref.at[e], dt_f_ref.at[e], 0, False, sf_ref.at[e], yf_ref.at[e]))
        chains.append((xbc_b_ref.at[e], dt_b_ref.at[e], 1, True, sb_ref.at[e], yb_ref.at[e]))

    def start(chain):
        xbc_ref, dt_ref, d, reverse, _, _ = chain
        return _ssd_prepare(xbc_ref[...], dt_ref[...], nega, expand_ref[d],
                            DT_LANE + SSD_HEADS * d, reverse)

    def finish(chain, groups):
        s_ref, y_ref = chain[4], chain[5]
        outs = []
        for grp, gd in enumerate(groups):
            state = s_ref[grp]
            intra = jnp.concatenate([_dot(m, rhs) for m, rhs in gd["pairs"]], axis=1)
            outs.append(_dot(gd["c"], state.astype(BF16)) * gd["e_cum"] + intra)
            s_ref[grp] = state * gd["e_tot"] + _dot_tn(gd["b"], gd["x_w"])
        y_ref[...] = jnp.concatenate(outs, axis=1).astype(y_ref.dtype)

    started = start(chains[0])
    for i, chain in enumerate(chains):
        following = start(chains[i + 1]) if i + 1 < len(chains) else None
        finish(chain, started)
        started = following


def _head_expand_matrix():
    r = jnp.arange(2 * TAIL_W)[:, None] % TAIL_W
    c = jnp.arange(SSD_INNER)[None, :] // SSD_P
    return jnp.stack([(r == DT_LANE + SSD_HEADS * d + c) for d in range(2)]).astype(BF16)


def _ssd_scan(xbc, dt, nega):
    bsz, tt, _ = xbc.shape
    n_all = tt // CHUNK
    fwd, bwd = _chunk_maps(CTX_LEN // CHUNK, n_all)
    spec = lambda w, m: pl.BlockSpec((SCAN_BATCH, CHUNK, w), lambda b, s: (b, m(s), 0))
    expand = _head_expand_matrix()
    return pl.pallas_call(
        _ssd_kernel,
        grid=(bsz // SCAN_BATCH, n_all),
        in_specs=[spec(SSD_CONV_DIM, fwd), spec(TAIL_W, fwd),
                  spec(SSD_CONV_DIM, bwd), spec(TAIL_W, bwd),
                  pl.BlockSpec((1, TAIL_W), lambda b, s: (0, 0)),
                  pl.BlockSpec(expand.shape, lambda b, s: (0, 0, 0))],
        out_specs=[spec(SSD_INNER, fwd), spec(SSD_INNER, bwd)],
        out_shape=[jax.ShapeDtypeStruct((bsz, tt, SSD_INNER), BF16)] * 2,
        scratch_shapes=[pltpu.VMEM((SCAN_BATCH, SSD_GROUPS, SSD_N, SSD_HPG * SSD_P), F32)] * 2,
        compiler_params=pltpu.CompilerParams(
            dimension_semantics=("parallel", "arbitrary"), vmem_limit_bytes=VMEM_LIMIT),
        name="ssd_scan",
    )(xbc, dt, xbc, dt, nega, expand)


def _merge_ffn_kernel(x_ref, ygf_ref, ygb_ref, ysf_ref, ysb_ref, xs_ref, rz_ref, m_ref,
                      gnorm_ref, dskip_ref, snorm_ref, npost_ref, nfpre_ref, nfpost_ref,
                      wout_ref, wgate_ref, wup_ref, wdown_ref, o_ref):
    b = pl.program_id(0)
    mrow = m_ref[pl.ds(b, 1), :]
    mod = [mrow[:, D_MODEL * j:D_MODEL * (j + 1)] for j in range(6)]
    gw = SSD_INNER // SSD_GROUPS

    def mixer_out(rs):
        y_gla = ygf_ref[0, rs, :].astype(F32) + ygb_ref[0, rs, :].astype(F32)
        o_gla = jnp.concatenate(
            [_rms(y_gla[:, GLA_DV * h:GLA_DV * (h + 1)]) for h in range(GLA_HEADS)], axis=1)
        o_gla = o_gla * gnorm_ref[...] * rz_ref[0, rs, 0:GLA_INNER].astype(F32)
        y_ssd = ysf_ref[0, rs, :].astype(F32) + ysb_ref[0, rs, :].astype(F32)
        y_ssd = ((y_ssd + dskip_ref[...] * xs_ref[0, rs, :].astype(F32))
                 * rz_ref[0, rs, GLA_INNER:].astype(F32))
        o_ssd = jnp.concatenate(
            [_rms(y_ssd[:, gw * g:gw * (g + 1)]) for g in range(SSD_GROUPS)], axis=1)
        o_ssd = o_ssd * snorm_ref[...]
        o = jnp.concatenate([o_gla, o_ssd], axis=1).astype(BF16)
        y = _dot(o, wout_ref[...])
        x1 = x_ref[0, rs, :] + mod[2] * (_rms(y) * npost_ref[...])
        h = ((_rms(x1) * nfpre_ref[...]) * (1.0 + mod[4]) + mod[3]).astype(BF16)
        return x1, h

    def ffn(rs, x1, h):
        act = (_silu(_dot(h, wgate_ref[...])) * _dot(h, wup_ref[...])).astype(BF16)
        f = _dot(act, wdown_ref[...])
        o_ref[0, rs, :] = x1 + mod[5] * (_rms(f) * nfpost_ref[...])

    subs = [slice(FFN_SUB * j, FFN_SUB * (j + 1)) for j in range(TM_FFN // FFN_SUB)]
    ready = mixer_out(subs[0])
    for j, rs in enumerate(subs):
        following = mixer_out(subs[j + 1]) if j + 1 < len(subs) else None
        ffn(rs, *ready)
        ready = following


def _merge_ffn(x, ygf, ygb, ysf, ysb, xbc, rz, m_all, gnorm, dskip, snorm, npost, nfpre,
               nfpost, wout, wgate, wup, wdown):
    bsz, seq, _ = x.shape
    n_tiles = seq // TM_FFN
    const = lambda b, i: (0, 0)

    def lat(width):
        return pl.BlockSpec((1, TM_FFN, width), lambda b, i: (b, i, 0))

    single = lambda shape: pl.BlockSpec(shape, const, pipeline_mode=pl.Buffered(1))
    vec = lambda w: pl.BlockSpec((1, w), const)
    return pl.pallas_call(
        _merge_ffn_kernel,
        grid=(bsz, n_tiles),
        in_specs=[pl.BlockSpec((1, TM_FFN, D_MODEL), lambda b, i: (b, i, 0)),
                  lat(GLA_INNER), lat(GLA_INNER), lat(SSD_INNER), lat(SSD_INNER),
                  lat(SSD_INNER), lat(GLA_INNER + SSD_INNER),
                  pl.BlockSpec(m_all.shape, const),
                  vec(GLA_INNER), vec(SSD_INNER), vec(SSD_INNER),
                  vec(D_MODEL), vec(D_MODEL), vec(D_MODEL),
                  single(wout.shape), single(wgate.shape), single(wup.shape), single(wdown.shape)],
        out_specs=pl.BlockSpec((1, TM_FFN, D_MODEL), lambda b, i: (b, i, 0)),
        out_shape=jax.ShapeDtypeStruct((bsz, seq, D_MODEL), F32),
        compiler_params=pltpu.CompilerParams(
            dimension_semantics=("parallel", "arbitrary"), vmem_limit_bytes=VMEM_LIMIT),
        name="merge_ffn",
    )(x, ygf, ygb, ysf, ysb, xbc, rz, m_all, gnorm, dskip, snorm, npost, nfpre, nfpost,
      wout, wgate, wup, wdown)


def kernel(x, c, ctx, c_ctx, w_mod, b_mod, norm_mix_pre, norm_mix_post, norm_ffn_pre, norm_ffn_post,
           w_in, conv_w, conv_b, gla_wg_f, gla_bg_f, gla_wg_b, gla_bg_b, gla_norm,
           a_log_f, a_log_b, dt_bias_f, dt_bias_b, d_skip, ssd_norm, w_out, w_gate, w_up, w_down):
    assert w_mod.shape[0] == 1, "single-layer kernel"
    bsz = x.shape[0]
    row = lambda v: v.reshape(1, -1)

    c_all = jnp.concatenate([c, c_ctx[None], jnp.zeros((8 - bsz - 1, D_MODEL), F32)], axis=0)
    wg_pad = jnp.zeros((TAIL_W, 2 * GLA_QK), F32)
    wg_pad = wg_pad.at[0:GLA_RANK, 0:GLA_QK].set(gla_wg_f[0])
    wg_pad = wg_pad.at[GLA_RANK:2 * GLA_RANK, GLA_QK:].set(gla_wg_b[0]).astype(BF16)
    bg_cat = jnp.concatenate([gla_bg_f[0], gla_bg_b[0]]).reshape(1, -1)
    lane_pad = lambda f, bk: jnp.zeros((1, TAIL_W), F32).at[0, DT_LANE:DT_LANE + 2 * SSD_HEADS].set(
        jnp.concatenate([f, bk]))
    dtb_pad = lane_pad(dt_bias_f[0], dt_bias_b[0])
    nega = lane_pad(-jnp.exp(a_log_f[0]), -jnp.exp(a_log_b[0]))
    conv_w9 = conv_w[0].reshape(CONV_K * CONV_K, SSD_CONV_DIM)

    m_all = _modulation(c_all, w_mod[0], row(b_mod[0]))
    qkv, rz, xbc, g, dt = _input_projection_conv(
        x, ctx, m_all, row(norm_mix_pre[0]), w_in[0].T, wg_pad, bg_cat, dtb_pad, conv_w9, row(conv_b[0]))
    ygf, ygb = _gla_scan(qkv, g)
    ysf, ysb = _ssd_scan(xbc, dt, nega)
    return _merge_ffn(
        x, ygf, ygb, ysf, ysb, xbc, rz, m_all,
        row(jnp.tile(gla_norm[0], GLA_HEADS)), row(jnp.repeat(d_skip[0], SSD_P)), row(ssd_norm[0]),
        row(norm_mix_post[0]), row(norm_ffn_pre[0]), row(norm_ffn_post[0]),
        w_out[0].astype(BF16), w_gate[0].astype(BF16), w_up[0].astype(BF16), w_down[0].astype(BF16))
```

```python
import functools

import jax
import jax.numpy as jnp
from jax import lax
from jax.experimental import pallas as pl
from jax.experimental.pallas import tpu as pltpu

F32 = jnp.float32
BF16 = jnp.bfloat16

D_MODEL = 1024
CTX_LEN = 256
GRID_W = 64
EPS = 1e-6

GLA_HEADS = 4
GLA_DK = 64
GLA_DV = 128
GLA_QK = GLA_HEADS * GLA_DK
GLA_INNER = GLA_HEADS * GLA_DV
GLA_RANK = 16
GLA_GATE_NORM = 16.0

SSD_HEADS = 8
SSD_P = 64
SSD_N = 64
SSD_GROUPS = 2
SSD_HPG = SSD_HEADS // SSD_GROUPS
SSD_INNER = SSD_HEADS * SSD_P
SSD_BC = SSD_GROUPS * SSD_N
SSD_CONV_DIM = SSD_INNER + 2 * SSD_BC
CONV_K = 3
D_FF = 2816

QKV_W = 2 * GLA_QK + GLA_INNER
COL_R = QKV_W
COL_Z = COL_R + GLA_INNER
COL_XBC = COL_Z + SSD_INNER
COL_TAIL = COL_XBC + SSD_CONV_DIM
TAIL_W = 128
PROJ_W = COL_TAIL + TAIL_W
DT_LANE = 2 * GLA_RANK
W_IN_LR = COL_Z
W_IN_Z = W_IN_LR + 2 * GLA_RANK
W_IN_DT = W_IN_Z + SSD_INNER + SSD_CONV_DIM
W_IN_COLS = W_IN_DT + 2 * SSD_HEADS

TM_PROJ = 512
PROJ_SUB = CTX_LEN
TM_CONV = 256
TM_FFN = 512
FFN_SUB = 128
CHUNK = 128
SCAN_BATCH = 4
SUB = 32
NEG_BIG = -1e30
LOG2_E = 1.4426950408889634
VMEM_LIMIT = 56 * 1024 * 1024


def _silu(x):
    return x / (1.0 + jnp.exp(-x))


def _softplus(x):
    return jnp.maximum(x, 0.0) + jnp.log(1.0 + jnp.exp(-jnp.abs(x)))


def _rms(x):
    return x * lax.rsqrt(jnp.mean(x * x, axis=-1, keepdims=True) + EPS)


def _dot(a, b):
    return jnp.dot(a, b, preferred_element_type=F32)


def _dot_nt(a, b):
    return lax.dot_general(a, b, (((1,), (1,)), ((), ())), preferred_element_type=F32)


def _dot_tn(a, b):
    return lax.dot_general(a, b, (((0,), (0,)), ((), ())), preferred_element_type=F32)


def _mod_kernel(c_ref, w_ref, b_ref, o_ref):
    s = _silu(c_ref[...])
    o_ref[...] = _dot(s.astype(BF16), w_ref[...].astype(BF16)) + b_ref[...]


def _modulation(c_all, w_mod, b_mod):
    n = w_mod.shape[1]
    tn = 1024
    return pl.pallas_call(
        _mod_kernel,
        grid=(n // tn,),
        in_specs=[pl.BlockSpec((8, D_MODEL), lambda j: (0, 0)),
                  pl.BlockSpec((D_MODEL, tn), lambda j: (0, j)),
                  pl.BlockSpec((1, tn), lambda j: (0, j))],
        out_specs=pl.BlockSpec((8, tn), lambda j: (0, j)),
        out_shape=jax.ShapeDtypeStruct((8, n), F32),
        name="adaln_mod",
    )(c_all, w_mod, b_mod)


def _permute_cast_w_in(wt_ref, w_s):
    blk = 128
    pad = jnp.zeros((TAIL_W - (W_IN_Z - W_IN_LR) - (W_IN_COLS - W_IN_DT), blk), F32)

    def body(kb, carry):
        k0 = pl.multiple_of(kb * blk, blk)
        cols = pl.ds(k0, blk)
        rows = pl.ds(k0, blk)
        w_s[rows, 0:W_IN_LR] = jnp.transpose(wt_ref[0:W_IN_LR, cols]).astype(BF16)
        w_s[rows, COL_Z:COL_TAIL] = jnp.transpose(wt_ref[W_IN_Z:W_IN_DT, cols]).astype(BF16)
        tail = jnp.concatenate([wt_ref[W_IN_LR:W_IN_Z, cols], wt_ref[W_IN_DT:W_IN_COLS, cols], pad],
                               axis=0)
        w_s[rows, COL_TAIL:PROJ_W] = jnp.transpose(tail).astype(BF16)
        return carry

    lax.fori_loop(0, D_MODEL // blk, body, 0)


def _grid_conv_rows(window, t0, is_ctx, w_ref, b_ref):
    n = window.shape[0] - 2 * GRID_W
    ext = window.astype(F32)
    row_on = jnp.where(is_ctx, 0.0, 1.0)
    taps = []
    for dx in range(CONV_K):
        tap = None
        for dy in range(CONV_K):
            wk = w_ref[CONV_K * dy + dx:CONV_K * dy + dx + 1, :]
            if dy != 1:
                wk = wk * row_on
            term = wk * ext[GRID_W * dy:GRID_W * dy + n, :]
            tap = term if tap is None else tap + term
        taps.append(tap)
    t = t0 + lax.broadcasted_iota(jnp.int32, (n, 1), 0)
    col = jnp.bitwise_and(t, GRID_W - 1)
    has_left = jnp.where(is_ctx, t, col) != 0
    has_right = jnp.where(is_ctx, t - (CTX_LEN - 1), col - (GRID_W - 1)) != 0
    acc = (taps[1] + b_ref[...]
           + jnp.where(has_left, pltpu.roll(taps[0], 1, axis=0), 0.0)
           + jnp.where(has_right, pltpu.roll(taps[2], n - 1, axis=0), 0.0))
    return _silu(acc)


def _inproj_conv_kernel(x_ref, ctx_ref, m_ref, gpre_ref, w_ref, wg_ref, bg_ref, dtb_ref,
                        cw_ref, cb_ref, qkv_ref, rz_ref, xbc_ref, g_ref, dt_ref,
                        w_s, ring, above_s):
    b = pl.program_id(0)
    i = pl.program_id(1)
    n_tiles = pl.num_programs(1) - 1
    ctx_tile = n_tiles - 1

    @pl.when(jnp.logical_and(b == 0, i == 0))
    def _():
        _permute_cast_w_in(w_ref, w_s)

    @pl.when(i == 0)
    def _():
        ring[...] = jnp.zeros_like(ring)
        above_s[...] = jnp.zeros_like(above_s)

    j = i - 1
    conv_ctx = j == ctx_tile
    split = TM_PROJ - GRID_W
    none = jnp.zeros((GRID_W, SSD_CONV_DIM), BF16)

    def conv_previous_head():
        main = ring[jnp.bitwise_and(j, 1)]
        above = jnp.where(jnp.logical_and(j >= 1, j < ctx_tile), above_s[...], none)
        window = jnp.concatenate([above, main], axis=0)
        xbc_ref[0, 0:split, :] = _grid_conv_rows(window, 0, conv_ctx, cw_ref, cb_ref).astype(BF16)
        above_s[...] = main[split:, :]

    def conv_previous_last_row(below):
        main = ring[jnp.bitwise_and(j, 1)]
        below = jnp.where(j < ctx_tile - 1, below, none)
        window = jnp.concatenate([main[split - GRID_W:, :], below], axis=0)
        xbc_ref[0, split:, :] = _grid_conv_rows(window, split, conv_ctx, cw_ref, cb_ref).astype(BF16)

    @pl.when(i < n_tiles)
    def _():
        is_ctx = i == ctx_tile
        mrow = m_ref[pl.ds(jnp.where(is_ctx, 4, b), 1), :]
        shift = mrow[:, 0:D_MODEL]
        scale = mrow[:, D_MODEL:2 * D_MODEL]
        slot = jnp.bitwise_and(i, 1)

        def normed(rs):
            x = jnp.where(is_ctx, ctx_ref[0], x_ref[0, rs, :])
            return ((_rms(x) * gpre_ref[...]) * (1.0 + scale) + shift).astype(BF16)

        def project(rs, h):
            acc = _dot(h, w_s[...])
            qkv_ref[0, rs, 0:GLA_QK] = (acc[:, 0:GLA_QK] * (GLA_DK ** -0.5)).astype(BF16)
            qkv_ref[0, rs, GLA_QK:QKV_W] = acc[:, GLA_QK:QKV_W].astype(BF16)
            rz_ref[0, rs, :] = _silu(acc[:, COL_R:COL_XBC]).astype(BF16)
            tail = acc[:, COL_TAIL:PROJ_W]
            logits = _dot(tail.astype(BF16), wg_ref[...]) + bg_ref[...]
            g_ref[0, rs, :] = -_softplus(-logits) * (LOG2_E / GLA_GATE_NORM)
            dt_ref[0, rs, :] = _softplus(tail + dtb_ref[...])
            return acc[:, COL_XBC:COL_TAIL].astype(BF16)

        subs = [slice(PROJ_SUB * s, PROJ_SUB * (s + 1)) for s in range(TM_PROJ // PROJ_SUB)]
        ready = normed(subs[0])
        fresh = []
        for s, rs in enumerate(subs):
            following = normed(subs[s + 1]) if s + 1 < len(subs) else None
            fresh.append(project(rs, ready))
            if s == 0:
                conv_previous_head()
                conv_previous_last_row(fresh[0][0:GRID_W, :])
            ready = following
        for rs, xbc_new in zip(subs, fresh):
            ring[slot, rs, :] = xbc_new

    @pl.when(i == n_tiles)
    def _():
        conv_previous_head()
        conv_previous_last_row(none)


def _input_projection_conv(x, ctx, m_all, gpre, w_in, wg_pad, bg_cat, dtb_pad, conv_w9, conv_b):
    bsz, seq, _ = x.shape
    n_lat = seq // TM_PROJ
    n_tiles = n_lat + 1
    tt = CTX_LEN + seq
    const = lambda b, i: (0, 0)
    tok = lambda b, i: (b, jnp.minimum(i, n_tiles - 1), 0)
    conv_tok = lambda b, i: (b, jnp.maximum(i - 1, 0), 0)
    return pl.pallas_call(
        _inproj_conv_kernel,
        grid=(bsz, n_tiles + 1),
        in_specs=[pl.BlockSpec((1, TM_PROJ, D_MODEL), lambda b, i: (b, jnp.minimum(i, n_lat - 1), 0)),
                  pl.BlockSpec((1, CTX_LEN, D_MODEL), lambda b, i: (b, 0, 0)),
                  pl.BlockSpec(m_all.shape, const),
                  pl.BlockSpec((1, D_MODEL), const),
                  pl.BlockSpec((W_IN_COLS, D_MODEL), const, pipeline_mode=pl.Buffered(1)),
                  pl.BlockSpec((TAIL_W, 2 * GLA_QK), const),
                  pl.BlockSpec((1, 2 * GLA_QK), const),
                  pl.BlockSpec((1, TAIL_W), const),
                  pl.BlockSpec((CONV_K * CONV_K, SSD_CONV_DIM), const),
                  pl.BlockSpec((1, SSD_CONV_DIM), const)],
        out_specs=[pl.BlockSpec((1, TM_PROJ, QKV_W), tok),
                   pl.BlockSpec((1, TM_PROJ, GLA_INNER + SSD_INNER), tok),
                   pl.BlockSpec((1, TM_PROJ, SSD_CONV_DIM), conv_tok),
                   pl.BlockSpec((1, TM_PROJ, 2 * GLA_QK), tok),
                   pl.BlockSpec((1, TM_PROJ, TAIL_W), tok)],
        out_shape=[jax.ShapeDtypeStruct((bsz, tt, QKV_W), BF16),
                   jax.ShapeDtypeStruct((bsz, tt, GLA_INNER + SSD_INNER), BF16),
                   jax.ShapeDtypeStruct((bsz, tt, SSD_CONV_DIM), BF16),
                   jax.ShapeDtypeStruct((bsz, tt, 2 * GLA_QK), F32),
                   jax.ShapeDtypeStruct((bsz, tt, TAIL_W), F32)],
        scratch_shapes=[pltpu.VMEM((D_MODEL, PROJ_W), BF16),
                        pltpu.VMEM((2, TM_PROJ, SSD_CONV_DIM), BF16),
                        pltpu.VMEM((GRID_W, SSD_CONV_DIM), BF16)],
        compiler_params=pltpu.CompilerParams(
            dimension_semantics=("arbitrary", "arbitrary"), vmem_limit_bytes=VMEM_LIMIT),
        name="in_proj_conv",
    )(x, ctx, m_all, gpre, w_in, wg_pad, bg_cat, dtb_pad, conv_w9, conv_b)


def _cumsum_rows(x, reverse):
    n = x.shape[0]
    row = lax.broadcasted_iota(jnp.int32, x.shape, 0)
    s = 1
    while s < n:
        if reverse:
            x = x + jnp.where(row < n - s, pltpu.roll(x, n - s, axis=0), 0.0)
        else:
            x = x + jnp.where(row >= s, pltpu.roll(x, s, axis=0), 0.0)
        s *= 2
    return x


def _chunk_maps(n_ctx, n_all):
    n_lat = n_all - n_ctx
    fwd = lambda s: jnp.where(s < n_ctx, n_lat + s, s - n_ctx)
    bwd = lambda s: n_all - 1 - s
    return fwd, bwd


def _gla_prepare(qkv, g, reverse):
    n = qkv.shape[0]
    nb = n // SUB
    blocks = range(nb)
    rows = lambda a, blk: a[SUB * blk:SUB * (blk + 1), :]
    q = qkv[:, 0:GLA_QK].astype(F32)
    k = qkv[:, GLA_QK:2 * GLA_QK].astype(F32)
    cum = _cumsum_rows(g, reverse)
    zero = jnp.zeros((1, GLA_QK), F32)
    if reverse:
        ends = [cum[SUB * blk:SUB * blk + 1, :] for blk in blocks]
        refs = ends[1:] + [zero]
        earlier = lambda j, blk: j > blk
        tot = ends[0]
    else:
        ends = [cum[SUB * (blk + 1) - 1:SUB * (blk + 1), :] for blk in blocks]
        refs = [zero] + ends[:-1]
        earlier = lambda j, blk: j < blk
        tot = ends[nb - 1]
    stack = lambda vs: jnp.concatenate([jnp.broadcast_to(v, (SUB, GLA_QK)) for v in vs], axis=0)
    e_in = cum - stack(refs)
    q_blk = q * jnp.exp2(e_in)
    k_diag = k * jnp.exp2(-e_in)
    k_end = k * jnp.exp2(stack(ends) - cum)
    q_in = jnp.concatenate([rows(q_blk, blk) * jnp.exp2(refs[blk]) for blk in blocks], axis=0)
    k_out = jnp.concatenate([rows(k_end, blk) * jnp.exp2(tot - ends[blk]) for blk in blocks], axis=0)
    q_in = q_in.astype(BF16)
    k_out = k_out.astype(BF16)
    q_bf = q_blk.astype(BF16)
    zeros_piece = jnp.zeros((SUB, GLA_QK), BF16)
    k_piece = [[rows(k_diag, j).astype(BF16) if seg == j
                else (rows(k_end, j) * jnp.exp2(refs[seg] - ends[j])).astype(BF16) if earlier(j, seg)
                else zeros_piece
                for seg in blocks] for j in blocks]
    ii = lax.broadcasted_iota(jnp.int32, (n, n), 0)
    jj = lax.broadcasted_iota(jnp.int32, (n, n), 1)
    heads = []
    for h in range(GLA_HEADS):
        ks = slice(GLA_DK * h, GLA_DK * (h + 1))
        q_cat = jnp.concatenate(
            [jnp.concatenate([rows(q_bf, blk)[:, ks] if seg == blk else zeros_piece[:, ks]
                              for seg in blocks], axis=1) for blk in blocks], axis=0)
        k_cat = jnp.concatenate(
            [jnp.concatenate([k_piece[j][seg][:, ks] for seg in blocks], axis=1)
             for j in blocks], axis=0)
        heads.append(dict(q_cat=q_cat, k_cat=k_cat, q_in=q_in[:, ks], k_out=k_out[:, ks],
                          v=qkv[:, 2 * GLA_QK + GLA_DV * h:2 * GLA_QK + GLA_DV * (h + 1)]))
    return dict(heads=heads, causal=(ii <= jj) if reverse else (ii >= jj),
                decay_col=jnp.transpose(jnp.broadcast_to(tot, (GLA_DV, GLA_QK))))


def _gla_kernel(qkv_f_ref, g_f_ref, qkv_b_ref, g_b_ref, yf_ref, yb_ref, sf_ref, sb_ref):
    @pl.when(pl.program_id(1) == 0)
    def _():
        sf_ref[...] = jnp.zeros_like(sf_ref)
        sb_ref[...] = jnp.zeros_like(sb_ref)

    chains = []
    for e in range(qkv_f_ref.shape[0]):
        chains.append((qkv_f_ref.at[e], g_f_ref.at[e], False, sf_ref.at[e], yf_ref.at[e]))
        chains.append((qkv_b_ref.at[e], g_b_ref.at[e], True, sb_ref.at[e], yb_ref.at[e]))

    def start(chain):
        qkv_ref, g_ref, reverse, _, _ = chain
        p = _gla_prepare(qkv_ref[...], g_ref[...], reverse)
        return p, [_dot_nt(hd["q_cat"], hd["k_cat"]) for hd in p["heads"]]

    def finish(chain, p, att):
        _, _, _, s_ref, y_ref = chain
        outs = []
        for h, (hd, a) in enumerate(zip(p["heads"], att)):
            state = s_ref[h]
            lhs = jnp.concatenate([jnp.where(p["causal"], a, 0.0).astype(BF16), hd["q_in"]], axis=1)
            rhs = jnp.concatenate([hd["v"], state.astype(BF16)], axis=0)
            outs.append(_dot(lhs, rhs))
            ks = slice(GLA_DK * h, GLA_DK * (h + 1))
            s_ref[h] = state * jnp.exp2(p["decay_col"][ks, :]) + _dot_tn(hd["k_out"], hd["v"])
        y_ref[...] = jnp.concatenate(outs, axis=1).astype(y_ref.dtype)

    started = start(chains[0])
    for i, chain in enumerate(chains):
        following = start(chains[i + 1]) if i + 1 < len(chains) else None
        finish(chain, *started)
        started = following


def _gla_scan(qkv, g):
    bsz, tt, _ = qkv.shape
    n_all = tt // CHUNK
    fwd, bwd = _chunk_maps(CTX_LEN // CHUNK, n_all)
    spec = lambda w, m, c: pl.BlockSpec((SCAN_BATCH, CHUNK, w), lambda b, s: (b, m(s), c))
    return pl.pallas_call(
        _gla_kernel,
        grid=(bsz // SCAN_BATCH, n_all),
        in_specs=[spec(QKV_W, fwd, 0), spec(GLA_QK, fwd, 0), spec(QKV_W, bwd, 0), spec(GLA_QK, bwd, 1)],
        out_specs=[spec(GLA_INNER, fwd, 0), spec(GLA_INNER, bwd, 0)],
        out_shape=[jax.ShapeDtypeStruct((bsz, tt, GLA_INNER), BF16)] * 2,
        scratch_shapes=[pltpu.VMEM((SCAN_BATCH, GLA_HEADS, GLA_DK, GLA_DV), F32)] * 2,
        compiler_params=pltpu.CompilerParams(
            dimension_semantics=("parallel", "arbitrary"), vmem_limit_bytes=VMEM_LIMIT),
        name="gla_scan",
    )(qkv, g, qkv, g)


def _ssd_prepare(xbc, dt, nega, expand, lane0, reverse):
    n = xbc.shape[0]
    end = 0 if reverse else n - 1
    cum = _cumsum_rows(dt * nega, reverse)
    e_cum = jnp.exp2(cum)
    w = jnp.exp2(cum[end:end + 1, :] - cum) * dt
    e_cum_x = _dot(_split_hi_lo(e_cum), expand)
    x_w = (xbc[:, 0:SSD_INNER].astype(F32) * _dot(_split_hi_lo(w), expand)).astype(BF16)
    cum_t = jnp.transpose(cum)
    dt_t = jnp.transpose(dt)
    ii = lax.broadcasted_iota(jnp.int32, (n, n), 0)
    jj = lax.broadcasted_iota(jnp.int32, (n, n), 1)
    causal = (ii <= jj) if reverse else (ii >= jj)
    first_head = lax.broadcasted_iota(jnp.int32, (n, 2 * SSD_P), 1) < SSD_P
    groups = []
    for grp in range(SSD_GROUPS):
        b_g = xbc[:, SSD_INNER + SSD_N * grp:SSD_INNER + SSD_N * (grp + 1)]
        c_g = xbc[:, SSD_INNER + SSD_BC + SSD_N * grp:SSD_INNER + SSD_BC + SSD_N * (grp + 1)]
        scores = _dot_nt(c_g, b_g)
        pairs = []
        for pair in range(SSD_HPG // 2):
            m_pair = []
            for hh in range(2):
                lane = lane0 + SSD_HPG * grp + 2 * pair + hh
                cum_b = jnp.broadcast_to(cum[:, lane:lane + 1], (n, n))
                seg = jnp.where(causal, cum_b - cum_t[lane:lane + 1, :], NEG_BIG)
                m_pair.append((scores * jnp.exp2(seg) * dt_t[lane:lane + 1, :]).astype(BF16))
            col = 2 * SSD_P * (SSD_HPG // 2 * grp + pair)
            x_pair = xbc[:, col:col + 2 * SSD_P]
            none = jnp.zeros_like(x_pair)
            rhs = jnp.concatenate([jnp.where(first_head, x_pair, none),
                                   jnp.where(first_head, none, x_pair)], axis=0)
            pairs.append((jnp.concatenate(m_pair, axis=1), rhs))
        cols = slice(SSD_HPG * SSD_P * grp, SSD_HPG * SSD_P * (grp + 1))
        groups.append(dict(b=b_g, c=c_g, pairs=pairs, e_cum=e_cum_x[:, cols],
                           e_tot=e_cum_x[end:end + 1, cols], x_w=x_w[:, cols]))
    return groups


def _split_hi_lo(a):
    hi = a.astype(BF16)
    lo = (a - hi.astype(F32)).astype(BF16)
    return jnp.concatenate([hi, lo], axis=1)


def _ssd_kernel(xbc_f_ref, dt_f_ref, xbc_b_ref, dt_b_ref, nega_ref, expand_ref,
                yf_ref, yb_ref, sf_ref, sb_ref):
    @pl.when(pl.program_id(1) == 0)
    def _():
        sf_ref[...] = jnp.zeros_like(sf_ref)
        sb_ref[...] = jnp.zeros_like(sb_ref)

    nega = nega_ref[...]
    chains = []
    for e in range(xbc_f_ref.shape[0]):
        chains.append((xbc_f_ref.at[e], dt_f_ref.at[e], 0, False, sf_ref.at[e], yf_ref.at[e]))
        chains.append((xbc_b_ref.at[e], dt_b_ref.at[e], 1, True, sb_ref.at[e], yb_ref.at[e]))

    def start(chain):
        xbc_ref, dt_ref, d, reverse, _, _ = chain
        return _ssd_prepare(xbc_ref[...], dt_ref[...], nega, expand_ref[d],
                            DT_LANE + SSD_HEADS * d, reverse)

    def finish(chain, groups):
        s_ref, y_ref = chain[4], chain[5]
        outs = []
        for grp, gd in enumerate(groups):
            state = s_ref[grp]
            intra = jnp.concatenate([_dot(m, rhs) for m, rhs in gd["pairs"]], axis=1)
            outs.append(_dot(gd["c"], state.astype(BF16)) * gd["e_cum"] + intra)
            s_ref[grp] = state * gd["e_tot"] + _dot_tn(gd["b"], gd["x_w"])
        y_ref[...] = jnp.concatenate(outs, axis=1).astype(y_ref.dtype)

    started = start(chains[0])
    for i, chain in enumerate(chains):
        following = start(chains[i + 1]) if i + 1 < len(chains) else None
        finish(chain, started)
        started = following


def _head_expand_matrix():
    r = jnp.arange(2 * TAIL_W)[:, None] % TAIL_W
    c = jnp.arange(SSD_INNER)[None, :] // SSD_P
    return jnp.stack([(r == DT_LANE + SSD_HEADS * d + c) for d in range(2)]).astype(BF16)


def _ssd_scan(xbc, dt, nega):
    bsz, tt, _ = xbc.shape
    n_all = tt // CHUNK
    fwd, bwd = _chunk_maps(CTX_LEN // CHUNK, n_all)
    spec = lambda w, m: pl.BlockSpec((SCAN_BATCH, CHUNK, w), lambda b, s: (b, m(s), 0))
    expand = _head_expand_matrix()
    return pl.pallas_call(
        _ssd_kernel,
        grid=(bsz // SCAN_BATCH, n_all),
        in_specs=[spec(SSD_CONV_DIM, fwd), spec(TAIL_W, fwd),
                  spec(SSD_CONV_DIM, bwd), spec(TAIL_W, bwd),
                  pl.BlockSpec((1, TAIL_W), lambda b, s: (0, 0)),
                  pl.BlockSpec(expand.shape, lambda b, s: (0, 0, 0))],
        out_specs=[spec(SSD_INNER, fwd), spec(SSD_INNER, bwd)],
        out_shape=[jax.ShapeDtypeStruct((bsz, tt, SSD_INNER), BF16)] * 2,
        scratch_shapes=[pltpu.VMEM((SCAN_BATCH, SSD_GROUPS, SSD_N, SSD_HPG * SSD_P), F32)] * 2,
        compiler_params=pltpu.CompilerParams(
            dimension_semantics=("parallel", "arbitrary"), vmem_limit_bytes=VMEM_LIMIT),
        name="ssd_scan",
    )(xbc, dt, xbc, dt, nega, expand)


def _merge_ffn_kernel(x_ref, ygf_ref, ygb_ref, ysf_ref, ysb_ref, xs_ref, rz_ref, m_ref,
                      gnorm_ref, dskip_ref, snorm_ref, npost_ref, nfpre_ref, nfpost_ref,
                      wout_ref, wgate_ref, wup_ref, wdown_ref, o_ref):
    b = pl.program_id(0)
    mrow = m_ref[pl.ds(b, 1), :]
    mod = [mrow[:, D_MODEL * j:D_MODEL * (j + 1)] for j in range(6)]
    gw = SSD_INNER // SSD_GROUPS

    def mixer_out(rs):
        y_gla = ygf_ref[0, rs, :].astype(F32) + ygb_ref[0, rs, :].astype(F32)
        o_gla = jnp.concatenate(
            [_rms(y_gla[:, GLA_DV * h:GLA_DV * (h + 1)]) for h in range(GLA_HEADS)], axis=1)
        o_gla = o_gla * gnorm_ref[...] * rz_ref[0, rs, 0:GLA_INNER].astype(F32)
        y_ssd = ysf_ref[0, rs, :].astype(F32) + ysb_ref[0, rs, :].astype(F32)
        y_ssd = ((y_ssd + dskip_ref[...] * xs_ref[0, rs, :].astype(F32))
                 * rz_ref[0, rs, GLA_INNER:].astype(F32))
        o_ssd = jnp.concatenate(
            [_rms(y_ssd[:, gw * g:gw * (g + 1)]) for g in range(SSD_GROUPS)], axis=1)
        o_ssd = o_ssd * snorm_ref[...]
        o = jnp.concatenate([o_gla, o_ssd], axis=1).astype(BF16)
        y = _dot(o, wout_ref[...])
        x1 = x_ref[0, rs, :] + mod[2] * (_rms(y) * npost_ref[...])
        h = ((_rms(x1) * nfpre_ref[...]) * (1.0 + mod[4]) + mod[3]).astype(BF16)
        return x1, h

    def ffn(rs, x1, h):
        act = (_silu(_dot(h, wgate_ref[...])) * _dot(h, wup_ref[...])).astype(BF16)
        f = _dot(act, wdown_ref[...])
        o_ref[0, rs, :] = x1 + mod[5] * (_rms(f) * nfpost_ref[...])

    subs = [slice(FFN_SUB * j, FFN_SUB * (j + 1)) for j in range(TM_FFN // FFN_SUB)]
    ready = mixer_out(subs[0])
    for j, rs in enumerate(subs):
        following = mixer_out(subs[j + 1]) if j + 1 < len(subs) else None
        ffn(rs, *ready)
        ready = following


def _merge_ffn(x, ygf, ygb, ysf, ysb, xbc, rz, m_all, gnorm, dskip, snorm, npost, nfpre,
               nfpost, wout, wgate, wup, wdown):
    bsz, seq, _ = x.shape
    n_tiles = seq // TM_FFN
    const = lambda b, i: (0, 0)

    def lat(width):
        return pl.BlockSpec((1, TM_FFN, width), lambda b, i: (b, i, 0))

    single = lambda shape: pl.BlockSpec(shape, const, pipeline_mode=pl.Buffered(1))
    vec = lambda w: pl.BlockSpec((1, w), const)
    return pl.pallas_call(
        _merge_ffn_kernel,
        grid=(bsz, n_tiles),
        in_specs=[pl.BlockSpec((1, TM_FFN, D_MODEL), lambda b, i: (b, i, 0)),
                  lat(GLA_INNER), lat(GLA_INNER), lat(SSD_INNER), lat(SSD_INNER),
                  lat(SSD_INNER), lat(GLA_INNER + SSD_INNER),
                  pl.BlockSpec(m_all.shape, const),
                  vec(GLA_INNER), vec(SSD_INNER), vec(SSD_INNER),
                  vec(D_MODEL), vec(D_MODEL), vec(D_MODEL),
                  single(wout.shape), single(wgate.shape), single(wup.shape), single(wdown.shape)],
        out_specs=pl.BlockSpec((1, TM_FFN, D_MODEL), lambda b, i: (b, i, 0)),
        out_shape=jax.ShapeDtypeStruct((bsz, seq, D_MODEL), F32),
        compiler_params=pltpu.CompilerParams(
            dimension_semantics=("parallel", "arbitrary"), vmem_limit_bytes=VMEM_LIMIT),
        name="merge_ffn",
    )(x, ygf, ygb, ysf, ysb, xbc, rz, m_all, gnorm, dskip, snorm, npost, nfpre, nfpost,
      wout, wgate, wup, wdown)


def kernel(x, c, ctx, c_ctx, w_mod, b_mod, norm_mix_pre, norm_mix_post, norm_ffn_pre, norm_ffn_post,
           w_in, conv_w, conv_b, gla_wg_f, gla_bg_f, gla_wg_b, gla_bg_b, gla_norm,
           a_log_f, a_log_b, dt_bias_f, dt_bias_b, d_skip, ssd_norm, w_out, w_gate, w_up, w_down):
    assert w_mod.shape[0] == 1, "single-layer kernel"
    bsz = x.shape[0]
    row = lambda v: v.reshape(1, -1)

    c_all = jnp.concatenate([c, c_ctx[None], jnp.zeros((8 - bsz - 1, D_MODEL), F32)], axis=0)
    wg_pad = jnp.zeros((TAIL_W, 2 * GLA_QK), F32)
    wg_pad = wg_pad.at[0:GLA_RANK, 0:GLA_QK].set(gla_wg_f[0])
    wg_pad = wg_pad.at[GLA_RANK:2 * GLA_RANK, GLA_QK:].set(gla_wg_b[0]).astype(BF16)
    bg_cat = jnp.concatenate([gla_bg_f[0], gla_bg_b[0]]).reshape(1, -1)
    lane_pad = lambda f, bk: jnp.zeros((1, TAIL_W), F32).at[0, DT_LANE:DT_LANE + 2 * SSD_HEADS].set(
        jnp.concatenate([f, bk]))
    dtb_pad = lane_pad(dt_bias_f[0], dt_bias_b[0])
    nega = lane_pad(-jnp.exp(a_log_f[0]) * LOG2_E, -jnp.exp(a_log_b[0]) * LOG2_E)
    conv_w9 = conv_w[0].reshape(CONV_K * CONV_K, SSD_CONV_DIM)

    m_all = _modulation(c_all, w_mod[0], row(b_mod[0]))
    qkv, rz, xbc, g, dt = _input_projection_conv(
        x, ctx, m_all, row(norm_mix_pre[0]), w_in[0].T, wg_pad, bg_cat, dtb_pad, conv_w9, row(conv_b[0]))
    ygf, ygb = _gla_scan(qkv, g)
    ysf, ysb = _ssd_scan(xbc, dt, nega)
    return _merge_ffn(
        x, ygf, ygb, ysf, ysb, xbc, rz, m_all,
        row(jnp.tile(gla_norm[0], GLA_HEADS)), row(jnp.repeat(d_skip[0], SSD_P)), row(ssd_norm[0]),
        row(norm_mix_post[0]), row(norm_ffn_pre[0]), row(norm_ffn_post[0]),
        w_out[0].astype(BF16), w_gate[0].astype(BF16), w_up[0].astype(BF16), w_down[0].astype(BF16))
```

```python
import functools

import jax
import jax.numpy as jnp
from jax import lax
from jax.experimental import pallas as pl
from jax.experimental.pallas import tpu as pltpu

F32 = jnp.float32
BF16 = jnp.bfloat16

D_MODEL = 1024
CTX_LEN = 256
GRID_W = 64
EPS = 1e-6

GLA_HEADS = 4
GLA_DK = 64
GLA_DV = 128
GLA_QK = GLA_HEADS * GLA_DK
GLA_INNER = GLA_HEADS * GLA_DV
GLA_RANK = 16
GLA_GATE_NORM = 16.0

SSD_HEADS = 8
SSD_P = 64
SSD_N = 64
SSD_GROUPS = 2
SSD_HPG = SSD_HEADS // SSD_GROUPS
SSD_INNER = SSD_HEADS * SSD_P
SSD_BC = SSD_GROUPS * SSD_N
SSD_CONV_DIM = SSD_INNER + 2 * SSD_BC
CONV_K = 3
D_FF = 2816

QKV_W = 2 * GLA_QK + GLA_INNER
COL_R = QKV_W
COL_Z = COL_R + GLA_INNER
COL_XBC = COL_Z + SSD_INNER
COL_TAIL = COL_XBC + SSD_CONV_DIM
TAIL_W = 128
PROJ_W = COL_TAIL + TAIL_W
DT_LANE = 2 * GLA_RANK
W_IN_LR = COL_Z
W_IN_Z = W_IN_LR + 2 * GLA_RANK
W_IN_DT = W_IN_Z + SSD_INNER + SSD_CONV_DIM
W_IN_COLS = W_IN_DT + 2 * SSD_HEADS

TM_PROJ = 512
PROJ_SUB = CTX_LEN
TM_CONV = 256
TM_FFN = 512
FFN_SUB = 128
CHUNK = 128
SCAN_BATCH = 4
SUB = 32
NEG_BIG = -1e30
LOG2_E = 1.4426950408889634
VMEM_LIMIT = 56 * 1024 * 1024


def _silu(x):
    return x / (1.0 + jnp.exp(-x))


def _softplus(x):
    return jnp.maximum(x, 0.0) + jnp.log(1.0 + jnp.exp(-jnp.abs(x)))


def _rms(x):
    return x * lax.rsqrt(jnp.mean(x * x, axis=-1, keepdims=True) + EPS)


def _dot(a, b):
    return jnp.dot(a, b, preferred_element_type=F32)


def _dot_nt(a, b):
    return lax.dot_general(a, b, (((1,), (1,)), ((), ())), preferred_element_type=F32)


def _dot_tn(a, b):
    return lax.dot_general(a, b, (((0,), (0,)), ((), ())), preferred_element_type=F32)


def _mod_kernel(c_ref, w_ref, b_ref, o_ref):
    s = _silu(c_ref[...])
    o_ref[...] = _dot(s.astype(BF16), w_ref[...].astype(BF16)) + b_ref[...]


def _modulation(c_all, w_mod, b_mod):
    n = w_mod.shape[1]
    tn = 1024
    return pl.pallas_call(
        _mod_kernel,
        grid=(n // tn,),
        in_specs=[pl.BlockSpec((8, D_MODEL), lambda j: (0, 0)),
                  pl.BlockSpec((D_MODEL, tn), lambda j: (0, j)),
                  pl.BlockSpec((1, tn), lambda j: (0, j))],
        out_specs=pl.BlockSpec((8, tn), lambda j: (0, j)),
        out_shape=jax.ShapeDtypeStruct((8, n), F32),
        name="adaln_mod",
    )(c_all, w_mod, b_mod)


def _permute_cast_w_in(wt_ref, w_s):
    blk = 128
    pad = jnp.zeros((TAIL_W - (W_IN_Z - W_IN_LR) - (W_IN_COLS - W_IN_DT), blk), F32)

    def body(kb, carry):
        k0 = pl.multiple_of(kb * blk, blk)
        cols = pl.ds(k0, blk)
        rows = pl.ds(k0, blk)
        w_s[rows, 0:W_IN_LR] = jnp.transpose(wt_ref[0:W_IN_LR, cols]).astype(BF16)
        w_s[rows, COL_Z:COL_TAIL] = jnp.transpose(wt_ref[W_IN_Z:W_IN_DT, cols]).astype(BF16)
        tail = jnp.concatenate([wt_ref[W_IN_LR:W_IN_Z, cols], wt_ref[W_IN_DT:W_IN_COLS, cols], pad],
                               axis=0)
        w_s[rows, COL_TAIL:PROJ_W] = jnp.transpose(tail).astype(BF16)
        return carry

    lax.fori_loop(0, D_MODEL // blk, body, 0)


def _grid_conv_rows(window, t0, is_ctx, w_ref, b_ref):
    n = window.shape[0] - 2 * GRID_W
    ext = window.astype(F32)
    row_on = jnp.where(is_ctx, 0.0, 1.0)
    taps = []
    for dx in range(CONV_K):
        tap = None
        for dy in range(CONV_K):
            wk = w_ref[CONV_K * dy + dx:CONV_K * dy + dx + 1, :]
            if dy != 1:
                wk = wk * row_on
            term = wk * ext[GRID_W * dy:GRID_W * dy + n, :]
            tap = term if tap is None else tap + term
        taps.append(tap)
    t = t0 + lax.broadcasted_iota(jnp.int32, (n, 1), 0)
    col = jnp.bitwise_and(t, GRID_W - 1)
    has_left = jnp.where(is_ctx, t, col) != 0
    has_right = jnp.where(is_ctx, t - (CTX_LEN - 1), col - (GRID_W - 1)) != 0
    acc = (taps[1] + b_ref[...]
           + jnp.where(has_left, pltpu.roll(taps[0], 1, axis=0), 0.0)
           + jnp.where(has_right, pltpu.roll(taps[2], n - 1, axis=0), 0.0))
    return _silu(acc)


def _inproj_conv_kernel(x_ref, ctx_ref, m_ref, gpre_ref, w_ref, wg_ref, bg_ref, dtb_ref,
                        cw_ref, cb_ref, qkv_ref, rz_ref, xbc_ref, g_ref, dt_ref,
                        w_s, ring, above_s):
    b = pl.program_id(0)
    i = pl.program_id(1)
    n_tiles = pl.num_programs(1) - 1
    ctx_tile = n_tiles - 1

    @pl.when(jnp.logical_and(b == 0, i == 0))
    def _():
        _permute_cast_w_in(w_ref, w_s)

    @pl.when(i == 0)
    def _():
        ring[...] = jnp.zeros_like(ring)
        above_s[...] = jnp.zeros_like(above_s)

    j = i - 1
    conv_ctx = j == ctx_tile
    split = TM_PROJ - GRID_W
    none = jnp.zeros((GRID_W, SSD_CONV_DIM), BF16)

    def conv_previous_head():
        main = ring[jnp.bitwise_and(j, 1)]
        above = jnp.where(jnp.logical_and(j >= 1, j < ctx_tile), above_s[...], none)
        window = jnp.concatenate([above, main], axis=0)
        xbc_ref[0, 0:split, :] = _grid_conv_rows(window, 0, conv_ctx, cw_ref, cb_ref).astype(BF16)
        above_s[...] = main[split:, :]

    def conv_previous_last_row(below):
        main = ring[jnp.bitwise_and(j, 1)]
        below = jnp.where(j < ctx_tile - 1, below, none)
        window = jnp.concatenate([main[split - GRID_W:, :], below], axis=0)
        xbc_ref[0, split:, :] = _grid_conv_rows(window, split, conv_ctx, cw_ref, cb_ref).astype(BF16)

    @pl.when(i < n_tiles)
    def _():
        is_ctx = i == ctx_tile
        mrow = m_ref[pl.ds(jnp.where(is_ctx, 4, b), 1), :]
        shift = mrow[:, 0:D_MODEL]
        scale = mrow[:, D_MODEL:2 * D_MODEL]
        slot = jnp.bitwise_and(i, 1)

        def normed(rs):
            x = jnp.where(is_ctx, ctx_ref[0], x_ref[0, rs, :])
            return ((_rms(x) * gpre_ref[...]) * (1.0 + scale) + shift).astype(BF16)

        def project(rs, h):
            acc = _dot(h, w_s[...])
            qkv_ref[0, rs, 0:GLA_QK] = (acc[:, 0:GLA_QK] * (GLA_DK ** -0.5)).astype(BF16)
            qkv_ref[0, rs, GLA_QK:QKV_W] = acc[:, GLA_QK:QKV_W].astype(BF16)
            rz_ref[0, rs, :] = _silu(acc[:, COL_R:COL_XBC]).astype(BF16)
            tail = acc[:, COL_TAIL:PROJ_W]
            logits = _dot(tail.astype(BF16), wg_ref[...]) + bg_ref[...]
            g_ref[0, rs, :] = -_softplus(-logits) * (LOG2_E / GLA_GATE_NORM)
            dt_ref[0, rs, :] = _softplus(tail + dtb_ref[...])
            return acc[:, COL_XBC:COL_TAIL].astype(BF16)

        subs = [slice(PROJ_SUB * s, PROJ_SUB * (s + 1)) for s in range(TM_PROJ // PROJ_SUB)]
        ready = normed(subs[0])
        fresh = []
        for s, rs in enumerate(subs):
            following = normed(subs[s + 1]) if s + 1 < len(subs) else None
            fresh.append(project(rs, ready))
            if s == 0:
                conv_previous_head()
                conv_previous_last_row(fresh[0][0:GRID_W, :])
            ready = following
        for rs, xbc_new in zip(subs, fresh):
            ring[slot, rs, :] = xbc_new

    @pl.when(i == n_tiles)
    def _():
        conv_previous_head()
        conv_previous_last_row(none)


def _input_projection_conv(x, ctx, m_all, gpre, w_in, wg_pad, bg_cat, dtb_pad, conv_w9, conv_b):
    bsz, seq, _ = x.shape
    n_lat = seq // TM_PROJ
    n_tiles = n_lat + 1
    tt = CTX_LEN + seq
    const = lambda b, i: (0, 0)
    tok = lambda b, i: (b, jnp.minimum(i, n_tiles - 1), 0)
    conv_tok = lambda b, i: (b, jnp.maximum(i - 1, 0), 0)
    return pl.pallas_call(
        _inproj_conv_kernel,
        grid=(bsz, n_tiles + 1),
        in_specs=[pl.BlockSpec((1, TM_PROJ, D_MODEL), lambda b, i: (b, jnp.minimum(i, n_lat - 1), 0)),
                  pl.BlockSpec((1, CTX_LEN, D_MODEL), lambda b, i: (b, 0, 0)),
                  pl.BlockSpec(m_all.shape, const),
                  pl.BlockSpec((1, D_MODEL), const),
                  pl.BlockSpec((W_IN_COLS, D_MODEL), const, pipeline_mode=pl.Buffered(1)),
                  pl.BlockSpec((TAIL_W, 2 * GLA_QK), const),
                  pl.BlockSpec((1, 2 * GLA_QK), const),
                  pl.BlockSpec((1, TAIL_W), const),
                  pl.BlockSpec((CONV_K * CONV_K, SSD_CONV_DIM), const),
                  pl.BlockSpec((1, SSD_CONV_DIM), const)],
        out_specs=[pl.BlockSpec((1, TM_PROJ, QKV_W), tok),
                   pl.BlockSpec((1, TM_PROJ, GLA_INNER + SSD_INNER), tok),
                   pl.BlockSpec((1, TM_PROJ, SSD_CONV_DIM), conv_tok),
                   pl.BlockSpec((1, TM_PROJ, 2 * GLA_QK), tok),
                   pl.BlockSpec((1, TM_PROJ, TAIL_W), tok)],
        out_shape=[jax.ShapeDtypeStruct((bsz, tt, QKV_W), BF16),
                   jax.ShapeDtypeStruct((bsz, tt, GLA_INNER + SSD_INNER), BF16),
                   jax.ShapeDtypeStruct((bsz, tt, SSD_CONV_DIM), BF16),
                   jax.ShapeDtypeStruct((bsz, tt, 2 * GLA_QK), F32),
                   jax.ShapeDtypeStruct((bsz, tt, TAIL_W), F32)],
        scratch_shapes=[pltpu.VMEM((D_MODEL, PROJ_W), BF16),
                        pltpu.VMEM((2, TM_PROJ, SSD_CONV_DIM), BF16),
                        pltpu.VMEM((GRID_W, SSD_CONV_DIM), BF16)],
        compiler_params=pltpu.CompilerParams(
            dimension_semantics=("arbitrary", "arbitrary"), vmem_limit_bytes=VMEM_LIMIT),
        name="in_proj_conv",
    )(x, ctx, m_all, gpre, w_in, wg_pad, bg_cat, dtb_pad, conv_w9, conv_b)


def _cumsum_rows(x, reverse):
    n = x.shape[0]
    row = lax.broadcasted_iota(jnp.int32, x.shape, 0)
    s = 1
    while s < n:
        if reverse:
            x = x + jnp.where(row < n - s, pltpu.roll(x, n - s, axis=0), 0.0)
        else:
            x = x + jnp.where(row >= s, pltpu.roll(x, s, axis=0), 0.0)
        s *= 2
    return x


def _chunk_maps(n_ctx, n_all):
    n_lat = n_all - n_ctx
    fwd = lambda s: jnp.where(s < n_ctx, n_lat + s, s - n_ctx)
    bwd = lambda s: n_all - 1 - s
    return fwd, bwd


def _gla_prepare(qkv, g, reverse):
    n = qkv.shape[0]
    nb = n // SUB
    blocks = range(nb)
    rows = lambda a, blk: a[SUB * blk:SUB * (blk + 1), :]
    q = qkv[:, 0:GLA_QK].astype(F32)
    k = qkv[:, GLA_QK:2 * GLA_QK].astype(F32)
    cum = _cumsum_rows(g, reverse)
    zero = jnp.zeros((1, GLA_QK), F32)
    if reverse:
        ends = [cum[SUB * blk:SUB * blk + 1, :] for blk in blocks]
        refs = ends[1:] + [zero]
        earlier = lambda j, blk: j > blk
        tot = ends[0]
    else:
        ends = [cum[SUB * (blk + 1) - 1:SUB * (blk + 1), :] for blk in blocks]
        refs = [zero] + ends[:-1]
        earlier = lambda j, blk: j < blk
        tot = ends[nb - 1]
    stack = lambda vs: jnp.concatenate([jnp.broadcast_to(v, (SUB, GLA_QK)) for v in vs], axis=0)
    e_in = cum - stack(refs)
    q_blk = q * jnp.exp2(e_in)
    k_diag = k * jnp.exp2(-e_in)
    k_end = k * jnp.exp2(stack(ends) - cum)
    q_in = jnp.concatenate([rows(q_blk, blk) * jnp.exp2(refs[blk]) for blk in blocks], axis=0)
    k_out = jnp.concatenate([rows(k_end, blk) * jnp.exp2(tot - ends[blk]) for blk in blocks], axis=0)
    q_in = q_in.astype(BF16)
    k_out = k_out.astype(BF16)
    q_bf = q_blk.astype(BF16)
    zeros_piece = jnp.zeros((SUB, GLA_QK), BF16)
    k_piece = [[rows(k_diag, j).astype(BF16) if seg == j
                else (rows(k_end, j) * jnp.exp2(refs[seg] - ends[j])).astype(BF16) if earlier(j, seg)
                else zeros_piece
                for seg in blocks] for j in blocks]
    ii = lax.broadcasted_iota(jnp.int32, (n, n), 0)
    jj = lax.broadcasted_iota(jnp.int32, (n, n), 1)
    heads = []
    for h in range(GLA_HEADS):
        ks = slice(GLA_DK * h, GLA_DK * (h + 1))
        q_cat = jnp.concatenate(
            [jnp.concatenate([rows(q_bf, blk)[:, ks] if seg == blk else zeros_piece[:, ks]
                              for seg in blocks], axis=1) for blk in blocks], axis=0)
        k_cat = jnp.concatenate(
            [jnp.concatenate([k_piece[j][seg][:, ks] for seg in blocks], axis=1)
             for j in blocks], axis=0)
        heads.append(dict(q_cat=q_cat, k_cat=k_cat, q_in=q_in[:, ks], k_out=k_out[:, ks],
                          v=qkv[:, 2 * GLA_QK + GLA_DV * h:2 * GLA_QK + GLA_DV * (h + 1)]))
    return dict(heads=heads, causal=(ii <= jj) if reverse else (ii >= jj),
                decay_col=jnp.transpose(jnp.broadcast_to(tot, (GLA_DV, GLA_QK))))


def _gla_chain(qkv_ref, g_ref, reverse, s_ref, y_ref):
    def start():
        p = _gla_prepare(qkv_ref[...], g_ref[...], reverse)
        return p, [_dot_nt(hd["q_cat"], hd["k_cat"]) for hd in p["heads"]]

    def finish(started):
        p, att = started
        outs = []
        for h, (hd, a) in enumerate(zip(p["heads"], att)):
            state = s_ref[h]
            lhs = jnp.concatenate([jnp.where(p["causal"], a, 0.0).astype(BF16), hd["q_in"]], axis=1)
            rhs = jnp.concatenate([hd["v"], state.astype(BF16)], axis=0)
            outs.append(_dot(lhs, rhs))
            ks = slice(GLA_DK * h, GLA_DK * (h + 1))
            s_ref[h] = state * jnp.exp2(p["decay_col"][ks, :]) + _dot_tn(hd["k_out"], hd["v"])
        y_ref[...] = jnp.concatenate(outs, axis=1).astype(y_ref.dtype)

    return start, finish


def _ssd_prepare(xbc, dt, nega, expand, lane0, reverse):
    n = xbc.shape[0]
    end = 0 if reverse else n - 1
    cum = _cumsum_rows(dt * nega, reverse)
    e_cum = jnp.exp2(cum)
    w = jnp.exp2(cum[end:end + 1, :] - cum) * dt
    e_cum_x = _dot(_split_hi_lo(e_cum), expand)
    x_w = (xbc[:, 0:SSD_INNER].astype(F32) * _dot(_split_hi_lo(w), expand)).astype(BF16)
    cum_t = jnp.transpose(cum)
    dt_t = jnp.transpose(dt)
    ii = lax.broadcasted_iota(jnp.int32, (n, n), 0)
    jj = lax.broadcasted_iota(jnp.int32, (n, n), 1)
    causal = (ii <= jj) if reverse else (ii >= jj)
    first_head = lax.broadcasted_iota(jnp.int32, (n, 2 * SSD_P), 1) < SSD_P
    groups = []
    for grp in range(SSD_GROUPS):
        b_g = xbc[:, SSD_INNER + SSD_N * grp:SSD_INNER + SSD_N * (grp + 1)]
        c_g = xbc[:, SSD_INNER + SSD_BC + SSD_N * grp:SSD_INNER + SSD_BC + SSD_N * (grp + 1)]
        scores = _dot_nt(c_g, b_g)
        pairs = []
        for pair in range(SSD_HPG // 2):
            m_pair = []
            for hh in range(2):
                lane = lane0 + SSD_HPG * grp + 2 * pair + hh
                cum_b = jnp.broadcast_to(cum[:, lane:lane + 1], (n, n))
                seg = jnp.where(causal, cum_b - cum_t[lane:lane + 1, :], NEG_BIG)
                m_pair.append((scores * jnp.exp2(seg) * dt_t[lane:lane + 1, :]).astype(BF16))
            col = 2 * SSD_P * (SSD_HPG // 2 * grp + pair)
            x_pair = xbc[:, col:col + 2 * SSD_P]
            none = jnp.zeros_like(x_pair)
            rhs = jnp.concatenate([jnp.where(first_head, x_pair, none),
                                   jnp.where(first_head, none, x_pair)], axis=0)
            pairs.append((jnp.concatenate(m_pair, axis=1), rhs))
        cols = slice(SSD_HPG * SSD_P * grp, SSD_HPG * SSD_P * (grp + 1))
        groups.append(dict(b=b_g, c=c_g, pairs=pairs, e_cum=e_cum_x[:, cols],
                           e_tot=e_cum_x[end:end + 1, cols], x_w=x_w[:, cols]))
    return groups


def _split_hi_lo(a):
    hi = a.astype(BF16)
    lo = (a - hi.astype(F32)).astype(BF16)
    return jnp.concatenate([hi, lo], axis=1)


def _ssd_chain(xbc_ref, dt_ref, nega, expand_ref, d, s_ref, y_ref):
    def start():
        return _ssd_prepare(xbc_ref[...], dt_ref[...], nega, expand_ref[d],
                            DT_LANE + SSD_HEADS * d, d == 1)

    def finish(groups):
        outs = []
        for grp, gd in enumerate(groups):
            state = s_ref[grp]
            intra = jnp.concatenate([_dot(m, rhs) for m, rhs in gd["pairs"]], axis=1)
            outs.append(_dot(gd["c"], state.astype(BF16)) * gd["e_cum"] + intra)
            s_ref[grp] = state * gd["e_tot"] + _dot_tn(gd["b"], gd["x_w"])
        y_ref[...] = jnp.concatenate(outs, axis=1).astype(y_ref.dtype)

    return start, finish


def _mixer_scan_kernel(qkv_f_ref, g_f_ref, qkv_b_ref, g_b_ref, xbc_f_ref, dt_f_ref, xbc_b_ref,
                       dt_b_ref, nega_ref, expand_ref, ygf_ref, ygb_ref, ysf_ref, ysb_ref,
                       gsf_ref, gsb_ref, ssf_ref, ssb_ref):
    @pl.when(pl.program_id(1) == 0)
    def _():
        for s_ref in (gsf_ref, gsb_ref, ssf_ref, ssb_ref):
            s_ref[...] = jnp.zeros_like(s_ref)

    nega = nega_ref[...]
    chains = []
    for e in range(qkv_f_ref.shape[0]):
        chains.append(_gla_chain(qkv_f_ref.at[e], g_f_ref.at[e], False, gsf_ref.at[e], ygf_ref.at[e]))
        chains.append(_ssd_chain(xbc_f_ref.at[e], dt_f_ref.at[e], nega, expand_ref, 0,
                                 ssf_ref.at[e], ysf_ref.at[e]))
        chains.append(_gla_chain(qkv_b_ref.at[e], g_b_ref.at[e], True, gsb_ref.at[e], ygb_ref.at[e]))
        chains.append(_ssd_chain(xbc_b_ref.at[e], dt_b_ref.at[e], nega, expand_ref, 1,
                                 ssb_ref.at[e], ysb_ref.at[e]))
    started = chains[0][0]()
    for i, (_, finish) in enumerate(chains):
        following = chains[i + 1][0]() if i + 1 < len(chains) else None
        finish(started)
        started = following


def _head_expand_matrix():
    r = jnp.arange(2 * TAIL_W)[:, None] % TAIL_W
    c = jnp.arange(SSD_INNER)[None, :] // SSD_P
    return jnp.stack([(r == DT_LANE + SSD_HEADS * d + c) for d in range(2)]).astype(BF16)


def _mixer_scan(qkv, g, xbc, dt, nega):
    bsz, tt, _ = qkv.shape
    n_all = tt // CHUNK
    fwd, bwd = _chunk_maps(CTX_LEN // CHUNK, n_all)
    spec = lambda w, m, c=0: pl.BlockSpec((SCAN_BATCH, CHUNK, w), lambda b, s: (b, m(s), c))
    expand = _head_expand_matrix()
    return pl.pallas_call(
        _mixer_scan_kernel,
        grid=(bsz // SCAN_BATCH, n_all),
        in_specs=[spec(QKV_W, fwd), spec(GLA_QK, fwd), spec(QKV_W, bwd), spec(GLA_QK, bwd, 1),
                  spec(SSD_CONV_DIM, fwd), spec(TAIL_W, fwd), spec(SSD_CONV_DIM, bwd), spec(TAIL_W, bwd),
                  pl.BlockSpec((1, TAIL_W), lambda b, s: (0, 0)),
                  pl.BlockSpec(expand.shape, lambda b, s: (0, 0, 0))],
        out_specs=[spec(GLA_INNER, fwd), spec(GLA_INNER, bwd), spec(SSD_INNER, fwd), spec(SSD_INNER, bwd)],
        out_shape=[jax.ShapeDtypeStruct((bsz, tt, GLA_INNER), BF16)] * 2
        + [jax.ShapeDtypeStruct((bsz, tt, SSD_INNER), BF16)] * 2,
        scratch_shapes=[pltpu.VMEM((SCAN_BATCH, GLA_HEADS, GLA_DK, GLA_DV), F32)] * 2
        + [pltpu.VMEM((SCAN_BATCH, SSD_GROUPS, SSD_N, SSD_HPG * SSD_P), F32)] * 2,
        compiler_params=pltpu.CompilerParams(
            dimension_semantics=("parallel", "arbitrary"), vmem_limit_bytes=VMEM_LIMIT),
        name="mixer_scan",
    )(qkv, g, qkv, g, xbc, dt, xbc, dt, nega, expand)


def _merge_ffn_kernel(x_ref, ygf_ref, ygb_ref, ysf_ref, ysb_ref, xs_ref, rz_ref, m_ref,
                      gnorm_ref, dskip_ref, snorm_ref, npost_ref, nfpre_ref, nfpost_ref,
                      wout_ref, wgate_ref, wup_ref, wdown_ref, o_ref):
    b = pl.program_id(0)
    mrow = m_ref[pl.ds(b, 1), :]
    mod = [mrow[:, D_MODEL * j:D_MODEL * (j + 1)] for j in range(6)]
    gw = SSD_INNER // SSD_GROUPS

    def mixer_out(rs):
        y_gla = ygf_ref[0, rs, :].astype(F32) + ygb_ref[0, rs, :].astype(F32)
        o_gla = jnp.concatenate(
            [_rms(y_gla[:, GLA_DV * h:GLA_DV * (h + 1)]) for h in range(GLA_HEADS)], axis=1)
        o_gla = o_gla * gnorm_ref[...] * rz_ref[0, rs, 0:GLA_INNER].astype(F32)
        y_ssd = ysf_ref[0, rs, :].astype(F32) + ysb_ref[0, rs, :].astype(F32)
        y_ssd = ((y_ssd + dskip_ref[...] * xs_ref[0, rs, :].astype(F32))
                 * rz_ref[0, rs, GLA_INNER:].astype(F32))
        o_ssd = jnp.concatenate(
            [_rms(y_ssd[:, gw * g:gw * (g + 1)]) for g in range(SSD_GROUPS)], axis=1)
        o_ssd = o_ssd * snorm_ref[...]
        o = jnp.concatenate([o_gla, o_ssd], axis=1).astype(BF16)
        y = _dot(o, wout_ref[...])
        x1 = x_ref[0, rs, :] + mod[2] * (_rms(y) * npost_ref[...])
        h = ((_rms(x1) * nfpre_ref[...]) * (1.0 + mod[4]) + mod[3]).astype(BF16)
        return x1, h

    def ffn(rs, x1, h):
        act = (_silu(_dot(h, wgate_ref[...])) * _dot(h, wup_ref[...])).astype(BF16)
        f = _dot(act, wdown_ref[...])
        o_ref[0, rs, :] = x1 + mod[5] * (_rms(f) * nfpost_ref[...])

    subs = [slice(FFN_SUB * j, FFN_SUB * (j + 1)) for j in range(TM_FFN // FFN_SUB)]
    ready = mixer_out(subs[0])
    for j, rs in enumerate(subs):
        following = mixer_out(subs[j + 1]) if j + 1 < len(subs) else None
        ffn(rs, *ready)
        ready = following


def _merge_ffn(x, ygf, ygb, ysf, ysb, xbc, rz, m_all, gnorm, dskip, snorm, npost, nfpre,
               nfpost, wout, wgate, wup, wdown):
    bsz, seq, _ = x.shape
    n_tiles = seq // TM_FFN
    const = lambda b, i: (0, 0)

    def lat(width):
        return pl.BlockSpec((1, TM_FFN, width), lambda b, i: (b, i, 0))

    single = lambda shape: pl.BlockSpec(shape, const, pipeline_mode=pl.Buffered(1))
    vec = lambda w: pl.BlockSpec((1, w), const)
    return pl.pallas_call(
        _merge_ffn_kernel,
        grid=(bsz, n_tiles),
        in_specs=[pl.BlockSpec((1, TM_FFN, D_MODEL), lambda b, i: (b, i, 0)),
                  lat(GLA_INNER), lat(GLA_INNER), lat(SSD_INNER), lat(SSD_INNER),
                  lat(SSD_INNER), lat(GLA_INNER + SSD_INNER),
                  pl.BlockSpec(m_all.shape, const),
                  vec(GLA_INNER), vec(SSD_INNER), vec(SSD_INNER),
                  vec(D_MODEL), vec(D_MODEL), vec(D_MODEL),
                  single(wout.shape), single(wgate.shape), single(wup.shape), single(wdown.shape)],
        out_specs=pl.BlockSpec((1, TM_FFN, D_MODEL), lambda b, i: (b, i, 0)),
        out_shape=jax.ShapeDtypeStruct((bsz, seq, D_MODEL), F32),
        compiler_params=pltpu.CompilerParams(
            dimension_semantics=("parallel", "arbitrary"), vmem_limit_bytes=VMEM_LIMIT),
        name="merge_ffn",
    )(x, ygf, ygb, ysf, ysb, xbc, rz, m_all, gnorm, dskip, snorm, npost, nfpre, nfpost,
      wout, wgate, wup, wdown)


def kernel(x, c, ctx, c_ctx, w_mod, b_mod, norm_mix_pre, norm_mix_post, norm_ffn_pre, norm_ffn_post,
           w_in, conv_w, conv_b, gla_wg_f, gla_bg_f, gla_wg_b, gla_bg_b, gla_norm,
           a_log_f, a_log_b, dt_bias_f, dt_bias_b, d_skip, ssd_norm, w_out, w_gate, w_up, w_down):
    assert w_mod.shape[0] == 1, "single-layer kernel"
    bsz = x.shape[0]
    row = lambda v: v.reshape(1, -1)

    c_all = jnp.concatenate([c, c_ctx[None], jnp.zeros((8 - bsz - 1, D_MODEL), F32)], axis=0)
    wg_pad = jnp.zeros((TAIL_W, 2 * GLA_QK), F32)
    wg_pad = wg_pad.at[0:GLA_RANK, 0:GLA_QK].set(gla_wg_f[0])
    wg_pad = wg_pad.at[GLA_RANK:2 * GLA_RANK, GLA_QK:].set(gla_wg_b[0]).astype(BF16)
    bg_cat = jnp.concatenate([gla_bg_f[0], gla_bg_b[0]]).reshape(1, -1)
    lane_pad = lambda f, bk: jnp.zeros((1, TAIL_W), F32).at[0, DT_LANE:DT_LANE + 2 * SSD_HEADS].set(
        jnp.concatenate([f, bk]))
    dtb_pad = lane_pad(dt_bias_f[0], dt_bias_b[0])
    nega = lane_pad(-jnp.exp(a_log_f[0]) * LOG2_E, -jnp.exp(a_log_b[0]) * LOG2_E)
    conv_w9 = conv_w[0].reshape(CONV_K * CONV_K, SSD_CONV_DIM)

    m_all = _modulation(c_all, w_mod[0], row(b_mod[0]))
    qkv, rz, xbc, g, dt = _input_projection_conv(
        x, ctx, m_all, row(norm_mix_pre[0]), w_in[0].T, wg_pad, bg_cat, dtb_pad, conv_w9, row(conv_b[0]))
    ygf, ygb, ysf, ysb = _mixer_scan(qkv, g, xbc, dt, nega)
    return _merge_ffn(
        x, ygf, ygb, ysf, ysb, xbc, rz, m_all,
        row(jnp.tile(gla_norm[0], GLA_HEADS)), row(jnp.repeat(d_skip[0], SSD_P)), row(ssd_norm[0]),
        row(norm_mix_post[0]), row(norm_ffn_pre[0]), row(norm_ffn_post[0]),
        w_out[0].astype(BF16), w_gate[0].astype(BF16), w_up[0].astype(BF16), w_down[0].astype(BF16))
```

```python
import jax
import jax.numpy as jnp
from jax import lax
from jax.experimental import pallas as pl
from jax.experimental.pallas import tpu as pltpu

F32 = jnp.float32
BF16 = jnp.bfloat16

D_MODEL = 1024
CTX_LEN = 256
GRID_W = 64
EPS = 1e-6

GLA_HEADS = 4
GLA_DK = 64
GLA_DV = 128
GLA_QK = GLA_HEADS * GLA_DK
GLA_INNER = GLA_HEADS * GLA_DV
GLA_RANK = 16
GLA_GATE_NORM = 16.0

SSD_HEADS = 8
SSD_P = 64
SSD_N = 64
SSD_GROUPS = 2
SSD_HPG = SSD_HEADS // SSD_GROUPS
SSD_INNER = SSD_HEADS * SSD_P
SSD_BC = SSD_GROUPS * SSD_N
SSD_CONV_DIM = SSD_INNER + 2 * SSD_BC
CONV_K = 3
D_FF = 2816

QKV_W = 2 * GLA_QK + GLA_INNER
COL_R = QKV_W
COL_Z = COL_R + GLA_INNER
COL_XBC = COL_Z + SSD_INNER
COL_TAIL = COL_XBC + SSD_CONV_DIM
TAIL_W = 128
PROJ_W = COL_TAIL + TAIL_W
DT_LANE = 2 * GLA_RANK
W_IN_LR = COL_Z
W_IN_Z = W_IN_LR + 2 * GLA_RANK
W_IN_DT = W_IN_Z + SSD_INNER + SSD_CONV_DIM
W_IN_COLS = W_IN_DT + 2 * SSD_HEADS

TM_PROJ = 512
PROJ_SUB = CTX_LEN
MOD_ROWS = 8
MOD_TN = 1024
TM_FFN = 512
FFN_SUB = 128
W_CHUNK = 128
CHUNK = 128
SCAN_BATCH = 4
SUB = 32
NEG_BIG = -1e30
LOG2_E = 1.4426950408889634
VMEM_LIMIT = 56 * 1024 * 1024


def _silu(x):
    return x / (1.0 + jnp.exp(-x))


def _softplus(x):
    return jnp.maximum(x, 0.0) + jnp.log(1.0 + jnp.exp(-jnp.abs(x)))


def _rms(x):
    return x * lax.rsqrt(jnp.mean(x * x, axis=-1, keepdims=True) + EPS)


def _dot(a, b):
    return jnp.dot(a, b, preferred_element_type=F32)


def _dot_nt(a, b):
    return lax.dot_general(a, b, (((1,), (1,)), ((), ())), preferred_element_type=F32)


def _dot_tn(a, b):
    return lax.dot_general(a, b, (((0,), (0,)), ((), ())), preferred_element_type=F32)


def _mod_kernel(c_ref, w_ref, b_ref, o_ref):
    s = _silu(c_ref[...])
    o_ref[...] = _dot(s.astype(BF16), w_ref[...].astype(BF16)) + b_ref[...]


def _modulation(c_all, w_mod, b_mod):
    n = w_mod.shape[1]
    return pl.pallas_call(
        _mod_kernel,
        grid=(n // MOD_TN,),
        in_specs=[pl.BlockSpec((MOD_ROWS, D_MODEL), lambda j: (0, 0)),
                  pl.BlockSpec((D_MODEL, MOD_TN), lambda j: (0, j)),
                  pl.BlockSpec((1, MOD_TN), lambda j: (0, j))],
        out_specs=pl.BlockSpec((MOD_ROWS, MOD_TN), lambda j: (0, j)),
        out_shape=jax.ShapeDtypeStruct((MOD_ROWS, n), F32),
        name="adaln_mod",
    )(c_all, w_mod, b_mod)


def _permute_cast_w_in(wt_ref, w_s):
    blk = 128
    pad = jnp.zeros((TAIL_W - (W_IN_Z - W_IN_LR) - (W_IN_COLS - W_IN_DT), blk), F32)

    def body(kb, carry):
        k0 = pl.multiple_of(kb * blk, blk)
        cols = pl.ds(k0, blk)
        rows = pl.ds(k0, blk)
        w_s[rows, 0:W_IN_LR] = jnp.transpose(wt_ref[0:W_IN_LR, cols]).astype(BF16)
        w_s[rows, COL_Z:COL_TAIL] = jnp.transpose(wt_ref[W_IN_Z:W_IN_DT, cols]).astype(BF16)
        tail = jnp.concatenate([wt_ref[W_IN_LR:W_IN_Z, cols], wt_ref[W_IN_DT:W_IN_COLS, cols], pad],
                               axis=0)
        w_s[rows, COL_TAIL:PROJ_W] = jnp.transpose(tail).astype(BF16)
        return carry

    lax.fori_loop(0, D_MODEL // blk, body, 0)


def _grid_conv_rows(window, t0, is_ctx, w_ref, b_ref):
    n = window.shape[0] - 2 * GRID_W
    ext = window.astype(F32)
    row_on = jnp.where(is_ctx, 0.0, 1.0)
    taps = []
    for dx in range(CONV_K):
        tap = None
        for dy in range(CONV_K):
            wk = w_ref[CONV_K * dy + dx:CONV_K * dy + dx + 1, :]
            if dy != 1:
                wk = wk * row_on
            term = wk * ext[GRID_W * dy:GRID_W * dy + n, :]
            tap = term if tap is None else tap + term
        taps.append(tap)
    t = t0 + lax.broadcasted_iota(jnp.int32, (n, 1), 0)
    col = jnp.bitwise_and(t, GRID_W - 1)
    has_left = jnp.where(is_ctx, t, col) != 0
    has_right = jnp.where(is_ctx, t - (CTX_LEN - 1), col - (GRID_W - 1)) != 0
    acc = (taps[1] + b_ref[...]
           + jnp.where(has_left, pltpu.roll(taps[0], 1, axis=0), 0.0)
           + jnp.where(has_right, pltpu.roll(taps[2], n - 1, axis=0), 0.0))
    return _silu(acc)


def _inproj_conv_kernel(x_ref, ctx_ref, m_ref, gpre_ref, w_ref, wg_ref, bg_ref, dtb_ref,
                        cw_ref, cb_ref, qkv_ref, rz_ref, xbc_ref, g_ref, dt_ref,
                        w_s, ring, above_s):
    b = pl.program_id(0)
    i = pl.program_id(1)
    n_tiles = pl.num_programs(1) - 1
    ctx_tile = n_tiles - 1

    @pl.when(jnp.logical_and(b == 0, i == 0))
    def _():
        _permute_cast_w_in(w_ref, w_s)

    @pl.when(i == 0)
    def _():
        ring[...] = jnp.zeros_like(ring)
        above_s[...] = jnp.zeros_like(above_s)

    j = i - 1
    conv_ctx = j == ctx_tile
    split = TM_PROJ - GRID_W
    none = jnp.zeros((GRID_W, SSD_CONV_DIM), BF16)

    def conv_previous_head():
        main = ring[jnp.bitwise_and(j, 1)]
        above = jnp.where(jnp.logical_and(j >= 1, j < ctx_tile), above_s[...], none)
        window = jnp.concatenate([above, main], axis=0)
        xbc_ref[0, 0:split, :] = _grid_conv_rows(window, 0, conv_ctx, cw_ref, cb_ref).astype(BF16)
        above_s[...] = main[split:, :]

    def conv_previous_last_row(below):
        main = ring[jnp.bitwise_and(j, 1)]
        below = jnp.where(j < ctx_tile - 1, below, none)
        window = jnp.concatenate([main[split - GRID_W:, :], below], axis=0)
        xbc_ref[0, split:, :] = _grid_conv_rows(window, split, conv_ctx, cw_ref, cb_ref).astype(BF16)

    @pl.when(i < n_tiles)
    def _():
        is_ctx = i == ctx_tile
        mrow = m_ref[pl.ds(jnp.where(is_ctx, pl.num_programs(0), b), 1), :]
        shift = mrow[:, 0:D_MODEL]
        scale = mrow[:, D_MODEL:2 * D_MODEL]
        slot = jnp.bitwise_and(i, 1)

        def normed(rs):
            x = jnp.where(is_ctx, ctx_ref[0], x_ref[0, rs, :])
            return ((_rms(x) * gpre_ref[...]) * (1.0 + scale) + shift).astype(BF16)

        def project(rs, h):
            acc = _dot(h, w_s[...])
            qkv_ref[0, rs, 0:GLA_QK] = (acc[:, 0:GLA_QK] * (GLA_DK ** -0.5)).astype(BF16)
            qkv_ref[0, rs, GLA_QK:QKV_W] = acc[:, GLA_QK:QKV_W].astype(BF16)
            rz_ref[0, rs, :] = _silu(acc[:, COL_R:COL_XBC]).astype(BF16)
            tail = acc[:, COL_TAIL:PROJ_W]
            logits = _dot(tail.astype(BF16), wg_ref[...]) + bg_ref[...]
            g_ref[0, rs, :] = -_softplus(-logits) * (LOG2_E / GLA_GATE_NORM)
            dt_ref[0, rs, :] = _softplus(tail + dtb_ref[...])
            return acc[:, COL_XBC:COL_TAIL].astype(BF16)

        subs = [slice(PROJ_SUB * s, PROJ_SUB * (s + 1)) for s in range(TM_PROJ // PROJ_SUB)]
        ready = normed(subs[0])
        fresh = []
        for s, rs in enumerate(subs):
            following = normed(subs[s + 1]) if s + 1 < len(subs) else None
            fresh.append(project(rs, ready))
            if s == 0:
                conv_previous_head()
                conv_previous_last_row(fresh[0][0:GRID_W, :])
            ready = following
        for rs, xbc_new in zip(subs, fresh):
            ring[slot, rs, :] = xbc_new

    @pl.when(i == n_tiles)
    def _():
        conv_previous_head()
        conv_previous_last_row(none)


def _input_projection_conv(x, ctx, m_all, gpre, w_in, wg_pad, bg_cat, dtb_pad, conv_w9, conv_b):
    bsz, seq, _ = x.shape
    n_lat = seq // TM_PROJ
    n_tiles = n_lat + 1
    tt = CTX_LEN + seq
    const = lambda b, i: (0, 0)
    tok = lambda b, i: (b, jnp.minimum(i, n_tiles - 1), 0)
    conv_tok = lambda b, i: (b, jnp.maximum(i - 1, 0), 0)
    return pl.pallas_call(
        _inproj_conv_kernel,
        grid=(bsz, n_tiles + 1),
        in_specs=[pl.BlockSpec((1, TM_PROJ, D_MODEL), lambda b, i: (b, jnp.minimum(i, n_lat - 1), 0)),
                  pl.BlockSpec((1, CTX_LEN, D_MODEL), lambda b, i: (b, 0, 0)),
                  pl.BlockSpec(m_all.shape, const),
                  pl.BlockSpec((1, D_MODEL), const),
                  pl.BlockSpec((W_IN_COLS, D_MODEL), const, pipeline_mode=pl.Buffered(1)),
                  pl.BlockSpec((TAIL_W, 2 * GLA_QK), const),
                  pl.BlockSpec((1, 2 * GLA_QK), const),
                  pl.BlockSpec((1, TAIL_W), const),
                  pl.BlockSpec((CONV_K * CONV_K, SSD_CONV_DIM), const),
                  pl.BlockSpec((1, SSD_CONV_DIM), const)],
        out_specs=[pl.BlockSpec((1, TM_PROJ, QKV_W), tok),
                   pl.BlockSpec((1, TM_PROJ, GLA_INNER + SSD_INNER), tok),
                   pl.BlockSpec((1, TM_PROJ, SSD_CONV_DIM), conv_tok),
                   pl.BlockSpec((1, TM_PROJ, 2 * GLA_QK), tok),
                   pl.BlockSpec((1, TM_PROJ, TAIL_W), tok)],
        out_shape=[jax.ShapeDtypeStruct((bsz, tt, QKV_W), BF16),
                   jax.ShapeDtypeStruct((bsz, tt, GLA_INNER + SSD_INNER), BF16),
                   jax.ShapeDtypeStruct((bsz, tt, SSD_CONV_DIM), BF16),
                   jax.ShapeDtypeStruct((bsz, tt, 2 * GLA_QK), F32),
                   jax.ShapeDtypeStruct((bsz, tt, TAIL_W), F32)],
        scratch_shapes=[pltpu.VMEM((D_MODEL, PROJ_W), BF16),
                        pltpu.VMEM((2, TM_PROJ, SSD_CONV_DIM), BF16),
                        pltpu.VMEM((GRID_W, SSD_CONV_DIM), BF16)],
        compiler_params=pltpu.CompilerParams(
            dimension_semantics=("arbitrary", "arbitrary"), vmem_limit_bytes=VMEM_LIMIT),
        name="in_proj_conv",
    )(x, ctx, m_all, gpre, w_in, wg_pad, bg_cat, dtb_pad, conv_w9, conv_b)


def _cumsum_rows(x, reverse):
    n = x.shape[0]
    row = lax.broadcasted_iota(jnp.int32, x.shape, 0)
    s = 1
    while s < n:
        if reverse:
            x = x + jnp.where(row < n - s, pltpu.roll(x, n - s, axis=0), 0.0)
        else:
            x = x + jnp.where(row >= s, pltpu.roll(x, s, axis=0), 0.0)
        s *= 2
    return x


def _chunk_maps(n_ctx, n_all):
    n_lat = n_all - n_ctx
    fwd = lambda s: jnp.where(s < n_ctx, n_lat + s, s - n_ctx)
    bwd = lambda s: n_all - 1 - s
    return fwd, bwd


def _gla_prepare(qkv, g, reverse):
    n = qkv.shape[0]
    nb = n // SUB
    blocks = range(nb)
    rows = lambda a, blk: a[SUB * blk:SUB * (blk + 1), :]
    q = qkv[:, 0:GLA_QK].astype(F32)
    k = qkv[:, GLA_QK:2 * GLA_QK].astype(F32)
    cum = _cumsum_rows(g, reverse)
    zero = jnp.zeros((1, GLA_QK), F32)
    if reverse:
        ends = [cum[SUB * blk:SUB * blk + 1, :] for blk in blocks]
        refs = ends[1:] + [zero]
        earlier = lambda j, blk: j > blk
        tot = ends[0]
    else:
        ends = [cum[SUB * (blk + 1) - 1:SUB * (blk + 1), :] for blk in blocks]
        refs = [zero] + ends[:-1]
        earlier = lambda j, blk: j < blk
        tot = ends[nb - 1]
    stack = lambda vs: jnp.concatenate([jnp.broadcast_to(v, (SUB, GLA_QK)) for v in vs], axis=0)
    e_in = cum - stack(refs)
    q_blk = q * jnp.exp2(e_in)
    k_diag = k * jnp.exp2(-e_in)
    k_end = k * jnp.exp2(stack(ends) - cum)
    q_in = jnp.concatenate([rows(q_blk, blk) * jnp.exp2(refs[blk]) for blk in blocks], axis=0)
    k_out = jnp.concatenate([rows(k_end, blk) * jnp.exp2(tot - ends[blk]) for blk in blocks], axis=0)
    q_in = q_in.astype(BF16)
    k_out = k_out.astype(BF16)
    q_bf = q_blk.astype(BF16)
    zeros_piece = jnp.zeros((SUB, GLA_QK), BF16)
    k_piece = [[rows(k_diag, j).astype(BF16) if seg == j
                else (rows(k_end, j) * jnp.exp2(refs[seg] - ends[j])).astype(BF16) if earlier(j, seg)
                else zeros_piece
                for seg in blocks] for j in blocks]
    ii = lax.broadcasted_iota(jnp.int32, (n, n), 0)
    jj = lax.broadcasted_iota(jnp.int32, (n, n), 1)
    heads = []
    for h in range(GLA_HEADS):
        ks = slice(GLA_DK * h, GLA_DK * (h + 1))
        q_cat = jnp.concatenate(
            [jnp.concatenate([rows(q_bf, blk)[:, ks] if seg == blk else zeros_piece[:, ks]
                              for seg in blocks], axis=1) for blk in blocks], axis=0)
        k_cat = jnp.concatenate(
            [jnp.concatenate([k_piece[j][seg][:, ks] for seg in blocks], axis=1)
             for j in blocks], axis=0)
        heads.append(dict(q_cat=q_cat, k_cat=k_cat, q_in=q_in[:, ks], k_out=k_out[:, ks],
                          v=qkv[:, 2 * GLA_QK + GLA_DV * h:2 * GLA_QK + GLA_DV * (h + 1)]))
    return dict(heads=heads, causal=(ii <= jj) if reverse else (ii >= jj),
                decay_col=jnp.transpose(jnp.broadcast_to(tot, (GLA_DV, GLA_QK))))


def _gla_chain(qkv_ref, g_ref, reverse, s_ref, y_ref):
    def start():
        p = _gla_prepare(qkv_ref[...], g_ref[...], reverse)
        return p, [_dot_nt(hd["q_cat"], hd["k_cat"]) for hd in p["heads"]]

    def finish(started):
        p, att = started
        outs = []
        for h, (hd, a) in enumerate(zip(p["heads"], att)):
            state = s_ref[h]
            lhs = jnp.concatenate([jnp.where(p["causal"], a, 0.0).astype(BF16), hd["q_in"]], axis=1)
            rhs = jnp.concatenate([hd["v"], state.astype(BF16)], axis=0)
            outs.append(_dot(lhs, rhs))
            ks = slice(GLA_DK * h, GLA_DK * (h + 1))
            s_ref[h] = state * jnp.exp2(p["decay_col"][ks, :]) + _dot_tn(hd["k_out"], hd["v"])
        y_ref[...] = jnp.concatenate(outs, axis=1).astype(y_ref.dtype)

    return start, finish


def _ssd_prepare(xbc, dt, nega, expand, lane0, reverse):
    n = xbc.shape[0]
    end = 0 if reverse else n - 1
    cum = _cumsum_rows(dt * nega, reverse)
    e_cum = jnp.exp2(cum)
    w = jnp.exp2(cum[end:end + 1, :] - cum) * dt
    e_cum_x = _dot(_split_hi_lo(e_cum), expand)
    x_w = (xbc[:, 0:SSD_INNER].astype(F32) * _dot(_split_hi_lo(w), expand)).astype(BF16)
    cum_t = jnp.transpose(cum)
    dt_t = jnp.transpose(dt)
    ii = lax.broadcasted_iota(jnp.int32, (n, n), 0)
    jj = lax.broadcasted_iota(jnp.int32, (n, n), 1)
    causal = (ii <= jj) if reverse else (ii >= jj)
    first_head = lax.broadcasted_iota(jnp.int32, (n, 2 * SSD_P), 1) < SSD_P
    groups = []
    for grp in range(SSD_GROUPS):
        b_g = xbc[:, SSD_INNER + SSD_N * grp:SSD_INNER + SSD_N * (grp + 1)]
        c_g = xbc[:, SSD_INNER + SSD_BC + SSD_N * grp:SSD_INNER + SSD_BC + SSD_N * (grp + 1)]
        scores = _dot_nt(c_g, b_g)
        pairs = []
        for pair in range(SSD_HPG // 2):
            m_pair = []
            for hh in range(2):
                lane = lane0 + SSD_HPG * grp + 2 * pair + hh
                cum_b = jnp.broadcast_to(cum[:, lane:lane + 1], (n, n))
                seg = jnp.where(causal, cum_b - cum_t[lane:lane + 1, :], NEG_BIG)
                m_pair.append((scores * jnp.exp2(seg) * dt_t[lane:lane + 1, :]).astype(BF16))
            col = 2 * SSD_P * (SSD_HPG // 2 * grp + pair)
            x_pair = xbc[:, col:col + 2 * SSD_P]
            none = jnp.zeros_like(x_pair)
            rhs = jnp.concatenate([jnp.where(first_head, x_pair, none),
                                   jnp.where(first_head, none, x_pair)], axis=0)
            pairs.append((jnp.concatenate(m_pair, axis=1), rhs))
        cols = slice(SSD_HPG * SSD_P * grp, SSD_HPG * SSD_P * (grp + 1))
        groups.append(dict(b=b_g, c=c_g, pairs=pairs, e_cum=e_cum_x[:, cols],
                           e_tot=e_cum_x[end:end + 1, cols], x_w=x_w[:, cols]))
    return groups


def _split_hi_lo(a):
    hi = a.astype(BF16)
    lo = (a - hi.astype(F32)).astype(BF16)
    return jnp.concatenate([hi, lo], axis=1)


def _ssd_chain(xbc_ref, dt_ref, nega, expand_ref, d, s_ref, y_ref):
    def start():
        return _ssd_prepare(xbc_ref[...], dt_ref[...], nega, expand_ref[d],
                            DT_LANE + SSD_HEADS * d, d == 1)

    def finish(groups):
        outs = []
        for grp, gd in enumerate(groups):
            state = s_ref[grp]
            intra = jnp.concatenate([_dot(m, rhs) for m, rhs in gd["pairs"]], axis=1)
            outs.append(_dot(gd["c"], state.astype(BF16)) * gd["e_cum"] + intra)
            s_ref[grp] = state * gd["e_tot"] + _dot_tn(gd["b"], gd["x_w"])
        y_ref[...] = jnp.concatenate(outs, axis=1).astype(y_ref.dtype)

    return start, finish


def _mixer_scan_kernel(qkv_f_ref, g_f_ref, qkv_b_ref, g_b_ref, xbc_f_ref, dt_f_ref, xbc_b_ref,
                       dt_b_ref, nega_ref, expand_ref, ygf_ref, ygb_ref, ysf_ref, ysb_ref,
                       gsf_ref, gsb_ref, ssf_ref, ssb_ref):
    @pl.when(pl.program_id(1) == 0)
    def _():
        for s_ref in (gsf_ref, gsb_ref, ssf_ref, ssb_ref):
            s_ref[...] = jnp.zeros_like(s_ref)

    nega = nega_ref[...]
    chains = []
    for e in range(qkv_f_ref.shape[0]):
        chains.append(_gla_chain(qkv_f_ref.at[e], g_f_ref.at[e], False, gsf_ref.at[e], ygf_ref.at[e]))
        chains.append(_ssd_chain(xbc_f_ref.at[e], dt_f_ref.at[e], nega, expand_ref, 0,
                                 ssf_ref.at[e], ysf_ref.at[e]))
        chains.append(_gla_chain(qkv_b_ref.at[e], g_b_ref.at[e], True, gsb_ref.at[e], ygb_ref.at[e]))
        chains.append(_ssd_chain(xbc_b_ref.at[e], dt_b_ref.at[e], nega, expand_ref, 1,
                                 ssb_ref.at[e], ysb_ref.at[e]))
    started = chains[0][0]()
    for i, (_, finish) in enumerate(chains):
        following = chains[i + 1][0]() if i + 1 < len(chains) else None
        finish(started)
        started = following


def _head_expand_matrix():
    r = jnp.arange(2 * TAIL_W)[:, None] % TAIL_W
    c = jnp.arange(SSD_INNER)[None, :] // SSD_P
    return jnp.stack([(r == DT_LANE + SSD_HEADS * d + c) for d in range(2)]).astype(BF16)


def _mixer_scan(qkv, g, xbc, dt, nega):
    bsz, tt, _ = qkv.shape
    n_all = tt // CHUNK
    fwd, bwd = _chunk_maps(CTX_LEN // CHUNK, n_all)
    spec = lambda w, m, c=0: pl.BlockSpec((SCAN_BATCH, CHUNK, w), lambda b, s: (b, m(s), c))
    expand = _head_expand_matrix()
    return pl.pallas_call(
        _mixer_scan_kernel,
        grid=(bsz // SCAN_BATCH, n_all),
        in_specs=[spec(QKV_W, fwd), spec(GLA_QK, fwd), spec(QKV_W, bwd), spec(GLA_QK, bwd, 1),
                  spec(SSD_CONV_DIM, fwd), spec(TAIL_W, fwd), spec(SSD_CONV_DIM, bwd), spec(TAIL_W, bwd),
                  pl.BlockSpec((1, TAIL_W), lambda b, s: (0, 0)),
                  pl.BlockSpec(expand.shape, lambda b, s: (0, 0, 0))],
        out_specs=[spec(GLA_INNER, fwd), spec(GLA_INNER, bwd), spec(SSD_INNER, fwd), spec(SSD_INNER, bwd)],
        out_shape=[jax.ShapeDtypeStruct((bsz, tt, GLA_INNER), BF16)] * 2
        + [jax.ShapeDtypeStruct((bsz, tt, SSD_INNER), BF16)] * 2,
        scratch_shapes=[pltpu.VMEM((SCAN_BATCH, GLA_HEADS, GLA_DK, GLA_DV), F32)] * 2
        + [pltpu.VMEM((SCAN_BATCH, SSD_GROUPS, SSD_N, SSD_HPG * SSD_P), F32)] * 2,
        compiler_params=pltpu.CompilerParams(
            dimension_semantics=("parallel", "arbitrary"), vmem_limit_bytes=VMEM_LIMIT),
        name="mixer_scan",
    )(qkv, g, qkv, g, xbc, dt, xbc, dt, nega, expand)


def _stream_cast_weights(pairs, stage, sem):
    def chunk_copy(w_hbm, c, slot):
        cols = w_hbm.shape[1]
        return pltpu.make_async_copy(w_hbm.at[pl.ds(c * W_CHUNK, W_CHUNK), :],
                                     stage.at[slot, :, 0:cols], sem.at[slot])

    chunk_copy(pairs[0][0], 0, 0).start()
    for k, (w_hbm, w_bf) in enumerate(pairs):
        n_chunks = w_hbm.shape[0] // W_CHUNK
        assert n_chunks % 2 == 0 and w_hbm.shape[0] % W_CHUNK == 0
        cols = w_hbm.shape[1]
        following = pairs[k + 1][0] if k + 1 < len(pairs) else None

        def body(c, carry, w_hbm=w_hbm, w_bf=w_bf, n_chunks=n_chunks, cols=cols, following=following):
            slot = jnp.bitwise_and(c, 1)

            @pl.when(c + 1 < n_chunks)
            def _():
                chunk_copy(w_hbm, c + 1, 1 - slot).start()

            if following is not None:
                @pl.when(c + 1 == n_chunks)
                def _():
                    chunk_copy(following, 0, 0).start()

            chunk_copy(w_hbm, c, slot).wait()
            rows = pl.ds(pl.multiple_of(c * W_CHUNK, W_CHUNK), W_CHUNK)
            w_bf[rows, :] = stage[slot, :, 0:cols].astype(BF16)
            return carry

        lax.fori_loop(0, n_chunks, body, 0)


def _merge_ffn_kernel(x_ref, ygf_ref, ygb_ref, ysf_ref, ysb_ref, xs_ref, rz_ref, m_ref,
                      gnorm_ref, dskip_ref, snorm_ref, npost_ref, nfpre_ref, nfpost_ref,
                      wout_hbm, wgate_hbm, wup_hbm, wdown_hbm, o_ref,
                      wout_ref, wgate_ref, wup_ref, wdown_ref, stage, sem):
    b = pl.program_id(0)

    @pl.when(jnp.logical_and(b == 0, pl.program_id(1) == 0))
    def _():
        _stream_cast_weights([(wout_hbm, wout_ref), (wgate_hbm, wgate_ref),
                              (wup_hbm, wup_ref), (wdown_hbm, wdown_ref)], stage, sem)

    mrow = m_ref[pl.ds(b, 1), :]
    mod = [mrow[:, D_MODEL * j:D_MODEL * (j + 1)] for j in range(6)]
    gw = SSD_INNER // SSD_GROUPS

    def mixer_out(rs):
        y_gla = ygf_ref[0, rs, :].astype(F32) + ygb_ref[0, rs, :].astype(F32)
        o_gla = jnp.concatenate(
            [_rms(y_gla[:, GLA_DV * h:GLA_DV * (h + 1)]) for h in range(GLA_HEADS)], axis=1)
        o_gla = o_gla * gnorm_ref[...] * rz_ref[0, rs, 0:GLA_INNER].astype(F32)
        y_ssd = ysf_ref[0, rs, :].astype(F32) + ysb_ref[0, rs, :].astype(F32)
        y_ssd = ((y_ssd + dskip_ref[...] * xs_ref[0, rs, :].astype(F32))
                 * rz_ref[0, rs, GLA_INNER:].astype(F32))
        o_ssd = jnp.concatenate(
            [_rms(y_ssd[:, gw * g:gw * (g + 1)]) for g in range(SSD_GROUPS)], axis=1)
        o_ssd = o_ssd * snorm_ref[...]
        o = jnp.concatenate([o_gla, o_ssd], axis=1).astype(BF16)
        y = _dot(o, wout_ref[...])
        x1 = x_ref[0, rs, :] + mod[2] * (_rms(y) * npost_ref[...])
        h = ((_rms(x1) * nfpre_ref[...]) * (1.0 + mod[4]) + mod[3]).astype(BF16)
        return x1, h

    def ffn(rs, x1, h):
        act = (_silu(_dot(h, wgate_ref[...])) * _dot(h, wup_ref[...])).astype(BF16)
        f = _dot(act, wdown_ref[...])
        o_ref[0, rs, :] = x1 + mod[5] * (_rms(f) * nfpost_ref[...])

    subs = [slice(FFN_SUB * j, FFN_SUB * (j + 1)) for j in range(TM_FFN // FFN_SUB)]
    ready = mixer_out(subs[0])
    for j, rs in enumerate(subs):
        following = mixer_out(subs[j + 1]) if j + 1 < len(subs) else None
        ffn(rs, *ready)
        ready = following


def _merge_ffn(x, ygf, ygb, ysf, ysb, xbc, rz, m_all, gnorm, dskip, snorm, npost, nfpre,
               nfpost, wout, wgate, wup, wdown):
    bsz, seq, _ = x.shape
    n_tiles = seq // TM_FFN
    const = lambda b, i: (0, 0)

    def lat(width):
        return pl.BlockSpec((1, TM_FFN, width), lambda b, i: (b, i, 0))

    weights = (wout, wgate, wup, wdown)
    in_hbm = pl.BlockSpec(memory_space=pl.ANY)
    vec = lambda w: pl.BlockSpec((1, w), const)
    return pl.pallas_call(
        _merge_ffn_kernel,
        grid=(bsz, n_tiles),
        in_specs=[pl.BlockSpec((1, TM_FFN, D_MODEL), lambda b, i: (b, i, 0)),
                  lat(GLA_INNER), lat(GLA_INNER), lat(SSD_INNER), lat(SSD_INNER),
                  lat(SSD_INNER), lat(GLA_INNER + SSD_INNER),
                  pl.BlockSpec(m_all.shape, const),
                  vec(GLA_INNER), vec(SSD_INNER), vec(SSD_INNER),
                  vec(D_MODEL), vec(D_MODEL), vec(D_MODEL),
                  in_hbm, in_hbm, in_hbm, in_hbm],
        out_specs=pl.BlockSpec((1, TM_FFN, D_MODEL), lambda b, i: (b, i, 0)),
        out_shape=jax.ShapeDtypeStruct((bsz, seq, D_MODEL), F32),
        scratch_shapes=[pltpu.VMEM(w.shape, BF16) for w in weights]
        + [pltpu.VMEM((2, W_CHUNK, max(w.shape[1] for w in weights)), F32),
           pltpu.SemaphoreType.DMA((2,))],
        compiler_params=pltpu.CompilerParams(
            dimension_semantics=("arbitrary", "arbitrary"), vmem_limit_bytes=VMEM_LIMIT),
        name="merge_ffn",
    )(x, ygf, ygb, ysf, ysb, xbc, rz, m_all, gnorm, dskip, snorm, npost, nfpre, nfpost,
      wout, wgate, wup, wdown)


def kernel(x, c, ctx, c_ctx, w_mod, b_mod, norm_mix_pre, norm_mix_post, norm_ffn_pre, norm_ffn_post,
           w_in, conv_w, conv_b, gla_wg_f, gla_bg_f, gla_wg_b, gla_bg_b, gla_norm,
           a_log_f, a_log_b, dt_bias_f, dt_bias_b, d_skip, ssd_norm, w_out, w_gate, w_up, w_down):
    assert w_mod.shape[0] == 1, "single-layer kernel"
    bsz = x.shape[0]
    row = lambda v: v.reshape(1, -1)

    c_all = jnp.concatenate([c, c_ctx[None], jnp.zeros((MOD_ROWS - bsz - 1, D_MODEL), F32)], axis=0)
    wg_pad = jnp.zeros((TAIL_W, 2 * GLA_QK), F32)
    wg_pad = wg_pad.at[0:GLA_RANK, 0:GLA_QK].set(gla_wg_f[0])
    wg_pad = wg_pad.at[GLA_RANK:2 * GLA_RANK, GLA_QK:].set(gla_wg_b[0]).astype(BF16)
    bg_cat = jnp.concatenate([gla_bg_f[0], gla_bg_b[0]]).reshape(1, -1)
    lane_pad = lambda f, bk: jnp.zeros((1, TAIL_W), F32).at[0, DT_LANE:DT_LANE + 2 * SSD_HEADS].set(
        jnp.concatenate([f, bk]))
    dtb_pad = lane_pad(dt_bias_f[0], dt_bias_b[0])
    nega = lane_pad(-jnp.exp(a_log_f[0]) * LOG2_E, -jnp.exp(a_log_b[0]) * LOG2_E)
    conv_w9 = conv_w[0].reshape(CONV_K * CONV_K, SSD_CONV_DIM)

    m_all = _modulation(c_all, w_mod[0], row(b_mod[0]))
    qkv, rz, xbc, g, dt = _input_projection_conv(
        x, ctx, m_all, row(norm_mix_pre[0]), w_in[0].T, wg_pad, bg_cat, dtb_pad, conv_w9, row(conv_b[0]))
    ygf, ygb, ysf, ysb = _mixer_scan(qkv, g, xbc, dt, nega)
    return _merge_ffn(
        x, ygf, ygb, ysf, ysb, xbc, rz, m_all,
        row(jnp.tile(gla_norm[0], GLA_HEADS)), row(jnp.repeat(d_skip[0], SSD_P)), row(ssd_norm[0]),
        row(norm_mix_post[0]), row(norm_ffn_pre[0]), row(norm_ffn_post[0]),
        w_out[0], w_gate[0], w_up[0], w_down[0])
```

```python
import jax
import jax.numpy as jnp
from jax import lax
from jax.experimental import pallas as pl
from jax.experimental.pallas import tpu as pltpu

F32 = jnp.float32
BF16 = jnp.bfloat16

D_MODEL = 1024
CTX_LEN = 256
GRID_W = 64
EPS = 1e-6

GLA_HEADS = 4
GLA_DK = 64
GLA_DV = 128
GLA_QK = GLA_HEADS * GLA_DK
GLA_INNER = GLA_HEADS * GLA_DV
GLA_RANK = 16
GLA_GATE_NORM = 16.0

SSD_HEADS = 8
SSD_P = 64
SSD_N = 64
SSD_GROUPS = 2
SSD_HPG = SSD_HEADS // SSD_GROUPS
SSD_INNER = SSD_HEADS * SSD_P
SSD_BC = SSD_GROUPS * SSD_N
SSD_CONV_DIM = SSD_INNER + 2 * SSD_BC
CONV_K = 3
D_FF = 2816

QKV_W = 2 * GLA_QK + GLA_INNER
COL_R = QKV_W
COL_Z = COL_R + GLA_INNER
COL_XBC = COL_Z + SSD_INNER
COL_TAIL = COL_XBC + SSD_CONV_DIM
TAIL_W = 128
PROJ_W = COL_TAIL + TAIL_W
DT_LANE = 2 * GLA_RANK
W_IN_LR = COL_Z
W_IN_Z = W_IN_LR + 2 * GLA_RANK
W_IN_DT = W_IN_Z + SSD_INNER + SSD_CONV_DIM
W_IN_COLS = W_IN_DT + 2 * SSD_HEADS

TM_PROJ = 512
PROJ_SUB = CTX_LEN
MOD_ROWS = 8
MOD_TN = 1024
TM_FFN = 512
FFN_SUB = 128
W_CHUNK = 256
CHUNK = 128
SCAN_BATCH = 4
SUB = 32
NEG_BIG = -1e30
LOG2_E = 1.4426950408889634
VMEM_LIMIT = 56 * 1024 * 1024


def _silu(x):
    return x / (1.0 + jnp.exp(-x))


def _softplus(x):
    return jnp.maximum(x, 0.0) + jnp.log(1.0 + jnp.exp(-jnp.abs(x)))


def _rms(x):
    return x * lax.rsqrt(jnp.mean(x * x, axis=-1, keepdims=True) + EPS)


def _dot(a, b):
    return jnp.dot(a, b, preferred_element_type=F32)


def _dot_nt(a, b):
    return lax.dot_general(a, b, (((1,), (1,)), ((), ())), preferred_element_type=F32)


def _dot_tn(a, b):
    return lax.dot_general(a, b, (((0,), (0,)), ((), ())), preferred_element_type=F32)


def _mod_kernel(c_ref, w_ref, b_ref, o_ref):
    s = _silu(c_ref[...])
    o_ref[...] = _dot(s.astype(BF16), w_ref[...].astype(BF16)) + b_ref[...]


def _modulation(c_all, w_mod, b_mod):
    n = w_mod.shape[1]
    return pl.pallas_call(
        _mod_kernel,
        grid=(n // MOD_TN,),
        in_specs=[pl.BlockSpec((MOD_ROWS, D_MODEL), lambda j: (0, 0)),
                  pl.BlockSpec((D_MODEL, MOD_TN), lambda j: (0, j)),
                  pl.BlockSpec((1, MOD_TN), lambda j: (0, j))],
        out_specs=pl.BlockSpec((MOD_ROWS, MOD_TN), lambda j: (0, j)),
        out_shape=jax.ShapeDtypeStruct((MOD_ROWS, n), F32),
        name="adaln_mod",
    )(c_all, w_mod, b_mod)


def _permute_cast_w_in(wt_ref, w_s):
    blk = 128
    pad = jnp.zeros((TAIL_W - (W_IN_Z - W_IN_LR) - (W_IN_COLS - W_IN_DT), blk), F32)

    def body(kb, carry):
        k0 = pl.multiple_of(kb * blk, blk)
        cols = pl.ds(k0, blk)
        rows = pl.ds(k0, blk)
        w_s[rows, 0:W_IN_LR] = jnp.transpose(wt_ref[0:W_IN_LR, cols]).astype(BF16)
        w_s[rows, COL_Z:COL_TAIL] = jnp.transpose(wt_ref[W_IN_Z:W_IN_DT, cols]).astype(BF16)
        tail = jnp.concatenate([wt_ref[W_IN_LR:W_IN_Z, cols], wt_ref[W_IN_DT:W_IN_COLS, cols], pad],
                               axis=0)
        w_s[rows, COL_TAIL:PROJ_W] = jnp.transpose(tail).astype(BF16)
        return carry

    lax.fori_loop(0, D_MODEL // blk, body, 0)


def _grid_conv_rows(window, t0, is_ctx, w_ref, b_ref):
    n = window.shape[0] - 2 * GRID_W
    ext = window.astype(F32)
    row_on = jnp.where(is_ctx, 0.0, 1.0)
    taps = []
    for dx in range(CONV_K):
        tap = None
        for dy in range(CONV_K):
            wk = w_ref[CONV_K * dy + dx:CONV_K * dy + dx + 1, :]
            if dy != 1:
                wk = wk * row_on
            term = wk * ext[GRID_W * dy:GRID_W * dy + n, :]
            tap = term if tap is None else tap + term
        taps.append(tap)
    t = t0 + lax.broadcasted_iota(jnp.int32, (n, 1), 0)
    col = jnp.bitwise_and(t, GRID_W - 1)
    has_left = jnp.where(is_ctx, t, col) != 0
    has_right = jnp.where(is_ctx, t - (CTX_LEN - 1), col - (GRID_W - 1)) != 0
    acc = (taps[1] + b_ref[...]
           + jnp.where(has_left, pltpu.roll(taps[0], 1, axis=0), 0.0)
           + jnp.where(has_right, pltpu.roll(taps[2], n - 1, axis=0), 0.0))
    return _silu(acc)


def _inproj_conv_kernel(x_ref, ctx_ref, m_ref, gpre_ref, w_ref, wg_ref, bg_ref, dtb_ref,
                        cw_ref, cb_ref, qkv_ref, rz_ref, xbc_ref, g_ref, dt_ref,
                        w_s, ring, above_s):
    b = pl.program_id(0)
    i = pl.program_id(1)
    n_tiles = pl.num_programs(1) - 1
    ctx_tile = n_tiles - 1

    @pl.when(jnp.logical_and(b == 0, i == 0))
    def _():
        _permute_cast_w_in(w_ref, w_s)

    @pl.when(i == 0)
    def _():
        ring[...] = jnp.zeros_like(ring)
        above_s[...] = jnp.zeros_like(above_s)

    j = i - 1
    conv_ctx = j == ctx_tile
    split = TM_PROJ - GRID_W
    none = jnp.zeros((GRID_W, SSD_CONV_DIM), BF16)

    def conv_previous_head():
        main = ring[jnp.bitwise_and(j, 1)]
        above = jnp.where(jnp.logical_and(j >= 1, j < ctx_tile), above_s[...], none)
        window = jnp.concatenate([above, main], axis=0)
        xbc_ref[0, 0:split, :] = _grid_conv_rows(window, 0, conv_ctx, cw_ref, cb_ref).astype(BF16)
        above_s[...] = main[split:, :]

    def conv_previous_last_row(below):
        main = ring[jnp.bitwise_and(j, 1)]
        below = jnp.where(j < ctx_tile - 1, below, none)
        window = jnp.concatenate([main[split - GRID_W:, :], below], axis=0)
        xbc_ref[0, split:, :] = _grid_conv_rows(window, split, conv_ctx, cw_ref, cb_ref).astype(BF16)

    @pl.when(i < n_tiles)
    def _():
        is_ctx = i == ctx_tile
        mrow = m_ref[pl.ds(jnp.where(is_ctx, pl.num_programs(0), b), 1), :]
        shift = mrow[:, 0:D_MODEL]
        gain = gpre_ref[...] * (1.0 + mrow[:, D_MODEL:2 * D_MODEL])
        slot = jnp.bitwise_and(i, 1)

        def normed(rs):
            x = jnp.where(is_ctx, ctx_ref[0], x_ref[0, rs, :])
            return (_rms(x) * gain + shift).astype(BF16)

        def project(rs, h):
            acc = _dot(h, w_s[...])
            qkv_ref[0, rs, 0:GLA_QK] = (acc[:, 0:GLA_QK] * (GLA_DK ** -0.5)).astype(BF16)
            qkv_ref[0, rs, GLA_QK:QKV_W] = acc[:, GLA_QK:QKV_W].astype(BF16)
            rz_ref[0, rs, :] = _silu(acc[:, COL_R:COL_XBC]).astype(BF16)
            tail = acc[:, COL_TAIL:PROJ_W]
            logits = _dot(tail.astype(BF16), wg_ref[...]) + bg_ref[...]
            g_ref[0, rs, :] = -_softplus(-logits) * (LOG2_E / GLA_GATE_NORM)
            dt_ref[0, rs, :] = _softplus(tail + dtb_ref[...])
            return acc[:, COL_XBC:COL_TAIL].astype(BF16)

        subs = [slice(PROJ_SUB * s, PROJ_SUB * (s + 1)) for s in range(TM_PROJ // PROJ_SUB)]
        ready = normed(subs[0])
        fresh = []
        for s, rs in enumerate(subs):
            following = normed(subs[s + 1]) if s + 1 < len(subs) else None
            fresh.append(project(rs, ready))
            if s == 0:
                conv_previous_head()
                conv_previous_last_row(fresh[0][0:GRID_W, :])
            ready = following
        for rs, xbc_new in zip(subs, fresh):
            ring[slot, rs, :] = xbc_new

    @pl.when(i == n_tiles)
    def _():
        conv_previous_head()
        conv_previous_last_row(none)


def _input_projection_conv(x, ctx, m_all, gpre, w_in, wg_pad, bg_cat, dtb_pad, conv_w9, conv_b):
    bsz, seq, _ = x.shape
    n_lat = seq // TM_PROJ
    n_tiles = n_lat + 1
    tt = CTX_LEN + seq
    const = lambda b, i: (0, 0)
    tok = lambda b, i: (b, jnp.minimum(i, n_tiles - 1), 0)
    conv_tok = lambda b, i: (b, jnp.maximum(i - 1, 0), 0)
    return pl.pallas_call(
        _inproj_conv_kernel,
        grid=(bsz, n_tiles + 1),
        in_specs=[pl.BlockSpec((1, TM_PROJ, D_MODEL), lambda b, i: (b, jnp.minimum(i, n_lat - 1), 0)),
                  pl.BlockSpec((1, CTX_LEN, D_MODEL), lambda b, i: (b, 0, 0)),
                  pl.BlockSpec(m_all.shape, const),
                  pl.BlockSpec((1, D_MODEL), const),
                  pl.BlockSpec((W_IN_COLS, D_MODEL), const, pipeline_mode=pl.Buffered(1)),
                  pl.BlockSpec((TAIL_W, 2 * GLA_QK), const),
                  pl.BlockSpec((1, 2 * GLA_QK), const),
                  pl.BlockSpec((1, TAIL_W), const),
                  pl.BlockSpec((CONV_K * CONV_K, SSD_CONV_DIM), const),
                  pl.BlockSpec((1, SSD_CONV_DIM), const)],
        out_specs=[pl.BlockSpec((1, TM_PROJ, QKV_W), tok),
                   pl.BlockSpec((1, TM_PROJ, GLA_INNER + SSD_INNER), tok),
                   pl.BlockSpec((1, TM_PROJ, SSD_CONV_DIM), conv_tok),
                   pl.BlockSpec((1, TM_PROJ, 2 * GLA_QK), tok),
                   pl.BlockSpec((1, TM_PROJ, TAIL_W), tok)],
        out_shape=[jax.ShapeDtypeStruct((bsz, tt, QKV_W), BF16),
                   jax.ShapeDtypeStruct((bsz, tt, GLA_INNER + SSD_INNER), BF16),
                   jax.ShapeDtypeStruct((bsz, tt, SSD_CONV_DIM), BF16),
                   jax.ShapeDtypeStruct((bsz, tt, 2 * GLA_QK), F32),
                   jax.ShapeDtypeStruct((bsz, tt, TAIL_W), F32)],
        scratch_shapes=[pltpu.VMEM((D_MODEL, PROJ_W), BF16),
                        pltpu.VMEM((2, TM_PROJ, SSD_CONV_DIM), BF16),
                        pltpu.VMEM((GRID_W, SSD_CONV_DIM), BF16)],
        compiler_params=pltpu.CompilerParams(
            dimension_semantics=("arbitrary", "arbitrary"), vmem_limit_bytes=VMEM_LIMIT),
        name="in_proj_conv",
    )(x, ctx, m_all, gpre, w_in, wg_pad, bg_cat, dtb_pad, conv_w9, conv_b)


def _cumsum_rows(x, reverse):
    n = x.shape[0]
    row = lax.broadcasted_iota(jnp.int32, x.shape, 0)
    s = 1
    while s < n:
        if reverse:
            x = x + jnp.where(row < n - s, pltpu.roll(x, n - s, axis=0), 0.0)
        else:
            x = x + jnp.where(row >= s, pltpu.roll(x, s, axis=0), 0.0)
        s *= 2
    return x


def _chunk_maps(n_ctx, n_all):
    n_lat = n_all - n_ctx
    fwd = lambda s: jnp.where(s < n_ctx, n_lat + s, s - n_ctx)
    bwd = lambda s: n_all - 1 - s
    return fwd, bwd


def _gla_prepare(qkv, g, reverse):
    n = qkv.shape[0]
    nb = n // SUB
    blocks = range(nb)
    rows = lambda a, blk: a[SUB * blk:SUB * (blk + 1), :]
    q = qkv[:, 0:GLA_QK].astype(F32)
    k = qkv[:, GLA_QK:2 * GLA_QK].astype(F32)
    cum = _cumsum_rows(g, reverse)
    zero = jnp.zeros((1, GLA_QK), F32)
    if reverse:
        ends = [cum[SUB * blk:SUB * blk + 1, :] for blk in blocks]
        refs = ends[1:] + [zero]
        earlier = lambda j, blk: j > blk
        tot = ends[0]
    else:
        ends = [cum[SUB * (blk + 1) - 1:SUB * (blk + 1), :] for blk in blocks]
        refs = [zero] + ends[:-1]
        earlier = lambda j, blk: j < blk
        tot = ends[nb - 1]
    stack = lambda vs: jnp.concatenate([jnp.broadcast_to(v, (SUB, GLA_QK)) for v in vs], axis=0)
    e_in = cum - stack(refs)
    q_blk = q * jnp.exp2(e_in)
    k_diag = k * jnp.exp2(-e_in)
    k_end = k * jnp.exp2(stack(ends) - cum)
    q_in = jnp.concatenate([rows(q_blk, blk) * jnp.exp2(refs[blk]) for blk in blocks], axis=0)
    k_out = jnp.concatenate([rows(k_end, blk) * jnp.exp2(tot - ends[blk]) for blk in blocks], axis=0)
    q_in = q_in.astype(BF16)
    k_out = k_out.astype(BF16)
    q_bf = q_blk.astype(BF16)
    zeros_piece = jnp.zeros((SUB, GLA_QK), BF16)
    k_piece = [[rows(k_diag, j).astype(BF16) if seg == j
                else (rows(k_end, j) * jnp.exp2(refs[seg] - ends[j])).astype(BF16) if earlier(j, seg)
                else zeros_piece
                for seg in blocks] for j in blocks]
    ii = lax.broadcasted_iota(jnp.int32, (n, n), 0)
    jj = lax.broadcasted_iota(jnp.int32, (n, n), 1)
    heads = []
    for h in range(GLA_HEADS):
        ks = slice(GLA_DK * h, GLA_DK * (h + 1))
        q_cat = jnp.concatenate(
            [jnp.concatenate([rows(q_bf, blk)[:, ks] if seg == blk else zeros_piece[:, ks]
                              for seg in blocks], axis=1) for blk in blocks], axis=0)
        k_cat = jnp.concatenate(
            [jnp.concatenate([k_piece[j][seg][:, ks] for seg in blocks], axis=1)
             for j in blocks], axis=0)
        heads.append(dict(q_cat=q_cat, k_cat=k_cat, q_in=q_in[:, ks], k_out=k_out[:, ks],
                          v=qkv[:, 2 * GLA_QK + GLA_DV * h:2 * GLA_QK + GLA_DV * (h + 1)]))
    return dict(heads=heads, causal=(ii <= jj) if reverse else (ii >= jj),
                decay_col=jnp.transpose(jnp.broadcast_to(tot, (GLA_DV, GLA_QK))))


def _gla_chain(qkv_ref, g_ref, reverse, s_ref, y_ref):
    def start():
        p = _gla_prepare(qkv_ref[...], g_ref[...], reverse)
        return p, [_dot_nt(hd["q_cat"], hd["k_cat"]) for hd in p["heads"]]

    def finish(started):
        p, att = started
        outs = []
        for h, (hd, a) in enumerate(zip(p["heads"], att)):
            state = s_ref[h]
            lhs = jnp.concatenate([jnp.where(p["causal"], a, 0.0).astype(BF16), hd["q_in"]], axis=1)
            rhs = jnp.concatenate([hd["v"], state.astype(BF16)], axis=0)
            outs.append(_dot(lhs, rhs))
            ks = slice(GLA_DK * h, GLA_DK * (h + 1))
            s_ref[h] = state * jnp.exp2(p["decay_col"][ks, :]) + _dot_tn(hd["k_out"], hd["v"])
        y_ref[...] = jnp.concatenate(outs, axis=1).astype(y_ref.dtype)

    return start, finish


def _ssd_prepare(xbc, dt, nega, expand, lane0, reverse):
    n = xbc.shape[0]
    end = 0 if reverse else n - 1
    cum = _cumsum_rows(dt * nega, reverse)
    e_cum = jnp.exp2(cum)
    w = jnp.exp2(cum[end:end + 1, :] - cum) * dt
    e_cum_x = _dot(_split_hi_lo(e_cum), expand)
    x_w = (xbc[:, 0:SSD_INNER].astype(F32) * _dot(_split_hi_lo(w), expand)).astype(BF16)
    cum_t = jnp.transpose(cum)
    dt_t = jnp.transpose(dt)
    ii = lax.broadcasted_iota(jnp.int32, (n, n), 0)
    jj = lax.broadcasted_iota(jnp.int32, (n, n), 1)
    causal = (ii <= jj) if reverse else (ii >= jj)
    first_head = lax.broadcasted_iota(jnp.int32, (n, 2 * SSD_P), 1) < SSD_P
    groups = []
    for grp in range(SSD_GROUPS):
        b_g = xbc[:, SSD_INNER + SSD_N * grp:SSD_INNER + SSD_N * (grp + 1)]
        c_g = xbc[:, SSD_INNER + SSD_BC + SSD_N * grp:SSD_INNER + SSD_BC + SSD_N * (grp + 1)]
        scores = _dot_nt(c_g, b_g)
        pairs = []
        for pair in range(SSD_HPG // 2):
            m_pair = []
            for hh in range(2):
                lane = lane0 + SSD_HPG * grp + 2 * pair + hh
                cum_b = jnp.broadcast_to(cum[:, lane:lane + 1], (n, n))
                seg = jnp.where(causal, cum_b - cum_t[lane:lane + 1, :], NEG_BIG)
                m_pair.append((scores * jnp.exp2(seg) * dt_t[lane:lane + 1, :]).astype(BF16))
            col = 2 * SSD_P * (SSD_HPG // 2 * grp + pair)
            x_pair = xbc[:, col:col + 2 * SSD_P]
            none = jnp.zeros_like(x_pair)
            rhs = jnp.concatenate([jnp.where(first_head, x_pair, none),
                                   jnp.where(first_head, none, x_pair)], axis=0)
            pairs.append((jnp.concatenate(m_pair, axis=1), rhs))
        cols = slice(SSD_HPG * SSD_P * grp, SSD_HPG * SSD_P * (grp + 1))
        groups.append(dict(b=b_g, c=c_g, pairs=pairs, e_cum=e_cum_x[:, cols],
                           e_tot=e_cum_x[end:end + 1, cols], x_w=x_w[:, cols]))
    return groups


def _split_hi_lo(a):
    hi = a.astype(BF16)
    lo = (a - hi.astype(F32)).astype(BF16)
    return jnp.concatenate([hi, lo], axis=1)


def _ssd_chain(xbc_ref, dt_ref, nega, expand_ref, d, s_ref, y_ref):
    def start():
        return _ssd_prepare(xbc_ref[...], dt_ref[...], nega, expand_ref[d],
                            DT_LANE + SSD_HEADS * d, d == 1)

    def finish(groups):
        outs = []
        for grp, gd in enumerate(groups):
            state = s_ref[grp]
            intra = jnp.concatenate([_dot(m, rhs) for m, rhs in gd["pairs"]], axis=1)
            outs.append(_dot(gd["c"], state.astype(BF16)) * gd["e_cum"] + intra)
            s_ref[grp] = state * gd["e_tot"] + _dot_tn(gd["b"], gd["x_w"])
        y_ref[...] = jnp.concatenate(outs, axis=1).astype(y_ref.dtype)

    return start, finish


def _mixer_scan_kernel(qkv_f_ref, g_f_ref, qkv_b_ref, g_b_ref, xbc_f_ref, dt_f_ref, xbc_b_ref,
                       dt_b_ref, nega_ref, expand_ref, ygf_ref, ygb_ref, ysf_ref, ysb_ref,
                       gsf_ref, gsb_ref, ssf_ref, ssb_ref):
    @pl.when(pl.program_id(1) == 0)
    def _():
        for s_ref in (gsf_ref, gsb_ref, ssf_ref, ssb_ref):
            s_ref[...] = jnp.zeros_like(s_ref)

    nega = nega_ref[...]
    chains = []
    for e in range(qkv_f_ref.shape[0]):
        chains.append(_gla_chain(qkv_f_ref.at[e], g_f_ref.at[e], False, gsf_ref.at[e], ygf_ref.at[e]))
        chains.append(_ssd_chain(xbc_f_ref.at[e], dt_f_ref.at[e], nega, expand_ref, 0,
                                 ssf_ref.at[e], ysf_ref.at[e]))
        chains.append(_gla_chain(qkv_b_ref.at[e], g_b_ref.at[e], True, gsb_ref.at[e], ygb_ref.at[e]))
        chains.append(_ssd_chain(xbc_b_ref.at[e], dt_b_ref.at[e], nega, expand_ref, 1,
                                 ssb_ref.at[e], ysb_ref.at[e]))
    started = chains[0][0]()
    for i, (_, finish) in enumerate(chains):
        following = chains[i + 1][0]() if i + 1 < len(chains) else None
        finish(started)
        started = following


def _head_expand_matrix():
    r = jnp.arange(2 * TAIL_W)[:, None] % TAIL_W
    c = jnp.arange(SSD_INNER)[None, :] // SSD_P
    return jnp.stack([(r == DT_LANE + SSD_HEADS * d + c) for d in range(2)]).astype(BF16)


def _mixer_scan(qkv, g, xbc, dt, nega):
    bsz, tt, _ = qkv.shape
    n_all = tt // CHUNK
    fwd, bwd = _chunk_maps(CTX_LEN // CHUNK, n_all)
    spec = lambda w, m, c=0: pl.BlockSpec((SCAN_BATCH, CHUNK, w), lambda b, s: (b, m(s), c))
    expand = _head_expand_matrix()
    return pl.pallas_call(
        _mixer_scan_kernel,
        grid=(bsz // SCAN_BATCH, n_all),
        in_specs=[spec(QKV_W, fwd), spec(GLA_QK, fwd), spec(QKV_W, bwd), spec(GLA_QK, bwd, 1),
                  spec(SSD_CONV_DIM, fwd), spec(TAIL_W, fwd), spec(SSD_CONV_DIM, bwd), spec(TAIL_W, bwd),
                  pl.BlockSpec((1, TAIL_W), lambda b, s: (0, 0)),
                  pl.BlockSpec(expand.shape, lambda b, s: (0, 0, 0))],
        out_specs=[spec(GLA_INNER, fwd), spec(GLA_INNER, bwd), spec(SSD_INNER, fwd), spec(SSD_INNER, bwd)],
        out_shape=[jax.ShapeDtypeStruct((bsz, tt, GLA_INNER), BF16)] * 2
        + [jax.ShapeDtypeStruct((bsz, tt, SSD_INNER), BF16)] * 2,
        scratch_shapes=[pltpu.VMEM((SCAN_BATCH, GLA_HEADS, GLA_DK, GLA_DV), F32)] * 2
        + [pltpu.VMEM((SCAN_BATCH, SSD_GROUPS, SSD_N, SSD_HPG * SSD_P), F32)] * 2,
        compiler_params=pltpu.CompilerParams(
            dimension_semantics=("parallel", "arbitrary"), vmem_limit_bytes=VMEM_LIMIT),
        name="mixer_scan",
    )(qkv, g, qkv, g, xbc, dt, xbc, dt, nega, expand)


def _stream_cast_weights(pairs, stage, sem):
    def chunk_copy(w_hbm, c, slot):
        cols = w_hbm.shape[1]
        return pltpu.make_async_copy(w_hbm.at[pl.ds(c * W_CHUNK, W_CHUNK), :],
                                     stage.at[slot, :, 0:cols], sem.at[slot])

    chunk_copy(pairs[0][0], 0, 0).start()
    first = 0
    for k, (w_hbm, w_bf) in enumerate(pairs):
        n_chunks, rest = divmod(w_hbm.shape[0], W_CHUNK)
        assert rest == 0
        cols = w_hbm.shape[1]
        following = pairs[k + 1][0] if k + 1 < len(pairs) else None

        def body(c, carry, w_hbm=w_hbm, w_bf=w_bf, n_chunks=n_chunks, cols=cols,
                 following=following, first=first):
            slot = jnp.bitwise_and(first + c, 1)

            @pl.when(c + 1 < n_chunks)
            def _():
                chunk_copy(w_hbm, c + 1, 1 - slot).start()

            if following is not None:
                @pl.when(c + 1 == n_chunks)
                def _():
                    chunk_copy(following, 0, 1 - slot).start()

            chunk_copy(w_hbm, c, slot).wait()
            rows = pl.ds(pl.multiple_of(c * W_CHUNK, W_CHUNK), W_CHUNK)
            w_bf[rows, :] = stage[slot, :, 0:cols].astype(BF16)
            return carry

        lax.fori_loop(0, n_chunks, body, 0)
        first += n_chunks


def _merge_ffn_kernel(x_ref, ygf_ref, ygb_ref, ysf_ref, ysb_ref, xs_ref, rz_ref, m_ref,
                      gnorm_ref, dskip_ref, snorm_ref, npost_ref, nfpre_ref, nfpost_ref,
                      wout_hbm, wgate_hbm, wup_hbm, wdown_hbm, o_ref,
                      wout_ref, wgate_ref, wup_ref, wdown_ref, stage, sem):
    b = pl.program_id(0)

    @pl.when(jnp.logical_and(b == 0, pl.program_id(1) == 0))
    def _():
        _stream_cast_weights([(wout_hbm, wout_ref), (wgate_hbm, wgate_ref),
                              (wup_hbm, wup_ref), (wdown_hbm, wdown_ref)], stage, sem)

    mrow = m_ref[pl.ds(b, 1), :]
    mod = [mrow[:, D_MODEL * j:D_MODEL * (j + 1)] for j in range(6)]
    gw = SSD_INNER // SSD_GROUPS

    def mixer_out(rs):
        y_gla = ygf_ref[0, rs, :].astype(F32) + ygb_ref[0, rs, :].astype(F32)
        o_gla = jnp.concatenate(
            [_rms(y_gla[:, GLA_DV * h:GLA_DV * (h + 1)]) for h in range(GLA_HEADS)], axis=1)
        o_gla = o_gla * gnorm_ref[...] * rz_ref[0, rs, 0:GLA_INNER].astype(F32)
        y_ssd = ysf_ref[0, rs, :].astype(F32) + ysb_ref[0, rs, :].astype(F32)
        y_ssd = ((y_ssd + dskip_ref[...] * xs_ref[0, rs, :].astype(F32))
                 * rz_ref[0, rs, GLA_INNER:].astype(F32))
        o_ssd = jnp.concatenate(
            [_rms(y_ssd[:, gw * g:gw * (g + 1)]) for g in range(SSD_GROUPS)], axis=1)
        o_ssd = o_ssd * snorm_ref[...]
        o = jnp.concatenate([o_gla, o_ssd], axis=1).astype(BF16)
        y = _dot(o, wout_ref[...])
        x1 = x_ref[0, rs, :] + mod[2] * (_rms(y) * npost_ref[...])
        h = ((_rms(x1) * nfpre_ref[...]) * (1.0 + mod[4]) + mod[3]).astype(BF16)
        return x1, h

    def ffn(rs, x1, h):
        act = (_silu(_dot(h, wgate_ref[...])) * _dot(h, wup_ref[...])).astype(BF16)
        f = _dot(act, wdown_ref[...])
        o_ref[0, rs, :] = x1 + mod[5] * (_rms(f) * nfpost_ref[...])

    subs = [slice(FFN_SUB * j, FFN_SUB * (j + 1)) for j in range(TM_FFN // FFN_SUB)]
    ready = mixer_out(subs[0])
    for j, rs in enumerate(subs):
        following = mixer_out(subs[j + 1]) if j + 1 < len(subs) else None
        ffn(rs, *ready)
        ready = following


def _merge_ffn(x, ygf, ygb, ysf, ysb, xbc, rz, m_all, gnorm, dskip, snorm, npost, nfpre,
               nfpost, wout, wgate, wup, wdown):
    bsz, seq, _ = x.shape
    n_tiles = seq // TM_FFN
    const = lambda b, i: (0, 0)

    def lat(width):
        return pl.BlockSpec((1, TM_FFN, width), lambda b, i: (b, i, 0))

    weights = (wout, wgate, wup, wdown)
    in_hbm = pl.BlockSpec(memory_space=pl.ANY)
    vec = lambda w: pl.BlockSpec((1, w), const)
    return pl.pallas_call(
        _merge_ffn_kernel,
        grid=(bsz, n_tiles),
        in_specs=[pl.BlockSpec((1, TM_FFN, D_MODEL), lambda b, i: (b, i, 0)),
                  lat(GLA_INNER), lat(GLA_INNER), lat(SSD_INNER), lat(SSD_INNER),
                  lat(SSD_INNER), lat(GLA_INNER + SSD_INNER),
                  pl.BlockSpec(m_all.shape, const),
                  vec(GLA_INNER), vec(SSD_INNER), vec(SSD_INNER),
                  vec(D_MODEL), vec(D_MODEL), vec(D_MODEL),
                  in_hbm, in_hbm, in_hbm, in_hbm],
        out_specs=pl.BlockSpec((1, TM_FFN, D_MODEL), lambda b, i: (b, i, 0)),
        out_shape=jax.ShapeDtypeStruct((bsz, seq, D_MODEL), F32),
        scratch_shapes=[pltpu.VMEM(w.shape, BF16) for w in weights]
        + [pltpu.VMEM((2, W_CHUNK, max(w.shape[1] for w in weights)), F32),
           pltpu.SemaphoreType.DMA((2,))],
        compiler_params=pltpu.CompilerParams(
            dimension_semantics=("arbitrary", "arbitrary"), vmem_limit_bytes=VMEM_LIMIT),
        name="merge_ffn",
    )(x, ygf, ygb, ysf, ysb, xbc, rz, m_all, gnorm, dskip, snorm, npost, nfpre, nfpost,
      wout, wgate, wup, wdown)


def kernel(x, c, ctx, c_ctx, w_mod, b_mod, norm_mix_pre, norm_mix_post, norm_ffn_pre, norm_ffn_post,
           w_in, conv_w, conv_b, gla_wg_f, gla_bg_f, gla_wg_b, gla_bg_b, gla_norm,
           a_log_f, a_log_b, dt_bias_f, dt_bias_b, d_skip, ssd_norm, w_out, w_gate, w_up, w_down):
    assert w_mod.shape[0] == 1, "single-layer kernel"
    bsz = x.shape[0]
    row = lambda v: v.reshape(1, -1)

    c_all = jnp.concatenate([c, c_ctx[None], jnp.zeros((MOD_ROWS - bsz - 1, D_MODEL), F32)], axis=0)
    wg_pad = jnp.zeros((TAIL_W, 2 * GLA_QK), F32)
    wg_pad = wg_pad.at[0:GLA_RANK, 0:GLA_QK].set(gla_wg_f[0])
    wg_pad = wg_pad.at[GLA_RANK:2 * GLA_RANK, GLA_QK:].set(gla_wg_b[0]).astype(BF16)
    bg_cat = jnp.concatenate([gla_bg_f[0], gla_bg_b[0]]).reshape(1, -1)
    lane_pad = lambda f, bk: jnp.zeros((1, TAIL_W), F32).at[0, DT_LANE:DT_LANE + 2 * SSD_HEADS].set(
        jnp.concatenate([f, bk]))
    dtb_pad = lane_pad(dt_bias_f[0], dt_bias_b[0])
    nega = lane_pad(-jnp.exp(a_log_f[0]) * LOG2_E, -jnp.exp(a_log_b[0]) * LOG2_E)
    conv_w9 = conv_w[0].reshape(CONV_K * CONV_K, SSD_CONV_DIM)

    m_all = _modulation(c_all, w_mod[0], row(b_mod[0]))
    qkv, rz, xbc, g, dt = _input_projection_conv(
        x, ctx, m_all, row(norm_mix_pre[0]), w_in[0].T, wg_pad, bg_cat, dtb_pad, conv_w9, row(conv_b[0]))
    ygf, ygb, ysf, ysb = _mixer_scan(qkv, g, xbc, dt, nega)
    return _merge_ffn(
        x, ygf, ygb, ysf, ysb, xbc, rz, m_all,
        row(jnp.tile(gla_norm[0], GLA_HEADS)), row(jnp.repeat(d_skip[0], SSD_P)), row(ssd_norm[0]),
        row(norm_mix_post[0]), row(norm_ffn_pre[0]), row(norm_ffn_post[0]),
        w_out[0], w_gate[0], w_up[0], w_down[0])
```

```python
import jax
import jax.numpy as jnp
from jax import lax
from jax.experimental import pallas as pl
from jax.experimental.pallas import tpu as pltpu

F32 = jnp.float32
BF16 = jnp.bfloat16

D_MODEL = 1024
CTX_LEN = 256
GRID_W = 64
EPS = 1e-6

GLA_HEADS = 4
GLA_DK = 64
GLA_DV = 128
GLA_QK = GLA_HEADS * GLA_DK
GLA_INNER = GLA_HEADS * GLA_DV
GLA_RANK = 16
GLA_GATE_NORM = 16.0

SSD_HEADS = 8
SSD_P = 64
SSD_N = 64
SSD_GROUPS = 2
SSD_HPG = SSD_HEADS // SSD_GROUPS
SSD_INNER = SSD_HEADS * SSD_P
SSD_BC = SSD_GROUPS * SSD_N
SSD_CONV_DIM = SSD_INNER + 2 * SSD_BC
CONV_K = 3
D_FF = 2816

QKV_W = 2 * GLA_QK + GLA_INNER
COL_R = QKV_W
COL_Z = COL_R + GLA_INNER
COL_XBC = COL_Z + SSD_INNER
COL_TAIL = COL_XBC + SSD_CONV_DIM
TAIL_W = 128
PROJ_W = COL_TAIL + TAIL_W
DT_LANE = 2 * GLA_RANK
W_IN_LR = COL_Z
W_IN_Z = W_IN_LR + 2 * GLA_RANK
W_IN_DT = W_IN_Z + SSD_INNER + SSD_CONV_DIM
W_IN_COLS = W_IN_DT + 2 * SSD_HEADS

TM_PROJ = 512
PROJ_SUB = 256
MOD_ROWS = 8
MOD_TN = 1024
TM_FFN = 512
FFN_SUB = 128
W_CHUNK = 256
CHUNK = 128
SCAN_BATCH = 4
SUB = 32
NEG_BIG = -1e30
LOG2_E = 1.4426950408889634
VMEM_LIMIT = 56 * 1024 * 1024


def _silu(x):
    return x / (1.0 + jnp.exp2(x * -LOG2_E))


def _softplus(x):
    return jnp.maximum(x, 0.0) + jnp.log(1.0 + jnp.exp(-jnp.abs(x)))


def _rms(x):
    return x * lax.rsqrt(jnp.mean(x * x, axis=-1, keepdims=True) + EPS)


def _dot(a, b):
    return jnp.dot(a, b, preferred_element_type=F32)


def _dot_nt(a, b):
    return lax.dot_general(a, b, (((1,), (1,)), ((), ())), preferred_element_type=F32)


def _dot_tn(a, b):
    return lax.dot_general(a, b, (((0,), (0,)), ((), ())), preferred_element_type=F32)


def _mod_kernel(c_ref, w_ref, b_ref, o_ref):
    s = _silu(c_ref[...])
    o_ref[...] = _dot(s.astype(BF16), w_ref[...].astype(BF16)) + b_ref[...]


def _modulation(c_all, w_mod, b_mod):
    n = w_mod.shape[1]
    return pl.pallas_call(
        _mod_kernel,
        grid=(n // MOD_TN,),
        in_specs=[pl.BlockSpec((MOD_ROWS, D_MODEL), lambda j: (0, 0)),
                  pl.BlockSpec((D_MODEL, MOD_TN), lambda j: (0, j)),
                  pl.BlockSpec((1, MOD_TN), lambda j: (0, j))],
        out_specs=pl.BlockSpec((MOD_ROWS, MOD_TN), lambda j: (0, j)),
        out_shape=jax.ShapeDtypeStruct((MOD_ROWS, n), F32),
        name="adaln_mod",
    )(c_all, w_mod, b_mod)


def _permute_cast_w_in(wt_ref, w_s):
    blk = 128
    pad = jnp.zeros((TAIL_W - (W_IN_Z - W_IN_LR) - (W_IN_COLS - W_IN_DT), blk), F32)

    def body(kb, carry):
        k0 = pl.multiple_of(kb * blk, blk)
        cols = pl.ds(k0, blk)
        rows = pl.ds(k0, blk)
        w_s[rows, 0:W_IN_LR] = jnp.transpose(wt_ref[0:W_IN_LR, cols]).astype(BF16)
        w_s[rows, COL_Z:COL_TAIL] = jnp.transpose(wt_ref[W_IN_Z:W_IN_DT, cols]).astype(BF16)
        tail = jnp.concatenate([wt_ref[W_IN_LR:W_IN_Z, cols], wt_ref[W_IN_DT:W_IN_COLS, cols], pad],
                               axis=0)
        w_s[rows, COL_TAIL:PROJ_W] = jnp.transpose(tail).astype(BF16)
        return carry

    lax.fori_loop(0, D_MODEL // blk, body, 0)


def _grid_conv_rows(window, t0, is_ctx, w_ref, b_ref):
    n = window.shape[0] - 2 * GRID_W
    ext = window.astype(F32)
    row_on = jnp.where(is_ctx, 0.0, 1.0)
    taps = []
    for dx in range(CONV_K):
        tap = None
        for dy in range(CONV_K):
            wk = w_ref[CONV_K * dy + dx:CONV_K * dy + dx + 1, :]
            if dy != 1:
                wk = wk * row_on
            term = wk * ext[GRID_W * dy:GRID_W * dy + n, :]
            tap = term if tap is None else tap + term
        taps.append(tap)
    t = t0 + lax.broadcasted_iota(jnp.int32, (n, 1), 0)
    col = jnp.bitwise_and(t, GRID_W - 1)
    has_left = jnp.where(is_ctx, t, col) != 0
    has_right = jnp.where(is_ctx, t - (CTX_LEN - 1), col - (GRID_W - 1)) != 0
    acc = (taps[1] + b_ref[...]
           + jnp.where(has_left, pltpu.roll(taps[0], 1, axis=0), 0.0)
           + jnp.where(has_right, pltpu.roll(taps[2], n - 1, axis=0), 0.0))
    return _silu(acc)


def _inproj_conv_kernel(x_ref, ctx_ref, m_ref, gpre_ref, w_ref, wg_ref, bg_ref, dtb_ref,
                        cw_ref, cb_ref, qkv_ref, rz_ref, xbc_ref, g_ref, dt_ref,
                        w_s, prev_s, above_s):
    b = pl.program_id(0)
    i = pl.program_id(1)
    n_tiles = pl.num_programs(1) - 1
    ctx_tile = n_tiles - 1

    @pl.when(jnp.logical_and(b == 0, i == 0))
    def _():
        _permute_cast_w_in(w_ref, w_s)

    @pl.when(i == 0)
    def _():
        prev_s[...] = jnp.zeros_like(prev_s)
        above_s[...] = jnp.zeros_like(above_s)

    j = i - 1
    conv_ctx = j == ctx_tile
    split = TM_PROJ - GRID_W
    none = jnp.zeros((GRID_W, SSD_CONV_DIM), BF16)

    def conv_previous_head():
        main = prev_s[...]
        above = jnp.where(jnp.logical_and(j >= 1, j < ctx_tile), above_s[...], none)
        window = jnp.concatenate([above, main], axis=0)
        xbc_ref[0, 0:split, :] = _grid_conv_rows(window, 0, conv_ctx, cw_ref, cb_ref).astype(BF16)
        above_s[...] = main[split:, :]

    def conv_previous_last_row(below):
        main = prev_s[...]
        below = jnp.where(j < ctx_tile - 1, below, none)
        window = jnp.concatenate([main[split - GRID_W:, :], below], axis=0)
        xbc_ref[0, split:, :] = _grid_conv_rows(window, split, conv_ctx, cw_ref, cb_ref).astype(BF16)

    @pl.when(i < n_tiles)
    def _():
        is_ctx = i == ctx_tile
        mrow = m_ref[pl.ds(jnp.where(is_ctx, pl.num_programs(0), b), 1), :]
        shift = mrow[:, 0:D_MODEL]
        gain = gpre_ref[...] * (1.0 + mrow[:, D_MODEL:2 * D_MODEL])

        def normed(rs):
            ctx_rows = slice(rs.start % CTX_LEN, rs.start % CTX_LEN + PROJ_SUB)
            x = jnp.where(is_ctx, ctx_ref[0, ctx_rows, :], x_ref[0, rs, :])
            return (_rms(x) * gain + shift).astype(BF16)

        def project(rs, h):
            acc = _dot(h, w_s[...])
            qkv_ref[0, rs, 0:GLA_QK] = (acc[:, 0:GLA_QK] * (GLA_DK ** -0.5)).astype(BF16)
            qkv_ref[0, rs, GLA_QK:QKV_W] = acc[:, GLA_QK:QKV_W].astype(BF16)
            rz_ref[0, rs, :] = _silu(acc[:, COL_R:COL_XBC]).astype(BF16)
            tail = acc[:, COL_TAIL:PROJ_W]
            logits = _dot(tail.astype(BF16), wg_ref[...]) + bg_ref[...]
            u = logits * LOG2_E
            g_ref[0, rs, :] = ((jnp.minimum(u, 0.0) - jnp.log2(1.0 + jnp.exp2(-jnp.abs(u))))
                               * (1.0 / GLA_GATE_NORM))
            dt_ref[0, rs, :] = _softplus(tail + dtb_ref[...])
            return acc[:, COL_XBC:COL_TAIL].astype(BF16)

        subs = [slice(PROJ_SUB * s, PROJ_SUB * (s + 1)) for s in range(TM_PROJ // PROJ_SUB)]
        conv_previous_head()
        ready = normed(subs[0])
        fresh = []
        for s, rs in enumerate(subs):
            following = normed(subs[s + 1]) if s + 1 < len(subs) else None
            fresh.append(project(rs, ready))
            if s == 0:
                conv_previous_last_row(fresh[0][0:GRID_W, :])
            ready = following
        for rs, xbc_new in zip(subs, fresh):
            prev_s[rs, :] = xbc_new

    @pl.when(i == n_tiles)
    def _():
        conv_previous_head()
        conv_previous_last_row(none)


def _input_projection_conv(x, ctx, m_all, gpre, w_in, wg_pad, bg_cat, dtb_pad, conv_w9, conv_b):
    bsz, seq, _ = x.shape
    n_lat = seq // TM_PROJ
    n_tiles = n_lat + 1
    tt = CTX_LEN + seq
    const = lambda b, i: (0, 0)
    tok = lambda b, i: (b, jnp.minimum(i, n_tiles - 1), 0)
    conv_tok = lambda b, i: (b, jnp.maximum(i - 1, 0), 0)
    return pl.pallas_call(
        _inproj_conv_kernel,
        grid=(bsz, n_tiles + 1),
        in_specs=[pl.BlockSpec((1, TM_PROJ, D_MODEL), lambda b, i: (b, jnp.minimum(i, n_lat - 1), 0)),
                  pl.BlockSpec((1, CTX_LEN, D_MODEL), lambda b, i: (b, 0, 0)),
                  pl.BlockSpec(m_all.shape, const),
                  pl.BlockSpec((1, D_MODEL), const),
                  pl.BlockSpec((W_IN_COLS, D_MODEL), const, pipeline_mode=pl.Buffered(1)),
                  pl.BlockSpec((TAIL_W, 2 * GLA_QK), const),
                  pl.BlockSpec((1, 2 * GLA_QK), const),
                  pl.BlockSpec((1, TAIL_W), const),
                  pl.BlockSpec((CONV_K * CONV_K, SSD_CONV_DIM), const),
                  pl.BlockSpec((1, SSD_CONV_DIM), const)],
        out_specs=[pl.BlockSpec((1, TM_PROJ, QKV_W), tok),
                   pl.BlockSpec((1, TM_PROJ, GLA_INNER + SSD_INNER), tok),
                   pl.BlockSpec((1, TM_PROJ, SSD_CONV_DIM), conv_tok),
                   pl.BlockSpec((1, TM_PROJ, 2 * GLA_QK), tok),
                   pl.BlockSpec((1, TM_PROJ, TAIL_W), tok)],
        out_shape=[jax.ShapeDtypeStruct((bsz, tt, QKV_W), BF16),
                   jax.ShapeDtypeStruct((bsz, tt, GLA_INNER + SSD_INNER), BF16),
                   jax.ShapeDtypeStruct((bsz, tt, SSD_CONV_DIM), BF16),
                   jax.ShapeDtypeStruct((bsz, tt, 2 * GLA_QK), F32),
                   jax.ShapeDtypeStruct((bsz, tt, TAIL_W), F32)],
        scratch_shapes=[pltpu.VMEM((D_MODEL, PROJ_W), BF16),
                        pltpu.VMEM((TM_PROJ, SSD_CONV_DIM), BF16),
                        pltpu.VMEM((GRID_W, SSD_CONV_DIM), BF16)],
        compiler_params=pltpu.CompilerParams(
            dimension_semantics=("arbitrary", "arbitrary"), vmem_limit_bytes=VMEM_LIMIT),
        name="in_proj_conv",
    )(x, ctx, m_all, gpre, w_in, wg_pad, bg_cat, dtb_pad, conv_w9, conv_b)


def _cumsum_rows(x, reverse):
    n = x.shape[0]
    row = lax.broadcasted_iota(jnp.int32, x.shape, 0)
    s = 1
    while s < n:
        if reverse:
            x = x + jnp.where(row < n - s, pltpu.roll(x, n - s, axis=0), 0.0)
        else:
            x = x + jnp.where(row >= s, pltpu.roll(x, s, axis=0), 0.0)
        s *= 2
    return x


def _chunk_maps(n_ctx, n_all):
    n_lat = n_all - n_ctx
    fwd = lambda s: jnp.where(s < n_ctx, n_lat + s, s - n_ctx)
    bwd = lambda s: n_all - 1 - s
    return fwd, bwd


def _gla_prepare(qkv, g, reverse):
    n = qkv.shape[0]
    nb = n // SUB
    blocks = range(nb)
    rows = lambda a, blk: a[SUB * blk:SUB * (blk + 1), :]
    q = qkv[:, 0:GLA_QK].astype(F32)
    k = qkv[:, GLA_QK:2 * GLA_QK].astype(F32)
    cum = _cumsum_rows(g, reverse)
    zero = jnp.zeros((1, GLA_QK), F32)
    if reverse:
        ends = [cum[SUB * blk:SUB * blk + 1, :] for blk in blocks]
        refs = ends[1:] + [zero]
        earlier = lambda j, blk: j > blk
        tot = ends[0]
    else:
        ends = [cum[SUB * (blk + 1) - 1:SUB * (blk + 1), :] for blk in blocks]
        refs = [zero] + ends[:-1]
        earlier = lambda j, blk: j < blk
        tot = ends[nb - 1]
    stack = lambda vs: jnp.concatenate([jnp.broadcast_to(v, (SUB, GLA_QK)) for v in vs], axis=0)
    e_in = cum - stack(refs)
    q_blk = q * jnp.exp2(e_in)
    k_diag = k * jnp.exp2(-e_in)
    k_end = k * jnp.exp2(stack(ends) - cum)
    q_in = jnp.concatenate([rows(q_blk, blk) * jnp.exp2(refs[blk]) for blk in blocks], axis=0)
    k_out = jnp.concatenate([rows(k_end, blk) * jnp.exp2(tot - ends[blk]) for blk in blocks], axis=0)
    q_in = q_in.astype(BF16)
    k_out = k_out.astype(BF16)
    q_bf = q_blk.astype(BF16)
    zeros_piece = jnp.zeros((SUB, GLA_QK), BF16)
    k_piece = [[rows(k_diag, j).astype(BF16) if seg == j
                else (rows(k_end, j) * jnp.exp2(refs[seg] - ends[j])).astype(BF16) if earlier(j, seg)
                else zeros_piece
                for seg in blocks] for j in blocks]
    ii = lax.broadcasted_iota(jnp.int32, (n, n), 0)
    jj = lax.broadcasted_iota(jnp.int32, (n, n), 1)
    heads = []
    for h in range(GLA_HEADS):
        ks = slice(GLA_DK * h, GLA_DK * (h + 1))
        q_cat = jnp.concatenate(
            [jnp.concatenate([rows(q_bf, blk)[:, ks] if seg == blk else zeros_piece[:, ks]
                              for seg in blocks], axis=1) for blk in blocks], axis=0)
        k_cat = jnp.concatenate(
            [jnp.concatenate([k_piece[j][seg][:, ks] for seg in blocks], axis=1)
             for j in blocks], axis=0)
        heads.append(dict(q_cat=q_cat, k_cat=k_cat, q_in=q_in[:, ks], k_out=k_out[:, ks],
                          v=qkv[:, 2 * GLA_QK + GLA_DV * h:2 * GLA_QK + GLA_DV * (h + 1)]))
    return dict(heads=heads, causal=(ii <= jj) if reverse else (ii >= jj),
                decay_col=jnp.transpose(jnp.broadcast_to(tot, (GLA_DV, GLA_QK))))


def _gla_chain(qkv_ref, g_ref, reverse, s_ref, y_ref):
    def start():
        p = _gla_prepare(qkv_ref[...], g_ref[...], reverse)
        return p, [_dot_nt(hd["q_cat"], hd["k_cat"]) for hd in p["heads"]]

    def finish(started):
        p, att = started
        outs = []
        for h, (hd, a) in enumerate(zip(p["heads"], att)):
            state = s_ref[h]
            lhs = jnp.concatenate([jnp.where(p["causal"], a, 0.0).astype(BF16), hd["q_in"]], axis=1)
            rhs = jnp.concatenate([hd["v"], state.astype(BF16)], axis=0)
            outs.append(_dot(lhs, rhs))
            ks = slice(GLA_DK * h, GLA_DK * (h + 1))
            s_ref[h] = state * jnp.exp2(p["decay_col"][ks, :]) + _dot_tn(hd["k_out"], hd["v"])
        y_ref[...] = jnp.concatenate(outs, axis=1).astype(y_ref.dtype)

    return start, finish


def _ssd_prepare(xbc, dt, nega, expand, lane0, reverse):
    n = xbc.shape[0]
    end = 0 if reverse else n - 1
    cum = _cumsum_rows(dt * nega, reverse)
    e_cum = jnp.exp2(cum)
    w = jnp.exp2(cum[end:end + 1, :] - cum) * dt
    e_cum_x = _dot(_split_hi_lo(e_cum), expand)
    x_w = (xbc[:, 0:SSD_INNER].astype(F32) * _dot(_split_hi_lo(w), expand)).astype(BF16)
    cum_t = jnp.transpose(cum)
    dt_t = jnp.transpose(dt)
    ii = lax.broadcasted_iota(jnp.int32, (n, n), 0)
    jj = lax.broadcasted_iota(jnp.int32, (n, n), 1)
    causal = (ii <= jj) if reverse else (ii >= jj)
    first_head = lax.broadcasted_iota(jnp.int32, (n, 2 * SSD_P), 1) < SSD_P
    groups = []
    for grp in range(SSD_GROUPS):
        b_g = xbc[:, SSD_INNER + SSD_N * grp:SSD_INNER + SSD_N * (grp + 1)]
        c_g = xbc[:, SSD_INNER + SSD_BC + SSD_N * grp:SSD_INNER + SSD_BC + SSD_N * (grp + 1)]
        scores = _dot_nt(c_g, b_g)
        pairs = []
        for pair in range(SSD_HPG // 2):
            m_pair = []
            for hh in range(2):
                lane = lane0 + SSD_HPG * grp + 2 * pair + hh
                cum_b = jnp.broadcast_to(cum[:, lane:lane + 1], (n, n))
                seg = jnp.where(causal, cum_b - cum_t[lane:lane + 1, :], NEG_BIG)
                m_pair.append((scores * jnp.exp2(seg) * dt_t[lane:lane + 1, :]).astype(BF16))
            col = 2 * SSD_P * (SSD_HPG // 2 * grp + pair)
            x_pair = xbc[:, col:col + 2 * SSD_P]
            none = jnp.zeros_like(x_pair)
            rhs = jnp.concatenate([jnp.where(first_head, x_pair, none),
                                   jnp.where(first_head, none, x_pair)], axis=0)
            pairs.append((jnp.concatenate(m_pair, axis=1), rhs))
        cols = slice(SSD_HPG * SSD_P * grp, SSD_HPG * SSD_P * (grp + 1))
        groups.append(dict(b=b_g, c=c_g, pairs=pairs, e_cum=e_cum_x[:, cols],
                           e_tot=e_cum_x[end:end + 1, cols], x_w=x_w[:, cols]))
    return groups


def _split_hi_lo(a):
    hi = a.astype(BF16)
    lo = (a - hi.astype(F32)).astype(BF16)
    return jnp.concatenate([hi, lo], axis=1)


def _ssd_chain(xbc_ref, dt_ref, nega, expand_ref, d, s_ref, y_ref):
    def start():
        return _ssd_prepare(xbc_ref[...], dt_ref[...], nega, expand_ref[d],
                            DT_LANE + SSD_HEADS * d, d == 1)

    def finish(groups):
        outs = []
        for grp, gd in enumerate(groups):
            state = s_ref[grp]
            intra = jnp.concatenate([_dot(m, rhs) for m, rhs in gd["pairs"]], axis=1)
            outs.append(_dot(gd["c"], state.astype(BF16)) * gd["e_cum"] + intra)
            s_ref[grp] = state * gd["e_tot"] + _dot_tn(gd["b"], gd["x_w"])
        y_ref[...] = jnp.concatenate(outs, axis=1).astype(y_ref.dtype)

    return start, finish


def _mixer_scan_kernel(qkv_f_ref, g_f_ref, qkv_b_ref, g_b_ref, xbc_f_ref, dt_f_ref, xbc_b_ref,
                       dt_b_ref, nega_ref, expand_ref, ygf_ref, ygb_ref, ysf_ref, ysb_ref,
                       gsf_ref, gsb_ref, ssf_ref, ssb_ref):
    @pl.when(pl.program_id(1) == 0)
    def _():
        for s_ref in (gsf_ref, gsb_ref, ssf_ref, ssb_ref):
            s_ref[...] = jnp.zeros_like(s_ref)

    nega = nega_ref[...]
    chains = []
    for e in range(qkv_f_ref.shape[0]):
        chains.append(_gla_chain(qkv_f_ref.at[e], g_f_ref.at[e], False, gsf_ref.at[e], ygf_ref.at[e]))
        chains.append(_ssd_chain(xbc_f_ref.at[e], dt_f_ref.at[e], nega, expand_ref, 0,
                                 ssf_ref.at[e], ysf_ref.at[e]))
        chains.append(_gla_chain(qkv_b_ref.at[e], g_b_ref.at[e], True, gsb_ref.at[e], ygb_ref.at[e]))
        chains.append(_ssd_chain(xbc_b_ref.at[e], dt_b_ref.at[e], nega, expand_ref, 1,
                                 ssb_ref.at[e], ysb_ref.at[e]))
    started = chains[0][0]()
    for i, (_, finish) in enumerate(chains):
        following = chains[i + 1][0]() if i + 1 < len(chains) else None
        finish(started)
        started = following


def _head_expand_matrix():
    r = jnp.arange(2 * TAIL_W)[:, None] % TAIL_W
    c = jnp.arange(SSD_INNER)[None, :] // SSD_P
    return jnp.stack([(r == DT_LANE + SSD_HEADS * d + c) for d in range(2)]).astype(BF16)


def _mixer_scan(qkv, g, xbc, dt, nega):
    bsz, tt, _ = qkv.shape
    n_all = tt // CHUNK
    fwd, bwd = _chunk_maps(CTX_LEN // CHUNK, n_all)
    spec = lambda w, m, c=0: pl.BlockSpec((SCAN_BATCH, CHUNK, w), lambda b, s: (b, m(s), c))
    expand = _head_expand_matrix()
    return pl.pallas_call(
        _mixer_scan_kernel,
        grid=(bsz // SCAN_BATCH, n_all),
        in_specs=[spec(QKV_W, fwd), spec(GLA_QK, fwd), spec(QKV_W, bwd), spec(GLA_QK, bwd, 1),
                  spec(SSD_CONV_DIM, fwd), spec(TAIL_W, fwd), spec(SSD_CONV_DIM, bwd), spec(TAIL_W, bwd),
                  pl.BlockSpec((1, TAIL_W), lambda b, s: (0, 0)),
                  pl.BlockSpec(expand.shape, lambda b, s: (0, 0, 0))],
        out_specs=[spec(GLA_INNER, fwd), spec(GLA_INNER, bwd), spec(SSD_INNER, fwd), spec(SSD_INNER, bwd)],
        out_shape=[jax.ShapeDtypeStruct((bsz, tt, GLA_INNER), BF16)] * 2
        + [jax.ShapeDtypeStruct((bsz, tt, SSD_INNER), BF16)] * 2,
        scratch_shapes=[pltpu.VMEM((SCAN_BATCH, GLA_HEADS, GLA_DK, GLA_DV), F32)] * 2
        + [pltpu.VMEM((SCAN_BATCH, SSD_GROUPS, SSD_N, SSD_HPG * SSD_P), F32)] * 2,
        compiler_params=pltpu.CompilerParams(
            dimension_semantics=("parallel", "arbitrary"), vmem_limit_bytes=VMEM_LIMIT),
        name="mixer_scan",
    )(qkv, g, qkv, g, xbc, dt, xbc, dt, nega, expand)


def _stream_cast_weights(pairs, stage, sem):
    def chunk_copy(w_hbm, c, slot):
        cols = w_hbm.shape[1]
        return pltpu.make_async_copy(w_hbm.at[pl.ds(c * W_CHUNK, W_CHUNK), :],
                                     stage.at[slot, :, 0:cols], sem.at[slot])

    chunk_copy(pairs[0][0], 0, 0).start()
    first = 0
    for k, (w_hbm, w_bf) in enumerate(pairs):
        n_chunks, rest = divmod(w_hbm.shape[0], W_CHUNK)
        assert rest == 0
        cols = w_hbm.shape[1]
        following = pairs[k + 1][0] if k + 1 < len(pairs) else None

        def body(c, carry, w_hbm=w_hbm, w_bf=w_bf, n_chunks=n_chunks, cols=cols,
                 following=following, first=first):
            slot = jnp.bitwise_and(first + c, 1)

            @pl.when(c + 1 < n_chunks)
            def _():
                chunk_copy(w_hbm, c + 1, 1 - slot).start()

            if following is not None:
                @pl.when(c + 1 == n_chunks)
                def _():
                    chunk_copy(following, 0, 1 - slot).start()

            chunk_copy(w_hbm, c, slot).wait()
            rows = pl.ds(pl.multiple_of(c * W_CHUNK, W_CHUNK), W_CHUNK)
            w_bf[rows, :] = stage[slot, :, 0:cols].astype(BF16)
            return carry

        lax.fori_loop(0, n_chunks, body, 0)
        first += n_chunks


def _merge_ffn_kernel(x_ref, ygf_ref, ygb_ref, ysf_ref, ysb_ref, xs_ref, rz_ref, m_ref,
                      gnorm_ref, dskip_ref, snorm_ref, npost_ref, nfpre_ref, nfpost_ref,
                      wout_hbm, wgate_hbm, wup_hbm, wdown_hbm, o_ref,
                      wout_ref, wgate_ref, wup_ref, wdown_ref, stage, sem):
    b = pl.program_id(0)

    @pl.when(jnp.logical_and(b == 0, pl.program_id(1) == 0))
    def _():
        _stream_cast_weights([(wout_hbm, wout_ref), (wgate_hbm, wgate_ref),
                              (wup_hbm, wup_ref), (wdown_hbm, wdown_ref)], stage, sem)

    mrow = m_ref[pl.ds(b, 1), :]
    mod = [mrow[:, D_MODEL * j:D_MODEL * (j + 1)] for j in range(6)]
    gw = SSD_INNER // SSD_GROUPS

    def mixer_out(rs):
        y_gla = ygf_ref[0, rs, :].astype(F32) + ygb_ref[0, rs, :].astype(F32)
        o_gla = jnp.concatenate(
            [_rms(y_gla[:, GLA_DV * h:GLA_DV * (h + 1)]) for h in range(GLA_HEADS)], axis=1)
        o_gla = o_gla * gnorm_ref[...] * rz_ref[0, rs, 0:GLA_INNER].astype(F32)
        y_ssd = ysf_ref[0, rs, :].astype(F32) + ysb_ref[0, rs, :].astype(F32)
        y_ssd = ((y_ssd + dskip_ref[...] * xs_ref[0, rs, :].astype(F32))
                 * rz_ref[0, rs, GLA_INNER:].astype(F32))
        o_ssd = jnp.concatenate(
            [_rms(y_ssd[:, gw * g:gw * (g + 1)]) for g in range(SSD_GROUPS)], axis=1)
        o_ssd = o_ssd * snorm_ref[...]
        o = jnp.concatenate([o_gla, o_ssd], axis=1).astype(BF16)
        y = _dot(o, wout_ref[...])
        x1 = x_ref[0, rs, :] + mod[2] * (_rms(y) * npost_ref[...])
        h = ((_rms(x1) * nfpre_ref[...]) * (1.0 + mod[4]) + mod[3]).astype(BF16)
        return x1, h

    def ffn(rs, x1, h):
        act = (_silu(_dot(h, wgate_ref[...])) * _dot(h, wup_ref[...])).astype(BF16)
        f = _dot(act, wdown_ref[...])
        o_ref[0, rs, :] = x1 + mod[5] * (_rms(f) * nfpost_ref[...])

    subs = [slice(FFN_SUB * j, FFN_SUB * (j + 1)) for j in range(TM_FFN // FFN_SUB)]
    ready = mixer_out(subs[0])
    for j, rs in enumerate(subs):
        following = mixer_out(subs[j + 1]) if j + 1 < len(subs) else None
        ffn(rs, *ready)
        ready = following


def _merge_ffn(x, ygf, ygb, ysf, ysb, xbc, rz, m_all, gnorm, dskip, snorm, npost, nfpre,
               nfpost, wout, wgate, wup, wdown):
    bsz, seq, _ = x.shape
    n_tiles = seq // TM_FFN
    const = lambda b, i: (0, 0)

    def lat(width):
        return pl.BlockSpec((1, TM_FFN, width), lambda b, i: (b, i, 0))

    weights = (wout, wgate, wup, wdown)
    in_hbm = pl.BlockSpec(memory_space=pl.ANY)
    vec = lambda w: pl.BlockSpec((1, w), const)
    return pl.pallas_call(
        _merge_ffn_kernel,
        grid=(bsz, n_tiles),
        in_specs=[pl.BlockSpec((1, TM_FFN, D_MODEL), lambda b, i: (b, i, 0)),
                  lat(GLA_INNER), lat(GLA_INNER), lat(SSD_INNER), lat(SSD_INNER),
                  lat(SSD_INNER), lat(GLA_INNER + SSD_INNER),
                  pl.BlockSpec(m_all.shape, const),
                  vec(GLA_INNER), vec(SSD_INNER), vec(SSD_INNER),
                  vec(D_MODEL), vec(D_MODEL), vec(D_MODEL),
                  in_hbm, in_hbm, in_hbm, in_hbm],
        out_specs=pl.BlockSpec((1, TM_FFN, D_MODEL), lambda b, i: (b, i, 0)),
        out_shape=jax.ShapeDtypeStruct((bsz, seq, D_MODEL), F32),
        scratch_shapes=[pltpu.VMEM(w.shape, BF16) for w in weights]
        + [pltpu.VMEM((2, W_CHUNK, max(w.shape[1] for w in weights)), F32),
           pltpu.SemaphoreType.DMA((2,))],
        compiler_params=pltpu.CompilerParams(
            dimension_semantics=("arbitrary", "arbitrary"), vmem_limit_bytes=VMEM_LIMIT),
        name="merge_ffn",
    )(x, ygf, ygb, ysf, ysb, xbc, rz, m_all, gnorm, dskip, snorm, npost, nfpre, nfpost,
      wout, wgate, wup, wdown)


def kernel(x, c, ctx, c_ctx, w_mod, b_mod, norm_mix_pre, norm_mix_post, norm_ffn_pre, norm_ffn_post,
           w_in, conv_w, conv_b, gla_wg_f, gla_bg_f, gla_wg_b, gla_bg_b, gla_norm,
           a_log_f, a_log_b, dt_bias_f, dt_bias_b, d_skip, ssd_norm, w_out, w_gate, w_up, w_down):
    assert w_mod.shape[0] == 1, "single-layer kernel"
    bsz = x.shape[0]
    row = lambda v: v.reshape(1, -1)

    c_all = jnp.concatenate([c, c_ctx[None], jnp.zeros((MOD_ROWS - bsz - 1, D_MODEL), F32)], axis=0)
    wg_pad = jnp.zeros((TAIL_W, 2 * GLA_QK), F32)
    wg_pad = wg_pad.at[0:GLA_RANK, 0:GLA_QK].set(gla_wg_f[0])
    wg_pad = wg_pad.at[GLA_RANK:2 * GLA_RANK, GLA_QK:].set(gla_wg_b[0]).astype(BF16)
    bg_cat = jnp.concatenate([gla_bg_f[0], gla_bg_b[0]]).reshape(1, -1)
    lane_pad = lambda f, bk: jnp.zeros((1, TAIL_W), F32).at[0, DT_LANE:DT_LANE + 2 * SSD_HEADS].set(
        jnp.concatenate([f, bk]))
    dtb_pad = lane_pad(dt_bias_f[0], dt_bias_b[0])
    nega = lane_pad(-jnp.exp(a_log_f[0]) * LOG2_E, -jnp.exp(a_log_b[0]) * LOG2_E)
    conv_w9 = conv_w[0].reshape(CONV_K * CONV_K, SSD_CONV_DIM)

    m_all = _modulation(c_all, w_mod[0], row(b_mod[0]))
    qkv, rz, xbc, g, dt = _input_projection_conv(
        x, ctx, m_all, row(norm_mix_pre[0]), w_in[0].T, wg_pad, bg_cat, dtb_pad, conv_w9, row(conv_b[0]))
    ygf, ygb, ysf, ysb = _mixer_scan(qkv, g, xbc, dt, nega)
    return _merge_ffn(
        x, ygf, ygb, ysf, ysb, xbc, rz, m_all,
        row(jnp.tile(gla_norm[0], GLA_HEADS)), row(jnp.repeat(d_skip[0], SSD_P)), row(ssd_norm[0]),
        row(norm_mix_post[0]), row(norm_ffn_pre[0]), row(norm_ffn_post[0]),
        w_out[0], w_gate[0], w_up[0], w_down[0])
```

```python
import jax
import jax.numpy as jnp
from jax import lax
from jax.experimental import pallas as pl
from jax.experimental.pallas import tpu as pltpu

F32 = jnp.float32
BF16 = jnp.bfloat16

D_MODEL = 1024
CTX_LEN = 256
GRID_W = 64
EPS = 1e-6

GLA_HEADS = 4
GLA_DK = 64
GLA_DV = 128
GLA_QK = GLA_HEADS * GLA_DK
GLA_INNER = GLA_HEADS * GLA_DV
GLA_RANK = 16
GLA_GATE_NORM = 16.0

SSD_HEADS = 8
SSD_P = 64
SSD_N = 64
SSD_GROUPS = 2
SSD_HPG = SSD_HEADS // SSD_GROUPS
SSD_INNER = SSD_HEADS * SSD_P
SSD_BC = SSD_GROUPS * SSD_N
SSD_CONV_DIM = SSD_INNER + 2 * SSD_BC
CONV_K = 3
D_FF = 2816

QKV_W = 2 * GLA_QK + GLA_INNER
COL_R = QKV_W
COL_Z = COL_R + GLA_INNER
COL_XBC = COL_Z + SSD_INNER
COL_TAIL = COL_XBC + SSD_CONV_DIM
TAIL_W = 128
PROJ_W = COL_TAIL + TAIL_W
DT_LANE = 2 * GLA_RANK
W_IN_LR = COL_Z
W_IN_Z = W_IN_LR + 2 * GLA_RANK
W_IN_DT = W_IN_Z + SSD_INNER + SSD_CONV_DIM
W_IN_COLS = W_IN_DT + 2 * SSD_HEADS

TM_PROJ = 512
PROJ_SUB = 256
CONV_COLS = 128
MOD_ROWS = 8
MOD_TN = 1024
TM_FFN = 512
FFN_SUB = 128
W_CHUNK = 256
CHUNK = 128
SCAN_BATCH = 4
SUB = 32
NEG_BIG = -1e30
LOG2_E = 1.4426950408889634
VMEM_LIMIT = 56 * 1024 * 1024


def _silu(x):
    return x / (1.0 + jnp.exp2(x * -LOG2_E))


def _softplus(x):
    return jnp.maximum(x, 0.0) + jnp.log(1.0 + jnp.exp(-jnp.abs(x)))


def _rms(x):
    return x * lax.rsqrt(jnp.mean(x * x, axis=-1, keepdims=True) + EPS)


def _dot(a, b):
    return jnp.dot(a, b, preferred_element_type=F32)


def _dot_nt(a, b):
    return lax.dot_general(a, b, (((1,), (1,)), ((), ())), preferred_element_type=F32)


def _dot_tn(a, b):
    return lax.dot_general(a, b, (((0,), (0,)), ((), ())), preferred_element_type=F32)


def _mod_kernel(c_ref, w_ref, b_ref, o_ref):
    s = _silu(c_ref[...])
    o_ref[...] = _dot(s.astype(BF16), w_ref[...].astype(BF16)) + b_ref[...]


def _modulation(c_all, w_mod, b_mod):
    n = w_mod.shape[1]
    return pl.pallas_call(
        _mod_kernel,
        grid=(n // MOD_TN,),
        in_specs=[pl.BlockSpec((MOD_ROWS, D_MODEL), lambda j: (0, 0)),
                  pl.BlockSpec((D_MODEL, MOD_TN), lambda j: (0, j)),
                  pl.BlockSpec((1, MOD_TN), lambda j: (0, j))],
        out_specs=pl.BlockSpec((MOD_ROWS, MOD_TN), lambda j: (0, j)),
        out_shape=jax.ShapeDtypeStruct((MOD_ROWS, n), F32),
        name="adaln_mod",
    )(c_all, w_mod, b_mod)


def _permute_cast_w_in(wt_ref, w_s):
    blk = 128
    pad = jnp.zeros((TAIL_W - (W_IN_Z - W_IN_LR) - (W_IN_COLS - W_IN_DT), blk), F32)

    def body(kb, carry):
        k0 = pl.multiple_of(kb * blk, blk)
        cols = pl.ds(k0, blk)
        rows = pl.ds(k0, blk)
        w_s[rows, 0:W_IN_LR] = jnp.transpose(wt_ref[0:W_IN_LR, cols]).astype(BF16)
        w_s[rows, COL_Z:COL_TAIL] = jnp.transpose(wt_ref[W_IN_Z:W_IN_DT, cols]).astype(BF16)
        tail = jnp.concatenate([wt_ref[W_IN_LR:W_IN_Z, cols], wt_ref[W_IN_DT:W_IN_COLS, cols], pad],
                               axis=0)
        w_s[rows, COL_TAIL:PROJ_W] = jnp.transpose(tail).astype(BF16)
        return carry

    lax.fori_loop(0, D_MODEL // blk, body, 0)


def _grid_conv_rows(window, t0, is_ctx, w_ref, b_ref, cols):
    n = window.shape[0] - 2 * GRID_W
    ext = window.astype(F32)
    row_on = jnp.where(is_ctx, 0.0, 1.0)
    taps = []
    for dx in range(CONV_K):
        tap = None
        for dy in range(CONV_K):
            wk = w_ref[CONV_K * dy + dx:CONV_K * dy + dx + 1, cols]
            if dy != 1:
                wk = wk * row_on
            term = wk * ext[GRID_W * dy:GRID_W * dy + n, :]
            tap = term if tap is None else tap + term
        taps.append(tap)
    t = t0 + lax.broadcasted_iota(jnp.int32, (n, 1), 0)
    col = jnp.bitwise_and(t, GRID_W - 1)
    has_left = jnp.where(is_ctx, t, col) != 0
    has_right = jnp.where(is_ctx, t - (CTX_LEN - 1), col - (GRID_W - 1)) != 0
    acc = (taps[1] + b_ref[:, cols]
           + jnp.where(has_left, pltpu.roll(taps[0], 1, axis=0), 0.0)
           + jnp.where(has_right, pltpu.roll(taps[2], n - 1, axis=0), 0.0))
    return _silu(acc)


def _inproj_conv_kernel(x_ref, ctx_ref, m_ref, gpre_ref, w_ref, wg_ref, bg_ref, dtb_ref,
                        cw_ref, cb_ref, qkv_ref, rz_ref, xbc_ref, g_ref, dt_ref,
                        w_s, prev_s, above_s):
    b = pl.program_id(0)
    i = pl.program_id(1)
    n_tiles = pl.num_programs(1) - 1
    ctx_tile = n_tiles - 1

    @pl.when(jnp.logical_and(b == 0, i == 0))
    def _():
        _permute_cast_w_in(w_ref, w_s)

    @pl.when(i == 0)
    def _():
        prev_s[...] = jnp.zeros_like(prev_s)
        above_s[...] = jnp.zeros_like(above_s)

    j = i - 1
    conv_ctx = j == ctx_tile
    split = TM_PROJ - GRID_W
    none = jnp.zeros((GRID_W, SSD_CONV_DIM), BF16)

    col_groups = [slice(c, c + CONV_COLS) for c in range(0, SSD_CONV_DIM, CONV_COLS)]

    def conv_block(t0, n, cols, window):
        out = _grid_conv_rows(window, t0, conv_ctx, cw_ref, cb_ref, cols)
        xbc_ref[0, t0:t0 + n, cols] = out.astype(BF16)

    def conv_previous_head():
        has_above = jnp.logical_and(j >= 1, j < ctx_tile)
        for cols in col_groups:
            above = jnp.where(has_above, above_s[:, cols], none[:, cols])
            conv_block(0, CTX_LEN, cols,
                       jnp.concatenate([above, prev_s[0:CTX_LEN + GRID_W, cols]], axis=0))
            conv_block(CTX_LEN, split - CTX_LEN, cols, prev_s[CTX_LEN - GRID_W:, cols])
        above_s[...] = prev_s[split:, :]

    def conv_previous_last_row(below):
        below = jnp.where(j < ctx_tile - 1, below, none)
        for cols in col_groups:
            conv_block(split, GRID_W, cols,
                       jnp.concatenate([prev_s[split - GRID_W:, cols], below[:, cols]], axis=0))

    @pl.when(i < n_tiles)
    def _():
        is_ctx = i == ctx_tile
        mrow = m_ref[pl.ds(jnp.where(is_ctx, pl.num_programs(0), b), 1), :]
        shift = mrow[:, 0:D_MODEL]
        gain = gpre_ref[...] * (1.0 + mrow[:, D_MODEL:2 * D_MODEL])

        def normed(rs):
            ctx_rows = slice(rs.start % CTX_LEN, rs.start % CTX_LEN + PROJ_SUB)
            x = jnp.where(is_ctx, ctx_ref[0, ctx_rows, :], x_ref[0, rs, :])
            return (_rms(x) * gain + shift).astype(BF16)

        def project(rs, h):
            tail = _dot(h, w_s[:, COL_TAIL:PROJ_W])
            qkv = _dot(h, w_s[:, 0:QKV_W])
            qkv_ref[0, rs, 0:GLA_QK] = (qkv[:, 0:GLA_QK] * (GLA_DK ** -0.5)).astype(BF16)
            qkv_ref[0, rs, GLA_QK:QKV_W] = qkv[:, GLA_QK:QKV_W].astype(BF16)
            rz_ref[0, rs, :] = _silu(_dot(h, w_s[:, COL_R:COL_XBC])).astype(BF16)
            xbc_new = _dot(h, w_s[:, COL_XBC:COL_TAIL]).astype(BF16)
            logits = _dot(tail.astype(BF16), wg_ref[...]) + bg_ref[...]
            u = logits * LOG2_E
            g_ref[0, rs, :] = ((jnp.minimum(u, 0.0) - jnp.log2(1.0 + jnp.exp2(-jnp.abs(u))))
                               * (1.0 / GLA_GATE_NORM))
            dt_ref[0, rs, :] = _softplus(tail + dtb_ref[...])
            return xbc_new

        subs = [slice(PROJ_SUB * s, PROJ_SUB * (s + 1)) for s in range(TM_PROJ // PROJ_SUB)]
        conv_previous_head()
        ready = normed(subs[0])
        fresh = []
        for s, rs in enumerate(subs):
            following = normed(subs[s + 1]) if s + 1 < len(subs) else None
            fresh.append(project(rs, ready))
            if s == 0:
                conv_previous_last_row(fresh[0][0:GRID_W, :])
            ready = following
        for rs, xbc_new in zip(subs, fresh):
            prev_s[rs, :] = xbc_new

    @pl.when(i == n_tiles)
    def _():
        conv_previous_head()
        conv_previous_last_row(none)


def _input_projection_conv(x, ctx, m_all, gpre, w_in, wg_pad, bg_cat, dtb_pad, conv_w9, conv_b):
    bsz, seq, _ = x.shape
    n_lat = seq // TM_PROJ
    n_tiles = n_lat + 1
    tt = CTX_LEN + seq
    const = lambda b, i: (0, 0)
    tok = lambda b, i: (b, jnp.minimum(i, n_tiles - 1), 0)
    conv_tok = lambda b, i: (b, jnp.maximum(i - 1, 0), 0)
    return pl.pallas_call(
        _inproj_conv_kernel,
        grid=(bsz, n_tiles + 1),
        in_specs=[pl.BlockSpec((1, TM_PROJ, D_MODEL), lambda b, i: (b, jnp.minimum(i, n_lat - 1), 0)),
                  pl.BlockSpec((1, CTX_LEN, D_MODEL), lambda b, i: (b, 0, 0)),
                  pl.BlockSpec(m_all.shape, const),
                  pl.BlockSpec((1, D_MODEL), const),
                  pl.BlockSpec((W_IN_COLS, D_MODEL), const, pipeline_mode=pl.Buffered(1)),
                  pl.BlockSpec((TAIL_W, 2 * GLA_QK), const),
                  pl.BlockSpec((1, 2 * GLA_QK), const),
                  pl.BlockSpec((1, TAIL_W), const),
                  pl.BlockSpec((CONV_K * CONV_K, SSD_CONV_DIM), const),
                  pl.BlockSpec((1, SSD_CONV_DIM), const)],
        out_specs=[pl.BlockSpec((1, TM_PROJ, QKV_W), tok),
                   pl.BlockSpec((1, TM_PROJ, GLA_INNER + SSD_INNER), tok),
                   pl.BlockSpec((1, TM_PROJ, SSD_CONV_DIM), conv_tok),
                   pl.BlockSpec((1, TM_PROJ, 2 * GLA_QK), tok),
                   pl.BlockSpec((1, TM_PROJ, TAIL_W), tok)],
        out_shape=[jax.ShapeDtypeStruct((bsz, tt, QKV_W), BF16),
                   jax.ShapeDtypeStruct((bsz, tt, GLA_INNER + SSD_INNER), BF16),
                   jax.ShapeDtypeStruct((bsz, tt, SSD_CONV_DIM), BF16),
                   jax.ShapeDtypeStruct((bsz, tt, 2 * GLA_QK), F32),
                   jax.ShapeDtypeStruct((bsz, tt, TAIL_W), F32)],
        scratch_shapes=[pltpu.VMEM((D_MODEL, PROJ_W), BF16),
                        pltpu.VMEM((TM_PROJ, SSD_CONV_DIM), BF16),
                        pltpu.VMEM((GRID_W, SSD_CONV_DIM), BF16)],
        compiler_params=pltpu.CompilerParams(
            dimension_semantics=("arbitrary", "arbitrary"), vmem_limit_bytes=VMEM_LIMIT),
        name="in_proj_conv",
    )(x, ctx, m_all, gpre, w_in, wg_pad, bg_cat, dtb_pad, conv_w9, conv_b)


def _cumsum_rows(x, reverse):
    n = x.shape[0]
    row = lax.broadcasted_iota(jnp.int32, x.shape, 0)
    s = 1
    while s < n:
        if reverse:
            x = x + jnp.where(row < n - s, pltpu.roll(x, n - s, axis=0), 0.0)
        else:
            x = x + jnp.where(row >= s, pltpu.roll(x, s, axis=0), 0.0)
        s *= 2
    return x


def _chunk_maps(n_ctx, n_all):
    n_lat = n_all - n_ctx
    fwd = lambda s: jnp.where(s < n_ctx, n_lat + s, s - n_ctx)
    bwd = lambda s: n_all - 1 - s
    return fwd, bwd


def _gla_prepare(qkv, g, reverse):
    n = qkv.shape[0]
    nb = n // SUB
    blocks = range(nb)
    rows = lambda a, blk: a[SUB * blk:SUB * (blk + 1), :]
    q = qkv[:, 0:GLA_QK].astype(F32)
    k = qkv[:, GLA_QK:2 * GLA_QK].astype(F32)
    cum = _cumsum_rows(g, reverse)
    zero = jnp.zeros((1, GLA_QK), F32)
    if reverse:
        ends = [cum[SUB * blk:SUB * blk + 1, :] for blk in blocks]
        refs = ends[1:] + [zero]
        earlier = lambda j, blk: j > blk
        tot = ends[0]
    else:
        ends = [cum[SUB * (blk + 1) - 1:SUB * (blk + 1), :] for blk in blocks]
        refs = [zero] + ends[:-1]
        earlier = lambda j, blk: j < blk
        tot = ends[nb - 1]
    stack = lambda vs: jnp.concatenate([jnp.broadcast_to(v, (SUB, GLA_QK)) for v in vs], axis=0)
    e_in = cum - stack(refs)
    q_blk = q * jnp.exp2(e_in)
    k_diag = k * jnp.exp2(-e_in)
    k_end = k * jnp.exp2(stack(ends) - cum)
    q_in = jnp.concatenate([rows(q_blk, blk) * jnp.exp2(refs[blk]) for blk in blocks], axis=0)
    k_out = jnp.concatenate([rows(k_end, blk) * jnp.exp2(tot - ends[blk]) for blk in blocks], axis=0)
    q_in = q_in.astype(BF16)
    k_out = k_out.astype(BF16)
    q_bf = q_blk.astype(BF16)
    zeros_piece = jnp.zeros((SUB, GLA_QK), BF16)
    k_piece = [[rows(k_diag, j).astype(BF16) if seg == j
                else (rows(k_end, j) * jnp.exp2(refs[seg] - ends[j])).astype(BF16) if earlier(j, seg)
                else zeros_piece
                for seg in blocks] for j in blocks]
    ii = lax.broadcasted_iota(jnp.int32, (n, n), 0)
    jj = lax.broadcasted_iota(jnp.int32, (n, n), 1)
    heads = []
    for h in range(GLA_HEADS):
        ks = slice(GLA_DK * h, GLA_DK * (h + 1))
        q_cat = jnp.concatenate(
            [jnp.concatenate([rows(q_bf, blk)[:, ks] if seg == blk else zeros_piece[:, ks]
                              for seg in blocks], axis=1) for blk in blocks], axis=0)
        k_cat = jnp.concatenate(
            [jnp.concatenate([k_piece[j][seg][:, ks] for seg in blocks], axis=1)
             for j in blocks], axis=0)
        heads.append(dict(q_cat=q_cat, k_cat=k_cat, q_in=q_in[:, ks], k_out=k_out[:, ks],
                          v=qkv[:, 2 * GLA_QK + GLA_DV * h:2 * GLA_QK + GLA_DV * (h + 1)]))
    return dict(heads=heads, causal=(ii <= jj) if reverse else (ii >= jj),
                decay_col=jnp.transpose(jnp.broadcast_to(tot, (GLA_DV, GLA_QK))))


def _gla_chain(qkv_ref, g_ref, reverse, s_ref, y_ref):
    def start():
        p = _gla_prepare(qkv_ref[...], g_ref[...], reverse)
        return p, [_dot_nt(hd["q_cat"], hd["k_cat"]) for hd in p["heads"]]

    def finish(started):
        p, att = started
        outs = []
        for h, (hd, a) in enumerate(zip(p["heads"], att)):
            state = s_ref[h]
            lhs = jnp.concatenate([jnp.where(p["causal"], a, 0.0).astype(BF16), hd["q_in"]], axis=1)
            rhs = jnp.concatenate([hd["v"], state.astype(BF16)], axis=0)
            outs.append(_dot(lhs, rhs))
            ks = slice(GLA_DK * h, GLA_DK * (h + 1))
            s_ref[h] = state * jnp.exp2(p["decay_col"][ks, :]) + _dot_tn(hd["k_out"], hd["v"])
        y_ref[...] = jnp.concatenate(outs, axis=1).astype(y_ref.dtype)

    return start, finish


def _ssd_prepare(xbc, dt, nega, expand, lane0, reverse):
    n = xbc.shape[0]
    end = 0 if reverse else n - 1
    cum = _cumsum_rows(dt * nega, reverse)
    e_cum = jnp.exp2(cum)
    w = jnp.exp2(cum[end:end + 1, :] - cum) * dt
    e_cum_x = _dot(_split_hi_lo(e_cum), expand)
    x_w = (xbc[:, 0:SSD_INNER].astype(F32) * _dot(_split_hi_lo(w), expand)).astype(BF16)
    cum_t = jnp.transpose(cum)
    dt_t = jnp.transpose(dt)
    ii = lax.broadcasted_iota(jnp.int32, (n, n), 0)
    jj = lax.broadcasted_iota(jnp.int32, (n, n), 1)
    causal = (ii <= jj) if reverse else (ii >= jj)
    first_head = lax.broadcasted_iota(jnp.int32, (n, 2 * SSD_P), 1) < SSD_P
    groups = []
    for grp in range(SSD_GROUPS):
        b_g = xbc[:, SSD_INNER + SSD_N * grp:SSD_INNER + SSD_N * (grp + 1)]
        c_g = xbc[:, SSD_INNER + SSD_BC + SSD_N * grp:SSD_INNER + SSD_BC + SSD_N * (grp + 1)]
        scores = _dot_nt(c_g, b_g)
        pairs = []
        for pair in range(SSD_HPG // 2):
            m_pair = []
            for hh in range(2):
                lane = lane0 + SSD_HPG * grp + 2 * pair + hh
                cum_b = jnp.broadcast_to(cum[:, lane:lane + 1], (n, n))
                seg = jnp.where(causal, cum_b - cum_t[lane:lane + 1, :], NEG_BIG)
                m_pair.append((scores * jnp.exp2(seg) * dt_t[lane:lane + 1, :]).astype(BF16))
            col = 2 * SSD_P * (SSD_HPG // 2 * grp + pair)
            x_pair = xbc[:, col:col + 2 * SSD_P]
            none = jnp.zeros_like(x_pair)
            rhs = jnp.concatenate([jnp.where(first_head, x_pair, none),
                                   jnp.where(first_head, none, x_pair)], axis=0)
            pairs.append((jnp.concatenate(m_pair, axis=1), rhs))
        cols = slice(SSD_HPG * SSD_P * grp, SSD_HPG * SSD_P * (grp + 1))
        groups.append(dict(b=b_g, c=c_g, pairs=pairs, e_cum=e_cum_x[:, cols],
                           e_tot=e_cum_x[end:end + 1, cols], x_w=x_w[:, cols]))
    return groups


def _split_hi_lo(a):
    hi = a.astype(BF16)
    lo = (a - hi.astype(F32)).astype(BF16)
    return jnp.concatenate([hi, lo], axis=1)


def _ssd_chain(xbc_ref, dt_ref, nega, expand_ref, d, s_ref, y_ref):
    def start():
        return _ssd_prepare(xbc_ref[...], dt_ref[...], nega, expand_ref[d],
                            DT_LANE + SSD_HEADS * d, d == 1)

    def finish(groups):
        outs = []
        for grp, gd in enumerate(groups):
            state = s_ref[grp]
            intra = jnp.concatenate([_dot(m, rhs) for m, rhs in gd["pairs"]], axis=1)
            outs.append(_dot(gd["c"], state.astype(BF16)) * gd["e_cum"] + intra)
            s_ref[grp] = state * gd["e_tot"] + _dot_tn(gd["b"], gd["x_w"])
        y_ref[...] = jnp.concatenate(outs, axis=1).astype(y_ref.dtype)

    return start, finish


def _mixer_scan_kernel(qkv_f_ref, g_f_ref, qkv_b_ref, g_b_ref, xbc_f_ref, dt_f_ref, xbc_b_ref,
                       dt_b_ref, nega_ref, expand_ref, ygf_ref, ygb_ref, ysf_ref, ysb_ref,
                       gsf_ref, gsb_ref, ssf_ref, ssb_ref):
    @pl.when(pl.program_id(1) == 0)
    def _():
        for s_ref in (gsf_ref, gsb_ref, ssf_ref, ssb_ref):
            s_ref[...] = jnp.zeros_like(s_ref)

    nega = nega_ref[...]
    chains = []
    for e in range(qkv_f_ref.shape[0]):
        chains.append(_gla_chain(qkv_f_ref.at[e], g_f_ref.at[e], False, gsf_ref.at[e], ygf_ref.at[e]))
        chains.append(_ssd_chain(xbc_f_ref.at[e], dt_f_ref.at[e], nega, expand_ref, 0,
                                 ssf_ref.at[e], ysf_ref.at[e]))
        chains.append(_gla_chain(qkv_b_ref.at[e], g_b_ref.at[e], True, gsb_ref.at[e], ygb_ref.at[e]))
        chains.append(_ssd_chain(xbc_b_ref.at[e], dt_b_ref.at[e], nega, expand_ref, 1,
                                 ssb_ref.at[e], ysb_ref.at[e]))
    started = chains[0][0]()
    for i, (_, finish) in enumerate(chains):
        following = chains[i + 1][0]() if i + 1 < len(chains) else None
        finish(started)
        started = following


def _head_expand_matrix():
    r = jnp.arange(2 * TAIL_W)[:, None] % TAIL_W
    c = jnp.arange(SSD_INNER)[None, :] // SSD_P
    return jnp.stack([(r == DT_LANE + SSD_HEADS * d + c) for d in range(2)]).astype(BF16)


def _mixer_scan(qkv, g, xbc, dt, nega):
    bsz, tt, _ = qkv.shape
    n_all = tt // CHUNK
    fwd, bwd = _chunk_maps(CTX_LEN // CHUNK, n_all)
    spec = lambda w, m, c=0: pl.BlockSpec((SCAN_BATCH, CHUNK, w), lambda b, s: (b, m(s), c))
    expand = _head_expand_matrix()
    return pl.pallas_call(
        _mixer_scan_kernel,
        grid=(bsz // SCAN_BATCH, n_all),
        in_specs=[spec(QKV_W, fwd), spec(GLA_QK, fwd), spec(QKV_W, bwd), spec(GLA_QK, bwd, 1),
                  spec(SSD_CONV_DIM, fwd), spec(TAIL_W, fwd), spec(SSD_CONV_DIM, bwd), spec(TAIL_W, bwd),
                  pl.BlockSpec((1, TAIL_W), lambda b, s: (0, 0)),
                  pl.BlockSpec(expand.shape, lambda b, s: (0, 0, 0))],
        out_specs=[spec(GLA_INNER, fwd), spec(GLA_INNER, bwd), spec(SSD_INNER, fwd), spec(SSD_INNER, bwd)],
        out_shape=[jax.ShapeDtypeStruct((bsz, tt, GLA_INNER), BF16)] * 2
        + [jax.ShapeDtypeStruct((bsz, tt, SSD_INNER), BF16)] * 2,
        scratch_shapes=[pltpu.VMEM((SCAN_BATCH, GLA_HEADS, GLA_DK, GLA_DV), F32)] * 2
        + [pltpu.VMEM((SCAN_BATCH, SSD_GROUPS, SSD_N, SSD_HPG * SSD_P), F32)] * 2,
        compiler_params=pltpu.CompilerParams(
            dimension_semantics=("parallel", "arbitrary"), vmem_limit_bytes=VMEM_LIMIT),
        name="mixer_scan",
    )(qkv, g, qkv, g, xbc, dt, xbc, dt, nega, expand)


def _stream_cast_weights(pairs, stage, sem):
    def chunk_copy(w_hbm, c, slot):
        cols = w_hbm.shape[1]
        return pltpu.make_async_copy(w_hbm.at[pl.ds(c * W_CHUNK, W_CHUNK), :],
                                     stage.at[slot, :, 0:cols], sem.at[slot])

    chunk_copy(pairs[0][0], 0, 0).start()
    first = 0
    for k, (w_hbm, w_bf) in enumerate(pairs):
        n_chunks, rest = divmod(w_hbm.shape[0], W_CHUNK)
        assert rest == 0
        cols = w_hbm.shape[1]
        following = pairs[k + 1][0] if k + 1 < len(pairs) else None

        def body(c, carry, w_hbm=w_hbm, w_bf=w_bf, n_chunks=n_chunks, cols=cols,
                 following=following, first=first):
            slot = jnp.bitwise_and(first + c, 1)

            @pl.when(c + 1 < n_chunks)
            def _():
                chunk_copy(w_hbm, c + 1, 1 - slot).start()

            if following is not None:
                @pl.when(c + 1 == n_chunks)
                def _():
                    chunk_copy(following, 0, 1 - slot).start()

            chunk_copy(w_hbm, c, slot).wait()
            rows = pl.ds(pl.multiple_of(c * W_CHUNK, W_CHUNK), W_CHUNK)
            w_bf[rows, :] = stage[slot, :, 0:cols].astype(BF16)
            return carry

        lax.fori_loop(0, n_chunks, body, 0)
        first += n_chunks


def _merge_ffn_kernel(x_ref, ygf_ref, ygb_ref, ysf_ref, ysb_ref, xs_ref, rz_ref, m_ref,
                      gnorm_ref, dskip_ref, snorm_ref, npost_ref, nfpre_ref, nfpost_ref,
                      wout_hbm, wgate_hbm, wup_hbm, wdown_hbm, o_ref,
                      wout_ref, wgate_ref, wup_ref, wdown_ref, stage, sem):
    b = pl.program_id(0)

    @pl.when(jnp.logical_and(b == 0, pl.program_id(1) == 0))
    def _():
        _stream_cast_weights([(wout_hbm, wout_ref), (wgate_hbm, wgate_ref),
                              (wup_hbm, wup_ref), (wdown_hbm, wdown_ref)], stage, sem)

    mrow = m_ref[pl.ds(b, 1), :]
    mod = [mrow[:, D_MODEL * j:D_MODEL * (j + 1)] for j in range(6)]
    gw = SSD_INNER // SSD_GROUPS

    def mixer_out(rs):
        y_gla = ygf_ref[0, rs, :].astype(F32) + ygb_ref[0, rs, :].astype(F32)
        o_gla = jnp.concatenate(
            [_rms(y_gla[:, GLA_DV * h:GLA_DV * (h + 1)]) for h in range(GLA_HEADS)], axis=1)
        o_gla = o_gla * gnorm_ref[...] * rz_ref[0, rs, 0:GLA_INNER].astype(F32)
        y_ssd = ysf_ref[0, rs, :].astype(F32) + ysb_ref[0, rs, :].astype(F32)
        y_ssd = ((y_ssd + dskip_ref[...] * xs_ref[0, rs, :].astype(F32))
                 * rz_ref[0, rs, GLA_INNER:].astype(F32))
        o_ssd = jnp.concatenate(
            [_rms(y_ssd[:, gw * g:gw * (g + 1)]) for g in range(SSD_GROUPS)], axis=1)
        o_ssd = o_ssd * snorm_ref[...]
        o = jnp.concatenate([o_gla, o_ssd], axis=1).astype(BF16)
        y = _dot(o, wout_ref[...])
        x1 = x_ref[0, rs, :] + mod[2] * (_rms(y) * npost_ref[...])
        h = ((_rms(x1) * nfpre_ref[...]) * (1.0 + mod[4]) + mod[3]).astype(BF16)
        return x1, h

    def ffn(rs, x1, h):
        act = (_silu(_dot(h, wgate_ref[...])) * _dot(h, wup_ref[...])).astype(BF16)
        f = _dot(act, wdown_ref[...])
        o_ref[0, rs, :] = x1 + mod[5] * (_rms(f) * nfpost_ref[...])

    subs = [slice(FFN_SUB * j, FFN_SUB * (j + 1)) for j in range(TM_FFN // FFN_SUB)]
    ready = mixer_out(subs[0])
    for j, rs in enumerate(subs):
        following = mixer_out(subs[j + 1]) if j + 1 < len(subs) else None
        ffn(rs, *ready)
        ready = following


def _merge_ffn(x, ygf, ygb, ysf, ysb, xbc, rz, m_all, gnorm, dskip, snorm, npost, nfpre,
               nfpost, wout, wgate, wup, wdown):
    bsz, seq, _ = x.shape
    n_tiles = seq // TM_FFN
    const = lambda b, i: (0, 0)

    def lat(width):
        return pl.BlockSpec((1, TM_FFN, width), lambda b, i: (b, i, 0))

    weights = (wout, wgate, wup, wdown)
    in_hbm = pl.BlockSpec(memory_space=pl.ANY)
    vec = lambda w: pl.BlockSpec((1, w), const)
    return pl.pallas_call(
        _merge_ffn_kernel,
        grid=(bsz, n_tiles),
        in_specs=[pl.BlockSpec((1, TM_FFN, D_MODEL), lambda b, i: (b, i, 0)),
                  lat(GLA_INNER), lat(GLA_INNER), lat(SSD_INNER), lat(SSD_INNER),
                  lat(SSD_INNER), lat(GLA_INNER + SSD_INNER),
                  pl.BlockSpec(m_all.shape, const),
                  vec(GLA_INNER), vec(SSD_INNER), vec(SSD_INNER),
                  vec(D_MODEL), vec(D_MODEL), vec(D_MODEL),
                  in_hbm, in_hbm, in_hbm, in_hbm],
        out_specs=pl.BlockSpec((1, TM_FFN, D_MODEL), lambda b, i: (b, i, 0)),
        out_shape=jax.ShapeDtypeStruct((bsz, seq, D_MODEL), F32),
        scratch_shapes=[pltpu.VMEM(w.shape, BF16) for w in weights]
        + [pltpu.VMEM((2, W_CHUNK, max(w.shape[1] for w in weights)), F32),
           pltpu.SemaphoreType.DMA((2,))],
        compiler_params=pltpu.CompilerParams(
            dimension_semantics=("arbitrary", "arbitrary"), vmem_limit_bytes=VMEM_LIMIT),
        name="merge_ffn",
    )(x, ygf, ygb, ysf, ysb, xbc, rz, m_all, gnorm, dskip, snorm, npost, nfpre, nfpost,
      wout, wgate, wup, wdown)


def kernel(x, c, ctx, c_ctx, w_mod, b_mod, norm_mix_pre, norm_mix_post, norm_ffn_pre, norm_ffn_post,
           w_in, conv_w, conv_b, gla_wg_f, gla_bg_f, gla_wg_b, gla_bg_b, gla_norm,
           a_log_f, a_log_b, dt_bias_f, dt_bias_b, d_skip, ssd_norm, w_out, w_gate, w_up, w_down):
    assert w_mod.shape[0] == 1, "single-layer kernel"
    bsz = x.shape[0]
    row = lambda v: v.reshape(1, -1)

    c_all = jnp.concatenate([c, c_ctx[None], jnp.zeros((MOD_ROWS - bsz - 1, D_MODEL), F32)], axis=0)
    wg_pad = jnp.zeros((TAIL_W, 2 * GLA_QK), F32)
    wg_pad = wg_pad.at[0:GLA_RANK, 0:GLA_QK].set(gla_wg_f[0])
    wg_pad = wg_pad.at[GLA_RANK:2 * GLA_RANK, GLA_QK:].set(gla_wg_b[0]).astype(BF16)
    bg_cat = jnp.concatenate([gla_bg_f[0], gla_bg_b[0]]).reshape(1, -1)
    lane_pad = lambda f, bk: jnp.zeros((1, TAIL_W), F32).at[0, DT_LANE:DT_LANE + 2 * SSD_HEADS].set(
        jnp.concatenate([f, bk]))
    dtb_pad = lane_pad(dt_bias_f[0], dt_bias_b[0])
    nega = lane_pad(-jnp.exp(a_log_f[0]) * LOG2_E, -jnp.exp(a_log_b[0]) * LOG2_E)
    conv_w9 = conv_w[0].reshape(CONV_K * CONV_K, SSD_CONV_DIM)

    m_all = _modulation(c_all, w_mod[0], row(b_mod[0]))
    qkv, rz, xbc, g, dt = _input_projection_conv(
        x, ctx, m_all, row(norm_mix_pre[0]), w_in[0].T, wg_pad, bg_cat, dtb_pad, conv_w9, row(conv_b[0]))
    ygf, ygb, ysf, ysb = _mixer_scan(qkv, g, xbc, dt, nega)
    return _merge_ffn(
        x, ygf, ygb, ysf, ysb, xbc, rz, m_all,
        row(jnp.tile(gla_norm[0], GLA_HEADS)), row(jnp.repeat(d_skip[0], SSD_P)), row(ssd_norm[0]),
        row(norm_mix_post[0]), row(norm_ffn_pre[0]), row(norm_ffn_post[0]),
        w_out[0], w_gate[0], w_up[0], w_down[0])
```

```python
import jax
import jax.numpy as jnp
from jax import lax
from jax.experimental import pallas as pl
from jax.experimental.pallas import tpu as pltpu

F32 = jnp.float32
BF16 = jnp.bfloat16

D_MODEL = 1024
CTX_LEN = 256
GRID_W = 64
EPS = 1e-6

GLA_HEADS = 4
GLA_DK = 64
GLA_DV = 128
GLA_QK = GLA_HEADS * GLA_DK
GLA_INNER = GLA_HEADS * GLA_DV
GLA_RANK = 16
GLA_GATE_NORM = 16.0

SSD_HEADS = 8
SSD_P = 64
SSD_N = 64
SSD_GROUPS = 2
SSD_HPG = SSD_HEADS // SSD_GROUPS
SSD_INNER = SSD_HEADS * SSD_P
SSD_BC = SSD_GROUPS * SSD_N
SSD_CONV_DIM = SSD_INNER + 2 * SSD_BC
CONV_K = 3
D_FF = 2816

QKV_W = 2 * GLA_QK + GLA_INNER
COL_R = QKV_W
COL_Z = COL_R + GLA_INNER
COL_XBC = COL_Z + SSD_INNER
COL_TAIL = COL_XBC + SSD_CONV_DIM
TAIL_W = 128
PROJ_W = COL_TAIL + TAIL_W
DT_LANE = 2 * GLA_RANK
W_IN_LR = COL_Z
W_IN_Z = W_IN_LR + 2 * GLA_RANK
W_IN_DT = W_IN_Z + SSD_INNER + SSD_CONV_DIM
W_IN_COLS = W_IN_DT + 2 * SSD_HEADS

TM_PROJ = 512
PROJ_SUB = 256
CONV_COLS = 256
MOD_ROWS = 8
MOD_TN = 1024
TM_FFN = 512
FFN_SUB = 128
W_CHUNK = 256
CHUNK = 128
SCAN_BATCH = 4
SUB = 32
NEG_BIG = -1e30
LOG2_E = 1.4426950408889634
VMEM_LIMIT = 56 * 1024 * 1024


def _silu(x):
    return x / (1.0 + jnp.exp2(x * -LOG2_E))


def _softplus(x):
    return jnp.maximum(x, 0.0) + jnp.log(1.0 + jnp.exp(-jnp.abs(x)))


def _rms(x):
    return x * lax.rsqrt(jnp.mean(x * x, axis=-1, keepdims=True) + EPS)


def _dot(a, b):
    return jnp.dot(a, b, preferred_element_type=F32)


def _dot_nt(a, b):
    return lax.dot_general(a, b, (((1,), (1,)), ((), ())), preferred_element_type=F32)


def _dot_tn(a, b):
    return lax.dot_general(a, b, (((0,), (0,)), ((), ())), preferred_element_type=F32)


def _mod_kernel(c_ref, w_ref, b_ref, o_ref):
    s = _silu(c_ref[...])
    o_ref[...] = _dot(s.astype(BF16), w_ref[...].astype(BF16)) + b_ref[...]


def _modulation(c_all, w_mod, b_mod):
    n = w_mod.shape[1]
    return pl.pallas_call(
        _mod_kernel,
        grid=(n // MOD_TN,),
        in_specs=[pl.BlockSpec((MOD_ROWS, D_MODEL), lambda j: (0, 0)),
                  pl.BlockSpec((D_MODEL, MOD_TN), lambda j: (0, j)),
                  pl.BlockSpec((1, MOD_TN), lambda j: (0, j))],
        out_specs=pl.BlockSpec((MOD_ROWS, MOD_TN), lambda j: (0, j)),
        out_shape=jax.ShapeDtypeStruct((MOD_ROWS, n), F32),
        name="adaln_mod",
    )(c_all, w_mod, b_mod)


def _permute_cast_w_in(wt_ref, w_s):
    blk = 128
    pad = jnp.zeros((TAIL_W - (W_IN_Z - W_IN_LR) - (W_IN_COLS - W_IN_DT), blk), F32)

    def body(kb, carry):
        k0 = pl.multiple_of(kb * blk, blk)
        cols = pl.ds(k0, blk)
        rows = pl.ds(k0, blk)
        w_s[rows, 0:W_IN_LR] = jnp.transpose(wt_ref[0:W_IN_LR, cols]).astype(BF16)
        w_s[rows, COL_Z:COL_TAIL] = jnp.transpose(wt_ref[W_IN_Z:W_IN_DT, cols]).astype(BF16)
        tail = jnp.concatenate([wt_ref[W_IN_LR:W_IN_Z, cols], wt_ref[W_IN_DT:W_IN_COLS, cols], pad],
                               axis=0)
        w_s[rows, COL_TAIL:PROJ_W] = jnp.transpose(tail).astype(BF16)
        return carry

    lax.fori_loop(0, D_MODEL // blk, body, 0)


def _grid_conv_rows(window, t0, is_ctx, w_ref, b_ref, cols):
    n = window.shape[0] - 2 * GRID_W
    ext = window.astype(F32)
    row_on = jnp.where(is_ctx, 0.0, 1.0)
    taps = []
    for dx in range(CONV_K):
        tap = None
        for dy in range(CONV_K):
            wk = w_ref[CONV_K * dy + dx:CONV_K * dy + dx + 1, cols]
            if dy != 1:
                wk = wk * row_on
            term = wk * ext[GRID_W * dy:GRID_W * dy + n, :]
            tap = term if tap is None else tap + term
        taps.append(tap)
    t = t0 + lax.broadcasted_iota(jnp.int32, (n, 1), 0)
    col = jnp.bitwise_and(t, GRID_W - 1)
    has_left = jnp.where(is_ctx, t, col) != 0
    has_right = jnp.where(is_ctx, t - (CTX_LEN - 1), col - (GRID_W - 1)) != 0
    acc = (taps[1] + b_ref[:, cols]
           + jnp.where(has_left, pltpu.roll(taps[0], 1, axis=0), 0.0)
           + jnp.where(has_right, pltpu.roll(taps[2], n - 1, axis=0), 0.0))
    return _silu(acc)


def _inproj_conv_kernel(x_ref, ctx_ref, m_ref, gpre_ref, w_ref, wg_ref, bg_ref, dtb_ref,
                        cw_ref, cb_ref, qkv_ref, rz_ref, xbc_ref, g_ref, dt_ref,
                        w_s, prev_s, above_s):
    b = pl.program_id(0)
    i = pl.program_id(1)
    n_tiles = pl.num_programs(1) - 1
    ctx_tile = n_tiles - 1

    @pl.when(jnp.logical_and(b == 0, i == 0))
    def _():
        _permute_cast_w_in(w_ref, w_s)

    @pl.when(i == 0)
    def _():
        prev_s[...] = jnp.zeros_like(prev_s)
        above_s[...] = jnp.zeros_like(above_s)

    j = i - 1
    conv_ctx = j == ctx_tile
    none = jnp.zeros((GRID_W, SSD_CONV_DIM), BF16)

    col_groups = [slice(c, c + CONV_COLS) for c in range(0, SSD_CONV_DIM, CONV_COLS)]

    half = TM_PROJ // 2

    def conv_block(t0, cols, window):
        out = _grid_conv_rows(window, t0, conv_ctx, cw_ref, cb_ref, cols)
        xbc_ref[0, t0:t0 + half, cols] = out.astype(BF16)

    def conv_upper_half():
        has_above = jnp.logical_and(j >= 1, j < ctx_tile)
        for cols in col_groups:
            above = jnp.where(has_above, above_s[:, cols], none[:, cols])
            conv_block(0, cols, jnp.concatenate([above, prev_s[0:half + GRID_W, cols]], axis=0))
        above_s[...] = prev_s[TM_PROJ - GRID_W:, :]

    def conv_lower_half(below):
        below = jnp.where(j < ctx_tile - 1, below, none)
        for cols in col_groups:
            conv_block(half, cols,
                       jnp.concatenate([prev_s[half - GRID_W:, cols], below[:, cols]], axis=0))

    @pl.when(i < n_tiles)
    def _():
        is_ctx = i == ctx_tile
        mrow = m_ref[pl.ds(jnp.where(is_ctx, pl.num_programs(0), b), 1), :]
        shift = mrow[:, 0:D_MODEL]
        gain = gpre_ref[...] * (1.0 + mrow[:, D_MODEL:2 * D_MODEL])

        def normed(rs):
            ctx_rows = slice(rs.start % CTX_LEN, rs.start % CTX_LEN + PROJ_SUB)
            x = jnp.where(is_ctx, ctx_ref[0, ctx_rows, :], x_ref[0, rs, :])
            return (_rms(x) * gain + shift).astype(BF16)

        def project(rs, h):
            xbc_new = _dot(h, w_s[:, COL_XBC:COL_TAIL]).astype(BF16)
            tail = _dot(h, w_s[:, COL_TAIL:PROJ_W])
            qkv = _dot(h, w_s[:, 0:QKV_W])
            qkv_ref[0, rs, 0:GLA_QK] = (qkv[:, 0:GLA_QK] * (GLA_DK ** -0.5)).astype(BF16)
            qkv_ref[0, rs, GLA_QK:QKV_W] = qkv[:, GLA_QK:QKV_W].astype(BF16)
            rz_ref[0, rs, :] = _silu(_dot(h, w_s[:, COL_R:COL_XBC])).astype(BF16)
            logits = _dot(tail.astype(BF16), wg_ref[...]) + bg_ref[...]
            u = logits * LOG2_E
            g_ref[0, rs, :] = ((jnp.minimum(u, 0.0) - jnp.log2(1.0 + jnp.exp2(-jnp.abs(u))))
                               * (1.0 / GLA_GATE_NORM))
            dt_ref[0, rs, :] = _softplus(tail + dtb_ref[...])
            return xbc_new

        subs = [slice(PROJ_SUB * s, PROJ_SUB * (s + 1)) for s in range(TM_PROJ // PROJ_SUB)]
        ready = normed(subs[0])
        conv_upper_half()
        fresh = []
        for s, rs in enumerate(subs):
            following = normed(subs[s + 1]) if s + 1 < len(subs) else None
            fresh.append(project(rs, ready))
            if s == 0:
                conv_lower_half(fresh[0][0:GRID_W, :])
            ready = following
        for rs, xbc_new in zip(subs, fresh):
            prev_s[rs, :] = xbc_new

    @pl.when(i == n_tiles)
    def _():
        conv_upper_half()
        conv_lower_half(none)


def _input_projection_conv(x, ctx, m_all, gpre, w_in, wg_pad, bg_cat, dtb_pad, conv_w9, conv_b):
    bsz, seq, _ = x.shape
    n_lat = seq // TM_PROJ
    n_tiles = n_lat + 1
    tt = CTX_LEN + seq
    const = lambda b, i: (0, 0)
    tok = lambda b, i: (b, jnp.minimum(i, n_tiles - 1), 0)
    conv_tok = lambda b, i: (b, jnp.maximum(i - 1, 0), 0)
    return pl.pallas_call(
        _inproj_conv_kernel,
        grid=(bsz, n_tiles + 1),
        in_specs=[pl.BlockSpec((1, TM_PROJ, D_MODEL), lambda b, i: (b, jnp.minimum(i, n_lat - 1), 0)),
                  pl.BlockSpec((1, CTX_LEN, D_MODEL), lambda b, i: (b, 0, 0)),
                  pl.BlockSpec(m_all.shape, const),
                  pl.BlockSpec((1, D_MODEL), const),
                  pl.BlockSpec((W_IN_COLS, D_MODEL), const, pipeline_mode=pl.Buffered(1)),
                  pl.BlockSpec((TAIL_W, 2 * GLA_QK), const),
                  pl.BlockSpec((1, 2 * GLA_QK), const),
                  pl.BlockSpec((1, TAIL_W), const),
                  pl.BlockSpec((CONV_K * CONV_K, SSD_CONV_DIM), const),
                  pl.BlockSpec((1, SSD_CONV_DIM), const)],
        out_specs=[pl.BlockSpec((1, TM_PROJ, QKV_W), tok),
                   pl.BlockSpec((1, TM_PROJ, GLA_INNER + SSD_INNER), tok),
                   pl.BlockSpec((1, TM_PROJ, SSD_CONV_DIM), conv_tok),
                   pl.BlockSpec((1, TM_PROJ, 2 * GLA_QK), tok),
                   pl.BlockSpec((1, TM_PROJ, TAIL_W), tok)],
        out_shape=[jax.ShapeDtypeStruct((bsz, tt, QKV_W), BF16),
                   jax.ShapeDtypeStruct((bsz, tt, GLA_INNER + SSD_INNER), BF16),
                   jax.ShapeDtypeStruct((bsz, tt, SSD_CONV_DIM), BF16),
                   jax.ShapeDtypeStruct((bsz, tt, 2 * GLA_QK), F32),
                   jax.ShapeDtypeStruct((bsz, tt, TAIL_W), F32)],
        scratch_shapes=[pltpu.VMEM((D_MODEL, PROJ_W), BF16),
                        pltpu.VMEM((TM_PROJ, SSD_CONV_DIM), BF16),
                        pltpu.VMEM((GRID_W, SSD_CONV_DIM), BF16)],
        compiler_params=pltpu.CompilerParams(
            dimension_semantics=("arbitrary", "arbitrary"), vmem_limit_bytes=VMEM_LIMIT),
        name="in_proj_conv",
    )(x, ctx, m_all, gpre, w_in, wg_pad, bg_cat, dtb_pad, conv_w9, conv_b)


def _cumsum_rows(x, reverse):
    n = x.shape[0]
    row = lax.broadcasted_iota(jnp.int32, x.shape, 0)
    s = 1
    while s < n:
        if reverse:
            x = x + jnp.where(row < n - s, pltpu.roll(x, n - s, axis=0), 0.0)
        else:
            x = x + jnp.where(row >= s, pltpu.roll(x, s, axis=0), 0.0)
        s *= 2
    return x


def _chunk_maps(n_ctx, n_all):
    n_lat = n_all - n_ctx
    fwd = lambda s: jnp.where(s < n_ctx, n_lat + s, s - n_ctx)
    bwd = lambda s: n_all - 1 - s
    return fwd, bwd


def _gla_prepare(qkv, g, reverse):
    n = qkv.shape[0]
    nb = n // SUB
    blocks = range(nb)
    rows = lambda a, blk: a[SUB * blk:SUB * (blk + 1), :]
    q = qkv[:, 0:GLA_QK].astype(F32)
    k = qkv[:, GLA_QK:2 * GLA_QK].astype(F32)
    cum = _cumsum_rows(g, reverse)
    zero = jnp.zeros((1, GLA_QK), F32)
    if reverse:
        ends = [cum[SUB * blk:SUB * blk + 1, :] for blk in blocks]
        refs = ends[1:] + [zero]
        earlier = lambda j, blk: j > blk
        tot = ends[0]
    else:
        ends = [cum[SUB * (blk + 1) - 1:SUB * (blk + 1), :] for blk in blocks]
        refs = [zero] + ends[:-1]
        earlier = lambda j, blk: j < blk
        tot = ends[nb - 1]
    stack = lambda vs: jnp.concatenate([jnp.broadcast_to(v, (SUB, GLA_QK)) for v in vs], axis=0)
    e_in = cum - stack(refs)
    q_blk = q * jnp.exp2(e_in)
    k_diag = k * jnp.exp2(-e_in)
    k_end = k * jnp.exp2(stack(ends) - cum)
    q_in = jnp.concatenate([rows(q_blk, blk) * jnp.exp2(refs[blk]) for blk in blocks], axis=0)
    k_out = jnp.concatenate([rows(k_end, blk) * jnp.exp2(tot - ends[blk]) for blk in blocks], axis=0)
    q_in = q_in.astype(BF16)
    k_out = k_out.astype(BF16)
    q_bf = q_blk.astype(BF16)
    zeros_piece = jnp.zeros((SUB, GLA_QK), BF16)
    k_piece = [[rows(k_diag, j).astype(BF16) if seg == j
                else (rows(k_end, j) * jnp.exp2(refs[seg] - ends[j])).astype(BF16) if earlier(j, seg)
                else zeros_piece
                for seg in blocks] for j in blocks]
    ii = lax.broadcasted_iota(jnp.int32, (n, n), 0)
    jj = lax.broadcasted_iota(jnp.int32, (n, n), 1)
    heads = []
    for h in range(GLA_HEADS):
        ks = slice(GLA_DK * h, GLA_DK * (h + 1))
        q_cat = jnp.concatenate(
            [jnp.concatenate([rows(q_bf, blk)[:, ks] if seg == blk else zeros_piece[:, ks]
                              for seg in blocks], axis=1) for blk in blocks], axis=0)
        k_cat = jnp.concatenate(
            [jnp.concatenate([k_piece[j][seg][:, ks] for seg in blocks], axis=1)
             for j in blocks], axis=0)
        heads.append(dict(q_cat=q_cat, k_cat=k_cat, q_in=q_in[:, ks], k_out=k_out[:, ks],
                          v=qkv[:, 2 * GLA_QK + GLA_DV * h:2 * GLA_QK + GLA_DV * (h + 1)]))
    return dict(heads=heads, causal=(ii <= jj) if reverse else (ii >= jj),
                decay_col=jnp.transpose(jnp.broadcast_to(tot, (GLA_DV, GLA_QK))))


def _gla_chain(qkv_ref, g_ref, reverse, s_ref, y_ref):
    def start():
        p = _gla_prepare(qkv_ref[...], g_ref[...], reverse)
        return p, [_dot_nt(hd["q_cat"], hd["k_cat"]) for hd in p["heads"]]

    def finish(started):
        p, att = started
        outs = []
        for h, (hd, a) in enumerate(zip(p["heads"], att)):
            state = s_ref[h]
            lhs = jnp.concatenate([jnp.where(p["causal"], a, 0.0).astype(BF16), hd["q_in"]], axis=1)
            rhs = jnp.concatenate([hd["v"], state.astype(BF16)], axis=0)
            outs.append(_dot(lhs, rhs))
            ks = slice(GLA_DK * h, GLA_DK * (h + 1))
            s_ref[h] = state * jnp.exp2(p["decay_col"][ks, :]) + _dot_tn(hd["k_out"], hd["v"])
        y_ref[...] = jnp.concatenate(outs, axis=1).astype(y_ref.dtype)

    return start, finish


def _ssd_prepare(xbc, dt, nega, expand, lane0, reverse):
    n = xbc.shape[0]
    end = 0 if reverse else n - 1
    cum = _cumsum_rows(dt * nega, reverse)
    e_cum = jnp.exp2(cum)
    w = jnp.exp2(cum[end:end + 1, :] - cum) * dt
    e_cum_x = _dot(_split_hi_lo(e_cum), expand)
    x_w = xbc[:, 0:SSD_INNER] * _dot(w.astype(BF16), expand[0:TAIL_W, :]).astype(BF16)
    cum_t = jnp.transpose(cum)
    dt_t = jnp.transpose(dt).astype(BF16)
    ii = lax.broadcasted_iota(jnp.int32, (n, n), 0)
    jj = lax.broadcasted_iota(jnp.int32, (n, n), 1)
    causal = (ii <= jj) if reverse else (ii >= jj)
    first_head = lax.broadcasted_iota(jnp.int32, (n, 2 * SSD_P), 1) < SSD_P
    groups = []
    for grp in range(SSD_GROUPS):
        b_g = xbc[:, SSD_INNER + SSD_N * grp:SSD_INNER + SSD_N * (grp + 1)]
        c_g = xbc[:, SSD_INNER + SSD_BC + SSD_N * grp:SSD_INNER + SSD_BC + SSD_N * (grp + 1)]
        scores = _dot_nt(c_g, b_g).astype(BF16)
        pairs = []
        for pair in range(SSD_HPG // 2):
            m_pair = []
            for hh in range(2):
                lane = lane0 + SSD_HPG * grp + 2 * pair + hh
                cum_b = jnp.broadcast_to(cum[:, lane:lane + 1], (n, n))
                seg = jnp.where(causal, cum_b - cum_t[lane:lane + 1, :], NEG_BIG)
                m_pair.append(scores * jnp.exp2(seg).astype(BF16) * dt_t[lane:lane + 1, :])
            col = 2 * SSD_P * (SSD_HPG // 2 * grp + pair)
            x_pair = xbc[:, col:col + 2 * SSD_P]
            none = jnp.zeros_like(x_pair)
            rhs = jnp.concatenate([jnp.where(first_head, x_pair, none),
                                   jnp.where(first_head, none, x_pair)], axis=0)
            pairs.append((jnp.concatenate(m_pair, axis=1), rhs))
        cols = slice(SSD_HPG * SSD_P * grp, SSD_HPG * SSD_P * (grp + 1))
        groups.append(dict(b=b_g, c=c_g, pairs=pairs, e_cum=e_cum_x[:, cols],
                           e_tot=e_cum_x[end:end + 1, cols], x_w=x_w[:, cols]))
    return groups


def _split_hi_lo(a):
    hi = a.astype(BF16)
    lo = (a - hi.astype(F32)).astype(BF16)
    return jnp.concatenate([hi, lo], axis=1)


def _ssd_chain(xbc_ref, dt_ref, nega, expand_ref, d, s_ref, y_ref):
    def start():
        return _ssd_prepare(xbc_ref[...], dt_ref[...], nega, expand_ref[d],
                            DT_LANE + SSD_HEADS * d, d == 1)

    def finish(groups):
        outs = []
        for grp, gd in enumerate(groups):
            state = s_ref[grp]
            intra = jnp.concatenate([_dot(m, rhs) for m, rhs in gd["pairs"]], axis=1)
            outs.append(_dot(gd["c"], state.astype(BF16)) * gd["e_cum"] + intra)
            s_ref[grp] = state * gd["e_tot"] + _dot_tn(gd["b"], gd["x_w"])
        y_ref[...] = jnp.concatenate(outs, axis=1).astype(y_ref.dtype)

    return start, finish


def _mixer_scan_kernel(qkv_f_ref, g_f_ref, qkv_b_ref, g_b_ref, xbc_f_ref, dt_f_ref, xbc_b_ref,
                       dt_b_ref, nega_ref, expand_ref, ygf_ref, ygb_ref, ysf_ref, ysb_ref,
                       gsf_ref, gsb_ref, ssf_ref, ssb_ref):
    @pl.when(pl.program_id(1) == 0)
    def _():
        for s_ref in (gsf_ref, gsb_ref, ssf_ref, ssb_ref):
            s_ref[...] = jnp.zeros_like(s_ref)

    nega = nega_ref[...]
    chains = []
    for e in range(qkv_f_ref.shape[0]):
        chains.append(_gla_chain(qkv_f_ref.at[e], g_f_ref.at[e], False, gsf_ref.at[e], ygf_ref.at[e]))
        chains.append(_ssd_chain(xbc_f_ref.at[e], dt_f_ref.at[e], nega, expand_ref, 0,
                                 ssf_ref.at[e], ysf_ref.at[e]))
        chains.append(_gla_chain(qkv_b_ref.at[e], g_b_ref.at[e], True, gsb_ref.at[e], ygb_ref.at[e]))
        chains.append(_ssd_chain(xbc_b_ref.at[e], dt_b_ref.at[e], nega, expand_ref, 1,
                                 ssb_ref.at[e], ysb_ref.at[e]))
    started = chains[0][0]()
    for i, (_, finish) in enumerate(chains):
        following = chains[i + 1][0]() if i + 1 < len(chains) else None
        finish(started)
        started = following


def _head_expand_matrix():
    r = jnp.arange(2 * TAIL_W)[:, None] % TAIL_W
    c = jnp.arange(SSD_INNER)[None, :] // SSD_P
    return jnp.stack([(r == DT_LANE + SSD_HEADS * d + c) for d in range(2)]).astype(BF16)


def _mixer_scan(qkv, g, xbc, dt, nega):
    bsz, tt, _ = qkv.shape
    n_all = tt // CHUNK
    fwd, bwd = _chunk_maps(CTX_LEN // CHUNK, n_all)
    spec = lambda w, m, c=0: pl.BlockSpec((SCAN_BATCH, CHUNK, w), lambda b, s: (b, m(s), c))
    expand = _head_expand_matrix()
    return pl.pallas_call(
        _mixer_scan_kernel,
        grid=(bsz // SCAN_BATCH, n_all),
        in_specs=[spec(QKV_W, fwd), spec(GLA_QK, fwd), spec(QKV_W, bwd), spec(GLA_QK, bwd, 1),
                  spec(SSD_CONV_DIM, fwd), spec(TAIL_W, fwd), spec(SSD_CONV_DIM, bwd), spec(TAIL_W, bwd),
                  pl.BlockSpec((1, TAIL_W), lambda b, s: (0, 0)),
                  pl.BlockSpec(expand.shape, lambda b, s: (0, 0, 0))],
        out_specs=[spec(GLA_INNER, fwd), spec(GLA_INNER, bwd), spec(SSD_INNER, fwd), spec(SSD_INNER, bwd)],
        out_shape=[jax.ShapeDtypeStruct((bsz, tt, GLA_INNER), BF16)] * 2
        + [jax.ShapeDtypeStruct((bsz, tt, SSD_INNER), BF16)] * 2,
        scratch_shapes=[pltpu.VMEM((SCAN_BATCH, GLA_HEADS, GLA_DK, GLA_DV), F32)] * 2
        + [pltpu.VMEM((SCAN_BATCH, SSD_GROUPS, SSD_N, SSD_HPG * SSD_P), F32)] * 2,
        compiler_params=pltpu.CompilerParams(
            dimension_semantics=("parallel", "arbitrary"), vmem_limit_bytes=VMEM_LIMIT),
        name="mixer_scan",
    )(qkv, g, qkv, g, xbc, dt, xbc, dt, nega, expand)


def _stream_cast_weights(pairs, stage, sem):
    def chunk_copy(w_hbm, c, slot):
        cols = w_hbm.shape[1]
        return pltpu.make_async_copy(w_hbm.at[pl.ds(c * W_CHUNK, W_CHUNK), :],
                                     stage.at[slot, :, 0:cols], sem.at[slot])

    chunk_copy(pairs[0][0], 0, 0).start()
    first = 0
    for k, (w_hbm, w_bf) in enumerate(pairs):
        n_chunks, rest = divmod(w_hbm.shape[0], W_CHUNK)
        assert rest == 0
        cols = w_hbm.shape[1]
        following = pairs[k + 1][0] if k + 1 < len(pairs) else None

        def body(c, carry, w_hbm=w_hbm, w_bf=w_bf, n_chunks=n_chunks, cols=cols,
                 following=following, first=first):
            slot = jnp.bitwise_and(first + c, 1)

            @pl.when(c + 1 < n_chunks)
            def _():
                chunk_copy(w_hbm, c + 1, 1 - slot).start()

            if following is not None:
                @pl.when(c + 1 == n_chunks)
                def _():
                    chunk_copy(following, 0, 1 - slot).start()

            chunk_copy(w_hbm, c, slot).wait()
            rows = pl.ds(pl.multiple_of(c * W_CHUNK, W_CHUNK), W_CHUNK)
            w_bf[rows, :] = stage[slot, :, 0:cols].astype(BF16)
            return carry

        lax.fori_loop(0, n_chunks, body, 0)
        first += n_chunks


def _merge_ffn_kernel(x_ref, ygf_ref, ygb_ref, ysf_ref, ysb_ref, xs_ref, rz_ref, m_ref,
                      gnorm_ref, dskip_ref, snorm_ref, npost_ref, nfpre_ref, nfpost_ref,
                      wout_hbm, wgate_hbm, wup_hbm, wdown_hbm, o_ref,
                      wout_ref, wgate_ref, wup_ref, wdown_ref, stage, sem):
    b = pl.program_id(0)

    @pl.when(jnp.logical_and(b == 0, pl.program_id(1) == 0))
    def _():
        _stream_cast_weights([(wout_hbm, wout_ref), (wgate_hbm, wgate_ref),
                              (wup_hbm, wup_ref), (wdown_hbm, wdown_ref)], stage, sem)

    mrow = m_ref[pl.ds(b, 1), :]
    mod = [mrow[:, D_MODEL * j:D_MODEL * (j + 1)] for j in range(6)]
    gw = SSD_INNER // SSD_GROUPS

    def merged(rs):
        y_gla = ygf_ref[0, rs, :].astype(F32) + ygb_ref[0, rs, :].astype(F32)
        o_gla = jnp.concatenate(
            [_rms(y_gla[:, GLA_DV * h:GLA_DV * (h + 1)]) for h in range(GLA_HEADS)], axis=1)
        o_gla = o_gla * gnorm_ref[...] * rz_ref[0, rs, 0:GLA_INNER].astype(F32)
        y_ssd = ysf_ref[0, rs, :].astype(F32) + ysb_ref[0, rs, :].astype(F32)
        y_ssd = ((y_ssd + dskip_ref[...] * xs_ref[0, rs, :].astype(F32))
                 * rz_ref[0, rs, GLA_INNER:].astype(F32))
        o_ssd = jnp.concatenate(
            [_rms(y_ssd[:, gw * g:gw * (g + 1)]) for g in range(SSD_GROUPS)], axis=1)
        o_ssd = o_ssd * snorm_ref[...]
        return jnp.concatenate([o_gla, o_ssd], axis=1).astype(BF16)

    def residual_in(rs, o):
        y = _dot(o, wout_ref[...])
        x1 = x_ref[0, rs, :] + mod[2] * (_rms(y) * npost_ref[...])
        h = ((_rms(x1) * nfpre_ref[...]) * (1.0 + mod[4]) + mod[3]).astype(BF16)
        return x1, h

    subs = [slice(FFN_SUB * j, FFN_SUB * (j + 1)) for j in range(TM_FFN // FFN_SUB)]
    x1, h = residual_in(subs[0], merged(subs[0]))
    for j, rs in enumerate(subs):
        following = residual_in(subs[j + 1], merged(subs[j + 1])) if j + 1 < len(subs) else None
        act = (_silu(_dot(h, wgate_ref[...])) * _dot(h, wup_ref[...])).astype(BF16)
        f = _dot(act, wdown_ref[...])
        o_ref[0, rs, :] = x1 + mod[5] * (_rms(f) * nfpost_ref[...])
        if following is not None:
            x1, h = following


def _merge_ffn(x, ygf, ygb, ysf, ysb, xbc, rz, m_all, gnorm, dskip, snorm, npost, nfpre,
               nfpost, wout, wgate, wup, wdown):
    bsz, seq, _ = x.shape
    n_tiles = seq // TM_FFN
    const = lambda b, i: (0, 0)

    def lat(width):
        return pl.BlockSpec((1, TM_FFN, width), lambda b, i: (b, i, 0))

    weights = (wout, wgate, wup, wdown)
    in_hbm = pl.BlockSpec(memory_space=pl.ANY)
    vec = lambda w: pl.BlockSpec((1, w), const)
    return pl.pallas_call(
        _merge_ffn_kernel,
        grid=(bsz, n_tiles),
        in_specs=[pl.BlockSpec((1, TM_FFN, D_MODEL), lambda b, i: (b, i, 0)),
                  lat(GLA_INNER), lat(GLA_INNER), lat(SSD_INNER), lat(SSD_INNER),
                  lat(SSD_INNER), lat(GLA_INNER + SSD_INNER),
                  pl.BlockSpec(m_all.shape, const),
                  vec(GLA_INNER), vec(SSD_INNER), vec(SSD_INNER),
                  vec(D_MODEL), vec(D_MODEL), vec(D_MODEL),
                  in_hbm, in_hbm, in_hbm, in_hbm],
        out_specs=pl.BlockSpec((1, TM_FFN, D_MODEL), lambda b, i: (b, i, 0)),
        out_shape=jax.ShapeDtypeStruct((bsz, seq, D_MODEL), F32),
        scratch_shapes=[pltpu.VMEM(w.shape, BF16) for w in weights]
        + [pltpu.VMEM((2, W_CHUNK, max(w.shape[1] for w in weights)), F32),
           pltpu.SemaphoreType.DMA((2,))],
        compiler_params=pltpu.CompilerParams(
            dimension_semantics=("arbitrary", "arbitrary"), vmem_limit_bytes=VMEM_LIMIT),
        name="merge_ffn",
    )(x, ygf, ygb, ysf, ysb, xbc, rz, m_all, gnorm, dskip, snorm, npost, nfpre, nfpost,
      wout, wgate, wup, wdown)


def kernel(x, c, ctx, c_ctx, w_mod, b_mod, norm_mix_pre, norm_mix_post, norm_ffn_pre, norm_ffn_post,
           w_in, conv_w, conv_b, gla_wg_f, gla_bg_f, gla_wg_b, gla_bg_b, gla_norm,
           a_log_f, a_log_b, dt_bias_f, dt_bias_b, d_skip, ssd_norm, w_out, w_gate, w_up, w_down):
    assert w_mod.shape[0] == 1, "single-layer kernel"
    bsz = x.shape[0]
    row = lambda v: v.reshape(1, -1)

    c_all = jnp.concatenate([c, c_ctx[None], jnp.zeros((MOD_ROWS - bsz - 1, D_MODEL), F32)], axis=0)
    wg_pad = jnp.zeros((TAIL_W, 2 * GLA_QK), F32)
    wg_pad = wg_pad.at[0:GLA_RANK, 0:GLA_QK].set(gla_wg_f[0])
    wg_pad = wg_pad.at[GLA_RANK:2 * GLA_RANK, GLA_QK:].set(gla_wg_b[0]).astype(BF16)
    bg_cat = jnp.concatenate([gla_bg_f[0], gla_bg_b[0]]).reshape(1, -1)
    lane_pad = lambda f, bk: jnp.zeros((1, TAIL_W), F32).at[0, DT_LANE:DT_LANE + 2 * SSD_HEADS].set(
        jnp.concatenate([f, bk]))
    dtb_pad = lane_pad(dt_bias_f[0], dt_bias_b[0])
    nega = lane_pad(-jnp.exp(a_log_f[0]) * LOG2_E, -jnp.exp(a_log_b[0]) * LOG2_E)
    conv_w9 = conv_w[0].reshape(CONV_K * CONV_K, SSD_CONV_DIM)

    m_all = _modulation(c_all, w_mod[0], row(b_mod[0]))
    qkv, rz, xbc, g, dt = _input_projection_conv(
        x, ctx, m_all, row(norm_mix_pre[0]), w_in[0].T, wg_pad, bg_cat, dtb_pad, conv_w9, row(conv_b[0]))
    ygf, ygb, ysf, ysb = _mixer_scan(qkv, g, xbc, dt, nega)
    return _merge_ffn(
        x, ygf, ygb, ysf, ysb, xbc, rz, m_all,
        row(jnp.tile(gla_norm[0], GLA_HEADS)), row(jnp.repeat(d_skip[0], SSD_P)), row(ssd_norm[0]),
        row(norm_mix_post[0]), row(norm_ffn_pre[0]), row(norm_ffn_post[0]),
        w_out[0], w_gate[0], w_up[0], w_down[0])
```

```python
import jax
import jax.numpy as jnp
from jax import lax
from jax.experimental import pallas as pl
from jax.experimental.pallas import tpu as pltpu

F32 = jnp.float32
BF16 = jnp.bfloat16

D_MODEL = 1024
CTX_LEN = 256
GRID_W = 64
EPS = 1e-6

GLA_HEADS = 4
GLA_DK = 64
GLA_DV = 128
GLA_QK = GLA_HEADS * GLA_DK
GLA_INNER = GLA_HEADS * GLA_DV
GLA_RANK = 16
GLA_GATE_NORM = 16.0

SSD_HEADS = 8
SSD_P = 64
SSD_N = 64
SSD_GROUPS = 2
SSD_HPG = SSD_HEADS // SSD_GROUPS
SSD_INNER = SSD_HEADS * SSD_P
SSD_BC = SSD_GROUPS * SSD_N
SSD_CONV_DIM = SSD_INNER + 2 * SSD_BC
CONV_K = 3

QKV_W = 2 * GLA_QK + GLA_INNER
COL_R = QKV_W
COL_Z = COL_R + GLA_INNER
COL_XBC = COL_Z + SSD_INNER
COL_TAIL = COL_XBC + SSD_CONV_DIM
TAIL_W = 128
PROJ_W = COL_TAIL + TAIL_W
DT_LANE = 2 * GLA_RANK
W_IN_LR = COL_Z
W_IN_Z = W_IN_LR + 2 * GLA_RANK
W_IN_DT = W_IN_Z + SSD_INNER + SSD_CONV_DIM
W_IN_COLS = W_IN_DT + 2 * SSD_HEADS

TM_PROJ = 512
PROJ_SUB = 256
CONV_COLS = 128
MOD_ROWS = 8
MOD_TN = 1024
TM_FFN = 512
FFN_SUB = 128
W_CHUNK = 256
CHUNK = 128
SCAN_BATCH = 4
SUB = 32
NEG_BIG = -1e30
LOG2_E = 1.4426950408889634
VMEM_LIMIT = 56 * 1024 * 1024


def _silu(x):
    return x / (1.0 + jnp.exp2(x * -LOG2_E))


def _softplus(x):
    return jnp.maximum(x, 0.0) + jnp.log(1.0 + jnp.exp(-jnp.abs(x)))


def _rms(x):
    return x * lax.rsqrt(jnp.mean(x * x, axis=-1, keepdims=True) + EPS)


def _dot(a, b):
    return jnp.dot(a, b, preferred_element_type=F32)


def _dot_nt(a, b):
    return lax.dot_general(a, b, (((1,), (1,)), ((), ())), preferred_element_type=F32)


def _dot_tn(a, b):
    return lax.dot_general(a, b, (((0,), (0,)), ((), ())), preferred_element_type=F32)


def _mod_kernel(c_ref, w_ref, b_ref, o_ref):
    s = _silu(c_ref[...])
    o_ref[...] = _dot(s.astype(BF16), w_ref[...].astype(BF16)) + b_ref[...]


def _modulation(c_all, w_mod, b_mod):
    n = w_mod.shape[1]
    return pl.pallas_call(
        _mod_kernel,
        grid=(n // MOD_TN,),
        in_specs=[pl.BlockSpec((MOD_ROWS, D_MODEL), lambda j: (0, 0)),
                  pl.BlockSpec((D_MODEL, MOD_TN), lambda j: (0, j)),
                  pl.BlockSpec((1, MOD_TN), lambda j: (0, j))],
        out_specs=pl.BlockSpec((MOD_ROWS, MOD_TN), lambda j: (0, j)),
        out_shape=jax.ShapeDtypeStruct((MOD_ROWS, n), F32),
        name="adaln_mod",
    )(c_all, w_mod, b_mod)


def _permute_cast_w_in(wt_ref, w_s):
    blk = 128
    pad = jnp.zeros((TAIL_W - (W_IN_Z - W_IN_LR) - (W_IN_COLS - W_IN_DT), blk), F32)

    def body(kb, carry):
        k0 = pl.multiple_of(kb * blk, blk)
        cols = pl.ds(k0, blk)
        rows = pl.ds(k0, blk)
        w_s[rows, 0:W_IN_LR] = jnp.transpose(wt_ref[0:W_IN_LR, cols]).astype(BF16)
        w_s[rows, COL_Z:COL_TAIL] = jnp.transpose(wt_ref[W_IN_Z:W_IN_DT, cols]).astype(BF16)
        tail = jnp.concatenate([wt_ref[W_IN_LR:W_IN_Z, cols], wt_ref[W_IN_DT:W_IN_COLS, cols], pad],
                               axis=0)
        w_s[rows, COL_TAIL:PROJ_W] = jnp.transpose(tail).astype(BF16)
        return carry

    lax.fori_loop(0, D_MODEL // blk, body, 0)


def _grid_conv_rows(window, t0, is_ctx, w_ref, b_ref, cols):
    n = window.shape[0] - 2 * GRID_W
    ext = window.astype(F32)
    row_on = jnp.where(is_ctx, 0.0, 1.0)
    taps = []
    for dx in range(CONV_K):
        tap = None
        for dy in range(CONV_K):
            wk = w_ref[CONV_K * dy + dx:CONV_K * dy + dx + 1, cols]
            if dy != 1:
                wk = wk * row_on
            term = wk * ext[GRID_W * dy:GRID_W * dy + n, :]
            tap = term if tap is None else tap + term
        taps.append(tap)
    t = t0 + lax.broadcasted_iota(jnp.int32, (n, 1), 0)
    col = jnp.bitwise_and(t, GRID_W - 1)
    has_left = jnp.where(is_ctx, t, col) != 0
    has_right = jnp.where(is_ctx, t - (CTX_LEN - 1), col - (GRID_W - 1)) != 0
    acc = (taps[1] + b_ref[:, cols]
           + jnp.where(has_left, pltpu.roll(taps[0], 1, axis=0), 0.0)
           + jnp.where(has_right, pltpu.roll(taps[2], n - 1, axis=0), 0.0))
    return _silu(acc)


def _inproj_conv_kernel(x_ref, ctx_ref, m_ref, gpre_ref, w_ref, wg_ref, bg_ref, dtb_ref,
                        cw_ref, cb_ref, qkv_ref, rz_ref, xbc_ref, g_ref, dt_ref,
                        w_s, prev_s, above_s):
    b = pl.program_id(0)
    i = pl.program_id(1)
    n_tiles = pl.num_programs(1) - 1
    ctx_tile = n_tiles - 1

    @pl.when(jnp.logical_and(b == 0, i == 0))
    def _():
        _permute_cast_w_in(w_ref, w_s)

    @pl.when(i == 0)
    def _():
        prev_s[...] = jnp.zeros_like(prev_s)
        above_s[...] = jnp.zeros_like(above_s)

    j = i - 1
    conv_ctx = j == ctx_tile
    split = TM_PROJ - GRID_W
    none = jnp.zeros((GRID_W, SSD_CONV_DIM), BF16)

    col_groups = [slice(c, c + CONV_COLS) for c in range(0, SSD_CONV_DIM, CONV_COLS)]

    def conv_block(t0, n, cols, window):
        out = _grid_conv_rows(window, t0, conv_ctx, cw_ref, cb_ref, cols)
        xbc_ref[0, t0:t0 + n, cols] = out.astype(BF16)

    def conv_previous_head():
        has_above = jnp.logical_and(j >= 1, j < ctx_tile)
        for cols in col_groups:
            above = jnp.where(has_above, above_s[:, cols], none[:, cols])
            conv_block(0, CTX_LEN, cols,
                       jnp.concatenate([above, prev_s[0:CTX_LEN + GRID_W, cols]], axis=0))
            conv_block(CTX_LEN, split - CTX_LEN, cols, prev_s[CTX_LEN - GRID_W:, cols])
        above_s[...] = prev_s[split:, :]

    def conv_previous_last_row(below):
        below = jnp.where(j < ctx_tile - 1, below, none)
        for cols in col_groups:
            conv_block(split, GRID_W, cols,
                       jnp.concatenate([prev_s[split - GRID_W:, cols], below[:, cols]], axis=0))

    @pl.when(i < n_tiles)
    def _():
        is_ctx = i == ctx_tile
        mrow = m_ref[pl.ds(jnp.where(is_ctx, pl.num_programs(0), b), 1), :]
        shift = mrow[:, 0:D_MODEL]
        gain = gpre_ref[...] * (1.0 + mrow[:, D_MODEL:2 * D_MODEL])

        def normed(rs):
            ctx_rows = slice(rs.start % CTX_LEN, rs.start % CTX_LEN + PROJ_SUB)
            x = jnp.where(is_ctx, ctx_ref[0, ctx_rows, :], x_ref[0, rs, :])
            return (_rms(x) * gain + shift).astype(BF16)

        def project(rs, h):
            tail = _dot(h, w_s[:, COL_TAIL:PROJ_W])
            qkv = _dot(h, w_s[:, 0:QKV_W])
            qkv_ref[0, rs, 0:GLA_QK] = (qkv[:, 0:GLA_QK] * (GLA_DK ** -0.5)).astype(BF16)
            qkv_ref[0, rs, GLA_QK:QKV_W] = qkv[:, GLA_QK:QKV_W].astype(BF16)
            rz_ref[0, rs, :] = _silu(_dot(h, w_s[:, COL_R:COL_XBC])).astype(BF16)
            xbc_new = _dot(h, w_s[:, COL_XBC:COL_TAIL]).astype(BF16)
            logits = _dot(tail.astype(BF16), wg_ref[...]) + bg_ref[...]
            u = logits * LOG2_E
            g_ref[0, rs, :] = ((jnp.minimum(u, 0.0) - jnp.log2(1.0 + jnp.exp2(-jnp.abs(u))))
                               * (1.0 / GLA_GATE_NORM))
            dt_ref[0, rs, :] = _softplus(tail + dtb_ref[...])
            return xbc_new

        subs = [slice(PROJ_SUB * s, PROJ_SUB * (s + 1)) for s in range(TM_PROJ // PROJ_SUB)]
        conv_previous_head()
        ready = normed(subs[0])
        fresh = []
        for s, rs in enumerate(subs):
            following = normed(subs[s + 1]) if s + 1 < len(subs) else None
            fresh.append(project(rs, ready))
            if s == 0:
                conv_previous_last_row(fresh[0][0:GRID_W, :])
            ready = following
        for rs, xbc_new in zip(subs, fresh):
            prev_s[rs, :] = xbc_new

    @pl.when(i == n_tiles)
    def _():
        conv_previous_head()
        conv_previous_last_row(none)


def _input_projection_conv(x, ctx, m_all, gpre, w_in, wg_pad, bg_cat, dtb_pad, conv_w9, conv_b):
    bsz, seq, _ = x.shape
    n_lat = seq // TM_PROJ
    n_tiles = n_lat + 1
    tt = CTX_LEN + seq
    const = lambda b, i: (0, 0)
    tok = lambda b, i: (b, jnp.minimum(i, n_tiles - 1), 0)
    conv_tok = lambda b, i: (b, jnp.maximum(i - 1, 0), 0)
    return pl.pallas_call(
        _inproj_conv_kernel,
        grid=(bsz, n_tiles + 1),
        in_specs=[pl.BlockSpec((1, TM_PROJ, D_MODEL), lambda b, i: (b, jnp.minimum(i, n_lat - 1), 0)),
                  pl.BlockSpec((1, CTX_LEN, D_MODEL), lambda b, i: (b, 0, 0)),
                  pl.BlockSpec(m_all.shape, const),
                  pl.BlockSpec((1, D_MODEL), const),
                  pl.BlockSpec((W_IN_COLS, D_MODEL), const, pipeline_mode=pl.Buffered(1)),
                  pl.BlockSpec((TAIL_W, 2 * GLA_QK), const),
                  pl.BlockSpec((1, 2 * GLA_QK), const),
                  pl.BlockSpec((1, TAIL_W), const),
                  pl.BlockSpec((CONV_K * CONV_K, SSD_CONV_DIM), const),
                  pl.BlockSpec((1, SSD_CONV_DIM), const)],
        out_specs=[pl.BlockSpec((1, TM_PROJ, QKV_W), tok),
                   pl.BlockSpec((1, TM_PROJ, GLA_INNER + SSD_INNER), tok),
                   pl.BlockSpec((1, TM_PROJ, SSD_CONV_DIM), conv_tok),
                   pl.BlockSpec((1, TM_PROJ, 2 * GLA_QK), tok),
                   pl.BlockSpec((1, TM_PROJ, TAIL_W), tok)],
        out_shape=[jax.ShapeDtypeStruct((bsz, tt, QKV_W), BF16),
                   jax.ShapeDtypeStruct((bsz, tt, GLA_INNER + SSD_INNER), BF16),
                   jax.ShapeDtypeStruct((bsz, tt, SSD_CONV_DIM), BF16),
                   jax.ShapeDtypeStruct((bsz, tt, 2 * GLA_QK), F32),
                   jax.ShapeDtypeStruct((bsz, tt, TAIL_W), F32)],
        scratch_shapes=[pltpu.VMEM((D_MODEL, PROJ_W), BF16),
                        pltpu.VMEM((TM_PROJ, SSD_CONV_DIM), BF16),
                        pltpu.VMEM((GRID_W, SSD_CONV_DIM), BF16)],
        compiler_params=pltpu.CompilerParams(
            dimension_semantics=("arbitrary", "arbitrary"), vmem_limit_bytes=VMEM_LIMIT),
        name="in_proj_conv",
    )(x, ctx, m_all, gpre, w_in, wg_pad, bg_cat, dtb_pad, conv_w9, conv_b)


def _cumsum_rows(x, reverse):
    n = x.shape[0]
    row = lax.broadcasted_iota(jnp.int32, x.shape, 0)
    s = 1
    while s < n:
        if reverse:
            x = x + jnp.where(row < n - s, pltpu.roll(x, n - s, axis=0), 0.0)
        else:
            x = x + jnp.where(row >= s, pltpu.roll(x, s, axis=0), 0.0)
        s *= 2
    return x


def _chunk_maps(n_ctx, n_all):
    n_lat = n_all - n_ctx
    fwd = lambda s: jnp.where(s < n_ctx, n_lat + s, s - n_ctx)
    bwd = lambda s: n_all - 1 - s
    return fwd, bwd


def _gla_prepare(qkv, g, reverse):
    n = qkv.shape[0]
    nb = n // SUB
    blocks = range(nb)
    rows = lambda a, blk: a[SUB * blk:SUB * (blk + 1), :]
    q = qkv[:, 0:GLA_QK].astype(F32)
    k = qkv[:, GLA_QK:2 * GLA_QK].astype(F32)
    cum = _cumsum_rows(g, reverse)
    zero = jnp.zeros((1, GLA_QK), F32)
    if reverse:
        ends = [cum[SUB * blk:SUB * blk + 1, :] for blk in blocks]
        refs = ends[1:] + [zero]
        earlier = lambda j, blk: j > blk
        tot = ends[0]
    else:
        ends = [cum[SUB * (blk + 1) - 1:SUB * (blk + 1), :] for blk in blocks]
        refs = [zero] + ends[:-1]
        earlier = lambda j, blk: j < blk
        tot = ends[nb - 1]
    stack = lambda vs: jnp.concatenate([jnp.broadcast_to(v, (SUB, GLA_QK)) for v in vs], axis=0)
    e_in = cum - stack(refs)
    q_blk = q * jnp.exp2(e_in)
    k_diag = k * jnp.exp2(-e_in)
    k_end = k * jnp.exp2(stack(ends) - cum)
    q_in = jnp.concatenate([rows(q_blk, blk) * jnp.exp2(refs[blk]) for blk in blocks], axis=0)
    k_out = jnp.concatenate([rows(k_end, blk) * jnp.exp2(tot - ends[blk]) for blk in blocks], axis=0)
    q_in = q_in.astype(BF16)
    k_out = k_out.astype(BF16)
    q_bf = q_blk.astype(BF16)
    zeros_piece = jnp.zeros((SUB, GLA_QK), BF16)
    k_piece = [[rows(k_diag, j).astype(BF16) if seg == j
                else (rows(k_end, j) * jnp.exp2(refs[seg] - ends[j])).astype(BF16) if earlier(j, seg)
                else zeros_piece
                for seg in blocks] for j in blocks]
    ii = lax.broadcasted_iota(jnp.int32, (n, n), 0)
    jj = lax.broadcasted_iota(jnp.int32, (n, n), 1)
    heads = []
    for h in range(GLA_HEADS):
        ks = slice(GLA_DK * h, GLA_DK * (h + 1))
        q_cat = jnp.concatenate(
            [jnp.concatenate([rows(q_bf, blk)[:, ks] if seg == blk else zeros_piece[:, ks]
                              for seg in blocks], axis=1) for blk in blocks], axis=0)
        k_cat = jnp.concatenate(
            [jnp.concatenate([k_piece[j][seg][:, ks] for seg in blocks], axis=1)
             for j in blocks], axis=0)
        heads.append(dict(q_cat=q_cat, k_cat=k_cat, q_in=q_in[:, ks], k_out=k_out[:, ks],
                          v=qkv[:, 2 * GLA_QK + GLA_DV * h:2 * GLA_QK + GLA_DV * (h + 1)]))
    return dict(heads=heads, causal=(ii <= jj) if reverse else (ii >= jj),
                decay_col=jnp.transpose(jnp.broadcast_to(tot, (GLA_DV, GLA_QK))))


def _gla_chain(qkv_ref, g_ref, reverse, s_ref, y_ref):
    def start():
        p = _gla_prepare(qkv_ref[...], g_ref[...], reverse)
        return p, [_dot_nt(hd["q_cat"], hd["k_cat"]) for hd in p["heads"]]

    def finish(started):
        p, att = started
        outs = []
        for h, (hd, a) in enumerate(zip(p["heads"], att)):
            state = s_ref[h]
            lhs = jnp.concatenate([jnp.where(p["causal"], a, 0.0).astype(BF16), hd["q_in"]], axis=1)
            rhs = jnp.concatenate([hd["v"], state.astype(BF16)], axis=0)
            outs.append(_dot(lhs, rhs))
            ks = slice(GLA_DK * h, GLA_DK * (h + 1))
            s_ref[h] = state * jnp.exp2(p["decay_col"][ks, :]) + _dot_tn(hd["k_out"], hd["v"])
        y_ref[...] = jnp.concatenate(outs, axis=1).astype(y_ref.dtype)

    return start, finish


def _ssd_prepare(xbc, dt, nega, expand, lane0, reverse):
    n = xbc.shape[0]
    end = 0 if reverse else n - 1
    cum = _cumsum_rows(dt * nega, reverse)
    e_cum = jnp.exp2(cum)
    w = jnp.exp2(cum[end:end + 1, :] - cum) * dt
    e_cum_x = _dot(_split_hi_lo(e_cum), expand)
    x_w = xbc[:, 0:SSD_INNER] * _dot(w.astype(BF16), expand[0:TAIL_W, :]).astype(BF16)
    cum_t = jnp.transpose(cum)
    dt_t = jnp.transpose(dt).astype(BF16)
    ii = lax.broadcasted_iota(jnp.int32, (n, n), 0)
    jj = lax.broadcasted_iota(jnp.int32, (n, n), 1)
    causal = (ii <= jj) if reverse else (ii >= jj)
    first_head = lax.broadcasted_iota(jnp.int32, (n, 2 * SSD_P), 1) < SSD_P
    groups = []
    for grp in range(SSD_GROUPS):
        b_g = xbc[:, SSD_INNER + SSD_N * grp:SSD_INNER + SSD_N * (grp + 1)]
        c_g = xbc[:, SSD_INNER + SSD_BC + SSD_N * grp:SSD_INNER + SSD_BC + SSD_N * (grp + 1)]
        scores = _dot_nt(c_g, b_g).astype(BF16)
        pairs = []
        for pair in range(SSD_HPG // 2):
            m_pair = []
            for hh in range(2):
                lane = lane0 + SSD_HPG * grp + 2 * pair + hh
                cum_b = jnp.broadcast_to(cum[:, lane:lane + 1], (n, n))
                seg = jnp.where(causal, cum_b - cum_t[lane:lane + 1, :], NEG_BIG)
                m_pair.append(scores * jnp.exp2(seg).astype(BF16) * dt_t[lane:lane + 1, :])
            col = 2 * SSD_P * (SSD_HPG // 2 * grp + pair)
            x_pair = xbc[:, col:col + 2 * SSD_P]
            none = jnp.zeros_like(x_pair)
            rhs = jnp.concatenate([jnp.where(first_head, x_pair, none),
                                   jnp.where(first_head, none, x_pair)], axis=0)
            pairs.append((jnp.concatenate(m_pair, axis=1), rhs))
        cols = slice(SSD_HPG * SSD_P * grp, SSD_HPG * SSD_P * (grp + 1))
        groups.append(dict(b=b_g, c=c_g, pairs=pairs, e_cum=e_cum_x[:, cols],
                           e_tot=e_cum_x[end:end + 1, cols], x_w=x_w[:, cols]))
    return groups


def _split_hi_lo(a):
    hi = a.astype(BF16)
    lo = (a - hi.astype(F32)).astype(BF16)
    return jnp.concatenate([hi, lo], axis=1)


def _ssd_chain(xbc_ref, dt_ref, nega, expand_ref, d, s_ref, y_ref):
    def start():
        return _ssd_prepare(xbc_ref[...], dt_ref[...], nega, expand_ref[d],
                            DT_LANE + SSD_HEADS * d, d == 1)

    def finish(groups):
        outs = []
        for grp, gd in enumerate(groups):
            state = s_ref[grp]
            intra = jnp.concatenate([_dot(m, rhs) for m, rhs in gd["pairs"]], axis=1)
            outs.append(_dot(gd["c"], state.astype(BF16)) * gd["e_cum"] + intra)
            s_ref[grp] = state * gd["e_tot"] + _dot_tn(gd["b"], gd["x_w"])
        y_ref[...] = jnp.concatenate(outs, axis=1).astype(y_ref.dtype)

    return start, finish


def _mixer_scan_kernel(qkv_f_ref, g_f_ref, qkv_b_ref, g_b_ref, xbc_f_ref, dt_f_ref, xbc_b_ref,
                       dt_b_ref, nega_ref, expand_ref, ygf_ref, ygb_ref, ysf_ref, ysb_ref,
                       gsf_ref, gsb_ref, ssf_ref, ssb_ref):
    @pl.when(pl.program_id(1) == 0)
    def _():
        for s_ref in (gsf_ref, gsb_ref, ssf_ref, ssb_ref):
            s_ref[...] = jnp.zeros_like(s_ref)

    nega = nega_ref[...]
    chains = []
    for e in range(qkv_f_ref.shape[0]):
        chains.append(_gla_chain(qkv_f_ref.at[e], g_f_ref.at[e], False, gsf_ref.at[e], ygf_ref.at[e]))
        chains.append(_ssd_chain(xbc_f_ref.at[e], dt_f_ref.at[e], nega, expand_ref, 0,
                                 ssf_ref.at[e], ysf_ref.at[e]))
        chains.append(_gla_chain(qkv_b_ref.at[e], g_b_ref.at[e], True, gsb_ref.at[e], ygb_ref.at[e]))
        chains.append(_ssd_chain(xbc_b_ref.at[e], dt_b_ref.at[e], nega, expand_ref, 1,
                                 ssb_ref.at[e], ysb_ref.at[e]))
    started = chains[0][0]()
    for i, (_, finish) in enumerate(chains):
        following = chains[i + 1][0]() if i + 1 < len(chains) else None
        finish(started)
        started = following


def _head_expand_matrix():
    r = jnp.arange(2 * TAIL_W)[:, None] % TAIL_W
    c = jnp.arange(SSD_INNER)[None, :] // SSD_P
    return jnp.stack([(r == DT_LANE + SSD_HEADS * d + c) for d in range(2)]).astype(BF16)


def _mixer_scan(qkv, g, xbc, dt, nega):
    bsz, tt, _ = qkv.shape
    n_all = tt // CHUNK
    fwd, bwd = _chunk_maps(CTX_LEN // CHUNK, n_all)
    spec = lambda w, m, c=0: pl.BlockSpec((SCAN_BATCH, CHUNK, w), lambda b, s: (b, m(s), c))
    expand = _head_expand_matrix()
    return pl.pallas_call(
        _mixer_scan_kernel,
        grid=(bsz // SCAN_BATCH, n_all),
        in_specs=[spec(QKV_W, fwd), spec(GLA_QK, fwd), spec(QKV_W, bwd), spec(GLA_QK, bwd, 1),
                  spec(SSD_CONV_DIM, fwd), spec(TAIL_W, fwd), spec(SSD_CONV_DIM, bwd), spec(TAIL_W, bwd),
                  pl.BlockSpec((1, TAIL_W), lambda b, s: (0, 0)),
                  pl.BlockSpec(expand.shape, lambda b, s: (0, 0, 0))],
        out_specs=[spec(GLA_INNER, fwd), spec(GLA_INNER, bwd), spec(SSD_INNER, fwd), spec(SSD_INNER, bwd)],
        out_shape=[jax.ShapeDtypeStruct((bsz, tt, GLA_INNER), BF16)] * 2
        + [jax.ShapeDtypeStruct((bsz, tt, SSD_INNER), BF16)] * 2,
        scratch_shapes=[pltpu.VMEM((SCAN_BATCH, GLA_HEADS, GLA_DK, GLA_DV), F32)] * 2
        + [pltpu.VMEM((SCAN_BATCH, SSD_GROUPS, SSD_N, SSD_HPG * SSD_P), F32)] * 2,
        compiler_params=pltpu.CompilerParams(
            dimension_semantics=("parallel", "arbitrary"), vmem_limit_bytes=VMEM_LIMIT),
        name="mixer_scan",
    )(qkv, g, qkv, g, xbc, dt, xbc, dt, nega, expand)


def _stream_cast_weights(pairs, stage, sem):
    def chunk_copy(w_hbm, c, slot):
        cols = w_hbm.shape[1]
        return pltpu.make_async_copy(w_hbm.at[pl.ds(c * W_CHUNK, W_CHUNK), :],
                                     stage.at[slot, :, 0:cols], sem.at[slot])

    chunk_copy(pairs[0][0], 0, 0).start()
    first = 0
    for k, (w_hbm, w_bf) in enumerate(pairs):
        n_chunks, rest = divmod(w_hbm.shape[0], W_CHUNK)
        assert rest == 0
        cols = w_hbm.shape[1]
        following = pairs[k + 1][0] if k + 1 < len(pairs) else None

        def body(c, carry, w_hbm=w_hbm, w_bf=w_bf, n_chunks=n_chunks, cols=cols,
                 following=following, first=first):
            slot = jnp.bitwise_and(first + c, 1)

            @pl.when(c + 1 < n_chunks)
            def _():
                chunk_copy(w_hbm, c + 1, 1 - slot).start()

            if following is not None:
                @pl.when(c + 1 == n_chunks)
                def _():
                    chunk_copy(following, 0, 1 - slot).start()

            chunk_copy(w_hbm, c, slot).wait()
            rows = pl.ds(pl.multiple_of(c * W_CHUNK, W_CHUNK), W_CHUNK)
            w_bf[rows, :] = stage[slot, :, 0:cols].astype(BF16)
            return carry

        lax.fori_loop(0, n_chunks, body, 0)
        first += n_chunks


def _merge_ffn_kernel(x_ref, ygf_ref, ygb_ref, ysf_ref, ysb_ref, xs_ref, rz_ref, m_ref,
                      gnorm_ref, dskip_ref, snorm_ref, npost_ref, nfpre_ref, nfpost_ref,
                      wout_hbm, wgate_hbm, wup_hbm, wdown_hbm, o_ref,
                      wout_ref, wgate_ref, wup_ref, wdown_ref, stage, sem):
    b = pl.program_id(0)

    @pl.when(jnp.logical_and(b == 0, pl.program_id(1) == 0))
    def _():
        _stream_cast_weights([(wout_hbm, wout_ref), (wgate_hbm, wgate_ref),
                              (wup_hbm, wup_ref), (wdown_hbm, wdown_ref)], stage, sem)

    mrow = m_ref[pl.ds(b, 1), :]
    mod = [mrow[:, D_MODEL * j:D_MODEL * (j + 1)] for j in range(6)]
    gw = SSD_INNER // SSD_GROUPS

    def merged(rs):
        y_gla = ygf_ref[0, rs, :].astype(F32) + ygb_ref[0, rs, :].astype(F32)
        o_gla = jnp.concatenate(
            [_rms(y_gla[:, GLA_DV * h:GLA_DV * (h + 1)]) for h in range(GLA_HEADS)], axis=1)
        o_gla = o_gla * gnorm_ref[...] * rz_ref[0, rs, 0:GLA_INNER].astype(F32)
        y_ssd = ysf_ref[0, rs, :].astype(F32) + ysb_ref[0, rs, :].astype(F32)
        y_ssd = ((y_ssd + dskip_ref[...] * xs_ref[0, rs, :].astype(F32))
                 * rz_ref[0, rs, GLA_INNER:].astype(F32))
        o_ssd = jnp.concatenate(
            [_rms(y_ssd[:, gw * g:gw * (g + 1)]) for g in range(SSD_GROUPS)], axis=1)
        o_ssd = o_ssd * snorm_ref[...]
        return jnp.concatenate([o_gla, o_ssd], axis=1).astype(BF16)

    def residual_in(rs, o):
        y = _dot(o, wout_ref[...])
        x1 = x_ref[0, rs, :] + mod[2] * (_rms(y) * npost_ref[...])
        h = ((_rms(x1) * nfpre_ref[...]) * (1.0 + mod[4]) + mod[3]).astype(BF16)
        return x1, h

    subs = [slice(FFN_SUB * j, FFN_SUB * (j + 1)) for j in range(TM_FFN // FFN_SUB)]
    x1, h = residual_in(subs[0], merged(subs[0]))
    for j, rs in enumerate(subs):
        following = residual_in(subs[j + 1], merged(subs[j + 1])) if j + 1 < len(subs) else None
        act = (_silu(_dot(h, wgate_ref[...])) * _dot(h, wup_ref[...])).astype(BF16)
        f = _dot(act, wdown_ref[...])
        o_ref[0, rs, :] = x1 + mod[5] * (_rms(f) * nfpost_ref[...])
        if following is not None:
            x1, h = following


def _merge_ffn(x, ygf, ygb, ysf, ysb, xbc, rz, m_all, gnorm, dskip, snorm, npost, nfpre,
               nfpost, wout, wgate, wup, wdown):
    bsz, seq, _ = x.shape
    n_tiles = seq // TM_FFN
    const = lambda b, i: (0, 0)

    def lat(width):
        return pl.BlockSpec((1, TM_FFN, width), lambda b, i: (b, i, 0))

    weights = (wout, wgate, wup, wdown)
    in_hbm = pl.BlockSpec(memory_space=pl.ANY)
    vec = lambda w: pl.BlockSpec((1, w), const)
    return pl.pallas_call(
        _merge_ffn_kernel,
        grid=(bsz, n_tiles),
        in_specs=[pl.BlockSpec((1, TM_FFN, D_MODEL), lambda b, i: (b, i, 0)),
                  lat(GLA_INNER), lat(GLA_INNER), lat(SSD_INNER), lat(SSD_INNER),
                  lat(SSD_INNER), lat(GLA_INNER + SSD_INNER),
                  pl.BlockSpec(m_all.shape, const),
                  vec(GLA_INNER), vec(SSD_INNER), vec(SSD_INNER),
                  vec(D_MODEL), vec(D_MODEL), vec(D_MODEL),
                  in_hbm, in_hbm, in_hbm, in_hbm],
        out_specs=pl.BlockSpec((1, TM_FFN, D_MODEL), lambda b, i: (b, i, 0)),
        out_shape=jax.ShapeDtypeStruct((bsz, seq, D_MODEL), F32),
        scratch_shapes=[pltpu.VMEM(w.shape, BF16) for w in weights]
        + [pltpu.VMEM((2, W_CHUNK, max(w.shape[1] for w in weights)), F32),
           pltpu.SemaphoreType.DMA((2,))],
        compiler_params=pltpu.CompilerParams(
            dimension_semantics=("arbitrary", "arbitrary"), vmem_limit_bytes=VMEM_LIMIT),
        name="merge_ffn",
    )(x, ygf, ygb, ysf, ysb, xbc, rz, m_all, gnorm, dskip, snorm, npost, nfpre, nfpost,
      wout, wgate, wup, wdown)


def kernel(x, c, ctx, c_ctx, w_mod, b_mod, norm_mix_pre, norm_mix_post, norm_ffn_pre, norm_ffn_post,
           w_in, conv_w, conv_b, gla_wg_f, gla_bg_f, gla_wg_b, gla_bg_b, gla_norm,
           a_log_f, a_log_b, dt_bias_f, dt_bias_b, d_skip, ssd_norm, w_out, w_gate, w_up, w_down):
    assert w_mod.shape[0] == 1, "single-layer kernel"
    bsz = x.shape[0]
    row = lambda v: v.reshape(1, -1)

    c_all = jnp.concatenate([c, c_ctx[None], jnp.zeros((MOD_ROWS - bsz - 1, D_MODEL), F32)], axis=0)
    wg_pad = jnp.zeros((TAIL_W, 2 * GLA_QK), F32)
    wg_pad = wg_pad.at[0:GLA_RANK, 0:GLA_QK].set(gla_wg_f[0])
    wg_pad = wg_pad.at[GLA_RANK:2 * GLA_RANK, GLA_QK:].set(gla_wg_b[0]).astype(BF16)
    bg_cat = jnp.concatenate([gla_bg_f[0], gla_bg_b[0]]).reshape(1, -1)
    lane_pad = lambda f, bk: jnp.zeros((1, TAIL_W), F32).at[0, DT_LANE:DT_LANE + 2 * SSD_HEADS].set(
        jnp.concatenate([f, bk]))
    dtb_pad = lane_pad(dt_bias_f[0], dt_bias_b[0])
    nega = lane_pad(-jnp.exp(a_log_f[0]) * LOG2_E, -jnp.exp(a_log_b[0]) * LOG2_E)
    conv_w9 = conv_w[0].reshape(CONV_K * CONV_K, SSD_CONV_DIM)

    m_all = _modulation(c_all, w_mod[0], row(b_mod[0]))
    qkv, rz, xbc, g, dt = _input_projection_conv(
        x, ctx, m_all, row(norm_mix_pre[0]), w_in[0].T, wg_pad, bg_cat, dtb_pad, conv_w9, row(conv_b[0]))
    ygf, ygb, ysf, ysb = _mixer_scan(qkv, g, xbc, dt, nega)
    return _merge_ffn(
        x, ygf, ygb, ysf, ysb, xbc, rz, m_all,
        row(jnp.tile(gla_norm[0], GLA_HEADS)), row(jnp.repeat(d_skip[0], SSD_P)), row(ssd_norm[0]),
        row(norm_mix_post[0]), row(norm_ffn_pre[0]), row(norm_ffn_post[0]),
        w_out[0], w_gate[0], w_up[0], w_down[0])
```

```python
import jax
import jax.numpy as jnp
from jax import lax
from jax.experimental import pallas as pl
from jax.experimental.pallas import tpu as pltpu

F32 = jnp.float32
BF16 = jnp.bfloat16

D_MODEL = 1024
CTX_LEN = 256
GRID_W = 64
EPS = 1e-6

GLA_HEADS = 4
GLA_DK = 64
GLA_DV = 128
GLA_QK = GLA_HEADS * GLA_DK
GLA_INNER = GLA_HEADS * GLA_DV
GLA_RANK = 16
GLA_GATE_NORM = 16.0

SSD_HEADS = 8
SSD_P = 64
SSD_N = 64
SSD_GROUPS = 2
SSD_HPG = SSD_HEADS // SSD_GROUPS
SSD_INNER = SSD_HEADS * SSD_P
SSD_BC = SSD_GROUPS * SSD_N
SSD_CONV_DIM = SSD_INNER + 2 * SSD_BC
CONV_K = 3

QKV_W = 2 * GLA_QK + GLA_INNER
COL_R = QKV_W
COL_Z = COL_R + GLA_INNER
COL_XBC = COL_Z + SSD_INNER
COL_TAIL = COL_XBC + SSD_CONV_DIM
TAIL_W = 128
PROJ_W = COL_TAIL + TAIL_W
DT_LANE = 2 * GLA_RANK
W_IN_LR = COL_Z
W_IN_Z = W_IN_LR + 2 * GLA_RANK
W_IN_DT = W_IN_Z + SSD_INNER + SSD_CONV_DIM
W_IN_COLS = W_IN_DT + 2 * SSD_HEADS

TM_PROJ = 512
PROJ_SUB = 256
CONV_COLS = 128
MOD_ROWS = 8
MOD_TN = 1024
TM_FFN = 512
FFN_SUB = 128
W_CHUNK = 256
CHUNK = 128
SCAN_BATCH = 4
SUB = 32
NEG_BIG = -1e30
LOG2_E = 1.4426950408889634
VMEM_LIMIT = 56 * 1024 * 1024


def _silu(x):
    return x / (1.0 + jnp.exp2(x * -LOG2_E))


def _softplus(x):
    return jnp.maximum(x, 0.0) + jnp.log(1.0 + jnp.exp(-jnp.abs(x)))


def _rms(x):
    return x * lax.rsqrt(jnp.mean(x * x, axis=-1, keepdims=True) + EPS)


def _dot(a, b):
    return jnp.dot(a, b, preferred_element_type=F32)


def _dot_nt(a, b):
    return lax.dot_general(a, b, (((1,), (1,)), ((), ())), preferred_element_type=F32)


def _dot_tn(a, b):
    return lax.dot_general(a, b, (((0,), (0,)), ((), ())), preferred_element_type=F32)


def _mod_kernel(c_ref, w_ref, b_ref, o_ref):
    s = _silu(c_ref[...])
    o_ref[...] = _dot(s.astype(BF16), w_ref[...].astype(BF16)) + b_ref[...]


def _modulation(c_all, w_mod, b_mod):
    n = w_mod.shape[1]
    return pl.pallas_call(
        _mod_kernel,
        grid=(n // MOD_TN,),
        in_specs=[pl.BlockSpec((MOD_ROWS, D_MODEL), lambda j: (0, 0)),
                  pl.BlockSpec((D_MODEL, MOD_TN), lambda j: (0, j)),
                  pl.BlockSpec((1, MOD_TN), lambda j: (0, j))],
        out_specs=pl.BlockSpec((MOD_ROWS, MOD_TN), lambda j: (0, j)),
        out_shape=jax.ShapeDtypeStruct((MOD_ROWS, n), F32),
        name="adaln_mod",
    )(c_all, w_mod, b_mod)


def _permute_cast_w_in(wt_ref, w_s):
    blk = 128
    pad = jnp.zeros((TAIL_W - (W_IN_Z - W_IN_LR) - (W_IN_COLS - W_IN_DT), blk), F32)

    def body(kb, carry):
        k0 = pl.multiple_of(kb * blk, blk)
        cols = pl.ds(k0, blk)
        rows = pl.ds(k0, blk)
        w_s[rows, 0:W_IN_LR] = jnp.transpose(wt_ref[0:W_IN_LR, cols]).astype(BF16)
        w_s[rows, COL_Z:COL_TAIL] = jnp.transpose(wt_ref[W_IN_Z:W_IN_DT, cols]).astype(BF16)
        tail = jnp.concatenate([wt_ref[W_IN_LR:W_IN_Z, cols], wt_ref[W_IN_DT:W_IN_COLS, cols], pad],
                               axis=0)
        w_s[rows, COL_TAIL:PROJ_W] = jnp.transpose(tail).astype(BF16)
        return carry

    lax.fori_loop(0, D_MODEL // blk, body, 0)


def _grid_conv_rows(window, t0, is_ctx, w_ref, b_ref, cols):
    n = window.shape[0] - 2 * GRID_W
    ext = window.astype(F32)
    row_on = jnp.where(is_ctx, 0.0, 1.0)
    taps = []
    for dx in range(CONV_K):
        tap = None
        for dy in range(CONV_K):
            wk = w_ref[CONV_K * dy + dx:CONV_K * dy + dx + 1, cols]
            if dy != 1:
                wk = wk * row_on
            term = wk * ext[GRID_W * dy:GRID_W * dy + n, :]
            tap = term if tap is None else tap + term
        taps.append(tap)
    t = t0 + lax.broadcasted_iota(jnp.int32, (n, 1), 0)
    col = jnp.bitwise_and(t, GRID_W - 1)
    has_left = jnp.where(is_ctx, t, col) != 0
    has_right = jnp.where(is_ctx, t - (CTX_LEN - 1), col - (GRID_W - 1)) != 0
    acc = (taps[1] + b_ref[:, cols]
           + jnp.where(has_left, pltpu.roll(taps[0], 1, axis=0), 0.0)
           + jnp.where(has_right, pltpu.roll(taps[2], n - 1, axis=0), 0.0))
    return _silu(acc)


def _inproj_conv_kernel(x_ref, ctx_ref, m_ref, gpre_ref, w_ref, wg_ref, bg_ref, dtb_ref,
                        cw_ref, cb_ref, qkv_ref, rz_ref, xbc_ref, g_ref, dt_ref,
                        w_s, prev_s, above_s):
    b = pl.program_id(0)
    i = pl.program_id(1)
    n_tiles = pl.num_programs(1) - 1
    ctx_tile = n_tiles - 1

    @pl.when(jnp.logical_and(b == 0, i == 0))
    def _():
        _permute_cast_w_in(w_ref, w_s)

    @pl.when(i == 0)
    def _():
        prev_s[...] = jnp.zeros_like(prev_s)
        above_s[...] = jnp.zeros_like(above_s)

    j = i - 1
    conv_ctx = j == ctx_tile
    split = TM_PROJ - GRID_W
    none = jnp.zeros((GRID_W, SSD_CONV_DIM), BF16)

    col_groups = [slice(c, c + CONV_COLS) for c in range(0, SSD_CONV_DIM, CONV_COLS)]

    def conv_block(t0, n, cols, window):
        out = _grid_conv_rows(window, t0, conv_ctx, cw_ref, cb_ref, cols)
        xbc_ref[0, t0:t0 + n, cols] = out.astype(BF16)

    def conv_previous_head():
        has_above = jnp.logical_and(j >= 1, j < ctx_tile)
        for cols in col_groups:
            above = jnp.where(has_above, above_s[:, cols], none[:, cols])
            conv_block(0, CTX_LEN, cols,
                       jnp.concatenate([above, prev_s[0:CTX_LEN + GRID_W, cols]], axis=0))
            conv_block(CTX_LEN, split - CTX_LEN, cols, prev_s[CTX_LEN - GRID_W:, cols])
        above_s[...] = prev_s[split:, :]

    def conv_previous_last_row(below):
        below = jnp.where(j < ctx_tile - 1, below, none)
        for cols in col_groups:
            conv_block(split, GRID_W, cols,
                       jnp.concatenate([prev_s[split - GRID_W:, cols], below[:, cols]], axis=0))

    @pl.when(i < n_tiles)
    def _():
        is_ctx = i == ctx_tile
        mrow = m_ref[pl.ds(jnp.where(is_ctx, pl.num_programs(0), b), 1), :]
        shift = mrow[:, 0:D_MODEL]
        gain = gpre_ref[...] * (1.0 + mrow[:, D_MODEL:2 * D_MODEL])

        def normed(rs):
            ctx_rows = slice(rs.start % CTX_LEN, rs.start % CTX_LEN + PROJ_SUB)
            x = jnp.where(is_ctx, ctx_ref[0, ctx_rows, :], x_ref[0, rs, :])
            return (_rms(x) * gain + shift).astype(BF16)

        def project(rs, h):
            tail = _dot(h, w_s[:, COL_TAIL:PROJ_W])
            qkv = _dot(h, w_s[:, 0:QKV_W])
            qkv_ref[0, rs, 0:GLA_QK] = (qkv[:, 0:GLA_QK] * (GLA_DK ** -0.5)).astype(BF16)
            qkv_ref[0, rs, GLA_QK:QKV_W] = qkv[:, GLA_QK:QKV_W].astype(BF16)
            rz_ref[0, rs, :] = _silu(_dot(h, w_s[:, COL_R:COL_XBC])).astype(BF16)
            xbc_new = _dot(h, w_s[:, COL_XBC:COL_TAIL]).astype(BF16)
            logits = _dot(tail.astype(BF16), wg_ref[...]) + bg_ref[...]
            u = logits * LOG2_E
            g_ref[0, rs, :] = ((jnp.minimum(u, 0.0) - jnp.log2(1.0 + jnp.exp2(-jnp.abs(u))))
                               * (1.0 / GLA_GATE_NORM))
            dt_ref[0, rs, :] = _softplus(tail + dtb_ref[...])
            return xbc_new

        subs = [slice(PROJ_SUB * s, PROJ_SUB * (s + 1)) for s in range(TM_PROJ // PROJ_SUB)]
        conv_previous_head()
        ready = normed(subs[0])
        fresh = []
        for s, rs in enumerate(subs):
            following = normed(subs[s + 1]) if s + 1 < len(subs) else None
            fresh.append(project(rs, ready))
            if s == 0:
                conv_previous_last_row(fresh[0][0:GRID_W, :])
            ready = following
        for rs, xbc_new in zip(subs, fresh):
            prev_s[rs, :] = xbc_new

    @pl.when(i == n_tiles)
    def _():
        conv_previous_head()
        conv_previous_last_row(none)


def _input_projection_conv(x, ctx, m_all, gpre, w_in, wg_pad, bg_cat, dtb_pad, conv_w9, conv_b):
    bsz, seq, _ = x.shape
    n_lat = seq // TM_PROJ
    n_tiles = n_lat + 1
    tt = CTX_LEN + seq
    const = lambda b, i: (0, 0)
    tok = lambda b, i: (b, jnp.minimum(i, n_tiles - 1), 0)
    conv_tok = lambda b, i: (b, jnp.maximum(i - 1, 0), 0)
    return pl.pallas_call(
        _inproj_conv_kernel,
        grid=(bsz, n_tiles + 1),
        in_specs=[pl.BlockSpec((1, TM_PROJ, D_MODEL), lambda b, i: (b, jnp.minimum(i, n_lat - 1), 0)),
                  pl.BlockSpec((1, CTX_LEN, D_MODEL), lambda b, i: (b, 0, 0)),
                  pl.BlockSpec(m_all.shape, const),
                  pl.BlockSpec((1, D_MODEL), const),
                  pl.BlockSpec((W_IN_COLS, D_MODEL), const, pipeline_mode=pl.Buffered(1)),
                  pl.BlockSpec((TAIL_W, 2 * GLA_QK), const),
                  pl.BlockSpec((1, 2 * GLA_QK), const),
                  pl.BlockSpec((1, TAIL_W), const),
                  pl.BlockSpec((CONV_K * CONV_K, SSD_CONV_DIM), const),
                  pl.BlockSpec((1, SSD_CONV_DIM), const)],
        out_specs=[pl.BlockSpec((1, TM_PROJ, QKV_W), tok),
                   pl.BlockSpec((1, TM_PROJ, GLA_INNER + SSD_INNER), tok),
                   pl.BlockSpec((1, TM_PROJ, SSD_CONV_DIM), conv_tok),
                   pl.BlockSpec((1, TM_PROJ, 2 * GLA_QK), tok),
                   pl.BlockSpec((1, TM_PROJ, TAIL_W), tok)],
        out_shape=[jax.ShapeDtypeStruct((bsz, tt, QKV_W), BF16),
                   jax.ShapeDtypeStruct((bsz, tt, GLA_INNER + SSD_INNER), BF16),
                   jax.ShapeDtypeStruct((bsz, tt, SSD_CONV_DIM), BF16),
                   jax.ShapeDtypeStruct((bsz, tt, 2 * GLA_QK), F32),
                   jax.ShapeDtypeStruct((bsz, tt, TAIL_W), F32)],
        scratch_shapes=[pltpu.VMEM((D_MODEL, PROJ_W), BF16),
                        pltpu.VMEM((TM_PROJ, SSD_CONV_DIM), BF16),
                        pltpu.VMEM((GRID_W, SSD_CONV_DIM), BF16)],
        compiler_params=pltpu.CompilerParams(
            dimension_semantics=("arbitrary", "arbitrary"), vmem_limit_bytes=VMEM_LIMIT),
        name="in_proj_conv",
    )(x, ctx, m_all, gpre, w_in, wg_pad, bg_cat, dtb_pad, conv_w9, conv_b)


def _cumsum_rows(x, reverse):
    n = x.shape[0]
    row = lax.broadcasted_iota(jnp.int32, x.shape, 0)
    s = 1
    while s < n:
        if reverse:
            x = x + jnp.where(row < n - s, pltpu.roll(x, n - s, axis=0), 0.0)
        else:
            x = x + jnp.where(row >= s, pltpu.roll(x, s, axis=0), 0.0)
        s *= 2
    return x


def _chunk_maps(n_ctx, n_all):
    n_lat = n_all - n_ctx
    fwd = lambda s: jnp.where(s < n_ctx, n_lat + s, s - n_ctx)
    bwd = lambda s: n_all - 1 - s
    return fwd, bwd


def _gla_prepare(qkv, g, reverse):
    n = qkv.shape[0]
    nb = n // SUB
    blocks = range(nb)
    rows = lambda a, blk: a[SUB * blk:SUB * (blk + 1), :]
    q = qkv[:, 0:GLA_QK].astype(F32)
    k = qkv[:, GLA_QK:2 * GLA_QK].astype(F32)
    cum = _cumsum_rows(g, reverse)
    zero = jnp.zeros((1, GLA_QK), F32)
    if reverse:
        ends = [cum[SUB * blk:SUB * blk + 1, :] for blk in blocks]
        refs = ends[1:] + [zero]
        earlier = lambda j, blk: j > blk
        tot = ends[0]
    else:
        ends = [cum[SUB * (blk + 1) - 1:SUB * (blk + 1), :] for blk in blocks]
        refs = [zero] + ends[:-1]
        earlier = lambda j, blk: j < blk
        tot = ends[nb - 1]
    stack = lambda vs: jnp.concatenate([jnp.broadcast_to(v, (SUB, GLA_QK)) for v in vs], axis=0)
    e_in = cum - stack(refs)
    q_blk = q * jnp.exp2(e_in)
    k_diag = k * jnp.exp2(-e_in)
    k_end = k * jnp.exp2(stack(ends) - cum)
    q_in = jnp.concatenate([rows(q_blk, blk) * jnp.exp2(refs[blk]) for blk in blocks], axis=0)
    k_out = jnp.concatenate([rows(k_end, blk) * jnp.exp2(tot - ends[blk]) for blk in blocks], axis=0)
    q_in = q_in.astype(BF16)
    k_out = k_out.astype(BF16)
    q_bf = q_blk.astype(BF16)
    zeros_piece = jnp.zeros((SUB, GLA_QK), BF16)
    k_piece = [[rows(k_diag, j).astype(BF16) if seg == j
                else (rows(k_end, j) * jnp.exp2(refs[seg] - ends[j])).astype(BF16) if earlier(j, seg)
                else zeros_piece
                for seg in blocks] for j in blocks]
    ii = lax.broadcasted_iota(jnp.int32, (n, n), 0)
    jj = lax.broadcasted_iota(jnp.int32, (n, n), 1)
    heads = []
    for h in range(GLA_HEADS):
        ks = slice(GLA_DK * h, GLA_DK * (h + 1))
        q_cat = jnp.concatenate(
            [jnp.concatenate([rows(q_bf, blk)[:, ks] if seg == blk else zeros_piece[:, ks]
                              for seg in blocks], axis=1) for blk in blocks], axis=0)
        k_cat = jnp.concatenate(
            [jnp.concatenate([k_piece[j][seg][:, ks] for seg in blocks], axis=1)
             for j in blocks], axis=0)
        heads.append(dict(q_cat=q_cat, k_cat=k_cat, q_in=q_in[:, ks], k_out=k_out[:, ks],
                          v=qkv[:, 2 * GLA_QK + GLA_DV * h:2 * GLA_QK + GLA_DV * (h + 1)]))
    return dict(heads=heads, causal=(ii <= jj) if reverse else (ii >= jj),
                decay_col=jnp.transpose(jnp.broadcast_to(tot, (GLA_DV, GLA_QK))))


def _gla_chain(qkv_ref, g_ref, reverse, s_ref, y_ref):
    def start():
        p = _gla_prepare(qkv_ref[...], g_ref[...], reverse)
        return p, [_dot_nt(hd["q_cat"], hd["k_cat"]) for hd in p["heads"]]

    def finish(started):
        p, att = started
        outs = []
        for h, (hd, a) in enumerate(zip(p["heads"], att)):
            state = s_ref[h]
            lhs = jnp.concatenate([jnp.where(p["causal"], a, 0.0).astype(BF16), hd["q_in"]], axis=1)
            rhs = jnp.concatenate([hd["v"], state.astype(BF16)], axis=0)
            outs.append(_dot(lhs, rhs))
            ks = slice(GLA_DK * h, GLA_DK * (h + 1))
            s_ref[h] = state * jnp.exp2(p["decay_col"][ks, :]) + _dot_tn(hd["k_out"], hd["v"])
        y_ref[...] = jnp.concatenate(outs, axis=1).astype(y_ref.dtype)

    return start, finish


def _ssd_prepare(xbc, dt, nega, expand, lane0, reverse):
    n = xbc.shape[0]
    end = 0 if reverse else n - 1
    cum = _cumsum_rows(dt * nega, reverse)
    e_cum = jnp.exp2(cum)
    w = jnp.exp2(cum[end:end + 1, :] - cum) * dt
    e_cum_x = _dot(_split_hi_lo(e_cum), expand)
    x_w = xbc[:, 0:SSD_INNER] * _dot(w.astype(BF16), expand[0:TAIL_W, :]).astype(BF16)
    cum_t = jnp.transpose(cum)
    dt_t = jnp.transpose(dt).astype(BF16)
    ii = lax.broadcasted_iota(jnp.int32, (n, n), 0)
    jj = lax.broadcasted_iota(jnp.int32, (n, n), 1)
    causal = (ii <= jj) if reverse else (ii >= jj)
    first_head = lax.broadcasted_iota(jnp.int32, (n, 2 * SSD_P), 1) < SSD_P
    groups = []
    for grp in range(SSD_GROUPS):
        b_g = xbc[:, SSD_INNER + SSD_N * grp:SSD_INNER + SSD_N * (grp + 1)]
        c_g = xbc[:, SSD_INNER + SSD_BC + SSD_N * grp:SSD_INNER + SSD_BC + SSD_N * (grp + 1)]
        scores = _dot_nt(c_g, b_g).astype(BF16)
        pairs = []
        for pair in range(SSD_HPG // 2):
            m_pair = []
            for hh in range(2):
                lane = lane0 + SSD_HPG * grp + 2 * pair + hh
                cum_b = jnp.broadcast_to(cum[:, lane:lane + 1], (n, n))
                seg = jnp.where(causal, cum_b - cum_t[lane:lane + 1, :], NEG_BIG)
                m_pair.append(scores * jnp.exp2(seg).astype(BF16) * dt_t[lane:lane + 1, :])
            col = 2 * SSD_P * (SSD_HPG // 2 * grp + pair)
            x_pair = xbc[:, col:col + 2 * SSD_P]
            none = jnp.zeros_like(x_pair)
            rhs = jnp.concatenate([jnp.where(first_head, x_pair, none),
                                   jnp.where(first_head, none, x_pair)], axis=0)
            pairs.append((jnp.concatenate(m_pair, axis=1), rhs))
        cols = slice(SSD_HPG * SSD_P * grp, SSD_HPG * SSD_P * (grp + 1))
        groups.append(dict(b=b_g, c=c_g, pairs=pairs, e_cum=e_cum_x[:, cols],
                           e_tot=e_cum_x[end:end + 1, cols], x_w=x_w[:, cols]))
    return groups


def _split_hi_lo(a):
    hi = a.astype(BF16)
    lo = (a - hi.astype(F32)).astype(BF16)
    return jnp.concatenate([hi, lo], axis=1)


def _ssd_chain(xbc_ref, dt_ref, nega, expand_ref, d, s_ref, y_ref):
    def start():
        return _ssd_prepare(xbc_ref[...], dt_ref[...], nega, expand_ref[d],
                            DT_LANE + SSD_HEADS * d, d == 1)

    def finish(groups):
        outs = []
        for grp, gd in enumerate(groups):
            state = s_ref[grp]
            intra = jnp.concatenate([_dot(m, rhs) for m, rhs in gd["pairs"]], axis=1)
            outs.append(_dot(gd["c"], state.astype(BF16)) * gd["e_cum"] + intra)
            s_ref[grp] = state * gd["e_tot"] + _dot_tn(gd["b"], gd["x_w"])
        y_ref[...] = jnp.concatenate(outs, axis=1).astype(y_ref.dtype)

    return start, finish


def _mixer_scan_kernel(qkv_f_ref, g_f_ref, qkv_b_ref, g_b_ref, xbc_f_ref, dt_f_ref, xbc_b_ref,
                       dt_b_ref, nega_ref, expand_ref, ygf_ref, ygb_ref, ysf_ref, ysb_ref,
                       gsf_ref, gsb_ref, ssf_ref, ssb_ref):
    @pl.when(pl.program_id(1) == 0)
    def _():
        for s_ref in (gsf_ref, gsb_ref, ssf_ref, ssb_ref):
            s_ref[...] = jnp.zeros_like(s_ref)

    nega = nega_ref[...]
    chains = []
    for e in range(qkv_f_ref.shape[0]):
        chains.append(_ssd_chain(xbc_f_ref.at[e], dt_f_ref.at[e], nega, expand_ref, 0,
                                 ssf_ref.at[e], ysf_ref.at[e]))
        chains.append(_gla_chain(qkv_f_ref.at[e], g_f_ref.at[e], False, gsf_ref.at[e], ygf_ref.at[e]))
        chains.append(_ssd_chain(xbc_b_ref.at[e], dt_b_ref.at[e], nega, expand_ref, 1,
                                 ssb_ref.at[e], ysb_ref.at[e]))
        chains.append(_gla_chain(qkv_b_ref.at[e], g_b_ref.at[e], True, gsb_ref.at[e], ygb_ref.at[e]))
    started = chains[0][0]()
    for i, (_, finish) in enumerate(chains):
        following = chains[i + 1][0]() if i + 1 < len(chains) else None
        finish(started)
        started = following


def _head_expand_matrix():
    r = jnp.arange(2 * TAIL_W)[:, None] % TAIL_W
    c = jnp.arange(SSD_INNER)[None, :] // SSD_P
    return jnp.stack([(r == DT_LANE + SSD_HEADS * d + c) for d in range(2)]).astype(BF16)


def _mixer_scan(qkv, g, xbc, dt, nega):
    bsz, tt, _ = qkv.shape
    n_all = tt // CHUNK
    fwd, bwd = _chunk_maps(CTX_LEN // CHUNK, n_all)
    spec = lambda w, m, c=0: pl.BlockSpec((SCAN_BATCH, CHUNK, w), lambda b, s: (b, m(s), c))
    expand = _head_expand_matrix()
    return pl.pallas_call(
        _mixer_scan_kernel,
        grid=(bsz // SCAN_BATCH, n_all),
        in_specs=[spec(QKV_W, fwd), spec(GLA_QK, fwd), spec(QKV_W, bwd), spec(GLA_QK, bwd, 1),
                  spec(SSD_CONV_DIM, fwd), spec(TAIL_W, fwd), spec(SSD_CONV_DIM, bwd), spec(TAIL_W, bwd),
                  pl.BlockSpec((1, TAIL_W), lambda b, s: (0, 0)),
                  pl.BlockSpec(expand.shape, lambda b, s: (0, 0, 0))],
        out_specs=[spec(GLA_INNER, fwd), spec(GLA_INNER, bwd), spec(SSD_INNER, fwd), spec(SSD_INNER, bwd)],
        out_shape=[jax.ShapeDtypeStruct((bsz, tt, GLA_INNER), BF16)] * 2
        + [jax.ShapeDtypeStruct((bsz, tt, SSD_INNER), BF16)] * 2,
        scratch_shapes=[pltpu.VMEM((SCAN_BATCH, GLA_HEADS, GLA_DK, GLA_DV), F32)] * 2
        + [pltpu.VMEM((SCAN_BATCH, SSD_GROUPS, SSD_N, SSD_HPG * SSD_P), F32)] * 2,
        compiler_params=pltpu.CompilerParams(
            dimension_semantics=("parallel", "arbitrary"), vmem_limit_bytes=VMEM_LIMIT),
        name="mixer_scan",
    )(qkv, g, qkv, g, xbc, dt, xbc, dt, nega, expand)


def _stream_cast_weights(pairs, stage, sem):
    def chunk_copy(w_hbm, c, slot):
        cols = w_hbm.shape[1]
        return pltpu.make_async_copy(w_hbm.at[pl.ds(c * W_CHUNK, W_CHUNK), :],
                                     stage.at[slot, :, 0:cols], sem.at[slot])

    chunk_copy(pairs[0][0], 0, 0).start()
    first = 0
    for k, (w_hbm, w_bf) in enumerate(pairs):
        n_chunks, rest = divmod(w_hbm.shape[0], W_CHUNK)
        assert rest == 0
        cols = w_hbm.shape[1]
        following = pairs[k + 1][0] if k + 1 < len(pairs) else None

        def body(c, carry, w_hbm=w_hbm, w_bf=w_bf, n_chunks=n_chunks, cols=cols,
                 following=following, first=first):
            slot = jnp.bitwise_and(first + c, 1)

            @pl.when(c + 1 < n_chunks)
            def _():
                chunk_copy(w_hbm, c + 1, 1 - slot).start()

            if following is not None:
                @pl.when(c + 1 == n_chunks)
                def _():
                    chunk_copy(following, 0, 1 - slot).start()

            chunk_copy(w_hbm, c, slot).wait()
            rows = pl.ds(pl.multiple_of(c * W_CHUNK, W_CHUNK), W_CHUNK)
            w_bf[rows, :] = stage[slot, :, 0:cols].astype(BF16)
            return carry

        lax.fori_loop(0, n_chunks, body, 0)
        first += n_chunks


def _merge_ffn_kernel(x_ref, ygf_ref, ygb_ref, ysf_ref, ysb_ref, xs_ref, rz_ref, m_ref,
                      gnorm_ref, dskip_ref, snorm_ref, npost_ref, nfpre_ref, nfpost_ref,
                      wout_hbm, wgate_hbm, wup_hbm, wdown_hbm, o_ref,
                      wout_ref, wgate_ref, wup_ref, wdown_ref, stage, sem):
    b = pl.program_id(0)

    @pl.when(jnp.logical_and(b == 0, pl.program_id(1) == 0))
    def _():
        _stream_cast_weights([(wout_hbm, wout_ref), (wgate_hbm, wgate_ref),
                              (wup_hbm, wup_ref), (wdown_hbm, wdown_ref)], stage, sem)

    mrow = m_ref[pl.ds(b, 1), :]
    mod = [mrow[:, D_MODEL * j:D_MODEL * (j + 1)] for j in range(6)]
    gw = SSD_INNER // SSD_GROUPS

    def merged(rs):
        y_gla = ygf_ref[0, rs, :].astype(F32) + ygb_ref[0, rs, :].astype(F32)
        o_gla = jnp.concatenate(
            [_rms(y_gla[:, GLA_DV * h:GLA_DV * (h + 1)]) for h in range(GLA_HEADS)], axis=1)
        o_gla = o_gla * gnorm_ref[...] * rz_ref[0, rs, 0:GLA_INNER].astype(F32)
        y_ssd = ysf_ref[0, rs, :].astype(F32) + ysb_ref[0, rs, :].astype(F32)
        y_ssd = ((y_ssd + dskip_ref[...] * xs_ref[0, rs, :].astype(F32))
                 * rz_ref[0, rs, GLA_INNER:].astype(F32))
        o_ssd = jnp.concatenate(
            [_rms(y_ssd[:, gw * g:gw * (g + 1)]) for g in range(SSD_GROUPS)], axis=1)
        o_ssd = o_ssd * snorm_ref[...]
        return jnp.concatenate([o_gla, o_ssd], axis=1).astype(BF16)

    def residual_in(rs, o):
        y = _dot(o, wout_ref[...])
        x1 = x_ref[0, rs, :] + mod[2] * (_rms(y) * npost_ref[...])
        h = ((_rms(x1) * nfpre_ref[...]) * (1.0 + mod[4]) + mod[3]).astype(BF16)
        return x1, h

    subs = [slice(FFN_SUB * j, FFN_SUB * (j + 1)) for j in range(TM_FFN // FFN_SUB)]
    x1, h = residual_in(subs[0], merged(subs[0]))
    for j, rs in enumerate(subs):
        following = residual_in(subs[j + 1], merged(subs[j + 1])) if j + 1 < len(subs) else None
        act = (_silu(_dot(h, wgate_ref[...])) * _dot(h, wup_ref[...])).astype(BF16)
        f = _dot(act, wdown_ref[...])
        o_ref[0, rs, :] = x1 + mod[5] * (_rms(f) * nfpost_ref[...])
        if following is not None:
            x1, h = following


def _merge_ffn(x, ygf, ygb, ysf, ysb, xbc, rz, m_all, gnorm, dskip, snorm, npost, nfpre,
               nfpost, wout, wgate, wup, wdown):
    bsz, seq, _ = x.shape
    n_tiles = seq // TM_FFN
    const = lambda b, i: (0, 0)

    def lat(width):
        return pl.BlockSpec((1, TM_FFN, width), lambda b, i: (b, i, 0))

    weights = (wout, wgate, wup, wdown)
    in_hbm = pl.BlockSpec(memory_space=pl.ANY)
    vec = lambda w: pl.BlockSpec((1, w), const)
    return pl.pallas_call(
        _merge_ffn_kernel,
        grid=(bsz, n_tiles),
        in_specs=[pl.BlockSpec((1, TM_FFN, D_MODEL), lambda b, i: (b, i, 0)),
                  lat(GLA_INNER), lat(GLA_INNER), lat(SSD_INNER), lat(SSD_INNER),
                  lat(SSD_INNER), lat(GLA_INNER + SSD_INNER),
                  pl.BlockSpec(m_all.shape, const),
                  vec(GLA_INNER), vec(SSD_INNER), vec(SSD_INNER),
                  vec(D_MODEL), vec(D_MODEL), vec(D_MODEL),
                  in_hbm, in_hbm, in_hbm, in_hbm],
        out_specs=pl.BlockSpec((1, TM_FFN, D_MODEL), lambda b, i: (b, i, 0)),
        out_shape=jax.ShapeDtypeStruct((bsz, seq, D_MODEL), F32),
        scratch_shapes=[pltpu.VMEM(w.shape, BF16) for w in weights]
        + [pltpu.VMEM((2, W_CHUNK, max(w.shape[1] for w in weights)), F32),
           pltpu.SemaphoreType.DMA((2,))],
        compiler_params=pltpu.CompilerParams(
            dimension_semantics=("arbitrary", "arbitrary"), vmem_limit_bytes=VMEM_LIMIT),
        name="merge_ffn",
    )(x, ygf, ygb, ysf, ysb, xbc, rz, m_all, gnorm, dskip, snorm, npost, nfpre, nfpost,
      wout, wgate, wup, wdown)


def kernel(x, c, ctx, c_ctx, w_mod, b_mod, norm_mix_pre, norm_mix_post, norm_ffn_pre, norm_ffn_post,
           w_in, conv_w, conv_b, gla_wg_f, gla_bg_f, gla_wg_b, gla_bg_b, gla_norm,
           a_log_f, a_log_b, dt_bias_f, dt_bias_b, d_skip, ssd_norm, w_out, w_gate, w_up, w_down):
    assert w_mod.shape[0] == 1, "single-layer kernel"
    bsz = x.shape[0]
    row = lambda v: v.reshape(1, -1)

    c_all = jnp.concatenate([c, c_ctx[None], jnp.zeros((MOD_ROWS - bsz - 1, D_MODEL), F32)], axis=0)
    wg_pad = jnp.zeros((TAIL_W, 2 * GLA_QK), F32)
    wg_pad = wg_pad.at[0:GLA_RANK, 0:GLA_QK].set(gla_wg_f[0])
    wg_pad = wg_pad.at[GLA_RANK:2 * GLA_RANK, GLA_QK:].set(gla_wg_b[0]).astype(BF16)
    bg_cat = jnp.concatenate([gla_bg_f[0], gla_bg_b[0]]).reshape(1, -1)
    lane_pad = lambda f, bk: jnp.zeros((1, TAIL_W), F32).at[0, DT_LANE:DT_LANE + 2 * SSD_HEADS].set(
        jnp.concatenate([f, bk]))
    dtb_pad = lane_pad(dt_bias_f[0], dt_bias_b[0])
    nega = lane_pad(-jnp.exp(a_log_f[0]) * LOG2_E, -jnp.exp(a_log_b[0]) * LOG2_E)
    conv_w9 = conv_w[0].reshape(CONV_K * CONV_K, SSD_CONV_DIM)

    m_all = _modulation(c_all, w_mod[0], row(b_mod[0]))
    qkv, rz, xbc, g, dt = _input_projection_conv(
        x, ctx, m_all, row(norm_mix_pre[0]), w_in[0].T, wg_pad, bg_cat, dtb_pad, conv_w9, row(conv_b[0]))
    ygf, ygb, ysf, ysb = _mixer_scan(qkv, g, xbc, dt, nega)
    return _merge_ffn(
        x, ygf, ygb, ysf, ysb, xbc, rz, m_all,
        row(jnp.tile(gla_norm[0], GLA_HEADS)), row(jnp.repeat(d_skip[0], SSD_P)), row(ssd_norm[0]),
        row(norm_mix_post[0]), row(norm_ffn_pre[0]), row(norm_ffn_post[0]),
        w_out[0], w_gate[0], w_up[0], w_down[0])
```

```python
import jax
import jax.numpy as jnp
from jax import lax
from jax.experimental import pallas as pl
from jax.experimental.pallas import tpu as pltpu

F32 = jnp.float32
BF16 = jnp.bfloat16

D_MODEL = 1024
CTX_LEN = 256
GRID_W = 64
EPS = 1e-6

GLA_HEADS = 4
GLA_DK = 64
GLA_DV = 128
GLA_QK = GLA_HEADS * GLA_DK
GLA_INNER = GLA_HEADS * GLA_DV
GLA_RANK = 16
GLA_GATE_NORM = 16.0

SSD_HEADS = 8
SSD_P = 64
SSD_N = 64
SSD_GROUPS = 2
SSD_HPG = SSD_HEADS // SSD_GROUPS
SSD_INNER = SSD_HEADS * SSD_P
SSD_BC = SSD_GROUPS * SSD_N
SSD_CONV_DIM = SSD_INNER + 2 * SSD_BC
CONV_K = 3

QKV_W = 2 * GLA_QK + GLA_INNER
COL_R = QKV_W
COL_Z = COL_R + GLA_INNER
COL_XBC = COL_Z + SSD_INNER
COL_TAIL = COL_XBC + SSD_CONV_DIM
TAIL_W = 128
PROJ_W = COL_TAIL + TAIL_W
DT_LANE = 2 * GLA_RANK
W_IN_LR = COL_Z
W_IN_Z = W_IN_LR + 2 * GLA_RANK
W_IN_DT = W_IN_Z + SSD_INNER + SSD_CONV_DIM
W_IN_COLS = W_IN_DT + 2 * SSD_HEADS

TM_PROJ = 1024
PROJ_SUB = 256
CONV_COLS = 128
MOD_ROWS = 8
MOD_TN = 1024
TM_FFN = 512
FFN_SUB = 128
W_CHUNK = 256
CHUNK = 128
SCAN_BATCH = 4
SUB = 32
NEG_BIG = -1e30
LOG2_E = 1.4426950408889634
VMEM_LIMIT = 56 * 1024 * 1024


def _silu(x):
    return x / (1.0 + jnp.exp2(x * -LOG2_E))


def _softplus(x):
    return jnp.maximum(x, 0.0) + jnp.log(1.0 + jnp.exp(-jnp.abs(x)))


def _rms(x):
    return x * lax.rsqrt(jnp.mean(x * x, axis=-1, keepdims=True) + EPS)


def _dot(a, b):
    return jnp.dot(a, b, preferred_element_type=F32)


def _dot_nt(a, b):
    return lax.dot_general(a, b, (((1,), (1,)), ((), ())), preferred_element_type=F32)


def _dot_tn(a, b):
    return lax.dot_general(a, b, (((0,), (0,)), ((), ())), preferred_element_type=F32)


def _mod_kernel(c_ref, w_ref, b_ref, o_ref):
    s = _silu(c_ref[...])
    o_ref[...] = _dot(s.astype(BF16), w_ref[...].astype(BF16)) + b_ref[...]


def _modulation(c_all, w_mod, b_mod):
    n = w_mod.shape[1]
    return pl.pallas_call(
        _mod_kernel,
        grid=(n // MOD_TN,),
        in_specs=[pl.BlockSpec((MOD_ROWS, D_MODEL), lambda j: (0, 0)),
                  pl.BlockSpec((D_MODEL, MOD_TN), lambda j: (0, j)),
                  pl.BlockSpec((1, MOD_TN), lambda j: (0, j))],
        out_specs=pl.BlockSpec((MOD_ROWS, MOD_TN), lambda j: (0, j)),
        out_shape=jax.ShapeDtypeStruct((MOD_ROWS, n), F32),
        name="adaln_mod",
    )(c_all, w_mod, b_mod)


def _permute_cast_w_in(wt_ref, w_s):
    blk = 128
    pad = jnp.zeros((TAIL_W - (W_IN_Z - W_IN_LR) - (W_IN_COLS - W_IN_DT), blk), F32)

    def body(kb, carry):
        k0 = pl.multiple_of(kb * blk, blk)
        cols = pl.ds(k0, blk)
        rows = pl.ds(k0, blk)
        w_s[rows, 0:W_IN_LR] = jnp.transpose(wt_ref[0:W_IN_LR, cols]).astype(BF16)
        w_s[rows, COL_Z:COL_TAIL] = jnp.transpose(wt_ref[W_IN_Z:W_IN_DT, cols]).astype(BF16)
        tail = jnp.concatenate([wt_ref[W_IN_LR:W_IN_Z, cols], wt_ref[W_IN_DT:W_IN_COLS, cols], pad],
                               axis=0)
        w_s[rows, COL_TAIL:PROJ_W] = jnp.transpose(tail).astype(BF16)
        return carry

    lax.fori_loop(0, D_MODEL // blk, body, 0)


def _grid_conv_rows(window, t0, is_ctx, w_ref, b_ref, cols):
    n = window.shape[0] - 2 * GRID_W
    ext = window.astype(F32)
    row_on = jnp.where(is_ctx, 0.0, 1.0)
    taps = []
    for dx in range(CONV_K):
        tap = None
        for dy in range(CONV_K):
            wk = w_ref[CONV_K * dy + dx:CONV_K * dy + dx + 1, cols]
            if dy != 1:
                wk = wk * row_on
            term = wk * ext[GRID_W * dy:GRID_W * dy + n, :]
            tap = term if tap is None else tap + term
        taps.append(tap)
    t = t0 + lax.broadcasted_iota(jnp.int32, (n, 1), 0)
    col = jnp.bitwise_and(t, GRID_W - 1)
    has_left = jnp.where(is_ctx, t, col) != 0
    has_right = jnp.where(is_ctx, t - (CTX_LEN - 1), col - (GRID_W - 1)) != 0
    acc = (taps[1] + b_ref[:, cols]
           + jnp.where(has_left, pltpu.roll(taps[0], 1, axis=0), 0.0)
           + jnp.where(has_right, pltpu.roll(taps[2], n - 1, axis=0), 0.0))
    return _silu(acc)


def _inproj_conv_kernel(x_ref, ctx_ref, m_ref, gpre_ref, w_ref, wg_ref, bg_ref, dtb_ref,
                        cw_ref, cb_ref, qkv_ref, rz_ref, xbc_ref, g_ref, dt_ref,
                        w_s, prev_s, above_s):
    b = pl.program_id(0)
    i = pl.program_id(1)
    n_tiles = pl.num_programs(1) - 1
    ctx_tile = n_tiles - 1

    @pl.when(jnp.logical_and(b == 0, i == 0))
    def _():
        _permute_cast_w_in(w_ref, w_s)

    @pl.when(i == 0)
    def _():
        prev_s[...] = jnp.zeros_like(prev_s)
        above_s[...] = jnp.zeros_like(above_s)

    j = i - 1
    split = TM_PROJ - GRID_W
    none = jnp.zeros((GRID_W, SSD_CONV_DIM), BF16)
    col_groups = [slice(c, c + CONV_COLS) for c in range(0, SSD_CONV_DIM, CONV_COLS)]
    pieces = [(r0, min(CTX_LEN, split - r0)) for r0 in range(0, split, CTX_LEN)]

    def conv_block(t0, n, cols, window, conv_ctx):
        out = _grid_conv_rows(window, t0, conv_ctx, cw_ref, cb_ref, cols)
        xbc_ref[0, t0:t0 + n, cols] = out.astype(BF16)

    def conv_previous_head(conv_ctx):
        has_above = jnp.logical_and(j >= 1, j < ctx_tile)
        for cols in col_groups:
            for r0, n in (pieces[:1] if conv_ctx else pieces):
                if r0 == 0:
                    above = jnp.where(has_above, above_s[:, cols], none[:, cols])
                    window = jnp.concatenate([above, prev_s[0:n + GRID_W, cols]], axis=0)
                else:
                    window = prev_s[r0 - GRID_W:r0 + n + GRID_W, cols]
                conv_block(r0, n, cols, window, conv_ctx)
        above_s[...] = prev_s[split:, :]

    def conv_previous_last_row(below):
        below = jnp.where(j < ctx_tile - 1, below, none)
        for cols in col_groups:
            conv_block(split, GRID_W, cols,
                       jnp.concatenate([prev_s[split - GRID_W:, cols], below[:, cols]], axis=0), False)

    def project_tile(is_ctx):
        mrow = m_ref[pl.ds(pl.num_programs(0) if is_ctx else b, 1), :]
        shift = mrow[:, 0:D_MODEL]
        gain = gpre_ref[...] * (1.0 + mrow[:, D_MODEL:2 * D_MODEL])

        def normed(rs):
            x = ctx_ref[0] if is_ctx else x_ref[0, rs, :]
            return (_rms(x) * gain + shift).astype(BF16)

        def project(rs, h):
            tail = _dot(h, w_s[:, COL_TAIL:PROJ_W])
            qkv = _dot(h, w_s[:, 0:QKV_W])
            qkv_ref[0, rs, 0:GLA_QK] = (qkv[:, 0:GLA_QK] * (GLA_DK ** -0.5)).astype(BF16)
            qkv_ref[0, rs, GLA_QK:QKV_W] = qkv[:, GLA_QK:QKV_W].astype(BF16)
            rz_ref[0, rs, :] = _silu(_dot(h, w_s[:, COL_R:COL_XBC])).astype(BF16)
            xbc_new = _dot(h, w_s[:, COL_XBC:COL_TAIL]).astype(BF16)
            logits = _dot(tail.astype(BF16), wg_ref[...]) + bg_ref[...]
            u = logits * LOG2_E
            g_ref[0, rs, :] = ((jnp.minimum(u, 0.0) - jnp.log2(1.0 + jnp.exp2(-jnp.abs(u))))
                               * (1.0 / GLA_GATE_NORM))
            dt_ref[0, rs, :] = _softplus(tail + dtb_ref[...])
            return xbc_new

        rows = CTX_LEN if is_ctx else TM_PROJ
        subs = [slice(PROJ_SUB * s, PROJ_SUB * (s + 1)) for s in range(rows // PROJ_SUB)]
        conv_previous_head(False)
        ready = normed(subs[0])
        fresh = []
        for s, rs in enumerate(subs):
            following = normed(subs[s + 1]) if s + 1 < len(subs) else None
            fresh.append(project(rs, ready))
            if s == 0:
                conv_previous_last_row(fresh[0][0:GRID_W, :])
            ready = following
        for rs, xbc_new in zip(subs, fresh):
            prev_s[rs, :] = xbc_new

    @pl.when(i < ctx_tile)
    def _():
        project_tile(False)

    @pl.when(i == ctx_tile)
    def _():
        project_tile(True)

    @pl.when(i == n_tiles)
    def _():
        conv_previous_head(True)


def _input_projection_conv(x, ctx, m_all, gpre, w_in, wg_pad, bg_cat, dtb_pad, conv_w9, conv_b):
    bsz, seq, _ = x.shape
    n_lat = seq // TM_PROJ
    n_tiles = n_lat + 1
    tt = CTX_LEN + seq
    const = lambda b, i: (0, 0)
    tok = lambda b, i: (b, jnp.minimum(i, n_tiles - 1), 0)
    conv_tok = lambda b, i: (b, jnp.maximum(i - 1, 0), 0)
    return pl.pallas_call(
        _inproj_conv_kernel,
        grid=(bsz, n_tiles + 1),
        in_specs=[pl.BlockSpec((1, TM_PROJ, D_MODEL), lambda b, i: (b, jnp.minimum(i, n_lat - 1), 0)),
                  pl.BlockSpec((1, CTX_LEN, D_MODEL), lambda b, i: (b, 0, 0)),
                  pl.BlockSpec(m_all.shape, const),
                  pl.BlockSpec((1, D_MODEL), const),
                  pl.BlockSpec((W_IN_COLS, D_MODEL), const, pipeline_mode=pl.Buffered(1)),
                  pl.BlockSpec((TAIL_W, 2 * GLA_QK), const),
                  pl.BlockSpec((1, 2 * GLA_QK), const),
                  pl.BlockSpec((1, TAIL_W), const),
                  pl.BlockSpec((CONV_K * CONV_K, SSD_CONV_DIM), const),
                  pl.BlockSpec((1, SSD_CONV_DIM), const)],
        out_specs=[pl.BlockSpec((1, TM_PROJ, QKV_W), tok),
                   pl.BlockSpec((1, TM_PROJ, GLA_INNER + SSD_INNER), tok),
                   pl.BlockSpec((1, TM_PROJ, SSD_CONV_DIM), conv_tok),
                   pl.BlockSpec((1, TM_PROJ, 2 * GLA_QK), tok),
                   pl.BlockSpec((1, TM_PROJ, TAIL_W), tok)],
        out_shape=[jax.ShapeDtypeStruct((bsz, tt, QKV_W), BF16),
                   jax.ShapeDtypeStruct((bsz, tt, GLA_INNER + SSD_INNER), BF16),
                   jax.ShapeDtypeStruct((bsz, tt, SSD_CONV_DIM), BF16),
                   jax.ShapeDtypeStruct((bsz, tt, 2 * GLA_QK), F32),
                   jax.ShapeDtypeStruct((bsz, tt, TAIL_W), F32)],
        scratch_shapes=[pltpu.VMEM((D_MODEL, PROJ_W), BF16),
                        pltpu.VMEM((TM_PROJ, SSD_CONV_DIM), BF16),
                        pltpu.VMEM((GRID_W, SSD_CONV_DIM), BF16)],
        compiler_params=pltpu.CompilerParams(
            dimension_semantics=("arbitrary", "arbitrary"), vmem_limit_bytes=VMEM_LIMIT),
        name="in_proj_conv",
    )(x, ctx, m_all, gpre, w_in, wg_pad, bg_cat, dtb_pad, conv_w9, conv_b)


def _cumsum_rows(x, reverse):
    n = x.shape[0]
    row = lax.broadcasted_iota(jnp.int32, x.shape, 0)
    s = 1
    while s < n:
        if reverse:
            x = x + jnp.where(row < n - s, pltpu.roll(x, n - s, axis=0), 0.0)
        else:
            x = x + jnp.where(row >= s, pltpu.roll(x, s, axis=0), 0.0)
        s *= 2
    return x


def _chunk_maps(n_ctx, n_all):
    n_lat = n_all - n_ctx
    fwd = lambda s: jnp.where(s < n_ctx, n_lat + s, s - n_ctx)
    bwd = lambda s: n_all - 1 - s
    return fwd, bwd


def _gla_prepare(qkv, g, reverse):
    n = qkv.shape[0]
    nb = n // SUB
    blocks = range(nb)
    rows = lambda a, blk: a[SUB * blk:SUB * (blk + 1), :]
    q = qkv[:, 0:GLA_QK].astype(F32)
    k = qkv[:, GLA_QK:2 * GLA_QK].astype(F32)
    cum = _cumsum_rows(g, reverse)
    zero = jnp.zeros((1, GLA_QK), F32)
    if reverse:
        ends = [cum[SUB * blk:SUB * blk + 1, :] for blk in blocks]
        refs = ends[1:] + [zero]
        earlier = lambda j, blk: j > blk
        tot = ends[0]
    else:
        ends = [cum[SUB * (blk + 1) - 1:SUB * (blk + 1), :] for blk in blocks]
        refs = [zero] + ends[:-1]
        earlier = lambda j, blk: j < blk
        tot = ends[nb - 1]
    stack = lambda vs: jnp.concatenate([jnp.broadcast_to(v, (SUB, GLA_QK)) for v in vs], axis=0)
    e_in = cum - stack(refs)
    q_blk = q * jnp.exp2(e_in)
    k_diag = k * jnp.exp2(-e_in)
    k_end = k * jnp.exp2(stack(ends) - cum)
    q_in = jnp.concatenate([rows(q_blk, blk) * jnp.exp2(refs[blk]) for blk in blocks], axis=0)
    k_out = jnp.concatenate([rows(k_end, blk) * jnp.exp2(tot - ends[blk]) for blk in blocks], axis=0)
    q_in = q_in.astype(BF16)
    k_out = k_out.astype(BF16)
    q_bf = q_blk.astype(BF16)
    zeros_piece = jnp.zeros((SUB, GLA_QK), BF16)
    k_piece = [[rows(k_diag, j).astype(BF16) if seg == j
                else (rows(k_end, j) * jnp.exp2(refs[seg] - ends[j])).astype(BF16) if earlier(j, seg)
                else zeros_piece
                for seg in blocks] for j in blocks]
    ii = lax.broadcasted_iota(jnp.int32, (n, n), 0)
    jj = lax.broadcasted_iota(jnp.int32, (n, n), 1)
    heads = []
    for h in range(GLA_HEADS):
        ks = slice(GLA_DK * h, GLA_DK * (h + 1))
        q_cat = jnp.concatenate(
            [jnp.concatenate([rows(q_bf, blk)[:, ks] if seg == blk else zeros_piece[:, ks]
                              for seg in blocks], axis=1) for blk in blocks], axis=0)
        k_cat = jnp.concatenate(
            [jnp.concatenate([k_piece[j][seg][:, ks] for seg in blocks], axis=1)
             for j in blocks], axis=0)
        heads.append(dict(q_cat=q_cat, k_cat=k_cat, q_in=q_in[:, ks], k_out=k_out[:, ks],
                          v=qkv[:, 2 * GLA_QK + GLA_DV * h:2 * GLA_QK + GLA_DV * (h + 1)]))
    return dict(heads=heads, causal=(ii <= jj) if reverse else (ii >= jj),
                decay_col=jnp.transpose(jnp.broadcast_to(tot, (GLA_DV, GLA_QK))))


def _gla_chain(qkv_ref, g_ref, reverse, s_ref, y_ref):
    def start():
        p = _gla_prepare(qkv_ref[...], g_ref[...], reverse)
        return p, [_dot_nt(hd["q_cat"], hd["k_cat"]) for hd in p["heads"]]

    def finish(started):
        p, att = started
        outs = []
        for h, (hd, a) in enumerate(zip(p["heads"], att)):
            state = s_ref[h]
            lhs = jnp.concatenate([jnp.where(p["causal"], a, 0.0).astype(BF16), hd["q_in"]], axis=1)
            rhs = jnp.concatenate([hd["v"], state.astype(BF16)], axis=0)
            outs.append(_dot(lhs, rhs))
            ks = slice(GLA_DK * h, GLA_DK * (h + 1))
            s_ref[h] = state * jnp.exp2(p["decay_col"][ks, :]) + _dot_tn(hd["k_out"], hd["v"])
        y_ref[...] = jnp.concatenate(outs, axis=1).astype(y_ref.dtype)

    return start, finish


def _ssd_prepare(xbc, dt, nega, expand, lane0, reverse):
    n = xbc.shape[0]
    end = 0 if reverse else n - 1
    cum = _cumsum_rows(dt * nega, reverse)
    e_cum = jnp.exp2(cum)
    w = jnp.exp2(cum[end:end + 1, :] - cum) * dt
    e_cum_x = _dot(_split_hi_lo(e_cum), expand)
    x_w = xbc[:, 0:SSD_INNER] * _dot(w.astype(BF16), expand[0:TAIL_W, :]).astype(BF16)
    cum_t = jnp.transpose(cum)
    dt_t = jnp.transpose(dt).astype(BF16)
    ii = lax.broadcasted_iota(jnp.int32, (n, n), 0)
    jj = lax.broadcasted_iota(jnp.int32, (n, n), 1)
    causal = (ii <= jj) if reverse else (ii >= jj)
    first_head = lax.broadcasted_iota(jnp.int32, (n, 2 * SSD_P), 1) < SSD_P
    groups = []
    for grp in range(SSD_GROUPS):
        b_g = xbc[:, SSD_INNER + SSD_N * grp:SSD_INNER + SSD_N * (grp + 1)]
        c_g = xbc[:, SSD_INNER + SSD_BC + SSD_N * grp:SSD_INNER + SSD_BC + SSD_N * (grp + 1)]
        scores = _dot_nt(c_g, b_g).astype(BF16)
        pairs = []
        for pair in range(SSD_HPG // 2):
            m_pair = []
            for hh in range(2):
                lane = lane0 + SSD_HPG * grp + 2 * pair + hh
                cum_b = jnp.broadcast_to(cum[:, lane:lane + 1], (n, n))
                seg = jnp.where(causal, cum_b - cum_t[lane:lane + 1, :], NEG_BIG)
                m_pair.append(scores * jnp.exp2(seg).astype(BF16) * dt_t[lane:lane + 1, :])
            col = 2 * SSD_P * (SSD_HPG // 2 * grp + pair)
            x_pair = xbc[:, col:col + 2 * SSD_P]
            none = jnp.zeros_like(x_pair)
            rhs = jnp.concatenate([jnp.where(first_head, x_pair, none),
                                   jnp.where(first_head, none, x_pair)], axis=0)
            pairs.append((jnp.concatenate(m_pair, axis=1), rhs))
        cols = slice(SSD_HPG * SSD_P * grp, SSD_HPG * SSD_P * (grp + 1))
        groups.append(dict(b=b_g, c=c_g, pairs=pairs, e_cum=e_cum_x[:, cols],
                           e_tot=e_cum_x[end:end + 1, cols], x_w=x_w[:, cols]))
    return groups


def _split_hi_lo(a):
    hi = a.astype(BF16)
    lo = (a - hi.astype(F32)).astype(BF16)
    return jnp.concatenate([hi, lo], axis=1)


def _ssd_chain(xbc_ref, dt_ref, nega, expand_ref, d, s_ref, y_ref):
    def start():
        return _ssd_prepare(xbc_ref[...], dt_ref[...], nega, expand_ref[d],
                            DT_LANE + SSD_HEADS * d, d == 1)

    def finish(groups):
        outs = []
        for grp, gd in enumerate(groups):
            state = s_ref[grp]
            intra = jnp.concatenate([_dot(m, rhs) for m, rhs in gd["pairs"]], axis=1)
            outs.append(_dot(gd["c"], state.astype(BF16)) * gd["e_cum"] + intra)
            s_ref[grp] = state * gd["e_tot"] + _dot_tn(gd["b"], gd["x_w"])
        y_ref[...] = jnp.concatenate(outs, axis=1).astype(y_ref.dtype)

    return start, finish


def _mixer_scan_kernel(qkv_f_ref, g_f_ref, qkv_b_ref, g_b_ref, xbc_f_ref, dt_f_ref, xbc_b_ref,
                       dt_b_ref, nega_ref, expand_ref, ygf_ref, ygb_ref, ysf_ref, ysb_ref,
                       gsf_ref, gsb_ref, ssf_ref, ssb_ref):
    @pl.when(pl.program_id(1) == 0)
    def _():
        for s_ref in (gsf_ref, gsb_ref, ssf_ref, ssb_ref):
            s_ref[...] = jnp.zeros_like(s_ref)

    nega = nega_ref[...]
    chains = []
    for e in range(qkv_f_ref.shape[0]):
        chains.append(_gla_chain(qkv_f_ref.at[e], g_f_ref.at[e], False, gsf_ref.at[e], ygf_ref.at[e]))
        chains.append(_ssd_chain(xbc_f_ref.at[e], dt_f_ref.at[e], nega, expand_ref, 0,
                                 ssf_ref.at[e], ysf_ref.at[e]))
        chains.append(_gla_chain(qkv_b_ref.at[e], g_b_ref.at[e], True, gsb_ref.at[e], ygb_ref.at[e]))
        chains.append(_ssd_chain(xbc_b_ref.at[e], dt_b_ref.at[e], nega, expand_ref, 1,
                                 ssb_ref.at[e], ysb_ref.at[e]))
    started = chains[0][0]()
    for i, (_, finish) in enumerate(chains):
        following = chains[i + 1][0]() if i + 1 < len(chains) else None
        finish(started)
        started = following


def _head_expand_matrix():
    r = jnp.arange(2 * TAIL_W)[:, None] % TAIL_W
    c = jnp.arange(SSD_INNER)[None, :] // SSD_P
    return jnp.stack([(r == DT_LANE + SSD_HEADS * d + c) for d in range(2)]).astype(BF16)


def _mixer_scan(qkv, g, xbc, dt, nega):
    bsz, tt, _ = qkv.shape
    n_all = tt // CHUNK
    fwd, bwd = _chunk_maps(CTX_LEN // CHUNK, n_all)
    spec = lambda w, m, c=0: pl.BlockSpec((SCAN_BATCH, CHUNK, w), lambda b, s: (b, m(s), c))
    expand = _head_expand_matrix()
    return pl.pallas_call(
        _mixer_scan_kernel,
        grid=(bsz // SCAN_BATCH, n_all),
        in_specs=[spec(QKV_W, fwd), spec(GLA_QK, fwd), spec(QKV_W, bwd), spec(GLA_QK, bwd, 1),
                  spec(SSD_CONV_DIM, fwd), spec(TAIL_W, fwd), spec(SSD_CONV_DIM, bwd), spec(TAIL_W, bwd),
                  pl.BlockSpec((1, TAIL_W), lambda b, s: (0, 0)),
                  pl.BlockSpec(expand.shape, lambda b, s: (0, 0, 0))],
        out_specs=[spec(GLA_INNER, fwd), spec(GLA_INNER, bwd), spec(SSD_INNER, fwd), spec(SSD_INNER, bwd)],
        out_shape=[jax.ShapeDtypeStruct((bsz, tt, GLA_INNER), BF16)] * 2
        + [jax.ShapeDtypeStruct((bsz, tt, SSD_INNER), BF16)] * 2,
        scratch_shapes=[pltpu.VMEM((SCAN_BATCH, GLA_HEADS, GLA_DK, GLA_DV), F32)] * 2
        + [pltpu.VMEM((SCAN_BATCH, SSD_GROUPS, SSD_N, SSD_HPG * SSD_P), F32)] * 2,
        compiler_params=pltpu.CompilerParams(
            dimension_semantics=("parallel", "arbitrary"), vmem_limit_bytes=VMEM_LIMIT),
        name="mixer_scan",
    )(qkv, g, qkv, g, xbc, dt, xbc, dt, nega, expand)


def _stream_cast_weights(pairs, stage, sem):
    def chunk_copy(w_hbm, c, slot):
        cols = w_hbm.shape[1]
        return pltpu.make_async_copy(w_hbm.at[pl.ds(c * W_CHUNK, W_CHUNK), :],
                                     stage.at[slot, :, 0:cols], sem.at[slot])

    chunk_copy(pairs[0][0], 0, 0).start()
    first = 0
    for k, (w_hbm, w_bf) in enumerate(pairs):
        n_chunks, rest = divmod(w_hbm.shape[0], W_CHUNK)
        assert rest == 0
        cols = w_hbm.shape[1]
        following = pairs[k + 1][0] if k + 1 < len(pairs) else None

        def body(c, carry, w_hbm=w_hbm, w_bf=w_bf, n_chunks=n_chunks, cols=cols,
                 following=following, first=first):
            slot = jnp.bitwise_and(first + c, 1)

            @pl.when(c + 1 < n_chunks)
            def _():
                chunk_copy(w_hbm, c + 1, 1 - slot).start()

            if following is not None:
                @pl.when(c + 1 == n_chunks)
                def _():
                    chunk_copy(following, 0, 1 - slot).start()

            chunk_copy(w_hbm, c, slot).wait()
            rows = pl.ds(pl.multiple_of(c * W_CHUNK, W_CHUNK), W_CHUNK)
            w_bf[rows, :] = stage[slot, :, 0:cols].astype(BF16)
            return carry

        lax.fori_loop(0, n_chunks, body, 0)
        first += n_chunks


def _merge_ffn_kernel(x_ref, ygf_ref, ygb_ref, ysf_ref, ysb_ref, xs_ref, rz_ref, m_ref,
                      gnorm_ref, dskip_ref, snorm_ref, npost_ref, nfpre_ref, nfpost_ref,
                      wout_hbm, wgate_hbm, wup_hbm, wdown_hbm, o_ref,
                      wout_ref, wgate_ref, wup_ref, wdown_ref, stage, sem):
    b = pl.program_id(0)

    @pl.when(jnp.logical_and(b == 0, pl.program_id(1) == 0))
    def _():
        _stream_cast_weights([(wout_hbm, wout_ref), (wgate_hbm, wgate_ref),
                              (wup_hbm, wup_ref), (wdown_hbm, wdown_ref)], stage, sem)

    mrow = m_ref[pl.ds(b, 1), :]
    mod = [mrow[:, D_MODEL * j:D_MODEL * (j + 1)] for j in range(6)]
    gw = SSD_INNER // SSD_GROUPS

    def merged(rs):
        y_gla = ygf_ref[0, rs, :].astype(F32) + ygb_ref[0, rs, :].astype(F32)
        o_gla = jnp.concatenate(
            [_rms(y_gla[:, GLA_DV * h:GLA_DV * (h + 1)]) for h in range(GLA_HEADS)], axis=1)
        o_gla = o_gla * gnorm_ref[...] * rz_ref[0, rs, 0:GLA_INNER].astype(F32)
        y_ssd = ysf_ref[0, rs, :].astype(F32) + ysb_ref[0, rs, :].astype(F32)
        y_ssd = ((y_ssd + dskip_ref[...] * xs_ref[0, rs, :].astype(F32))
                 * rz_ref[0, rs, GLA_INNER:].astype(F32))
        o_ssd = jnp.concatenate(
            [_rms(y_ssd[:, gw * g:gw * (g + 1)]) for g in range(SSD_GROUPS)], axis=1)
        o_ssd = o_ssd * snorm_ref[...]
        return jnp.concatenate([o_gla, o_ssd], axis=1).astype(BF16)

    def residual_in(rs, o):
        y = _dot(o, wout_ref[...])
        x1 = x_ref[0, rs, :] + mod[2] * (_rms(y) * npost_ref[...])
        h = ((_rms(x1) * nfpre_ref[...]) * (1.0 + mod[4]) + mod[3]).astype(BF16)
        return x1, h

    subs = [slice(FFN_SUB * j, FFN_SUB * (j + 1)) for j in range(TM_FFN // FFN_SUB)]
    x1, h = residual_in(subs[0], merged(subs[0]))
    for j, rs in enumerate(subs):
        following = residual_in(subs[j + 1], merged(subs[j + 1])) if j + 1 < len(subs) else None
        act = (_silu(_dot(h, wgate_ref[...])) * _dot(h, wup_ref[...])).astype(BF16)
        f = _dot(act, wdown_ref[...])
        o_ref[0, rs, :] = x1 + mod[5] * (_rms(f) * nfpost_ref[...])
        if following is not None:
            x1, h = following


def _merge_ffn(x, ygf, ygb, ysf, ysb, xbc, rz, m_all, gnorm, dskip, snorm, npost, nfpre,
               nfpost, wout, wgate, wup, wdown):
    bsz, seq, _ = x.shape
    n_tiles = seq // TM_FFN
    const = lambda b, i: (0, 0)

    def lat(width):
        return pl.BlockSpec((1, TM_FFN, width), lambda b, i: (b, i, 0))

    weights = (wout, wgate, wup, wdown)
    in_hbm = pl.BlockSpec(memory_space=pl.ANY)
    vec = lambda w: pl.BlockSpec((1, w), const)
    return pl.pallas_call(
        _merge_ffn_kernel,
        grid=(bsz, n_tiles),
        in_specs=[pl.BlockSpec((1, TM_FFN, D_MODEL), lambda b, i: (b, i, 0)),
                  lat(GLA_INNER), lat(GLA_INNER), lat(SSD_INNER), lat(SSD_INNER),
                  lat(SSD_INNER), lat(GLA_INNER + SSD_INNER),
                  pl.BlockSpec(m_all.shape, const),
                  vec(GLA_INNER), vec(SSD_INNER), vec(SSD_INNER),
                  vec(D_MODEL), vec(D_MODEL), vec(D_MODEL),
                  in_hbm, in_hbm, in_hbm, in_hbm],
        out_specs=pl.BlockSpec((1, TM_FFN, D_MODEL), lambda b, i: (b, i, 0)),
        out_shape=jax.ShapeDtypeStruct((bsz, seq, D_MODEL), F32),
        scratch_shapes=[pltpu.VMEM(w.shape, BF16) for w in weights]
        + [pltpu.VMEM((2, W_CHUNK, max(w.shape[1] for w in weights)), F32),
           pltpu.SemaphoreType.DMA((2,))],
        compiler_params=pltpu.CompilerParams(
            dimension_semantics=("arbitrary", "arbitrary"), vmem_limit_bytes=VMEM_LIMIT),
        name="merge_ffn",
    )(x, ygf, ygb, ysf, ysb, xbc, rz, m_all, gnorm, dskip, snorm, npost, nfpre, nfpost,
      wout, wgate, wup, wdown)


def kernel(x, c, ctx, c_ctx, w_mod, b_mod, norm_mix_pre, norm_mix_post, norm_ffn_pre, norm_ffn_post,
           w_in, conv_w, conv_b, gla_wg_f, gla_bg_f, gla_wg_b, gla_bg_b, gla_norm,
           a_log_f, a_log_b, dt_bias_f, dt_bias_b, d_skip, ssd_norm, w_out, w_gate, w_up, w_down):
    assert w_mod.shape[0] == 1, "single-layer kernel"
    bsz = x.shape[0]
    row = lambda v: v.reshape(1, -1)

    c_all = jnp.concatenate([c, c_ctx[None], jnp.zeros((MOD_ROWS - bsz - 1, D_MODEL), F32)], axis=0)
    wg_pad = jnp.zeros((TAIL_W, 2 * GLA_QK), F32)
    wg_pad = wg_pad.at[0:GLA_RANK, 0:GLA_QK].set(gla_wg_f[0])
    wg_pad = wg_pad.at[GLA_RANK:2 * GLA_RANK, GLA_QK:].set(gla_wg_b[0]).astype(BF16)
    bg_cat = jnp.concatenate([gla_bg_f[0], gla_bg_b[0]]).reshape(1, -1)
    lane_pad = lambda f, bk: jnp.zeros((1, TAIL_W), F32).at[0, DT_LANE:DT_LANE + 2 * SSD_HEADS].set(
        jnp.concatenate([f, bk]))
    dtb_pad = lane_pad(dt_bias_f[0], dt_bias_b[0])
    nega = lane_pad(-jnp.exp(a_log_f[0]) * LOG2_E, -jnp.exp(a_log_b[0]) * LOG2_E)
    conv_w9 = conv_w[0].reshape(CONV_K * CONV_K, SSD_CONV_DIM)

    m_all = _modulation(c_all, w_mod[0], row(b_mod[0]))
    qkv, rz, xbc, g, dt = _input_projection_conv(
        x, ctx, m_all, row(norm_mix_pre[0]), w_in[0].T, wg_pad, bg_cat, dtb_pad, conv_w9, row(conv_b[0]))
    ygf, ygb, ysf, ysb = _mixer_scan(qkv, g, xbc, dt, nega)
    return _merge_ffn(
        x, ygf, ygb, ysf, ysb, xbc, rz, m_all,
        row(jnp.tile(gla_norm[0], GLA_HEADS)), row(jnp.repeat(d_skip[0], SSD_P)), row(ssd_norm[0]),
        row(norm_mix_post[0]), row(norm_ffn_pre[0]), row(norm_ffn_post[0]),
        w_out[0], w_gate[0], w_up[0], w_down[0])
```

```python
import jax
import jax.numpy as jnp
from jax import lax
from jax.experimental import pallas as pl
from jax.experimental.pallas import tpu as pltpu

F32 = jnp.float32
BF16 = jnp.bfloat16

D_MODEL = 1024
CTX_LEN = 256
GRID_W = 64
EPS = 1e-6

GLA_HEADS = 4
GLA_DK = 64
GLA_DV = 128
GLA_QK = GLA_HEADS * GLA_DK
GLA_INNER = GLA_HEADS * GLA_DV
GLA_RANK = 16
GLA_GATE_NORM = 16.0

SSD_HEADS = 8
SSD_P = 64
SSD_N = 64
SSD_GROUPS = 2
SSD_HPG = SSD_HEADS // SSD_GROUPS
SSD_INNER = SSD_HEADS * SSD_P
SSD_BC = SSD_GROUPS * SSD_N
SSD_CONV_DIM = SSD_INNER + 2 * SSD_BC
CONV_K = 3

QKV_W = 2 * GLA_QK + GLA_INNER
COL_R = QKV_W
COL_Z = COL_R + GLA_INNER
COL_XBC = COL_Z + SSD_INNER
COL_TAIL = COL_XBC + SSD_CONV_DIM
TAIL_W = 128
PROJ_W = COL_TAIL + TAIL_W
DT_LANE = 2 * GLA_RANK
W_IN_LR = COL_Z
W_IN_Z = W_IN_LR + 2 * GLA_RANK
W_IN_DT = W_IN_Z + SSD_INNER + SSD_CONV_DIM
W_IN_COLS = W_IN_DT + 2 * SSD_HEADS

TM_PROJ = 1024
PROJ_SUB = 512
CONV_COLS = 128
MOD_ROWS = 8
MOD_TN = 1024
TM_FFN = 512
FFN_SUB = 128
W_CHUNK = 256
CHUNK = 128
SCAN_BATCH = 4
SUB = 32
NEG_BIG = -1e30
LOG2_E = 1.4426950408889634
VMEM_LIMIT = 56 * 1024 * 1024


def _silu(x):
    return x / (1.0 + jnp.exp2(x * -LOG2_E))


def _softplus(x):
    return jnp.maximum(x, 0.0) + jnp.log(1.0 + jnp.exp(-jnp.abs(x)))


def _rms(x):
    return x * lax.rsqrt(jnp.mean(x * x, axis=-1, keepdims=True) + EPS)


def _dot(a, b):
    return jnp.dot(a, b, preferred_element_type=F32)


def _dot_nt(a, b):
    return lax.dot_general(a, b, (((1,), (1,)), ((), ())), preferred_element_type=F32)


def _dot_tn(a, b):
    return lax.dot_general(a, b, (((0,), (0,)), ((), ())), preferred_element_type=F32)


def _mod_kernel(c_ref, w_ref, b_ref, o_ref):
    s = _silu(c_ref[...])
    o_ref[...] = _dot(s.astype(BF16), w_ref[...].astype(BF16)) + b_ref[...]


def _modulation(c_all, w_mod, b_mod):
    n = w_mod.shape[1]
    return pl.pallas_call(
        _mod_kernel,
        grid=(n // MOD_TN,),
        in_specs=[pl.BlockSpec((MOD_ROWS, D_MODEL), lambda j: (0, 0)),
                  pl.BlockSpec((D_MODEL, MOD_TN), lambda j: (0, j)),
                  pl.BlockSpec((1, MOD_TN), lambda j: (0, j))],
        out_specs=pl.BlockSpec((MOD_ROWS, MOD_TN), lambda j: (0, j)),
        out_shape=jax.ShapeDtypeStruct((MOD_ROWS, n), F32),
        name="adaln_mod",
    )(c_all, w_mod, b_mod)


def _permute_cast_w_in(wt_ref, w_s):
    blk = 128
    pad = jnp.zeros((TAIL_W - (W_IN_Z - W_IN_LR) - (W_IN_COLS - W_IN_DT), blk), F32)

    def body(kb, carry):
        k0 = pl.multiple_of(kb * blk, blk)
        cols = pl.ds(k0, blk)
        rows = pl.ds(k0, blk)
        w_s[rows, 0:W_IN_LR] = jnp.transpose(wt_ref[0:W_IN_LR, cols]).astype(BF16)
        w_s[rows, COL_Z:COL_TAIL] = jnp.transpose(wt_ref[W_IN_Z:W_IN_DT, cols]).astype(BF16)
        tail = jnp.concatenate([wt_ref[W_IN_LR:W_IN_Z, cols], wt_ref[W_IN_DT:W_IN_COLS, cols], pad],
                               axis=0)
        w_s[rows, COL_TAIL:PROJ_W] = jnp.transpose(tail).astype(BF16)
        return carry

    lax.fori_loop(0, D_MODEL // blk, body, 0)


def _grid_conv_rows(window, t0, is_ctx, w_ref, b_ref, cols):
    n = window.shape[0] - 2 * GRID_W
    ext = window.astype(F32)
    row_on = jnp.where(is_ctx, 0.0, 1.0)
    taps = []
    for dx in range(CONV_K):
        tap = None
        for dy in range(CONV_K):
            wk = w_ref[CONV_K * dy + dx:CONV_K * dy + dx + 1, cols]
            if dy != 1:
                wk = wk * row_on
            term = wk * ext[GRID_W * dy:GRID_W * dy + n, :]
            tap = term if tap is None else tap + term
        taps.append(tap)
    t = t0 + lax.broadcasted_iota(jnp.int32, (n, 1), 0)
    col = jnp.bitwise_and(t, GRID_W - 1)
    has_left = jnp.where(is_ctx, t, col) != 0
    has_right = jnp.where(is_ctx, t - (CTX_LEN - 1), col - (GRID_W - 1)) != 0
    acc = (taps[1] + b_ref[:, cols]
           + jnp.where(has_left, pltpu.roll(taps[0], 1, axis=0), 0.0)
           + jnp.where(has_right, pltpu.roll(taps[2], n - 1, axis=0), 0.0))
    return _silu(acc)


def _inproj_conv_kernel(x_ref, ctx_ref, m_ref, gpre_ref, w_ref, wg_ref, bg_ref, dtb_ref,
                        cw_ref, cb_ref, qkv_ref, rz_ref, xbc_ref, g_ref, dt_ref,
                        w_s, prev_s, above_s):
    b = pl.program_id(0)
    i = pl.program_id(1)
    n_tiles = pl.num_programs(1) - 1
    ctx_tile = n_tiles - 1

    @pl.when(jnp.logical_and(b == 0, i == 0))
    def _():
        _permute_cast_w_in(w_ref, w_s)

    @pl.when(i == 0)
    def _():
        prev_s[...] = jnp.zeros_like(prev_s)
        above_s[...] = jnp.zeros_like(above_s)

    j = i - 1
    split = TM_PROJ - GRID_W
    none = jnp.zeros((GRID_W, SSD_CONV_DIM), BF16)
    col_groups = [slice(c, c + CONV_COLS) for c in range(0, SSD_CONV_DIM, CONV_COLS)]
    pieces = [(r0, min(CTX_LEN, split - r0)) for r0 in range(0, split, CTX_LEN)]

    def conv_block(t0, n, cols, window, conv_ctx):
        out = _grid_conv_rows(window, t0, conv_ctx, cw_ref, cb_ref, cols)
        xbc_ref[0, t0:t0 + n, cols] = out.astype(BF16)

    def conv_previous_head(conv_ctx):
        has_above = jnp.logical_and(j >= 1, j < ctx_tile)
        for cols in col_groups:
            for r0, n in (pieces[:1] if conv_ctx else pieces):
                if r0 == 0:
                    above = jnp.where(has_above, above_s[:, cols], none[:, cols])
                    window = jnp.concatenate([above, prev_s[0:n + GRID_W, cols]], axis=0)
                else:
                    window = prev_s[r0 - GRID_W:r0 + n + GRID_W, cols]
                conv_block(r0, n, cols, window, conv_ctx)
        above_s[...] = prev_s[split:, :]

    def conv_previous_last_row(below):
        below = jnp.where(j < ctx_tile - 1, below, none)
        for cols in col_groups:
            conv_block(split, GRID_W, cols,
                       jnp.concatenate([prev_s[split - GRID_W:, cols], below[:, cols]], axis=0), False)

    def project_tile(is_ctx):
        mrow = m_ref[pl.ds(pl.num_programs(0) if is_ctx else b, 1), :]
        shift = mrow[:, 0:D_MODEL]
        gain = gpre_ref[...] * (1.0 + mrow[:, D_MODEL:2 * D_MODEL])

        def normed(rs):
            x = ctx_ref[0] if is_ctx else x_ref[0, rs, :]
            return (_rms(x) * gain + shift).astype(BF16)

        def project(rs, h):
            tail = _dot(h, w_s[:, COL_TAIL:PROJ_W])
            qkv = _dot(h, w_s[:, 0:QKV_W])
            qkv_ref[0, rs, 0:GLA_QK] = (qkv[:, 0:GLA_QK] * (GLA_DK ** -0.5)).astype(BF16)
            qkv_ref[0, rs, GLA_QK:QKV_W] = qkv[:, GLA_QK:QKV_W].astype(BF16)
            rz_ref[0, rs, :] = _silu(_dot(h, w_s[:, COL_R:COL_XBC])).astype(BF16)
            xbc_new = _dot(h, w_s[:, COL_XBC:COL_TAIL]).astype(BF16)
            logits = _dot(tail.astype(BF16), wg_ref[...]) + bg_ref[...]
            u = logits * LOG2_E
            g_ref[0, rs, :] = ((jnp.minimum(u, 0.0) - jnp.log2(1.0 + jnp.exp2(-jnp.abs(u))))
                               * (1.0 / GLA_GATE_NORM))
            dt_ref[0, rs, :] = _softplus(tail + dtb_ref[...])
            return xbc_new

        rows = CTX_LEN if is_ctx else TM_PROJ
        sub = min(PROJ_SUB, rows)
        subs = [slice(sub * s, sub * (s + 1)) for s in range(rows // sub)]
        conv_previous_head(False)
        ready = normed(subs[0])
        fresh = []
        for s, rs in enumerate(subs):
            following = normed(subs[s + 1]) if s + 1 < len(subs) else None
            fresh.append(project(rs, ready))
            if s == 0:
                conv_previous_last_row(fresh[0][0:GRID_W, :])
            ready = following
        for rs, xbc_new in zip(subs, fresh):
            prev_s[rs, :] = xbc_new

    @pl.when(i < ctx_tile)
    def _():
        project_tile(False)

    @pl.when(i == ctx_tile)
    def _():
        project_tile(True)

    @pl.when(i == n_tiles)
    def _():
        conv_previous_head(True)


def _input_projection_conv(x, ctx, m_all, gpre, w_in, wg_pad, bg_cat, dtb_pad, conv_w9, conv_b):
    bsz, seq, _ = x.shape
    n_lat = seq // TM_PROJ
    n_tiles = n_lat + 1
    tt = CTX_LEN + seq
    const = lambda b, i: (0, 0)
    tok = lambda b, i: (b, jnp.minimum(i, n_tiles - 1), 0)
    conv_tok = lambda b, i: (b, jnp.maximum(i - 1, 0), 0)
    return pl.pallas_call(
        _inproj_conv_kernel,
        grid=(bsz, n_tiles + 1),
        in_specs=[pl.BlockSpec((1, TM_PROJ, D_MODEL), lambda b, i: (b, jnp.minimum(i, n_lat - 1), 0)),
                  pl.BlockSpec((1, CTX_LEN, D_MODEL), lambda b, i: (b, 0, 0)),
                  pl.BlockSpec(m_all.shape, const),
                  pl.BlockSpec((1, D_MODEL), const),
                  pl.BlockSpec((W_IN_COLS, D_MODEL), const, pipeline_mode=pl.Buffered(1)),
                  pl.BlockSpec((TAIL_W, 2 * GLA_QK), const),
                  pl.BlockSpec((1, 2 * GLA_QK), const),
                  pl.BlockSpec((1, TAIL_W), const),
                  pl.BlockSpec((CONV_K * CONV_K, SSD_CONV_DIM), const),
                  pl.BlockSpec((1, SSD_CONV_DIM), const)],
        out_specs=[pl.BlockSpec((1, TM_PROJ, QKV_W), tok),
                   pl.BlockSpec((1, TM_PROJ, GLA_INNER + SSD_INNER), tok),
                   pl.BlockSpec((1, TM_PROJ, SSD_CONV_DIM), conv_tok),
                   pl.BlockSpec((1, TM_PROJ, 2 * GLA_QK), tok),
                   pl.BlockSpec((1, TM_PROJ, TAIL_W), tok)],
        out_shape=[jax.ShapeDtypeStruct((bsz, tt, QKV_W), BF16),
                   jax.ShapeDtypeStruct((bsz, tt, GLA_INNER + SSD_INNER), BF16),
                   jax.ShapeDtypeStruct((bsz, tt, SSD_CONV_DIM), BF16),
                   jax.ShapeDtypeStruct((bsz, tt, 2 * GLA_QK), F32),
                   jax.ShapeDtypeStruct((bsz, tt, TAIL_W), F32)],
        scratch_shapes=[pltpu.VMEM((D_MODEL, PROJ_W), BF16),
                        pltpu.VMEM((TM_PROJ, SSD_CONV_DIM), BF16),
                        pltpu.VMEM((GRID_W, SSD_CONV_DIM), BF16)],
        compiler_params=pltpu.CompilerParams(
            dimension_semantics=("arbitrary", "arbitrary"), vmem_limit_bytes=VMEM_LIMIT),
        name="in_proj_conv",
    )(x, ctx, m_all, gpre, w_in, wg_pad, bg_cat, dtb_pad, conv_w9, conv_b)


def _cumsum_rows(x, reverse):
    n = x.shape[0]
    row = lax.broadcasted_iota(jnp.int32, x.shape, 0)
    s = 1
    while s < n:
        if reverse:
            x = x + jnp.where(row < n - s, pltpu.roll(x, n - s, axis=0), 0.0)
        else:
            x = x + jnp.where(row >= s, pltpu.roll(x, s, axis=0), 0.0)
        s *= 2
    return x


def _chunk_maps(n_ctx, n_all):
    n_lat = n_all - n_ctx
    fwd = lambda s: jnp.where(s < n_ctx, n_lat + s, s - n_ctx)
    bwd = lambda s: n_all - 1 - s
    return fwd, bwd


def _gla_prepare(qkv, g, reverse):
    n = qkv.shape[0]
    nb = n // SUB
    blocks = range(nb)
    rows = lambda a, blk: a[SUB * blk:SUB * (blk + 1), :]
    q = qkv[:, 0:GLA_QK].astype(F32)
    k = qkv[:, GLA_QK:2 * GLA_QK].astype(F32)
    cum = _cumsum_rows(g, reverse)
    zero = jnp.zeros((1, GLA_QK), F32)
    if reverse:
        ends = [cum[SUB * blk:SUB * blk + 1, :] for blk in blocks]
        refs = ends[1:] + [zero]
        earlier = lambda j, blk: j > blk
        tot = ends[0]
    else:
        ends = [cum[SUB * (blk + 1) - 1:SUB * (blk + 1), :] for blk in blocks]
        refs = [zero] + ends[:-1]
        earlier = lambda j, blk: j < blk
        tot = ends[nb - 1]
    stack = lambda vs: jnp.concatenate([jnp.broadcast_to(v, (SUB, GLA_QK)) for v in vs], axis=0)
    e_in = cum - stack(refs)
    q_blk = q * jnp.exp2(e_in)
    k_diag = k * jnp.exp2(-e_in)
    k_end = k * jnp.exp2(stack(ends) - cum)
    q_in = jnp.concatenate([rows(q_blk, blk) * jnp.exp2(refs[blk]) for blk in blocks], axis=0)
    k_out = jnp.concatenate([rows(k_end, blk) * jnp.exp2(tot - ends[blk]) for blk in blocks], axis=0)
    q_in = q_in.astype(BF16)
    k_out = k_out.astype(BF16)
    q_bf = q_blk.astype(BF16)
    zeros_piece = jnp.zeros((SUB, GLA_QK), BF16)
    k_piece = [[rows(k_diag, j).astype(BF16) if seg == j
                else (rows(k_end, j) * jnp.exp2(refs[seg] - ends[j])).astype(BF16) if earlier(j, seg)
                else zeros_piece
                for seg in blocks] for j in blocks]
    ii = lax.broadcasted_iota(jnp.int32, (n, n), 0)
    jj = lax.broadcasted_iota(jnp.int32, (n, n), 1)
    heads = []
    for h in range(GLA_HEADS):
        ks = slice(GLA_DK * h, GLA_DK * (h + 1))
        q_cat = jnp.concatenate(
            [jnp.concatenate([rows(q_bf, blk)[:, ks] if seg == blk else zeros_piece[:, ks]
                              for seg in blocks], axis=1) for blk in blocks], axis=0)
        k_cat = jnp.concatenate(
            [jnp.concatenate([k_piece[j][seg][:, ks] for seg in blocks], axis=1)
             for j in blocks], axis=0)
        heads.append(dict(q_cat=q_cat, k_cat=k_cat, q_in=q_in[:, ks], k_out=k_out[:, ks],
                          v=qkv[:, 2 * GLA_QK + GLA_DV * h:2 * GLA_QK + GLA_DV * (h + 1)]))
    return dict(heads=heads, causal=(ii <= jj) if reverse else (ii >= jj),
                decay_col=jnp.transpose(jnp.broadcast_to(tot, (GLA_DV, GLA_QK))))


def _gla_chain(qkv_ref, g_ref, reverse, s_ref, y_ref):
    def start():
        p = _gla_prepare(qkv_ref[...], g_ref[...], reverse)
        return p, [_dot_nt(hd["q_cat"], hd["k_cat"]) for hd in p["heads"]]

    def finish(started):
        p, att = started
        outs = []
        for h, (hd, a) in enumerate(zip(p["heads"], att)):
            state = s_ref[h]
            lhs = jnp.concatenate([jnp.where(p["causal"], a, 0.0).astype(BF16), hd["q_in"]], axis=1)
            rhs = jnp.concatenate([hd["v"], state.astype(BF16)], axis=0)
            outs.append(_dot(lhs, rhs))
            ks = slice(GLA_DK * h, GLA_DK * (h + 1))
            s_ref[h] = state * jnp.exp2(p["decay_col"][ks, :]) + _dot_tn(hd["k_out"], hd["v"])
        y_ref[...] = jnp.concatenate(outs, axis=1).astype(y_ref.dtype)

    return start, finish


def _ssd_prepare(xbc, dt, nega, expand, lane0, reverse):
    n = xbc.shape[0]
    end = 0 if reverse else n - 1
    cum = _cumsum_rows(dt * nega, reverse)
    e_cum = jnp.exp2(cum)
    w = jnp.exp2(cum[end:end + 1, :] - cum) * dt
    e_cum_x = _dot(_split_hi_lo(e_cum), expand)
    x_w = xbc[:, 0:SSD_INNER] * _dot(w.astype(BF16), expand[0:TAIL_W, :]).astype(BF16)
    cum_t = jnp.transpose(cum)
    dt_t = jnp.transpose(dt).astype(BF16)
    ii = lax.broadcasted_iota(jnp.int32, (n, n), 0)
    jj = lax.broadcasted_iota(jnp.int32, (n, n), 1)
    causal = (ii <= jj) if reverse else (ii >= jj)
    first_head = lax.broadcasted_iota(jnp.int32, (n, 2 * SSD_P), 1) < SSD_P
    groups = []
    for grp in range(SSD_GROUPS):
        b_g = xbc[:, SSD_INNER + SSD_N * grp:SSD_INNER + SSD_N * (grp + 1)]
        c_g = xbc[:, SSD_INNER + SSD_BC + SSD_N * grp:SSD_INNER + SSD_BC + SSD_N * (grp + 1)]
        scores = _dot_nt(c_g, b_g).astype(BF16)
        pairs = []
        for pair in range(SSD_HPG // 2):
            m_pair = []
            for hh in range(2):
                lane = lane0 + SSD_HPG * grp + 2 * pair + hh
                cum_b = jnp.broadcast_to(cum[:, lane:lane + 1], (n, n))
                seg = jnp.where(causal, cum_b - cum_t[lane:lane + 1, :], NEG_BIG)
                m_pair.append(scores * jnp.exp2(seg).astype(BF16) * dt_t[lane:lane + 1, :])
            col = 2 * SSD_P * (SSD_HPG // 2 * grp + pair)
            x_pair = xbc[:, col:col + 2 * SSD_P]
            none = jnp.zeros_like(x_pair)
            rhs = jnp.concatenate([jnp.where(first_head, x_pair, none),
                                   jnp.where(first_head, none, x_pair)], axis=0)
            pairs.append((jnp.concatenate(m_pair, axis=1), rhs))
        cols = slice(SSD_HPG * SSD_P * grp, SSD_HPG * SSD_P * (grp + 1))
        groups.append(dict(b=b_g, c=c_g, pairs=pairs, e_cum=e_cum_x[:, cols],
                           e_tot=e_cum_x[end:end + 1, cols], x_w=x_w[:, cols]))
    return groups


def _split_hi_lo(a):
    hi = a.astype(BF16)
    lo = (a - hi.astype(F32)).astype(BF16)
    return jnp.concatenate([hi, lo], axis=1)


def _ssd_chain(xbc_ref, dt_ref, nega, expand_ref, d, s_ref, y_ref):
    def start():
        return _ssd_prepare(xbc_ref[...], dt_ref[...], nega, expand_ref[d],
                            DT_LANE + SSD_HEADS * d, d == 1)

    def finish(groups):
        outs = []
        for grp, gd in enumerate(groups):
            state = s_ref[grp]
            intra = jnp.concatenate([_dot(m, rhs) for m, rhs in gd["pairs"]], axis=1)
            outs.append(_dot(gd["c"], state.astype(BF16)) * gd["e_cum"] + intra)
            s_ref[grp] = state * gd["e_tot"] + _dot_tn(gd["b"], gd["x_w"])
        y_ref[...] = jnp.concatenate(outs, axis=1).astype(y_ref.dtype)

    return start, finish


def _mixer_scan_kernel(qkv_f_ref, g_f_ref, qkv_b_ref, g_b_ref, xbc_f_ref, dt_f_ref, xbc_b_ref,
                       dt_b_ref, nega_ref, expand_ref, ygf_ref, ygb_ref, ysf_ref, ysb_ref,
                       gsf_ref, gsb_ref, ssf_ref, ssb_ref):
    @pl.when(pl.program_id(1) == 0)
    def _():
        for s_ref in (gsf_ref, gsb_ref, ssf_ref, ssb_ref):
            s_ref[...] = jnp.zeros_like(s_ref)

    nega = nega_ref[...]
    chains = []
    for e in range(qkv_f_ref.shape[0]):
        chains.append(_gla_chain(qkv_f_ref.at[e], g_f_ref.at[e], False, gsf_ref.at[e], ygf_ref.at[e]))
        chains.append(_ssd_chain(xbc_f_ref.at[e], dt_f_ref.at[e], nega, expand_ref, 0,
                                 ssf_ref.at[e], ysf_ref.at[e]))
        chains.append(_gla_chain(qkv_b_ref.at[e], g_b_ref.at[e], True, gsb_ref.at[e], ygb_ref.at[e]))
        chains.append(_ssd_chain(xbc_b_ref.at[e], dt_b_ref.at[e], nega, expand_ref, 1,
                                 ssb_ref.at[e], ysb_ref.at[e]))
    started = chains[0][0]()
    for i, (_, finish) in enumerate(chains):
        following = chains[i + 1][0]() if i + 1 < len(chains) else None
        finish(started)
        started = following


def _head_expand_matrix():
    r = jnp.arange(2 * TAIL_W)[:, None] % TAIL_W
    c = jnp.arange(SSD_INNER)[None, :] // SSD_P
    return jnp.stack([(r == DT_LANE + SSD_HEADS * d + c) for d in range(2)]).astype(BF16)


def _mixer_scan(qkv, g, xbc, dt, nega):
    bsz, tt, _ = qkv.shape
    n_all = tt // CHUNK
    fwd, bwd = _chunk_maps(CTX_LEN // CHUNK, n_all)
    spec = lambda w, m, c=0: pl.BlockSpec((SCAN_BATCH, CHUNK, w), lambda b, s: (b, m(s), c))
    expand = _head_expand_matrix()
    return pl.pallas_call(
        _mixer_scan_kernel,
        grid=(bsz // SCAN_BATCH, n_all),
        in_specs=[spec(QKV_W, fwd), spec(GLA_QK, fwd), spec(QKV_W, bwd), spec(GLA_QK, bwd, 1),
                  spec(SSD_CONV_DIM, fwd), spec(TAIL_W, fwd), spec(SSD_CONV_DIM, bwd), spec(TAIL_W, bwd),
                  pl.BlockSpec((1, TAIL_W), lambda b, s: (0, 0)),
                  pl.BlockSpec(expand.shape, lambda b, s: (0, 0, 0))],
        out_specs=[spec(GLA_INNER, fwd), spec(GLA_INNER, bwd), spec(SSD_INNER, fwd), spec(SSD_INNER, bwd)],
        out_shape=[jax.ShapeDtypeStruct((bsz, tt, GLA_INNER), BF16)] * 2
        + [jax.ShapeDtypeStruct((bsz, tt, SSD_INNER), BF16)] * 2,
        scratch_shapes=[pltpu.VMEM((SCAN_BATCH, GLA_HEADS, GLA_DK, GLA_DV), F32)] * 2
        + [pltpu.VMEM((SCAN_BATCH, SSD_GROUPS, SSD_N, SSD_HPG * SSD_P), F32)] * 2,
        compiler_params=pltpu.CompilerParams(
            dimension_semantics=("parallel", "arbitrary"), vmem_limit_bytes=VMEM_LIMIT),
        name="mixer_scan",
    )(qkv, g, qkv, g, xbc, dt, xbc, dt, nega, expand)


def _stream_cast_weights(pairs, stage, sem):
    def chunk_copy(w_hbm, c, slot):
        cols = w_hbm.shape[1]
        return pltpu.make_async_copy(w_hbm.at[pl.ds(c * W_CHUNK, W_CHUNK), :],
                                     stage.at[slot, :, 0:cols], sem.at[slot])

    chunk_copy(pairs[0][0], 0, 0).start()
    first = 0
    for k, (w_hbm, w_bf) in enumerate(pairs):
        n_chunks, rest = divmod(w_hbm.shape[0], W_CHUNK)
        assert rest == 0
        cols = w_hbm.shape[1]
        following = pairs[k + 1][0] if k + 1 < len(pairs) else None

        def body(c, carry, w_hbm=w_hbm, w_bf=w_bf, n_chunks=n_chunks, cols=cols,
                 following=following, first=first):
            slot = jnp.bitwise_and(first + c, 1)

            @pl.when(c + 1 < n_chunks)
            def _():
                chunk_copy(w_hbm, c + 1, 1 - slot).start()

            if following is not None:
                @pl.when(c + 1 == n_chunks)
                def _():
                    chunk_copy(following, 0, 1 - slot).start()

            chunk_copy(w_hbm, c, slot).wait()
            rows = pl.ds(pl.multiple_of(c * W_CHUNK, W_CHUNK), W_CHUNK)
            w_bf[rows, :] = stage[slot, :, 0:cols].astype(BF16)
            return carry

        lax.fori_loop(0, n_chunks, body, 0)
        first += n_chunks


def _merge_ffn_kernel(x_ref, ygf_ref, ygb_ref, ysf_ref, ysb_ref, xs_ref, rz_ref, m_ref,
                      gnorm_ref, dskip_ref, snorm_ref, npost_ref, nfpre_ref, nfpost_ref,
                      wout_hbm, wgate_hbm, wup_hbm, wdown_hbm, o_ref,
                      wout_ref, wgate_ref, wup_ref, wdown_ref, stage, sem):
    b = pl.program_id(0)

    @pl.when(jnp.logical_and(b == 0, pl.program_id(1) == 0))
    def _():
        _stream_cast_weights([(wout_hbm, wout_ref), (wgate_hbm, wgate_ref),
                              (wup_hbm, wup_ref), (wdown_hbm, wdown_ref)], stage, sem)

    mrow = m_ref[pl.ds(b, 1), :]
    mod = [mrow[:, D_MODEL * j:D_MODEL * (j + 1)] for j in range(6)]
    gw = SSD_INNER // SSD_GROUPS

    def merged(rs):
        y_gla = ygf_ref[0, rs, :].astype(F32) + ygb_ref[0, rs, :].astype(F32)
        o_gla = jnp.concatenate(
            [_rms(y_gla[:, GLA_DV * h:GLA_DV * (h + 1)]) for h in range(GLA_HEADS)], axis=1)
        o_gla = o_gla * gnorm_ref[...] * rz_ref[0, rs, 0:GLA_INNER].astype(F32)
        y_ssd = ysf_ref[0, rs, :].astype(F32) + ysb_ref[0, rs, :].astype(F32)
        y_ssd = ((y_ssd + dskip_ref[...] * xs_ref[0, rs, :].astype(F32))
                 * rz_ref[0, rs, GLA_INNER:].astype(F32))
        o_ssd = jnp.concatenate(
            [_rms(y_ssd[:, gw * g:gw * (g + 1)]) for g in range(SSD_GROUPS)], axis=1)
        o_ssd = o_ssd * snorm_ref[...]
        return jnp.concatenate([o_gla, o_ssd], axis=1).astype(BF16)

    def residual_in(rs, o):
        y = _dot(o, wout_ref[...])
        x1 = x_ref[0, rs, :] + mod[2] * (_rms(y) * npost_ref[...])
        h = ((_rms(x1) * nfpre_ref[...]) * (1.0 + mod[4]) + mod[3]).astype(BF16)
        return x1, h

    subs = [slice(FFN_SUB * j, FFN_SUB * (j + 1)) for j in range(TM_FFN // FFN_SUB)]
    x1, h = residual_in(subs[0], merged(subs[0]))
    for j, rs in enumerate(subs):
        following = residual_in(subs[j + 1], merged(subs[j + 1])) if j + 1 < len(subs) else None
        act = (_silu(_dot(h, wgate_ref[...])) * _dot(h, wup_ref[...])).astype(BF16)
        f = _dot(act, wdown_ref[...])
        o_ref[0, rs, :] = x1 + mod[5] * (_rms(f) * nfpost_ref[...])
        if following is not None:
            x1, h = following


def _merge_ffn(x, ygf, ygb, ysf, ysb, xbc, rz, m_all, gnorm, dskip, snorm, npost, nfpre,
               nfpost, wout, wgate, wup, wdown):
    bsz, seq, _ = x.shape
    n_tiles = seq // TM_FFN
    const = lambda b, i: (0, 0)

    def lat(width):
        return pl.BlockSpec((1, TM_FFN, width), lambda b, i: (b, i, 0))

    weights = (wout, wgate, wup, wdown)
    in_hbm = pl.BlockSpec(memory_space=pl.ANY)
    vec = lambda w: pl.BlockSpec((1, w), const)
    return pl.pallas_call(
        _merge_ffn_kernel,
        grid=(bsz, n_tiles),
        in_specs=[pl.BlockSpec((1, TM_FFN, D_MODEL), lambda b, i: (b, i, 0)),
                  lat(GLA_INNER), lat(GLA_INNER), lat(SSD_INNER), lat(SSD_INNER),
                  lat(SSD_INNER), lat(GLA_INNER + SSD_INNER),
                  pl.BlockSpec(m_all.shape, const),
                  vec(GLA_INNER), vec(SSD_INNER), vec(SSD_INNER),
                  vec(D_MODEL), vec(D_MODEL), vec(D_MODEL),
                  in_hbm, in_hbm, in_hbm, in_hbm],
        out_specs=pl.BlockSpec((1, TM_FFN, D_MODEL), lambda b, i: (b, i, 0)),
        out_shape=jax.ShapeDtypeStruct((bsz, seq, D_MODEL), F32),
        scratch_shapes=[pltpu.VMEM(w.shape, BF16) for w in weights]
        + [pltpu.VMEM((2, W_CHUNK, max(w.shape[1] for w in weights)), F32),
           pltpu.SemaphoreType.DMA((2,))],
        compiler_params=pltpu.CompilerParams(
            dimension_semantics=("arbitrary", "arbitrary"), vmem_limit_bytes=VMEM_LIMIT),
        name="merge_ffn",
    )(x, ygf, ygb, ysf, ysb, xbc, rz, m_all, gnorm, dskip, snorm, npost, nfpre, nfpost,
      wout, wgate, wup, wdown)


def kernel(x, c, ctx, c_ctx, w_mod, b_mod, norm_mix_pre, norm_mix_post, norm_ffn_pre, norm_ffn_post,
           w_in, conv_w, conv_b, gla_wg_f, gla_bg_f, gla_wg_b, gla_bg_b, gla_norm,
           a_log_f, a_log_b, dt_bias_f, dt_bias_b, d_skip, ssd_norm, w_out, w_gate, w_up, w_down):
    assert w_mod.shape[0] == 1, "single-layer kernel"
    bsz = x.shape[0]
    row = lambda v: v.reshape(1, -1)

    c_all = jnp.concatenate([c, c_ctx[None], jnp.zeros((MOD_ROWS - bsz - 1, D_MODEL), F32)], axis=0)
    wg_pad = jnp.zeros((TAIL_W, 2 * GLA_QK), F32)
    wg_pad = wg_pad.at[0:GLA_RANK, 0:GLA_QK].set(gla_wg_f[0])
    wg_pad = wg_pad.at[GLA_RANK:2 * GLA_RANK, GLA_QK:].set(gla_wg_b[0]).astype(BF16)
    bg_cat = jnp.concatenate([gla_bg_f[0], gla_bg_b[0]]).reshape(1, -1)
    lane_pad = lambda f, bk: jnp.zeros((1, TAIL_W), F32).at[0, DT_LANE:DT_LANE + 2 * SSD_HEADS].set(
        jnp.concatenate([f, bk]))
    dtb_pad = lane_pad(dt_bias_f[0], dt_bias_b[0])
    nega = lane_pad(-jnp.exp(a_log_f[0]) * LOG2_E, -jnp.exp(a_log_b[0]) * LOG2_E)
    conv_w9 = conv_w[0].reshape(CONV_K * CONV_K, SSD_CONV_DIM)

    m_all = _modulation(c_all, w_mod[0], row(b_mod[0]))
    qkv, rz, xbc, g, dt = _input_projection_conv(
        x, ctx, m_all, row(norm_mix_pre[0]), w_in[0].T, wg_pad, bg_cat, dtb_pad, conv_w9, row(conv_b[0]))
    ygf, ygb, ysf, ysb = _mixer_scan(qkv, g, xbc, dt, nega)
    return _merge_ffn(
        x, ygf, ygb, ysf, ysb, xbc, rz, m_all,
        row(jnp.tile(gla_norm[0], GLA_HEADS)), row(jnp.repeat(d_skip[0], SSD_P)), row(ssd_norm[0]),
        row(norm_mix_post[0]), row(norm_ffn_pre[0]), row(norm_ffn_post[0]),
        w_out[0], w_gate[0], w_up[0], w_down[0])
```

```python
import jax
import jax.numpy as jnp
from jax import lax
from jax.experimental import pallas as pl
from jax.experimental.pallas import tpu as pltpu

F32 = jnp.float32
BF16 = jnp.bfloat16

D_MODEL = 1024
CTX_LEN = 256
GRID_W = 64
EPS = 1e-6

GLA_HEADS = 4
GLA_DK = 64
GLA_DV = 128
GLA_QK = GLA_HEADS * GLA_DK
GLA_INNER = GLA_HEADS * GLA_DV
GLA_RANK = 16
GLA_GATE_NORM = 16.0

SSD_HEADS = 8
SSD_P = 64
SSD_N = 64
SSD_GROUPS = 2
SSD_HPG = SSD_HEADS // SSD_GROUPS
SSD_INNER = SSD_HEADS * SSD_P
SSD_BC = SSD_GROUPS * SSD_N
SSD_CONV_DIM = SSD_INNER + 2 * SSD_BC
CONV_K = 3

QKV_W = 2 * GLA_QK + GLA_INNER
COL_R = QKV_W
COL_Z = COL_R + GLA_INNER
COL_XBC = COL_Z + SSD_INNER
COL_TAIL = COL_XBC + SSD_CONV_DIM
TAIL_W = 128
PROJ_W = COL_TAIL + TAIL_W
DT_LANE = 2 * GLA_RANK
W_IN_LR = COL_Z
W_IN_Z = W_IN_LR + 2 * GLA_RANK
W_IN_DT = W_IN_Z + SSD_INNER + SSD_CONV_DIM
W_IN_COLS = W_IN_DT + 2 * SSD_HEADS

TM_PROJ = 1024
PROJ_SUB = 256
CONV_COLS = 128
MOD_ROWS = 8
MOD_TN = 1024
TM_FFN = 512
FFN_SUB = 128
W_CHUNK = 256
CHUNK = 128
SCAN_BATCH = 4
SUB = 32
NEG_BIG = -1e30
LOG2_E = 1.4426950408889634
VMEM_LIMIT = 56 * 1024 * 1024


def _silu(x):
    return x / (1.0 + jnp.exp2(x * -LOG2_E))


def _softplus(x):
    return jnp.maximum(x, 0.0) + jnp.log(1.0 + jnp.exp(-jnp.abs(x)))


def _rms(x):
    return x * lax.rsqrt(jnp.mean(x * x, axis=-1, keepdims=True) + EPS)


def _dot(a, b):
    return jnp.dot(a, b, preferred_element_type=F32)


def _dot_nt(a, b):
    return lax.dot_general(a, b, (((1,), (1,)), ((), ())), preferred_element_type=F32)


def _dot_tn(a, b):
    return lax.dot_general(a, b, (((0,), (0,)), ((), ())), preferred_element_type=F32)


def _mod_kernel(c_ref, w_ref, b_ref, o_ref):
    s = _silu(c_ref[...])
    o_ref[...] = _dot(s.astype(BF16), w_ref[...].astype(BF16)) + b_ref[...]


def _modulation(c_all, w_mod, b_mod):
    n = w_mod.shape[1]
    return pl.pallas_call(
        _mod_kernel,
        grid=(n // MOD_TN,),
        in_specs=[pl.BlockSpec((MOD_ROWS, D_MODEL), lambda j: (0, 0)),
                  pl.BlockSpec((D_MODEL, MOD_TN), lambda j: (0, j)),
                  pl.BlockSpec((1, MOD_TN), lambda j: (0, j))],
        out_specs=pl.BlockSpec((MOD_ROWS, MOD_TN), lambda j: (0, j)),
        out_shape=jax.ShapeDtypeStruct((MOD_ROWS, n), F32),
        name="adaln_mod",
    )(c_all, w_mod, b_mod)


def _permute_cast_w_in(wt_ref, w_s):
    blk = 128
    pad = jnp.zeros((TAIL_W - (W_IN_Z - W_IN_LR) - (W_IN_COLS - W_IN_DT), blk), F32)

    def body(kb, carry):
        k0 = pl.multiple_of(kb * blk, blk)
        cols = pl.ds(k0, blk)
        rows = pl.ds(k0, blk)
        w_s[rows, 0:W_IN_LR] = jnp.transpose(wt_ref[0:W_IN_LR, cols]).astype(BF16)
        w_s[rows, COL_Z:COL_TAIL] = jnp.transpose(wt_ref[W_IN_Z:W_IN_DT, cols]).astype(BF16)
        tail = jnp.concatenate([wt_ref[W_IN_LR:W_IN_Z, cols], wt_ref[W_IN_DT:W_IN_COLS, cols], pad],
                               axis=0)
        w_s[rows, COL_TAIL:PROJ_W] = jnp.transpose(tail).astype(BF16)
        return carry

    lax.fori_loop(0, D_MODEL // blk, body, 0)


def _grid_conv_rows(window, t0, is_ctx, w_ref, b_ref, cols):
    n = window.shape[0] - 2 * GRID_W
    ext = window.astype(F32)
    row_on = jnp.where(is_ctx, 0.0, 1.0)
    taps = []
    for dx in range(CONV_K):
        tap = None
        for dy in range(CONV_K):
            wk = w_ref[CONV_K * dy + dx:CONV_K * dy + dx + 1, cols]
            if dy != 1:
                wk = wk * row_on
            term = wk * ext[GRID_W * dy:GRID_W * dy + n, :]
            tap = term if tap is None else tap + term
        taps.append(tap)
    t = t0 + lax.broadcasted_iota(jnp.int32, (n, 1), 0)
    col = jnp.bitwise_and(t, GRID_W - 1)
    has_left = jnp.where(is_ctx, t, col) != 0
    has_right = jnp.where(is_ctx, t - (CTX_LEN - 1), col - (GRID_W - 1)) != 0
    acc = (taps[1] + b_ref[:, cols]
           + jnp.where(has_left, pltpu.roll(taps[0], 1, axis=0), 0.0)
           + jnp.where(has_right, pltpu.roll(taps[2], n - 1, axis=0), 0.0))
    return _silu(acc)


def _inproj_conv_kernel(x_ref, ctx_ref, m_ref, gpre_ref, w_ref, wg_ref, bg_ref, dtb_ref,
                        cw_ref, cb_ref, qkv_ref, rz_ref, xbc_ref, g_ref, dt_ref,
                        w_s, prev_s, above_s):
    b = pl.program_id(0)
    i = pl.program_id(1)
    n_tiles = pl.num_programs(1) - 1
    ctx_tile = n_tiles - 1

    @pl.when(jnp.logical_and(b == 0, i == 0))
    def _():
        _permute_cast_w_in(w_ref, w_s)

    @pl.when(i == 0)
    def _():
        prev_s[...] = jnp.zeros_like(prev_s)
        above_s[...] = jnp.zeros_like(above_s)

    j = i - 1
    split = TM_PROJ - GRID_W
    none = jnp.zeros((GRID_W, SSD_CONV_DIM), BF16)
    col_groups = [slice(c, c + CONV_COLS) for c in range(0, SSD_CONV_DIM, CONV_COLS)]
    pieces = [(r0, min(CTX_LEN, split - r0)) for r0 in range(0, split, CTX_LEN)]

    def conv_block(t0, n, cols, window, conv_ctx):
        out = _grid_conv_rows(window, t0, conv_ctx, cw_ref, cb_ref, cols)
        xbc_ref[0, t0:t0 + n, cols] = out.astype(BF16)

    def conv_previous_head(conv_ctx):
        has_above = jnp.logical_and(j >= 1, j < ctx_tile)
        for cols in col_groups:
            for r0, n in (pieces[:1] if conv_ctx else pieces):
                if r0 == 0:
                    above = jnp.where(has_above, above_s[:, cols], none[:, cols])
                    window = jnp.concatenate([above, prev_s[0:n + GRID_W, cols]], axis=0)
                else:
                    window = prev_s[r0 - GRID_W:r0 + n + GRID_W, cols]
                conv_block(r0, n, cols, window, conv_ctx)
        above_s[...] = prev_s[split:, :]

    def conv_previous_last_row(below):
        below = jnp.where(j < ctx_tile - 1, below, none)
        for cols in col_groups:
            conv_block(split, GRID_W, cols,
                       jnp.concatenate([prev_s[split - GRID_W:, cols], below[:, cols]], axis=0), False)

    def project_tile(is_ctx, with_conv=True):
        mrow = m_ref[pl.ds(pl.num_programs(0) if is_ctx else b, 1), :]
        shift = mrow[:, 0:D_MODEL]
        gain = gpre_ref[...] * (1.0 + mrow[:, D_MODEL:2 * D_MODEL])

        def normed(rs):
            x = ctx_ref[0] if is_ctx else x_ref[0, rs, :]
            return (_rms(x) * gain + shift).astype(BF16)

        def project(rs, h):
            tail = _dot(h, w_s[:, COL_TAIL:PROJ_W])
            qkv = _dot(h, w_s[:, 0:QKV_W])
            qkv_ref[0, rs, 0:GLA_QK] = (qkv[:, 0:GLA_QK] * (GLA_DK ** -0.5)).astype(BF16)
            qkv_ref[0, rs, GLA_QK:QKV_W] = qkv[:, GLA_QK:QKV_W].astype(BF16)
            rz_ref[0, rs, :] = _silu(_dot(h, w_s[:, COL_R:COL_XBC])).astype(BF16)
            xbc_new = _dot(h, w_s[:, COL_XBC:COL_TAIL]).astype(BF16)
            logits = _dot(tail.astype(BF16), wg_ref[...]) + bg_ref[...]
            u = logits * LOG2_E
            g_ref[0, rs, :] = ((jnp.minimum(u, 0.0) - jnp.log2(1.0 + jnp.exp2(-jnp.abs(u))))
                               * (1.0 / GLA_GATE_NORM))
            dt_ref[0, rs, :] = _softplus(tail + dtb_ref[...])
            return xbc_new

        rows = CTX_LEN if is_ctx else TM_PROJ
        subs = [slice(PROJ_SUB * s, PROJ_SUB * (s + 1)) for s in range(rows // PROJ_SUB)]
        if with_conv:
            conv_previous_head(False)
        ready = normed(subs[0])
        fresh = []
        for s, rs in enumerate(subs):
            following = normed(subs[s + 1]) if s + 1 < len(subs) else None
            fresh.append(project(rs, ready))
            if s == 0 and with_conv:
                conv_previous_last_row(fresh[0][0:GRID_W, :])
            ready = following
        for rs, xbc_new in zip(subs, fresh):
            prev_s[rs, :] = xbc_new

    @pl.when(i == 0)
    def _():
        project_tile(False, with_conv=False)

    @pl.when(jnp.logical_and(i > 0, i < ctx_tile))
    def _():
        project_tile(False)

    @pl.when(i == ctx_tile)
    def _():
        project_tile(True)

    @pl.when(i == n_tiles)
    def _():
        conv_previous_head(True)


def _input_projection_conv(x, ctx, m_all, gpre, w_in, wg_pad, bg_cat, dtb_pad, conv_w9, conv_b):
    bsz, seq, _ = x.shape
    n_lat = seq // TM_PROJ
    n_tiles = n_lat + 1
    tt = CTX_LEN + seq
    const = lambda b, i: (0, 0)
    tok = lambda b, i: (b, jnp.minimum(i, n_tiles - 1), 0)
    conv_tok = lambda b, i: (b, jnp.maximum(i - 1, 0), 0)
    return pl.pallas_call(
        _inproj_conv_kernel,
        grid=(bsz, n_tiles + 1),
        in_specs=[pl.BlockSpec((1, TM_PROJ, D_MODEL), lambda b, i: (b, jnp.minimum(i, n_lat - 1), 0)),
                  pl.BlockSpec((1, CTX_LEN, D_MODEL), lambda b, i: (b, 0, 0)),
                  pl.BlockSpec(m_all.shape, const),
                  pl.BlockSpec((1, D_MODEL), const),
                  pl.BlockSpec((W_IN_COLS, D_MODEL), const, pipeline_mode=pl.Buffered(1)),
                  pl.BlockSpec((TAIL_W, 2 * GLA_QK), const),
                  pl.BlockSpec((1, 2 * GLA_QK), const),
                  pl.BlockSpec((1, TAIL_W), const),
                  pl.BlockSpec((CONV_K * CONV_K, SSD_CONV_DIM), const),
                  pl.BlockSpec((1, SSD_CONV_DIM), const)],
        out_specs=[pl.BlockSpec((1, TM_PROJ, QKV_W), tok),
                   pl.BlockSpec((1, TM_PROJ, GLA_INNER + SSD_INNER), tok),
                   pl.BlockSpec((1, TM_PROJ, SSD_CONV_DIM), conv_tok),
                   pl.BlockSpec((1, TM_PROJ, 2 * GLA_QK), tok),
                   pl.BlockSpec((1, TM_PROJ, TAIL_W), tok)],
        out_shape=[jax.ShapeDtypeStruct((bsz, tt, QKV_W), BF16),
                   jax.ShapeDtypeStruct((bsz, tt, GLA_INNER + SSD_INNER), BF16),
                   jax.ShapeDtypeStruct((bsz, tt, SSD_CONV_DIM), BF16),
                   jax.ShapeDtypeStruct((bsz, tt, 2 * GLA_QK), F32),
                   jax.ShapeDtypeStruct((bsz, tt, TAIL_W), F32)],
        scratch_shapes=[pltpu.VMEM((D_MODEL, PROJ_W), BF16),
                        pltpu.VMEM((TM_PROJ, SSD_CONV_DIM), BF16),
                        pltpu.VMEM((GRID_W, SSD_CONV_DIM), BF16)],
        compiler_params=pltpu.CompilerParams(
            dimension_semantics=("arbitrary", "arbitrary"), vmem_limit_bytes=VMEM_LIMIT),
        name="in_proj_conv",
    )(x, ctx, m_all, gpre, w_in, wg_pad, bg_cat, dtb_pad, conv_w9, conv_b)


def _cumsum_rows(x, reverse):
    n = x.shape[0]
    row = lax.broadcasted_iota(jnp.int32, x.shape, 0)
    s = 1
    while s < n:
        if reverse:
            x = x + jnp.where(row < n - s, pltpu.roll(x, n - s, axis=0), 0.0)
        else:
            x = x + jnp.where(row >= s, pltpu.roll(x, s, axis=0), 0.0)
        s *= 2
    return x


def _chunk_maps(n_ctx, n_all):
    n_lat = n_all - n_ctx
    fwd = lambda s: jnp.where(s < n_ctx, n_lat + s, s - n_ctx)
    bwd = lambda s: n_all - 1 - s
    return fwd, bwd


def _gla_prepare(qkv, g, reverse):
    n = qkv.shape[0]
    nb = n // SUB
    blocks = range(nb)
    rows = lambda a, blk: a[SUB * blk:SUB * (blk + 1), :]
    q = qkv[:, 0:GLA_QK].astype(F32)
    k = qkv[:, GLA_QK:2 * GLA_QK].astype(F32)
    cum = _cumsum_rows(g, reverse)
    zero = jnp.zeros((1, GLA_QK), F32)
    if reverse:
        ends = [cum[SUB * blk:SUB * blk + 1, :] for blk in blocks]
        refs = ends[1:] + [zero]
        earlier = lambda j, blk: j > blk
        tot = ends[0]
    else:
        ends = [cum[SUB * (blk + 1) - 1:SUB * (blk + 1), :] for blk in blocks]
        refs = [zero] + ends[:-1]
        earlier = lambda j, blk: j < blk
        tot = ends[nb - 1]
    stack = lambda vs: jnp.concatenate([jnp.broadcast_to(v, (SUB, GLA_QK)) for v in vs], axis=0)
    e_in = cum - stack(refs)
    q_blk = q * jnp.exp2(e_in)
    k_diag = k * jnp.exp2(-e_in)
    k_end = k * jnp.exp2(stack(ends) - cum)
    q_in = jnp.concatenate([rows(q_blk, blk) * jnp.exp2(refs[blk]) for blk in blocks], axis=0)
    k_out = jnp.concatenate([rows(k_end, blk) * jnp.exp2(tot - ends[blk]) for blk in blocks], axis=0)
    q_in = q_in.astype(BF16)
    k_out = k_out.astype(BF16)
    q_bf = q_blk.astype(BF16)
    zeros_piece = jnp.zeros((SUB, GLA_QK), BF16)
    k_piece = [[rows(k_diag, j).astype(BF16) if seg == j
                else (rows(k_end, j) * jnp.exp2(refs[seg] - ends[j])).astype(BF16) if earlier(j, seg)
                else zeros_piece
                for seg in blocks] for j in blocks]
    ii = lax.broadcasted_iota(jnp.int32, (n, n), 0)
    jj = lax.broadcasted_iota(jnp.int32, (n, n), 1)
    heads = []
    for h in range(GLA_HEADS):
        ks = slice(GLA_DK * h, GLA_DK * (h + 1))
        q_cat = jnp.concatenate(
            [jnp.concatenate([rows(q_bf, blk)[:, ks] if seg == blk else zeros_piece[:, ks]
                              for seg in blocks], axis=1) for blk in blocks], axis=0)
        k_cat = jnp.concatenate(
            [jnp.concatenate([k_piece[j][seg][:, ks] for seg in blocks], axis=1)
             for j in blocks], axis=0)
        heads.append(dict(q_cat=q_cat, k_cat=k_cat, q_in=q_in[:, ks], k_out=k_out[:, ks],
                          v=qkv[:, 2 * GLA_QK + GLA_DV * h:2 * GLA_QK + GLA_DV * (h + 1)]))
    return dict(heads=heads, causal=(ii <= jj) if reverse else (ii >= jj),
                decay_col=jnp.transpose(jnp.broadcast_to(tot, (GLA_DV, GLA_QK))))


def _gla_chain(qkv_ref, g_ref, reverse, s_ref, y_ref):
    def start():
        p = _gla_prepare(qkv_ref[...], g_ref[...], reverse)
        return p, [_dot_nt(hd["q_cat"], hd["k_cat"]) for hd in p["heads"]]

    def finish(started):
        p, att = started
        outs = []
        for h, (hd, a) in enumerate(zip(p["heads"], att)):
            state = s_ref[h]
            lhs = jnp.concatenate([jnp.where(p["causal"], a, 0.0).astype(BF16), hd["q_in"]], axis=1)
            rhs = jnp.concatenate([hd["v"], state.astype(BF16)], axis=0)
            outs.append(_dot(lhs, rhs))
            ks = slice(GLA_DK * h, GLA_DK * (h + 1))
            s_ref[h] = state * jnp.exp2(p["decay_col"][ks, :]) + _dot_tn(hd["k_out"], hd["v"])
        y_ref[...] = jnp.concatenate(outs, axis=1).astype(y_ref.dtype)

    return start, finish


def _ssd_prepare(xbc, dt, nega, expand, lane0, reverse):
    n = xbc.shape[0]
    end = 0 if reverse else n - 1
    cum = _cumsum_rows(dt * nega, reverse)
    e_cum = jnp.exp2(cum)
    w = jnp.exp2(cum[end:end + 1, :] - cum) * dt
    e_cum_x = _dot(_split_hi_lo(e_cum), expand)
    x_w = xbc[:, 0:SSD_INNER] * _dot(w.astype(BF16), expand[0:TAIL_W, :]).astype(BF16)
    cum_t = jnp.transpose(cum)
    dt_t = jnp.transpose(dt).astype(BF16)
    ii = lax.broadcasted_iota(jnp.int32, (n, n), 0)
    jj = lax.broadcasted_iota(jnp.int32, (n, n), 1)
    causal = (ii <= jj) if reverse else (ii >= jj)
    first_head = lax.broadcasted_iota(jnp.int32, (n, 2 * SSD_P), 1) < SSD_P
    groups = []
    for grp in range(SSD_GROUPS):
        b_g = xbc[:, SSD_INNER + SSD_N * grp:SSD_INNER + SSD_N * (grp + 1)]
        c_g = xbc[:, SSD_INNER + SSD_BC + SSD_N * grp:SSD_INNER + SSD_BC + SSD_N * (grp + 1)]
        scores = _dot_nt(c_g, b_g).astype(BF16)
        pairs = []
        for pair in range(SSD_HPG // 2):
            m_pair = []
            for hh in range(2):
                lane = lane0 + SSD_HPG * grp + 2 * pair + hh
                cum_b = jnp.broadcast_to(cum[:, lane:lane + 1], (n, n))
                seg = jnp.where(causal, cum_b - cum_t[lane:lane + 1, :], NEG_BIG)
                m_pair.append(scores * jnp.exp2(seg).astype(BF16) * dt_t[lane:lane + 1, :])
            col = 2 * SSD_P * (SSD_HPG // 2 * grp + pair)
            x_pair = xbc[:, col:col + 2 * SSD_P]
            none = jnp.zeros_like(x_pair)
            rhs = jnp.concatenate([jnp.where(first_head, x_pair, none),
                                   jnp.where(first_head, none, x_pair)], axis=0)
            pairs.append((jnp.concatenate(m_pair, axis=1), rhs))
        cols = slice(SSD_HPG * SSD_P * grp, SSD_HPG * SSD_P * (grp + 1))
        groups.append(dict(b=b_g, c=c_g, pairs=pairs, e_cum=e_cum_x[:, cols],
                           e_tot=e_cum_x[end:end + 1, cols], x_w=x_w[:, cols]))
    return groups


def _split_hi_lo(a):
    hi = a.astype(BF16)
    lo = (a - hi.astype(F32)).astype(BF16)
    return jnp.concatenate([hi, lo], axis=1)


def _ssd_chain(xbc_ref, dt_ref, nega, expand_ref, d, s_ref, y_ref):
    def start():
        return _ssd_prepare(xbc_ref[...], dt_ref[...], nega, expand_ref[d],
                            DT_LANE + SSD_HEADS * d, d == 1)

    def finish(groups):
        outs = []
        for grp, gd in enumerate(groups):
            state = s_ref[grp]
            intra = jnp.concatenate([_dot(m, rhs) for m, rhs in gd["pairs"]], axis=1)
            outs.append(_dot(gd["c"], state.astype(BF16)) * gd["e_cum"] + intra)
            s_ref[grp] = state * gd["e_tot"] + _dot_tn(gd["b"], gd["x_w"])
        y_ref[...] = jnp.concatenate(outs, axis=1).astype(y_ref.dtype)

    return start, finish


def _mixer_scan_kernel(qkv_f_ref, g_f_ref, qkv_b_ref, g_b_ref, xbc_f_ref, dt_f_ref, xbc_b_ref,
                       dt_b_ref, nega_ref, expand_ref, ygf_ref, ygb_ref, ysf_ref, ysb_ref,
                       gsf_ref, gsb_ref, ssf_ref, ssb_ref):
    @pl.when(pl.program_id(1) == 0)
    def _():
        for s_ref in (gsf_ref, gsb_ref, ssf_ref, ssb_ref):
            s_ref[...] = jnp.zeros_like(s_ref)

    nega = nega_ref[...]
    chains = []
    for e in range(qkv_f_ref.shape[0]):
        chains.append(_gla_chain(qkv_f_ref.at[e], g_f_ref.at[e], False, gsf_ref.at[e], ygf_ref.at[e]))
        chains.append(_ssd_chain(xbc_f_ref.at[e], dt_f_ref.at[e], nega, expand_ref, 0,
                                 ssf_ref.at[e], ysf_ref.at[e]))
        chains.append(_gla_chain(qkv_b_ref.at[e], g_b_ref.at[e], True, gsb_ref.at[e], ygb_ref.at[e]))
        chains.append(_ssd_chain(xbc_b_ref.at[e], dt_b_ref.at[e], nega, expand_ref, 1,
                                 ssb_ref.at[e], ysb_ref.at[e]))
    started = chains[0][0]()
    for i, (_, finish) in enumerate(chains):
        following = chains[i + 1][0]() if i + 1 < len(chains) else None
        finish(started)
        started = following


def _head_expand_matrix():
    r = jnp.arange(2 * TAIL_W)[:, None] % TAIL_W
    c = jnp.arange(SSD_INNER)[None, :] // SSD_P
    return jnp.stack([(r == DT_LANE + SSD_HEADS * d + c) for d in range(2)]).astype(BF16)


def _mixer_scan(qkv, g, xbc, dt, nega):
    bsz, tt, _ = qkv.shape
    n_all = tt // CHUNK
    fwd, bwd = _chunk_maps(CTX_LEN // CHUNK, n_all)
    spec = lambda w, m, c=0: pl.BlockSpec((SCAN_BATCH, CHUNK, w), lambda b, s: (b, m(s), c))
    expand = _head_expand_matrix()
    return pl.pallas_call(
        _mixer_scan_kernel,
        grid=(bsz // SCAN_BATCH, n_all),
        in_specs=[spec(QKV_W, fwd), spec(GLA_QK, fwd), spec(QKV_W, bwd), spec(GLA_QK, bwd, 1),
                  spec(SSD_CONV_DIM, fwd), spec(TAIL_W, fwd), spec(SSD_CONV_DIM, bwd), spec(TAIL_W, bwd),
                  pl.BlockSpec((1, TAIL_W), lambda b, s: (0, 0)),
                  pl.BlockSpec(expand.shape, lambda b, s: (0, 0, 0))],
        out_specs=[spec(GLA_INNER, fwd), spec(GLA_INNER, bwd), spec(SSD_INNER, fwd), spec(SSD_INNER, bwd)],
        out_shape=[jax.ShapeDtypeStruct((bsz, tt, GLA_INNER), BF16)] * 2
        + [jax.ShapeDtypeStruct((bsz, tt, SSD_INNER), BF16)] * 2,
        scratch_shapes=[pltpu.VMEM((SCAN_BATCH, GLA_HEADS, GLA_DK, GLA_DV), F32)] * 2
        + [pltpu.VMEM((SCAN_BATCH, SSD_GROUPS, SSD_N, SSD_HPG * SSD_P), F32)] * 2,
        compiler_params=pltpu.CompilerParams(
            dimension_semantics=("parallel", "arbitrary"), vmem_limit_bytes=VMEM_LIMIT),
        name="mixer_scan",
    )(qkv, g, qkv, g, xbc, dt, xbc, dt, nega, expand)


def _stream_cast_weights(pairs, stage, sem):
    def chunk_copy(w_hbm, c, slot):
        cols = w_hbm.shape[1]
        return pltpu.make_async_copy(w_hbm.at[pl.ds(c * W_CHUNK, W_CHUNK), :],
                                     stage.at[slot, :, 0:cols], sem.at[slot])

    chunk_copy(pairs[0][0], 0, 0).start()
    first = 0
    for k, (w_hbm, w_bf) in enumerate(pairs):
        n_chunks, rest = divmod(w_hbm.shape[0], W_CHUNK)
        assert rest == 0
        cols = w_hbm.shape[1]
        following = pairs[k + 1][0] if k + 1 < len(pairs) else None

        def body(c, carry, w_hbm=w_hbm, w_bf=w_bf, n_chunks=n_chunks, cols=cols,
                 following=following, first=first):
            slot = jnp.bitwise_and(first + c, 1)

            @pl.when(c + 1 < n_chunks)
            def _():
                chunk_copy(w_hbm, c + 1, 1 - slot).start()

            if following is not None:
                @pl.when(c + 1 == n_chunks)
                def _():
                    chunk_copy(following, 0, 1 - slot).start()

            chunk_copy(w_hbm, c, slot).wait()
            rows = pl.ds(pl.multiple_of(c * W_CHUNK, W_CHUNK), W_CHUNK)
            w_bf[rows, :] = stage[slot, :, 0:cols].astype(BF16)
            return carry

        lax.fori_loop(0, n_chunks, body, 0)
        first += n_chunks


def _merge_ffn_kernel(x_ref, ygf_ref, ygb_ref, ysf_ref, ysb_ref, xs_ref, rz_ref, m_ref,
                      gnorm_ref, dskip_ref, snorm_ref, npost_ref, nfpre_ref, nfpost_ref,
                      wout_hbm, wgate_hbm, wup_hbm, wdown_hbm, o_ref,
                      wout_ref, wgate_ref, wup_ref, wdown_ref, stage, sem):
    b = pl.program_id(0)

    @pl.when(jnp.logical_and(b == 0, pl.program_id(1) == 0))
    def _():
        _stream_cast_weights([(wout_hbm, wout_ref), (wgate_hbm, wgate_ref),
                              (wup_hbm, wup_ref), (wdown_hbm, wdown_ref)], stage, sem)

    mrow = m_ref[pl.ds(b, 1), :]
    mod = [mrow[:, D_MODEL * j:D_MODEL * (j + 1)] for j in range(6)]
    gw = SSD_INNER // SSD_GROUPS

    def merged(rs):
        y_gla = ygf_ref[0, rs, :].astype(F32) + ygb_ref[0, rs, :].astype(F32)
        o_gla = jnp.concatenate(
            [_rms(y_gla[:, GLA_DV * h:GLA_DV * (h + 1)]) for h in range(GLA_HEADS)], axis=1)
        o_gla = o_gla * gnorm_ref[...] * rz_ref[0, rs, 0:GLA_INNER].astype(F32)
        y_ssd = ysf_ref[0, rs, :].astype(F32) + ysb_ref[0, rs, :].astype(F32)
        y_ssd = ((y_ssd + dskip_ref[...] * xs_ref[0, rs, :].astype(F32))
                 * rz_ref[0, rs, GLA_INNER:].astype(F32))
        o_ssd = jnp.concatenate(
            [_rms(y_ssd[:, gw * g:gw * (g + 1)]) for g in range(SSD_GROUPS)], axis=1)
        o_ssd = o_ssd * snorm_ref[...]
        return jnp.concatenate([o_gla, o_ssd], axis=1).astype(BF16)

    def residual_in(rs, o):
        y = _dot(o, wout_ref[...])
        x1 = x_ref[0, rs, :] + mod[2] * (_rms(y) * npost_ref[...])
        h = ((_rms(x1) * nfpre_ref[...]) * (1.0 + mod[4]) + mod[3]).astype(BF16)
        return x1, h

    subs = [slice(FFN_SUB * j, FFN_SUB * (j + 1)) for j in range(TM_FFN // FFN_SUB)]
    x1, h = residual_in(subs[0], merged(subs[0]))
    for j, rs in enumerate(subs):
        following = residual_in(subs[j + 1], merged(subs[j + 1])) if j + 1 < len(subs) else None
        act = (_silu(_dot(h, wgate_ref[...])) * _dot(h, wup_ref[...])).astype(BF16)
        f = _dot(act, wdown_ref[...])
        o_ref[0, rs, :] = x1 + mod[5] * (_rms(f) * nfpost_ref[...])
        if following is not None:
            x1, h = following


def _merge_ffn(x, ygf, ygb, ysf, ysb, xbc, rz, m_all, gnorm, dskip, snorm, npost, nfpre,
               nfpost, wout, wgate, wup, wdown):
    bsz, seq, _ = x.shape
    n_tiles = seq // TM_FFN
    const = lambda b, i: (0, 0)

    def lat(width):
        return pl.BlockSpec((1, TM_FFN, width), lambda b, i: (b, i, 0))

    weights = (wout, wgate, wup, wdown)
    in_hbm = pl.BlockSpec(memory_space=pl.ANY)
    vec = lambda w: pl.BlockSpec((1, w), const)
    return pl.pallas_call(
        _merge_ffn_kernel,
        grid=(bsz, n_tiles),
        in_specs=[pl.BlockSpec((1, TM_FFN, D_MODEL), lambda b, i: (b, i, 0)),
                  lat(GLA_INNER), lat(GLA_INNER), lat(SSD_INNER), lat(SSD_INNER),
                  lat(SSD_INNER), lat(GLA_INNER + SSD_INNER),
                  pl.BlockSpec(m_all.shape, const),
                  vec(GLA_INNER), vec(SSD_INNER), vec(SSD_INNER),
                  vec(D_MODEL), vec(D_MODEL), vec(D_MODEL),
                  in_hbm, in_hbm, in_hbm, in_hbm],
        out_specs=pl.BlockSpec((1, TM_FFN, D_MODEL), lambda b, i: (b, i, 0)),
        out_shape=jax.ShapeDtypeStruct((bsz, seq, D_MODEL), F32),
        scratch_shapes=[pltpu.VMEM(w.shape, BF16) for w in weights]
        + [pltpu.VMEM((2, W_CHUNK, max(w.shape[1] for w in weights)), F32),
           pltpu.SemaphoreType.DMA((2,))],
        compiler_params=pltpu.CompilerParams(
            dimension_semantics=("arbitrary", "arbitrary"), vmem_limit_bytes=VMEM_LIMIT),
        name="merge_ffn",
    )(x, ygf, ygb, ysf, ysb, xbc, rz, m_all, gnorm, dskip, snorm, npost, nfpre, nfpost,
      wout, wgate, wup, wdown)


def kernel(x, c, ctx, c_ctx, w_mod, b_mod, norm_mix_pre, norm_mix_post, norm_ffn_pre, norm_ffn_post,
           w_in, conv_w, conv_b, gla_wg_f, gla_bg_f, gla_wg_b, gla_bg_b, gla_norm,
           a_log_f, a_log_b, dt_bias_f, dt_bias_b, d_skip, ssd_norm, w_out, w_gate, w_up, w_down):
    assert w_mod.shape[0] == 1, "single-layer kernel"
    bsz = x.shape[0]
    row = lambda v: v.reshape(1, -1)

    c_all = jnp.concatenate([c, c_ctx[None], jnp.zeros((MOD_ROWS - bsz - 1, D_MODEL), F32)], axis=0)
    wg_pad = jnp.zeros((TAIL_W, 2 * GLA_QK), F32)
    wg_pad = wg_pad.at[0:GLA_RANK, 0:GLA_QK].set(gla_wg_f[0])
    wg_pad = wg_pad.at[GLA_RANK:2 * GLA_RANK, GLA_QK:].set(gla_wg_b[0]).astype(BF16)
    bg_cat = jnp.concatenate([gla_bg_f[0], gla_bg_b[0]]).reshape(1, -1)
    lane_pad = lambda f, bk: jnp.zeros((1, TAIL_W), F32).at[0, DT_LANE:DT_LANE + 2 * SSD_HEADS].set(
        jnp.concatenate([f, bk]))
    dtb_pad = lane_pad(dt_bias_f[0], dt_bias_b[0])
    nega = lane_pad(-jnp.exp(a_log_f[0]) * LOG2_E, -jnp.exp(a_log_b[0]) * LOG2_E)
    conv_w9 = conv_w[0].reshape(CONV_K * CONV_K, SSD_CONV_DIM)

    m_all = _modulation(c_all, w_mod[0], row(b_mod[0]))
    qkv, rz, xbc, g, dt = _input_projection_conv(
        x, ctx, m_all, row(norm_mix_pre[0]), w_in[0].T, wg_pad, bg_cat, dtb_pad, conv_w9, row(conv_b[0]))
    ygf, ygb, ysf, ysb = _mixer_scan(qkv, g, xbc, dt, nega)
    return _merge_ffn(
        x, ygf, ygb, ysf, ysb, xbc, rz, m_all,
        row(jnp.tile(gla_norm[0], GLA_HEADS)), row(jnp.repeat(d_skip[0], SSD_P)), row(ssd_norm[0]),
        row(norm_mix_post[0]), row(norm_ffn_pre[0]), row(norm_ffn_post[0]),
        w_out[0], w_gate[0], w_up[0], w_down[0])
```

```python
import jax
import jax.numpy as jnp
from jax import lax
from jax.experimental import pallas as pl
from jax.experimental.pallas import tpu as pltpu

F32 = jnp.float32
BF16 = jnp.bfloat16

D_MODEL = 1024
CTX_LEN = 256
GRID_W = 64
EPS = 1e-6

GLA_HEADS = 4
GLA_DK = 64
GLA_DV = 128
GLA_QK = GLA_HEADS * GLA_DK
GLA_INNER = GLA_HEADS * GLA_DV
GLA_RANK = 16
GLA_GATE_NORM = 16.0

SSD_HEADS = 8
SSD_P = 64
SSD_N = 64
SSD_GROUPS = 2
SSD_HPG = SSD_HEADS // SSD_GROUPS
SSD_INNER = SSD_HEADS * SSD_P
SSD_BC = SSD_GROUPS * SSD_N
SSD_CONV_DIM = SSD_INNER + 2 * SSD_BC
CONV_K = 3

QKV_W = 2 * GLA_QK + GLA_INNER
COL_R = QKV_W
COL_Z = COL_R + GLA_INNER
COL_XBC = COL_Z + SSD_INNER
COL_TAIL = COL_XBC + SSD_CONV_DIM
TAIL_W = 128
PROJ_W = COL_TAIL + TAIL_W
DT_LANE = 2 * GLA_RANK
W_IN_LR = COL_Z
W_IN_Z = W_IN_LR + 2 * GLA_RANK
W_IN_DT = W_IN_Z + SSD_INNER + SSD_CONV_DIM
W_IN_COLS = W_IN_DT + 2 * SSD_HEADS

TM_PROJ = 1024
PROJ_SUB = 256
CONV_COLS = 128
MOD_ROWS = 8
MOD_TN = 1024
TM_FFN = 512
FFN_SUB = 128
W_CHUNK = 256
CHUNK = 128
SCAN_BATCH = 4
SUB = 32
NEG_BIG = -1e30
LOG2_E = 1.4426950408889634
VMEM_LIMIT = 56 * 1024 * 1024


def _silu(x):
    return x / (1.0 + jnp.exp2(x * -LOG2_E))


def _softplus(x):
    return jnp.maximum(x, 0.0) + jnp.log(1.0 + jnp.exp(-jnp.abs(x)))


def _rms(x):
    return x * lax.rsqrt(jnp.mean(x * x, axis=-1, keepdims=True) + EPS)


def _dot(a, b):
    return jnp.dot(a, b, preferred_element_type=F32)


def _dot_nt(a, b):
    return lax.dot_general(a, b, (((1,), (1,)), ((), ())), preferred_element_type=F32)


def _dot_tn(a, b):
    return lax.dot_general(a, b, (((0,), (0,)), ((), ())), preferred_element_type=F32)


def _mod_kernel(c_ref, w_ref, b_ref, o_ref):
    s = _silu(c_ref[...])
    o_ref[...] = _dot(s.astype(BF16), w_ref[...].astype(BF16)) + b_ref[...]


def _modulation(c_all, w_mod, b_mod):
    n = w_mod.shape[1]
    return pl.pallas_call(
        _mod_kernel,
        grid=(n // MOD_TN,),
        in_specs=[pl.BlockSpec((MOD_ROWS, D_MODEL), lambda j: (0, 0)),
                  pl.BlockSpec((D_MODEL, MOD_TN), lambda j: (0, j)),
                  pl.BlockSpec((1, MOD_TN), lambda j: (0, j))],
        out_specs=pl.BlockSpec((MOD_ROWS, MOD_TN), lambda j: (0, j)),
        out_shape=jax.ShapeDtypeStruct((MOD_ROWS, n), F32),
        name="adaln_mod",
    )(c_all, w_mod, b_mod)


def _permute_cast_w_in(wt_ref, w_s):
    blk = 128
    pad = jnp.zeros((TAIL_W - (W_IN_Z - W_IN_LR) - (W_IN_COLS - W_IN_DT), blk), F32)

    def body(kb, carry):
        k0 = pl.multiple_of(kb * blk, blk)
        cols = pl.ds(k0, blk)
        rows = pl.ds(k0, blk)
        w_s[rows, 0:W_IN_LR] = jnp.transpose(wt_ref[0:W_IN_LR, cols]).astype(BF16)
        w_s[rows, COL_Z:COL_TAIL] = jnp.transpose(wt_ref[W_IN_Z:W_IN_DT, cols]).astype(BF16)
        tail = jnp.concatenate([wt_ref[W_IN_LR:W_IN_Z, cols], wt_ref[W_IN_DT:W_IN_COLS, cols], pad],
                               axis=0)
        w_s[rows, COL_TAIL:PROJ_W] = jnp.transpose(tail).astype(BF16)
        return carry

    lax.fori_loop(0, D_MODEL // blk, body, 0)


def _grid_conv_rows(window, t0, is_ctx, w_ref, b_ref, cols):
    n = window.shape[0] - 2 * GRID_W
    ext = window.astype(F32)
    row_on = jnp.where(is_ctx, 0.0, 1.0)
    taps = []
    for dx in range(CONV_K):
        tap = None
        for dy in range(CONV_K):
            wk = w_ref[CONV_K * dy + dx:CONV_K * dy + dx + 1, cols]
            if dy != 1:
                wk = wk * row_on
            term = wk * ext[GRID_W * dy:GRID_W * dy + n, :]
            tap = term if tap is None else tap + term
        taps.append(tap)
    t = t0 + lax.broadcasted_iota(jnp.int32, (n, 1), 0)
    col = jnp.bitwise_and(t, GRID_W - 1)
    has_left = jnp.where(is_ctx, t, col) != 0
    has_right = jnp.where(is_ctx, t - (CTX_LEN - 1), col - (GRID_W - 1)) != 0
    acc = (taps[1] + b_ref[:, cols]
           + jnp.where(has_left, pltpu.roll(taps[0], 1, axis=0), 0.0)
           + jnp.where(has_right, pltpu.roll(taps[2], n - 1, axis=0), 0.0))
    return _silu(acc)


def _inproj_conv_kernel(x_ref, ctx_ref, m_ref, gpre_ref, w_ref, wg_ref, bg_ref, dtb_ref,
                        cw_ref, cb_ref, qkv_ref, rz_ref, xbc_ref, g_ref, dt_ref,
                        w_s, prev_s, above_s):
    b = pl.program_id(0)
    i = pl.program_id(1)
    n_tiles = pl.num_programs(1) - 1
    ctx_tile = n_tiles - 1

    @pl.when(jnp.logical_and(b == 0, i == 0))
    def _():
        _permute_cast_w_in(w_ref, w_s)

    @pl.when(i == 0)
    def _():
        prev_s[...] = jnp.zeros_like(prev_s)
        above_s[...] = jnp.zeros_like(above_s)

    j = i - 1
    split = TM_PROJ - GRID_W
    none = jnp.zeros((GRID_W, SSD_CONV_DIM), BF16)
    col_groups = [slice(c, c + CONV_COLS) for c in range(0, SSD_CONV_DIM, CONV_COLS)]
    pieces = [(r0, min(CTX_LEN, split - r0)) for r0 in range(0, split, CTX_LEN)]

    def conv_block(t0, n, cols, window, conv_ctx):
        out = _grid_conv_rows(window, t0, conv_ctx, cw_ref, cb_ref, cols)
        xbc_ref[0, t0:t0 + n, cols] = out.astype(BF16)

    def conv_previous_head(conv_ctx):
        has_above = jnp.logical_and(j >= 1, j < ctx_tile)
        for cols in col_groups:
            for r0, n in (pieces[:1] if conv_ctx else pieces):
                if r0 == 0:
                    above = jnp.where(has_above, above_s[:, cols], none[:, cols])
                    window = jnp.concatenate([above, prev_s[0:n + GRID_W, cols]], axis=0)
                else:
                    window = prev_s[r0 - GRID_W:r0 + n + GRID_W, cols]
                conv_block(r0, n, cols, window, conv_ctx)
        above_s[...] = prev_s[split:, :]

    def conv_previous_last_row(below):
        below = jnp.where(j < ctx_tile - 1, below, none)
        for cols in col_groups:
            conv_block(split, GRID_W, cols,
                       jnp.concatenate([prev_s[split - GRID_W:, cols], below[:, cols]], axis=0), False)

    def project_tile(is_ctx, with_conv=True):
        mrow = m_ref[pl.ds(pl.num_programs(0) if is_ctx else b, 1), :]
        shift = mrow[:, 0:D_MODEL]
        gain = gpre_ref[...] * (1.0 + mrow[:, D_MODEL:2 * D_MODEL])

        def normed(rs):
            x = ctx_ref[0] if is_ctx else x_ref[0, rs, :]
            return (_rms(x) * gain + shift).astype(BF16)

        def project(rs, h):
            tail = _dot(h, w_s[:, COL_TAIL:PROJ_W])
            qkv = _dot(h, w_s[:, 0:QKV_W])
            qkv_ref[0, rs, 0:GLA_QK] = (qkv[:, 0:GLA_QK] * (GLA_DK ** -0.5)).astype(BF16)
            qkv_ref[0, rs, GLA_QK:QKV_W] = qkv[:, GLA_QK:QKV_W].astype(BF16)
            rz_ref[0, rs, :] = _silu(_dot(h, w_s[:, COL_R:COL_XBC])).astype(BF16)
            xbc_new = _dot(h, w_s[:, COL_XBC:COL_TAIL]).astype(BF16)
            logits = _dot(tail.astype(BF16), wg_ref[...]) + bg_ref[...]
            u = logits * LOG2_E
            g_ref[0, rs, :] = ((jnp.minimum(u, 0.0) - jnp.log2(1.0 + jnp.exp2(-jnp.abs(u))))
                               * (1.0 / GLA_GATE_NORM))
            dt_ref[0, rs, :] = _softplus(tail + dtb_ref[...])
            return xbc_new

        rows = CTX_LEN if is_ctx else TM_PROJ
        subs = [slice(PROJ_SUB * s, PROJ_SUB * (s + 1)) for s in range(rows // PROJ_SUB)]
        if with_conv:
            conv_previous_head(False)
        ready = normed(subs[0])
        fresh = []
        for s, rs in enumerate(subs):
            following = normed(subs[s + 1]) if s + 1 < len(subs) else None
            fresh.append(project(rs, ready))
            if s == 0 and with_conv:
                conv_previous_last_row(fresh[0][0:GRID_W, :])
            ready = following
        for rs, xbc_new in zip(subs, fresh):
            prev_s[rs, :] = xbc_new

    @pl.when(i == 0)
    def _():
        project_tile(False, with_conv=False)

    @pl.when(jnp.logical_and(i > 0, i < ctx_tile))
    def _():
        project_tile(False)

    @pl.when(i == ctx_tile)
    def _():
        project_tile(True)

    @pl.when(i == n_tiles)
    def _():
        conv_previous_head(True)


def _input_projection_conv(x, ctx, m_all, gpre, w_in, wg_pad, bg_cat, dtb_pad, conv_w9, conv_b):
    bsz, seq, _ = x.shape
    n_lat = seq // TM_PROJ
    n_tiles = n_lat + 1
    tt = CTX_LEN + seq
    const = lambda b, i: (0, 0)
    tok = lambda b, i: (b, jnp.minimum(i, n_tiles - 1), 0)
    conv_tok = lambda b, i: (b, jnp.maximum(i - 1, 0), 0)
    return pl.pallas_call(
        _inproj_conv_kernel,
        grid=(bsz, n_tiles + 1),
        in_specs=[pl.BlockSpec((1, TM_PROJ, D_MODEL), lambda b, i: (b, jnp.minimum(i, n_lat - 1), 0)),
                  pl.BlockSpec((1, CTX_LEN, D_MODEL), lambda b, i: (b, 0, 0)),
                  pl.BlockSpec(m_all.shape, const),
                  pl.BlockSpec((1, D_MODEL), const),
                  pl.BlockSpec((W_IN_COLS, D_MODEL), const, pipeline_mode=pl.Buffered(1)),
                  pl.BlockSpec((TAIL_W, 2 * GLA_QK), const),
                  pl.BlockSpec((1, 2 * GLA_QK), const),
                  pl.BlockSpec((1, TAIL_W), const),
                  pl.BlockSpec((CONV_K * CONV_K, SSD_CONV_DIM), const),
                  pl.BlockSpec((1, SSD_CONV_DIM), const)],
        out_specs=[pl.BlockSpec((1, TM_PROJ, QKV_W), tok),
                   pl.BlockSpec((1, TM_PROJ, GLA_INNER + SSD_INNER), tok),
                   pl.BlockSpec((1, TM_PROJ, SSD_CONV_DIM), conv_tok),
                   pl.BlockSpec((1, TM_PROJ, 2 * GLA_QK), tok),
                   pl.BlockSpec((1, TM_PROJ, TAIL_W), tok)],
        out_shape=[jax.ShapeDtypeStruct((bsz, tt, QKV_W), BF16),
                   jax.ShapeDtypeStruct((bsz, tt, GLA_INNER + SSD_INNER), BF16),
                   jax.ShapeDtypeStruct((bsz, tt, SSD_CONV_DIM), BF16),
                   jax.ShapeDtypeStruct((bsz, tt, 2 * GLA_QK), F32),
                   jax.ShapeDtypeStruct((bsz, tt, TAIL_W), F32)],
        scratch_shapes=[pltpu.VMEM((D_MODEL, PROJ_W), BF16),
                        pltpu.VMEM((TM_PROJ, SSD_CONV_DIM), BF16),
                        pltpu.VMEM((GRID_W, SSD_CONV_DIM), BF16)],
        compiler_params=pltpu.CompilerParams(
            dimension_semantics=("arbitrary", "arbitrary"), vmem_limit_bytes=VMEM_LIMIT),
        name="in_proj_conv",
    )(x, ctx, m_all, gpre, w_in, wg_pad, bg_cat, dtb_pad, conv_w9, conv_b)


def _cumsum_rows(x, reverse):
    n = x.shape[0]
    row = lax.broadcasted_iota(jnp.int32, x.shape, 0)
    s = 1
    while s < n:
        if reverse:
            x = x + jnp.where(row < n - s, pltpu.roll(x, n - s, axis=0), 0.0)
        else:
            x = x + jnp.where(row >= s, pltpu.roll(x, s, axis=0), 0.0)
        s *= 2
    return x


def _chunk_maps(n_ctx, n_all):
    n_lat = n_all - n_ctx
    fwd = lambda s: jnp.where(s < n_ctx, n_lat + s, s - n_ctx)
    bwd = lambda s: n_all - 1 - s
    return fwd, bwd


def _gla_prepare(qkv, g, reverse):
    n = qkv.shape[0]
    nb = n // SUB
    blocks = range(nb)
    rows = lambda a, blk: a[SUB * blk:SUB * (blk + 1), :]
    q = qkv[:, 0:GLA_QK].astype(F32)
    k = qkv[:, GLA_QK:2 * GLA_QK].astype(F32)
    cum = _cumsum_rows(g, reverse)
    zero = jnp.zeros((1, GLA_QK), F32)
    if reverse:
        ends = [cum[SUB * blk:SUB * blk + 1, :] for blk in blocks]
        refs = ends[1:] + [zero]
        earlier = lambda j, blk: j > blk
        tot = ends[0]
    else:
        ends = [cum[SUB * (blk + 1) - 1:SUB * (blk + 1), :] for blk in blocks]
        refs = [zero] + ends[:-1]
        earlier = lambda j, blk: j < blk
        tot = ends[nb - 1]
    stack = lambda vs: jnp.concatenate([jnp.broadcast_to(v, (SUB, GLA_QK)) for v in vs], axis=0)
    e_in = cum - stack(refs)
    q_blk = q * jnp.exp2(e_in)
    k_diag = k * jnp.exp2(-e_in)
    k_end = k * jnp.exp2(stack(ends) - cum)
    q_in = jnp.concatenate([rows(q_blk, blk) * jnp.exp2(refs[blk]) for blk in blocks], axis=0)
    k_out = jnp.concatenate([rows(k_end, blk) * jnp.exp2(tot - ends[blk]) for blk in blocks], axis=0)
    q_in = q_in.astype(BF16)
    k_out = k_out.astype(BF16)
    q_bf = q_blk.astype(BF16)
    zeros_piece = jnp.zeros((SUB, GLA_QK), BF16)
    k_piece = [[rows(k_diag, j).astype(BF16) if seg == j
                else (rows(k_end, j) * jnp.exp2(refs[seg] - ends[j])).astype(BF16) if earlier(j, seg)
                else zeros_piece
                for seg in blocks] for j in blocks]
    ii = lax.broadcasted_iota(jnp.int32, (n, n), 0)
    jj = lax.broadcasted_iota(jnp.int32, (n, n), 1)
    heads = []
    for h in range(GLA_HEADS):
        ks = slice(GLA_DK * h, GLA_DK * (h + 1))
        q_cat = jnp.concatenate(
            [jnp.concatenate([rows(q_bf, blk)[:, ks] if seg == blk else zeros_piece[:, ks]
                              for seg in blocks], axis=1) for blk in blocks], axis=0)
        k_cat = jnp.concatenate(
            [jnp.concatenate([k_piece[j][seg][:, ks] for seg in blocks], axis=1)
             for j in blocks], axis=0)
        heads.append(dict(q_cat=q_cat, k_cat=k_cat, q_in=q_in[:, ks], k_out=k_out[:, ks],
                          v=qkv[:, 2 * GLA_QK + GLA_DV * h:2 * GLA_QK + GLA_DV * (h + 1)]))
    return dict(heads=heads, causal=(ii <= jj) if reverse else (ii >= jj),
                decay_col=jnp.transpose(jnp.broadcast_to(tot, (GLA_DV, GLA_QK))))


def _gla_chain(qkv_ref, g_ref, reverse, s_ref, y_ref):
    def start():
        p = _gla_prepare(qkv_ref[...], g_ref[...], reverse)
        return p, [_dot_nt(hd["q_cat"], hd["k_cat"]) for hd in p["heads"]]

    def finish(started):
        p, att = started
        outs = []
        for h, (hd, a) in enumerate(zip(p["heads"], att)):
            state = s_ref[h]
            lhs = jnp.concatenate([jnp.where(p["causal"], a, 0.0).astype(BF16), hd["q_in"]], axis=1)
            rhs = jnp.concatenate([hd["v"], state.astype(BF16)], axis=0)
            outs.append(_dot(lhs, rhs))
            ks = slice(GLA_DK * h, GLA_DK * (h + 1))
            s_ref[h] = state * jnp.exp2(p["decay_col"][ks, :]) + _dot_tn(hd["k_out"], hd["v"])
        y_ref[...] = jnp.concatenate(outs, axis=1).astype(y_ref.dtype)

    return start, finish


def _ssd_prepare(xbc, dt, nega, expand, lane0, reverse):
    n = xbc.shape[0]
    end = 0 if reverse else n - 1
    cum = _cumsum_rows(dt * nega, reverse)
    e_cum = jnp.exp2(cum)
    w = jnp.exp2(cum[end:end + 1, :] - cum) * dt
    e_cum_x = _dot(_split_hi_lo(e_cum), expand)
    x_w = xbc[:, 0:SSD_INNER] * _dot(w.astype(BF16), expand[0:TAIL_W, :]).astype(BF16)
    cum_t = jnp.transpose(cum)
    dt_t = jnp.transpose(dt).astype(BF16)
    ii = lax.broadcasted_iota(jnp.int32, (n, n), 0)
    jj = lax.broadcasted_iota(jnp.int32, (n, n), 1)
    causal = (ii <= jj) if reverse else (ii >= jj)
    first_head = lax.broadcasted_iota(jnp.int32, (n, 2 * SSD_P), 1) < SSD_P
    groups = []
    for grp in range(SSD_GROUPS):
        b_g = xbc[:, SSD_INNER + SSD_N * grp:SSD_INNER + SSD_N * (grp + 1)]
        c_g = xbc[:, SSD_INNER + SSD_BC + SSD_N * grp:SSD_INNER + SSD_BC + SSD_N * (grp + 1)]
        scores = _dot_nt(c_g, b_g).astype(BF16)
        pairs = []
        for pair in range(SSD_HPG // 2):
            m_pair = []
            for hh in range(2):
                lane = lane0 + SSD_HPG * grp + 2 * pair + hh
                cum_b = jnp.broadcast_to(cum[:, lane:lane + 1], (n, n))
                seg = jnp.where(causal, cum_b - cum_t[lane:lane + 1, :], NEG_BIG)
                m_pair.append(scores * jnp.exp2(seg).astype(BF16) * dt_t[lane:lane + 1, :])
            col = 2 * SSD_P * (SSD_HPG // 2 * grp + pair)
            x_pair = xbc[:, col:col + 2 * SSD_P]
            none = jnp.zeros_like(x_pair)
            rhs = jnp.concatenate([jnp.where(first_head, x_pair, none),
                                   jnp.where(first_head, none, x_pair)], axis=0)
            pairs.append((jnp.concatenate(m_pair, axis=1), rhs))
        cols = slice(SSD_HPG * SSD_P * grp, SSD_HPG * SSD_P * (grp + 1))
        groups.append(dict(b=b_g, c=c_g, pairs=pairs, e_cum=e_cum_x[:, cols],
                           e_tot=e_cum_x[end:end + 1, cols], x_w=x_w[:, cols]))
    return groups


def _split_hi_lo(a):
    hi = a.astype(BF16)
    lo = (a - hi.astype(F32)).astype(BF16)
    return jnp.concatenate([hi, lo], axis=1)


def _ssd_chain(xbc_ref, dt_ref, nega, expand_ref, d, s_ref, y_ref):
    def start():
        return _ssd_prepare(xbc_ref[...], dt_ref[...], nega, expand_ref[d],
                            DT_LANE + SSD_HEADS * d, d == 1)

    def finish(groups):
        outs = []
        for grp, gd in enumerate(groups):
            state = s_ref[grp]
            intra = jnp.concatenate([_dot(m, rhs) for m, rhs in gd["pairs"]], axis=1)
            outs.append(_dot(gd["c"], state.astype(BF16)) * gd["e_cum"] + intra)
            s_ref[grp] = state * gd["e_tot"] + _dot_tn(gd["b"], gd["x_w"])
        y_ref[...] = jnp.concatenate(outs, axis=1).astype(y_ref.dtype)

    return start, finish


def _mixer_scan_kernel(qkv_f_ref, g_f_ref, qkv_b_ref, g_b_ref, xbc_f_ref, dt_f_ref, xbc_b_ref,
                       dt_b_ref, nega_ref, expand_ref, ygf_ref, ygb_ref, ysf_ref, ysb_ref,
                       gsf_ref, gsb_ref, ssf_ref, ssb_ref):
    @pl.when(pl.program_id(1) == 0)
    def _():
        for s_ref in (gsf_ref, gsb_ref, ssf_ref, ssb_ref):
            s_ref[...] = jnp.zeros_like(s_ref)

    nega = nega_ref[...]
    chains = []
    for e in range(qkv_f_ref.shape[0]):
        chains.append(_gla_chain(qkv_f_ref.at[e], g_f_ref.at[e], False, gsf_ref.at[e], ygf_ref.at[e]))
        chains.append(_ssd_chain(xbc_f_ref.at[e], dt_f_ref.at[e], nega, expand_ref, 0,
                                 ssf_ref.at[e], ysf_ref.at[e]))
        chains.append(_gla_chain(qkv_b_ref.at[e], g_b_ref.at[e], True, gsb_ref.at[e], ygb_ref.at[e]))
        chains.append(_ssd_chain(xbc_b_ref.at[e], dt_b_ref.at[e], nega, expand_ref, 1,
                                 ssb_ref.at[e], ysb_ref.at[e]))
    started = chains[0][0]()
    for i, (_, finish) in enumerate(chains):
        following = chains[i + 1][0]() if i + 1 < len(chains) else None
        finish(started)
        started = following


def _head_expand_matrix():
    r = jnp.arange(2 * TAIL_W)[:, None] % TAIL_W
    c = jnp.arange(SSD_INNER)[None, :] // SSD_P
    return jnp.stack([(r == DT_LANE + SSD_HEADS * d + c) for d in range(2)]).astype(BF16)


def _mixer_scan(qkv, g, xbc, dt, nega):
    bsz, tt, _ = qkv.shape
    n_all = tt // CHUNK
    fwd, bwd = _chunk_maps(CTX_LEN // CHUNK, n_all)
    spec = lambda w, m, c=0: pl.BlockSpec((SCAN_BATCH, CHUNK, w), lambda b, s: (b, m(s), c))
    expand = _head_expand_matrix()
    return pl.pallas_call(
        _mixer_scan_kernel,
        grid=(bsz // SCAN_BATCH, n_all),
        in_specs=[spec(QKV_W, fwd), spec(GLA_QK, fwd), spec(QKV_W, bwd), spec(GLA_QK, bwd, 1),
                  spec(SSD_CONV_DIM, fwd), spec(TAIL_W, fwd), spec(SSD_CONV_DIM, bwd), spec(TAIL_W, bwd),
                  pl.BlockSpec((1, TAIL_W), lambda b, s: (0, 0)),
                  pl.BlockSpec(expand.shape, lambda b, s: (0, 0, 0))],
        out_specs=[spec(GLA_INNER, fwd), spec(GLA_INNER, bwd), spec(SSD_INNER, fwd), spec(SSD_INNER, bwd)],
        out_shape=[jax.ShapeDtypeStruct((bsz, tt, GLA_INNER), BF16)] * 2
        + [jax.ShapeDtypeStruct((bsz, tt, SSD_INNER), BF16)] * 2,
        scratch_shapes=[pltpu.VMEM((SCAN_BATCH, GLA_HEADS, GLA_DK, GLA_DV), F32)] * 2
        + [pltpu.VMEM((SCAN_BATCH, SSD_GROUPS, SSD_N, SSD_HPG * SSD_P), F32)] * 2,
        compiler_params=pltpu.CompilerParams(
            dimension_semantics=("parallel", "arbitrary"), vmem_limit_bytes=VMEM_LIMIT),
        name="mixer_scan",
    )(qkv, g, qkv, g, xbc, dt, xbc, dt, nega, expand)


def _stream_cast_weights(pairs, stage, sem):
    def chunk_copy(w_hbm, c, slot):
        cols = w_hbm.shape[1]
        return pltpu.make_async_copy(w_hbm.at[pl.ds(c * W_CHUNK, W_CHUNK), :],
                                     stage.at[slot, :, 0:cols], sem.at[slot])

    chunk_copy(pairs[0][0], 0, 0).start()
    first = 0
    for k, (w_hbm, w_bf) in enumerate(pairs):
        n_chunks, rest = divmod(w_hbm.shape[0], W_CHUNK)
        assert rest == 0
        cols = w_hbm.shape[1]
        following = pairs[k + 1][0] if k + 1 < len(pairs) else None

        def body(c, carry, w_hbm=w_hbm, w_bf=w_bf, n_chunks=n_chunks, cols=cols,
                 following=following, first=first):
            slot = jnp.bitwise_and(first + c, 1)

            @pl.when(c + 1 < n_chunks)
            def _():
                chunk_copy(w_hbm, c + 1, 1 - slot).start()

            if following is not None:
                @pl.when(c + 1 == n_chunks)
                def _():
                    chunk_copy(following, 0, 1 - slot).start()

            chunk_copy(w_hbm, c, slot).wait()
            rows = pl.ds(pl.multiple_of(c * W_CHUNK, W_CHUNK), W_CHUNK)
            w_bf[rows, :] = stage[slot, :, 0:cols].astype(BF16)
            return carry

        lax.fori_loop(0, n_chunks, body, 0)
        first += n_chunks


def _merge_ffn_kernel(x_ref, ygf_ref, ygb_ref, ysf_ref, ysb_ref, xs_ref, rz_ref, m_ref,
                      gnorm_ref, dskip_ref, snorm_ref, npost_ref, nfpre_ref, nfpost_ref,
                      wout_hbm, wgate_hbm, wup_hbm, wdown_hbm, o_ref,
                      wout_ref, wgate_ref, wup_ref, wdown_ref, stage, sem):
    b = pl.program_id(0)

    @pl.when(jnp.logical_and(b == 0, pl.program_id(1) == 0))
    def _():
        _stream_cast_weights([(wout_hbm, wout_ref), (wgate_hbm, wgate_ref),
                              (wup_hbm, wup_ref), (wdown_hbm, wdown_ref)], stage, sem)

    mrow = m_ref[pl.ds(b, 1), :]
    mod = [mrow[:, D_MODEL * j:D_MODEL * (j + 1)] for j in range(6)]
    gw = SSD_INNER // SSD_GROUPS

    def merged(rs):
        parts = []
        for h in range(GLA_HEADS):
            cs = slice(GLA_DV * h, GLA_DV * (h + 1))
            y = ygf_ref[0, rs, cs].astype(F32) + ygb_ref[0, rs, cs].astype(F32)
            parts.append((_rms(y) * gnorm_ref[:, cs] * rz_ref[0, rs, cs].astype(F32)).astype(BF16))
        for g in range(SSD_GROUPS):
            cs = slice(gw * g, gw * (g + 1))
            gate = rz_ref[0, rs, GLA_INNER + gw * g:GLA_INNER + gw * (g + 1)].astype(F32)
            y = ysf_ref[0, rs, cs].astype(F32) + ysb_ref[0, rs, cs].astype(F32)
            y = (y + dskip_ref[:, cs] * xs_ref[0, rs, cs].astype(F32)) * gate
            parts.append((_rms(y) * snorm_ref[:, cs]).astype(BF16))
        return jnp.concatenate(parts, axis=1)

    def residual_in(rs, o):
        y = _dot(o, wout_ref[...])
        x1 = x_ref[0, rs, :] + mod[2] * (_rms(y) * npost_ref[...])
        h = ((_rms(x1) * nfpre_ref[...]) * (1.0 + mod[4]) + mod[3]).astype(BF16)
        return x1, h

    subs = [slice(FFN_SUB * j, FFN_SUB * (j + 1)) for j in range(TM_FFN // FFN_SUB)]
    x1, h = residual_in(subs[0], merged(subs[0]))
    for j, rs in enumerate(subs):
        following = residual_in(subs[j + 1], merged(subs[j + 1])) if j + 1 < len(subs) else None
        act = (_silu(_dot(h, wgate_ref[...])) * _dot(h, wup_ref[...])).astype(BF16)
        f = _dot(act, wdown_ref[...])
        o_ref[0, rs, :] = x1 + mod[5] * (_rms(f) * nfpost_ref[...])
        if following is not None:
            x1, h = following


def _merge_ffn(x, ygf, ygb, ysf, ysb, xbc, rz, m_all, gnorm, dskip, snorm, npost, nfpre,
               nfpost, wout, wgate, wup, wdown):
    bsz, seq, _ = x.shape
    n_tiles = seq // TM_FFN
    const = lambda b, i: (0, 0)

    def lat(width):
        return pl.BlockSpec((1, TM_FFN, width), lambda b, i: (b, i, 0))

    weights = (wout, wgate, wup, wdown)
    in_hbm = pl.BlockSpec(memory_space=pl.ANY)
    vec = lambda w: pl.BlockSpec((1, w), const)
    return pl.pallas_call(
        _merge_ffn_kernel,
        grid=(bsz, n_tiles),
        in_specs=[pl.BlockSpec((1, TM_FFN, D_MODEL), lambda b, i: (b, i, 0)),
                  lat(GLA_INNER), lat(GLA_INNER), lat(SSD_INNER), lat(SSD_INNER),
                  lat(SSD_INNER), lat(GLA_INNER + SSD_INNER),
                  pl.BlockSpec(m_all.shape, const),
                  vec(GLA_INNER), vec(SSD_INNER), vec(SSD_INNER),
                  vec(D_MODEL), vec(D_MODEL), vec(D_MODEL),
                  in_hbm, in_hbm, in_hbm, in_hbm],
        out_specs=pl.BlockSpec((1, TM_FFN, D_MODEL), lambda b, i: (b, i, 0)),
        out_shape=jax.ShapeDtypeStruct((bsz, seq, D_MODEL), F32),
        scratch_shapes=[pltpu.VMEM(w.shape, BF16) for w in weights]
        + [pltpu.VMEM((2, W_CHUNK, max(w.shape[1] for w in weights)), F32),
           pltpu.SemaphoreType.DMA((2,))],
        compiler_params=pltpu.CompilerParams(
            dimension_semantics=("arbitrary", "arbitrary"), vmem_limit_bytes=VMEM_LIMIT),
        name="merge_ffn",
    )(x, ygf, ygb, ysf, ysb, xbc, rz, m_all, gnorm, dskip, snorm, npost, nfpre, nfpost,
      wout, wgate, wup, wdown)


def kernel(x, c, ctx, c_ctx, w_mod, b_mod, norm_mix_pre, norm_mix_post, norm_ffn_pre, norm_ffn_post,
           w_in, conv_w, conv_b, gla_wg_f, gla_bg_f, gla_wg_b, gla_bg_b, gla_norm,
           a_log_f, a_log_b, dt_bias_f, dt_bias_b, d_skip, ssd_norm, w_out, w_gate, w_up, w_down):
    assert w_mod.shape[0] == 1, "single-layer kernel"
    bsz = x.shape[0]
    row = lambda v: v.reshape(1, -1)

    c_all = jnp.concatenate([c, c_ctx[None], jnp.zeros((MOD_ROWS - bsz - 1, D_MODEL), F32)], axis=0)
    wg_pad = jnp.zeros((TAIL_W, 2 * GLA_QK), F32)
    wg_pad = wg_pad.at[0:GLA_RANK, 0:GLA_QK].set(gla_wg_f[0])
    wg_pad = wg_pad.at[GLA_RANK:2 * GLA_RANK, GLA_QK:].set(gla_wg_b[0]).astype(BF16)
    bg_cat = jnp.concatenate([gla_bg_f[0], gla_bg_b[0]]).reshape(1, -1)
    lane_pad = lambda f, bk: jnp.zeros((1, TAIL_W), F32).at[0, DT_LANE:DT_LANE + 2 * SSD_HEADS].set(
        jnp.concatenate([f, bk]))
    dtb_pad = lane_pad(dt_bias_f[0], dt_bias_b[0])
    nega = lane_pad(-jnp.exp(a_log_f[0]) * LOG2_E, -jnp.exp(a_log_b[0]) * LOG2_E)
    conv_w9 = conv_w[0].reshape(CONV_K * CONV_K, SSD_CONV_DIM)

    m_all = _modulation(c_all, w_mod[0], row(b_mod[0]))
    qkv, rz, xbc, g, dt = _input_projection_conv(
        x, ctx, m_all, row(norm_mix_pre[0]), w_in[0].T, wg_pad, bg_cat, dtb_pad, conv_w9, row(conv_b[0]))
    ygf, ygb, ysf, ysb = _mixer_scan(qkv, g, xbc, dt, nega)
    return _merge_ffn(
        x, ygf, ygb, ysf, ysb, xbc, rz, m_all,
        row(jnp.tile(gla_norm[0], GLA_HEADS)), row(jnp.repeat(d_skip[0], SSD_P)), row(ssd_norm[0]),
        row(norm_mix_post[0]), row(norm_ffn_pre[0]), row(norm_ffn_post[0]),
        w_out[0], w_gate[0], w_up[0], w_down[0])
```
